```python
import math
import jax
import jax.numpy as jnp
from jax import lax
import numpy as np


D_MODEL = 1024
BATCH = 8
SEQ = 2048
DEPTH = 2

GRID_W = 64
PLE_DIM = 256
N_EVEN = (DEPTH + 1) // 2
N_ODD = DEPTH // 2

NA_HEADS = 8
NA_HEAD_DIM = 64
NA_WIDTH = NA_HEADS * NA_HEAD_DIM
NA_WIN_ROWS_MAX = 8
NA_WIN_COLS = 16

HY_WIDTH = D_MODEL - NA_WIDTH
HY_ORDER = 2
HY_SHORT_K = 3
HY_EMB_DIM = 33
HY_FILTER_HIDDEN = 64
HY_FAST_DECAY_PCT = 0.3
HY_SLOW_DECAY_PCT = 1.5
HY_DECAY_TARGET = 1e-2
HY_FILTER_OUT = HY_ORDER * 2 * HY_WIDTH

AB_IN_WIDTH = 3 * NA_WIDTH + (HY_ORDER + 1) * HY_WIDTH

MLA_HEADS = 16
MLA_Q_LORA = 384
MLA_KV_LORA = 256
MLA_NOPE = 64
MLA_ROPE = 32
MLA_V = 64
MLA_IN_WIDTH = MLA_Q_LORA + MLA_KV_LORA + MLA_ROPE
ROPE_THETA = 10000.0
Q_BLOCK = 128

N_EXPERTS = 16
EC_CAPACITY_FACTOR = 2
D_FF_EXPERT = 2048

DN_ALPHA = (2 * DEPTH) ** 0.25
DN_BETA = (8 * DEPTH) ** -0.25
NORM_EPS = 1e-5
NEG_INF = -1e30

kernel_name = 'hybrid_na_hyena_mla_ec_encoder'


def layer_norm(x, g, b):
    xf = x.astype(jnp.float32)
    mu = jnp.mean(xf, axis=-1, keepdims=True)
    var = jnp.mean(jnp.square(xf - mu), axis=-1, keepdims=True)
    return ((xf - mu) * lax.rsqrt(var + NORM_EPS) * g + b).astype(x.dtype)


def rms_norm(x, g):
    xf = x.astype(jnp.float32)
    return (xf * lax.rsqrt(jnp.mean(jnp.square(xf), axis=-1, keepdims=True) + NORM_EPS) * g).astype(x.dtype)


def neighbourhood_attention(q, k, v, rpb):
    B, L, H, Dh = q.shape
    rows = L // GRID_W
    wr = min(NA_WIN_ROWS_MAX, rows)
    qg = q.reshape(B, rows, GRID_W, H, Dh)
    kg = k.reshape(B, rows, GRID_W, H, Dh)
    vg = v.reshape(B, rows, GRID_W, H, Dh)
    r = np.arange(rows)
    r0 = np.clip(r - wr // 2, 0, rows - wr)
    row_idx = r0[:, None] + np.arange(wr)[None, :]
    c = np.arange(GRID_W)
    c0 = np.clip(c - NA_WIN_COLS // 2, 0, GRID_W - NA_WIN_COLS)
    kc = np.arange(GRID_W)
    col_ok = (kc[None, :] >= c0[:, None]) & (kc[None, :] < c0[:, None] + NA_WIN_COLS)
    dr_idx = row_idx - r[:, None] + (NA_WIN_ROWS_MAX - 1)
    dc_idx = np.clip(kc[None, :] - c[:, None], -(NA_WIN_COLS - 1), NA_WIN_COLS - 1) + (NA_WIN_COLS - 1)
    kr = kg[:, row_idx]
    vr = vg[:, row_idx]
    s = jnp.einsum('brchd,brwkhd->bhrcwk', qg, kr).astype(jnp.float32) * (Dh ** -0.5)
    bias = rpb[:, dr_idx[:, None, :, None], dc_idx[None, :, None, :]].astype(jnp.float32)
    s = jnp.where(col_ok[None, None, None, :, None, :], s + bias[None], NEG_INF)
    sh = s.shape
    prob = jax.nn.softmax(s.reshape(sh[:4] + (wr * GRID_W,)), axis=-1).reshape(sh).astype(v.dtype)
    out = jnp.einsum('bhrcwk,brwkhd->brchd', prob, vr)
    return out.reshape(B, L, H * Dh)


def short_conv_centred(x, w, b):
    L = x.shape[1]
    pad = HY_SHORT_K // 2
    xp = jnp.pad(x, ((0, 0), (pad, pad), (0, 0)))
    y = b
    for j in range(HY_SHORT_K):
        y = y + xp[:, j:j + L] * w[j]
    return y


def hyena_filters(L, w1, b1, freq, w2, b2, w3):
    f32 = jnp.float32
    bands = (HY_EMB_DIM - 1) // 2
    t = jnp.linspace(0.0, 1.0, L, dtype=f32)[:, None]
    w = 2.0 * math.pi * jnp.arange(L, dtype=f32)[:, None] / L
    f = jnp.linspace(1e-4, bands - 1, bands, dtype=f32)[None, :]
    z = jnp.concatenate([t, jnp.cos(f * w), -jnp.sin(f * w)], axis=-1)
    h = jnp.sin(freq * (z @ w1 + b1))
    h = jnp.sin(freq * (h @ w2 + b2))
    h = (h @ w3).astype(f32).reshape(L, HY_ORDER, 2, HY_WIDTH)
    min_decay = math.log(HY_DECAY_TARGET) / HY_SLOW_DECAY_PCT
    max_decay = math.log(HY_DECAY_TARGET) / HY_FAST_DECAY_PCT
    deltas = jnp.abs(jnp.linspace(min_decay, max_decay, HY_WIDTH, dtype=f32))
    h = h * jnp.exp(-t * deltas)[:, None, None, :]
    k_fwd = h[:, :, 0]
    k_bwd = h[:, :, 1]
    K = jnp.concatenate([k_fwd, jnp.zeros((1, HY_ORDER, HY_WIDTH), f32), k_bwd[1:][::-1]], axis=0)
    return K * lax.rsqrt(jnp.sum(jnp.square(K), axis=0, keepdims=True) + 1e-12)


def long_conv_bidirectional(u, K, skip):
    L = u.shape[1]
    uf = jnp.fft.rfft(u.astype(jnp.float32), n=2 * L, axis=1)
    kf = jnp.fft.rfft(K, n=2 * L, axis=0)
    y = jnp.fft.irfft(uf * kf[None], n=2 * L, axis=1)[:, :L]
    return (y + u.astype(jnp.float32) * skip.astype(jnp.float32)).astype(u.dtype)


def na_hyena_mixer(x, w_in, rpb, conv_w, conv_b, f_w1, f_b1, f_freq, f_w2, f_b2, f_w3, skip, w_out):
    B, L, _ = x.shape
    h = x @ w_in
    qa, ka, va, hb = jnp.split(h, [NA_WIDTH, 2 * NA_WIDTH, 3 * NA_WIDTH], axis=-1)
    shp = (B, L, NA_HEADS, NA_HEAD_DIM)
    y_a = neighbourhood_attention(qa.reshape(shp), ka.reshape(shp), va.reshape(shp), rpb)
    hb = short_conv_centred(hb, conv_w, conv_b)
    parts = jnp.split(hb, HY_ORDER + 1, axis=-1)
    K = hyena_filters(L, f_w1, f_b1, f_freq, f_w2, f_b2, f_w3)
    z = parts[0]
    for o in range(HY_ORDER):
        z = parts[o + 1] * long_conv_bidirectional(z, K[:, o], skip[o])
    return jnp.concatenate([y_a, z], axis=-1) @ w_out


def rope_tables(L, dim):
    inv = 1.0 / (ROPE_THETA ** (jnp.arange(0, dim, 2, dtype=jnp.float32) / dim))
    ang = jnp.arange(L, dtype=jnp.float32)[:, None] * inv[None, :]
    return jnp.cos(ang), jnp.sin(ang)


def apply_rope(x, cos, sin):
    x1, x2 = jnp.split(x.astype(jnp.float32), 2, axis=-1)
    c = cos[None, :, None, :]
    s = sin[None, :, None, :]
    return jnp.concatenate([x1 * c - x2 * s, x1 * s + x2 * c], axis=-1).astype(x.dtype)


def dense_attention_blocked(q, k, v):
    B, L, H, Dq = q.shape
    nb = L // Q_BLOCK
    qb = q.reshape(B, nb, Q_BLOCK, H, Dq).transpose(1, 0, 2, 3, 4)
    scale = Dq ** -0.5

    def one_block(qi):
        s = jnp.einsum('bqhd,bkhd->bhqk', qi, k).astype(jnp.float32) * scale
        prob = jax.nn.softmax(s, axis=-1).astype(v.dtype)
        return jnp.einsum('bhqk,bkhd->bqhd', prob, v)

    out = lax.map(one_block, qb)
    return out.transpose(1, 0, 2, 3, 4).reshape(B, L, H * v.shape[-1])


def mla_mixer(x, w_in, q_norm_g, w_q_up, kv_norm_g, w_kv_up, w_out):
    B, L, _ = x.shape
    h = x @ w_in
    cq, ckv, k_rope = jnp.split(h, [MLA_Q_LORA, MLA_Q_LORA + MLA_KV_LORA], axis=-1)
    cos, sin = rope_tables(L, MLA_ROPE)
    q = (rms_norm(cq, q_norm_g) @ w_q_up).reshape(B, L, MLA_HEADS, MLA_NOPE + MLA_ROPE)
    q_nope, q_rope = jnp.split(q, [MLA_NOPE], axis=-1)
    q = jnp.concatenate([q_nope, apply_rope(q_rope, cos, sin)], axis=-1)
    kv = (rms_norm(ckv, kv_norm_g) @ w_kv_up).reshape(B, L, MLA_HEADS, MLA_NOPE + MLA_V)
    k_nope, v = jnp.split(kv, [MLA_NOPE], axis=-1)
    k_r = apply_rope(k_rope[:, :, None, :], cos, sin)
    k = jnp.concatenate([k_nope, jnp.broadcast_to(k_r, (B, L, MLA_HEADS, MLA_ROPE))], axis=-1)
    return dense_attention_blocked(q, k, v) @ w_out


def expert_choice_moe(x, w_router, w_gate, w_up, w_down):
    B, L, _ = x.shape
    cap = EC_CAPACITY_FACTOR * L // N_EXPERTS
    aff = jax.nn.softmax((x @ w_router).astype(jnp.float32), axis=-1)
    g, idx = lax.top_k(aff.transpose(0, 2, 1), cap)
    bidx = jnp.arange(B)[:, None, None]
    xe = x[bidx, idx]
    hg = jnp.einsum('becd,edf->becf', xe, w_gate)
    hu = jnp.einsum('becd,edf->becf', xe, w_up)
    ye = jnp.einsum('becf,efd->becd', jax.nn.silu(hg) * hu, w_down)
    ye = ye * g[..., None].astype(ye.dtype)
    return jnp.zeros_like(x).at[bidx, idx].add(ye)


def setup_inputs(seed: int = 0) -> dict:
    keys = iter(jax.random.split(jax.random.key(seed), 64))
    f32 = jnp.float32

    def nrm(shape, scale):
        return jax.random.normal(next(keys), shape, f32) * scale

    D = D_MODEL
    return {
        'x': nrm((BATCH, SEQ, D), 1.0),
        'p': nrm((DEPTH, BATCH, SEQ, PLE_DIM), 1.0),
        'ab_w_in': nrm((N_EVEN, D, AB_IN_WIDTH), D ** -0.5),
        'na_rpb': nrm((N_EVEN, NA_HEADS, 2 * NA_WIN_ROWS_MAX - 1, 2 * NA_WIN_COLS - 1), 0.02),
        'hy_conv_w': nrm((N_EVEN, HY_SHORT_K, (HY_ORDER + 1) * HY_WIDTH), HY_SHORT_K ** -0.5),
        'hy_conv_b': nrm((N_EVEN, (HY_ORDER + 1) * HY_WIDTH), 0.02),
        'hy_f_w1': nrm((N_EVEN, HY_EMB_DIM, HY_FILTER_HIDDEN), HY_EMB_DIM ** -0.5),
        'hy_f_b1': nrm((N_EVEN, HY_FILTER_HIDDEN), 0.02),
        'hy_f_freq': 1.0 + nrm((N_EVEN, HY_FILTER_HIDDEN), 0.02),
        'hy_f_w2': nrm((N_EVEN, HY_FILTER_HIDDEN, HY_FILTER_HIDDEN), HY_FILTER_HIDDEN ** -0.5),
        'hy_f_b2': nrm((N_EVEN, HY_FILTER_HIDDEN), 0.02),
        'hy_f_w3': nrm((N_EVEN, HY_FILTER_HIDDEN, HY_FILTER_OUT), HY_FILTER_HIDDEN ** -0.5),
        'hy_skip': nrm((N_EVEN, HY_ORDER, HY_WIDTH), 1.0),
        'ab_w_out': nrm((N_EVEN, NA_WIDTH + HY_WIDTH, D), DN_BETA * (NA_WIDTH + HY_WIDTH) ** -0.5),
        'mla_w_in': nrm((N_ODD, D, MLA_IN_WIDTH), D ** -0.5),
        'mla_q_norm': 1.0 + nrm((N_ODD, MLA_Q_LORA), 0.02),
        'mla_w_q_up': nrm((N_ODD, MLA_Q_LORA, MLA_HEADS * (MLA_NOPE + MLA_ROPE)), MLA_Q_LORA ** -0.5),
        'mla_kv_norm': 1.0 + nrm((N_ODD, MLA_KV_LORA), 0.02),
        'mla_w_kv_up': nrm((N_ODD, MLA_KV_LORA, MLA_HEADS * (MLA_NOPE + MLA_V)), MLA_KV_LORA ** -0.5),
        'mla_w_out': nrm((N_ODD, MLA_HEADS * MLA_V, D), DN_BETA * (MLA_HEADS * MLA_V) ** -0.5),
        'ln1_g': 1.0 + nrm((DEPTH, D), 0.02),
        'ln1_b': nrm((DEPTH, D), 0.02),
        'ln2_g': 1.0 + nrm((DEPTH, D), 0.02),
        'ln2_b': nrm((DEPTH, D), 0.02),
        'moe_router': nrm((DEPTH, D, N_EXPERTS), D ** -0.5),
        'moe_w_gate': nrm((DEPTH, N_EXPERTS, D, D_FF_EXPERT), D ** -0.5),
        'moe_w_up': nrm((DEPTH, N_EXPERTS, D, D_FF_EXPERT), D ** -0.5),
        'moe_w_down': nrm((DEPTH, N_EXPERTS, D_FF_EXPERT, D), DN_BETA * D_FF_EXPERT ** -0.5),
        'ple_gate': nrm((DEPTH, D, D), D ** -0.5),
        'ple_proj': nrm((DEPTH, PLE_DIM, D), PLE_DIM ** -0.5),
    }


def reference(x, p, ab_w_in, na_rpb, hy_conv_w, hy_conv_b, hy_f_w1, hy_f_b1, hy_f_freq, hy_f_w2, hy_f_b2, hy_f_w3, hy_skip, ab_w_out, mla_w_in, mla_q_norm, mla_w_q_up, mla_kv_norm, mla_w_kv_up, mla_w_out, ln1_g, ln1_b, ln2_g, ln2_b, moe_router, moe_w_gate, moe_w_up, moe_w_down, ple_gate, ple_proj):
    for i in range(DEPTH):
        j = i // 2
        if i % 2 == 0:
            m = na_hyena_mixer(x, ab_w_in[j], na_rpb[j], hy_conv_w[j], hy_conv_b[j], hy_f_w1[j], hy_f_b1[j],
                               hy_f_freq[j], hy_f_w2[j], hy_f_b2[j], hy_f_w3[j], hy_skip[j], ab_w_out[j])
        else:
            m = mla_mixer(x, mla_w_in[j], mla_q_norm[j], mla_w_q_up[j], mla_kv_norm[j], mla_w_kv_up[j], mla_w_out[j])
        x = layer_norm(DN_ALPHA * x + m, ln1_g[i], ln1_b[i])
        f = expert_choice_moe(x, moe_router[i], moe_w_gate[i], moe_w_up[i], moe_w_down[i])
        x = layer_norm(DN_ALPHA * x + f, ln2_g[i], ln2_b[i])
        x = x + jax.nn.sigmoid(x @ ple_gate[i]) * (p[i] @ ple_proj[i])
    return x
```

```python
import functools
import math

import numpy as np
import jax
import jax.numpy as jnp
from jax import lax
from jax.experimental import pallas as pl
from jax.experimental.pallas import tpu as pltpu

D_MODEL = 1024
BATCH = 8
SEQ = 2048
DEPTH = 2
GRID_W = 64
PLE_DIM = 256
NA_HEADS = 8
NA_HEAD_DIM = 64
NA_WIDTH = NA_HEADS * NA_HEAD_DIM
NA_WIN_ROWS_MAX = 8
NA_WIN_COLS = 16
HY_WIDTH = D_MODEL - NA_WIDTH
HY_ORDER = 2
HY_SHORT_K = 3
HY_EMB_DIM = 33
HY_FILTER_HIDDEN = 64
HY_FAST_DECAY_PCT = 0.3
HY_SLOW_DECAY_PCT = 1.5
HY_DECAY_TARGET = 1e-2
AB_IN_WIDTH = 3 * NA_WIDTH + (HY_ORDER + 1) * HY_WIDTH
MLA_HEADS = 16
MLA_Q_LORA = 384
MLA_KV_LORA = 256
MLA_NOPE = 64
MLA_ROPE = 32
MLA_V = 64
ROPE_THETA = 10000.0
N_EXPERTS = 16
EC_CAPACITY_FACTOR = 2
D_FF_EXPERT = 2048
DN_ALPHA = (2 * DEPTH) ** 0.25
NORM_EPS = 1e-5
NEG_INF = -1e30

CAP = EC_CAPACITY_FACTOR * SEQ // N_EXPERTS
NA_ROWS = SEQ // GRID_W
NA_WIN_ROWS = min(NA_WIN_ROWS_MAX, NA_ROWS)
NA_SLAB = NA_WIN_ROWS * GRID_W
FFT_N = 2 * SEQ

LANE = 128
MIB = 1 << 20
V7X_VMEM_SCOPED_MIB = 56

_BF = jnp.bfloat16
_F32 = jnp.float32
_HP = lax.Precision.HIGHEST


def _cp(sem, vmem_mib):
    return pltpu.CompilerParams(dimension_semantics=sem, vmem_limit_bytes=vmem_mib * MIB)


def _dot(a, b):
    return jnp.dot(a.astype(_BF), b.astype(_BF), preferred_element_type=_F32)


def _dot_nt(a, b):
    return lax.dot_general(a.astype(_BF), b.astype(_BF), (((1,), (1,)), ((), ())),
                           preferred_element_type=_F32)


def _dot_tn(a, b):
    return lax.dot_general(a.astype(_BF), b.astype(_BF), (((0,), (0,)), ((), ())),
                           preferred_element_type=_F32)


def _layer_norm(y, g, b):
    mu = jnp.mean(y, axis=-1, keepdims=True)
    d = y - mu
    var = jnp.mean(d * d, axis=-1, keepdims=True)
    return d * lax.rsqrt(var + NORM_EPS) * g + b


def _inproj_kernel(x_ref, w_ref, qkv_ref, hb_ref):
    h = _dot(x_ref[...], w_ref[...])
    qkv_ref[...] = h[:, :3 * NA_WIDTH].astype(_BF)
    hb_ref[...] = h[:, 3 * NA_WIDTH:]


def _inproj(x2d, w):
    m = x2d.shape[0]
    tm = 512
    nq, nh = 3 * NA_WIDTH, (HY_ORDER + 1) * HY_WIDTH
    return pl.pallas_call(
        _inproj_kernel, grid=(m // tm,),
        in_specs=[pl.BlockSpec((tm, D_MODEL), lambda i: (i, 0)),
                  pl.BlockSpec((D_MODEL, AB_IN_WIDTH), lambda i: (0, 0))],
        out_specs=[pl.BlockSpec((tm, nq), lambda i: (i, 0)),
                   pl.BlockSpec((tm, nh), lambda i: (i, 0))],
        out_shape=[jax.ShapeDtypeStruct((m, nq), _BF), jax.ShapeDtypeStruct((m, nh), _F32)],
        compiler_params=_cp(("parallel",), 48), name="ab_inproj")(x2d, w)


NA_GROUP = 4
NA_GW = NA_GROUP * NA_HEAD_DIM


def _na_kernel(q_ref, k_ref, v_ref, tbl_ref, o_ref):
    lane_head = lax.broadcasted_iota(jnp.int32, (GRID_W, NA_GW), 1) >> int(math.log2(NA_HEAD_DIM))
    scale = NA_HEAD_DIM ** -0.5

    def body(r, carry):
        r0 = jnp.clip(r - NA_WIN_ROWS // 2, 0, NA_ROWS - NA_WIN_ROWS)
        off = r0 - r + (NA_WIN_ROWS_MAX - 1)
        qr = q_ref[0, pl.ds(pl.multiple_of(r * GRID_W, GRID_W), GRID_W), :].astype(_F32)
        q4 = jnp.concatenate([jnp.where(lane_head == h, qr, 0.0) for h in range(NA_GROUP)], axis=0)
        ks = k_ref[0, pl.ds(pl.multiple_of(r0 * GRID_W, GRID_W), NA_SLAB), :]
        vs = v_ref[0, pl.ds(pl.multiple_of(r0 * GRID_W, GRID_W), NA_SLAB), :]
        bias = jnp.concatenate([tbl_ref[h, off] for h in range(NA_GROUP)], axis=0)
        s = _dot_nt(q4, ks) * scale + bias
        mx = jnp.max(s, axis=1, keepdims=True)
        p = jnp.exp(s - mx)
        den = jnp.sum(p, axis=1, keepdims=True)
        o4 = _dot(p, vs) / den
        out = jnp.zeros((GRID_W, NA_GW), _F32)
        for h in range(NA_GROUP):
            out = out + jnp.where(lane_head == h, o4[h * GRID_W:(h + 1) * GRID_W], 0.0)
        o_ref[0, pl.ds(pl.multiple_of(r * GRID_W, GRID_W), GRID_W), :] = out.astype(_BF)
        return carry

    lax.fori_loop(0, NA_ROWS, body, 0)


def _na_bias_table(rpb):
    c = np.arange(GRID_W)
    c0 = np.clip(c - NA_WIN_COLS // 2, 0, GRID_W - NA_WIN_COLS)
    kc = np.arange(GRID_W)
    col_ok = (kc[None, :] >= c0[:, None]) & (kc[None, :] < c0[:, None] + NA_WIN_COLS)
    dc_idx = np.clip(kc[None, :] - c[:, None], -(NA_WIN_COLS - 1), NA_WIN_COLS - 1) + (NA_WIN_COLS - 1)
    dr = np.arange(NA_WIN_ROWS_MAX)[:, None] + np.arange(NA_WIN_ROWS)[None, :]
    bias = rpb[:, dr[:, None, :, None], dc_idx[None, :, None, :]].astype(_F32)
    bias = jnp.where(col_ok[None, None, :, None, :], bias, NEG_INF)
    return bias.reshape(NA_HEADS, NA_WIN_ROWS_MAX, GRID_W, NA_SLAB)


def _na_attention(qkv, rpb):
    b = qkv.shape[0]
    tbl = _na_bias_table(rpb)
    ng = NA_HEADS // NA_GROUP
    blk = lambda col0: pl.BlockSpec((1, SEQ, NA_GW), lambda i, g, c=col0: (i, 0, c + g))
    return pl.pallas_call(
        _na_kernel, grid=(b, ng),
        in_specs=[blk(0), blk(ng), blk(2 * ng),
                  pl.BlockSpec((NA_GROUP, NA_WIN_ROWS_MAX, GRID_W, NA_SLAB), lambda i, g: (g, 0, 0, 0))],
        out_specs=pl.BlockSpec((1, SEQ, NA_GW), lambda i, g: (i, 0, g)),
        out_shape=jax.ShapeDtypeStruct((b, SEQ, NA_WIDTH), _BF),
        compiler_params=_cp(("parallel", "parallel"), 32), name="na_attention")(qkv, qkv, qkv, tbl)


def _sconv_kernel(x_ref, w_ref, b_ref, o_ref):
    x = x_ref[0]
    n = x.shape[0]
    row = lax.broadcasted_iota(jnp.int32, x.shape, 0)
    xm = jnp.where(row == 0, 0.0, pltpu.roll(x, 1, axis=0))
    xp = jnp.where(row == n - 1, 0.0, pltpu.roll(x, n - 1, axis=0))
    o_ref[0] = b_ref[...] + xm * w_ref[0:1, :] + x * w_ref[1:2, :] + xp * w_ref[2:3, :]


def _short_conv(hb, w, bias):
    b, n, c = hb.shape
    tc = HY_WIDTH
    return pl.pallas_call(
        _sconv_kernel, grid=(b, c // tc),
        in_specs=[pl.BlockSpec((1, n, tc), lambda i, j: (i, 0, j)),
                  pl.BlockSpec((HY_SHORT_K, tc), lambda i, j: (0, j)),
                  pl.BlockSpec((1, tc), lambda i, j: (0, j))],
        out_specs=pl.BlockSpec((1, n, tc), lambda i, j: (i, 0, j)),
        out_shape=jax.ShapeDtypeStruct((b, n, c), _F32),
        compiler_params=_cp(("parallel", "parallel"), 32), name="hy_short_conv")(hb, w, bias.reshape(1, c))


HY_HID_PAD = LANE
HY_FILT_TC = 256


def _filter_kernel(z_ref, w1_ref, b1_ref, fr_ref, w2_ref, b2_ref, w3f_ref, w3b_ref, dec_ref, k_ref):
    fr = fr_ref[...]
    h = jnp.sin(fr * (jnp.dot(z_ref[...], w1_ref[...], precision=_HP, preferred_element_type=_F32) + b1_ref[...]))
    h = jnp.sin(fr * (jnp.dot(h, w2_ref[...], precision=_HP, preferred_element_type=_F32) + b2_ref[...]))
    fwd = jnp.dot(h[:SEQ], w3f_ref[...], precision=_HP, preferred_element_type=_F32)
    bwd = jnp.dot(h[SEQ:], w3b_ref[...], precision=_HP, preferred_element_type=_F32)
    k = jnp.concatenate([fwd, bwd], axis=0) * dec_ref[...]
    ss = jnp.sum(k * k, axis=0, keepdims=True)
    k_ref[...] = (k * lax.rsqrt(ss + 1e-12)).astype(_BF)


def _filter_tables():
    bands = (HY_EMB_DIM - 1) // 2
    t = jnp.linspace(0.0, 1.0, SEQ, dtype=_F32)[:, None]
    w = 2.0 * math.pi * jnp.arange(SEQ, dtype=_F32)[:, None] / SEQ
    f = jnp.linspace(1e-4, bands - 1, bands, dtype=_F32)[None, :]
    z = jnp.concatenate([t, jnp.cos(f * w), -jnp.sin(f * w)], axis=-1)
    min_decay = math.log(HY_DECAY_TARGET) / HY_SLOW_DECAY_PCT
    max_decay = math.log(HY_DECAY_TARGET) / HY_FAST_DECAY_PCT
    deltas = jnp.abs(jnp.linspace(min_decay, max_decay, HY_WIDTH, dtype=_F32))
    dec = jnp.exp(-t * deltas)
    src = np.concatenate([np.arange(SEQ), [0], np.arange(SEQ - 1, 0, -1)])
    live = np.ones((FFT_N, 1), np.float32)
    live[SEQ] = 0.0
    z2 = jnp.pad(z[src], ((0, 0), (0, HY_HID_PAD - HY_EMB_DIM)))
    dec2 = dec[src] * live
    return z2, dec2


def _hyena_filters(w1, b1, freq, w2, b2, w3):
    z2, dec2 = _filter_tables()
    hp = HY_HID_PAD - HY_FILTER_HIDDEN
    w1p = jnp.pad(w1, ((0, HY_HID_PAD - HY_EMB_DIM), (0, hp)))
    w2p = jnp.pad(w2, ((0, hp), (0, hp)))
    w3p = jnp.pad(w3, ((0, hp), (0, 0)))
    row = lambda v: jnp.pad(v, (0, hp)).reshape(1, HY_HID_PAD)
    nc = HY_WIDTH // HY_FILT_TC
    per_order = 2 * nc
    full = lambda shape: pl.BlockSpec(shape, lambda o, j: (0, 0))
    return pl.pallas_call(
        _filter_kernel, grid=(HY_ORDER, nc),
        in_specs=[full((FFT_N, HY_HID_PAD)), full((HY_HID_PAD, HY_HID_PAD)), full((1, HY_HID_PAD)),
                  full((1, HY_HID_PAD)), full((HY_HID_PAD, HY_HID_PAD)), full((1, HY_HID_PAD)),
                  pl.BlockSpec((HY_HID_PAD, HY_FILT_TC), lambda o, j: (0, o * per_order + j)),
                  pl.BlockSpec((HY_HID_PAD, HY_FILT_TC), lambda o, j: (0, o * per_order + nc + j)),
                  pl.BlockSpec((FFT_N, HY_FILT_TC), lambda o, j: (0, j))],
        out_specs=pl.BlockSpec((FFT_N, HY_FILT_TC), lambda o, j: (0, o * nc + j)),
        out_shape=jax.ShapeDtypeStruct((FFT_N, HY_ORDER * HY_WIDTH), _BF),
        compiler_params=_cp(("parallel", "parallel"), 40), name="hy_filters")(
            z2, w1p, row(b1), row(freq), w2p, row(b2), w3p, w3p, dec2)


def _dft_matrix():
    hi, lo = SEQ // 64, 64
    t = np.arange(SEQ)
    m_hi = (np.arange(hi)[:, None] * lo * t[None, :]) % FFT_N
    m_lo = (np.arange(lo)[:, None] * t[None, :]) % FFT_N
    ang = lambda m: jnp.asarray(m, _F32) * (2.0 * math.pi / FFT_N)
    c1, s1 = jnp.cos(ang(m_hi))[:, None, :], jnp.sin(ang(m_hi))[:, None, :]
    c0, s0 = jnp.cos(ang(m_lo))[None, :, :], jnp.sin(ang(m_lo))[None, :, :]
    re = (c1 * c0 - s1 * s0).reshape(SEQ, SEQ)
    im = -(s1 * c0 + c1 * s0).reshape(SEQ, SEQ)
    nyq = jnp.asarray(1.0 - 2.0 * (t % 2), _F32)
    im = im.at[0].set(nyq)
    return jnp.concatenate([re, im], axis=0).astype(_BF)


def _kf_kernel(f_ref, k_ref, o_ref):
    i = pl.program_id(0)
    f = f_ref[...]
    tm = f.shape[0]
    p1 = jnp.dot(f, k_ref[:SEQ, :], preferred_element_type=_F32)
    p2 = jnp.dot(f, k_ref[SEQ:, :], preferred_element_type=_F32)
    row = lax.broadcasted_iota(jnp.int32, (tm, 1), 0) + i * tm
    sign = (1 - 2 * (row & 1)).astype(_F32)
    o_ref[...] = p1 + sign * p2


def _filter_spectrum(fmat, kfilt):
    tm = 512
    nw = kfilt.shape[1]
    return pl.pallas_call(
        _kf_kernel, grid=(FFT_N // tm,),
        in_specs=[pl.BlockSpec((tm, SEQ), lambda i: (i, 0)),
                  pl.BlockSpec((FFT_N, nw), lambda i: (0, 0))],
        out_specs=pl.BlockSpec((tm, nw), lambda i: (i, 0)),
        out_shape=jax.ShapeDtypeStruct((FFT_N, nw), _F32),
        compiler_params=_cp(("parallel",), 40), name="hy_filter_spectrum")(fmat, kfilt)


HY_FB = 256


def _hconv_kernel(z_ref, fr_ref, fi_ref, ftr_ref, fti_ref, kr_ref, ki_ref, skip_ref, xn_ref, o_ref):
    fk = pl.program_id(1)
    z = z_ref[0]
    zb = z.astype(_BF)
    ur = jnp.dot(fr_ref[...], zb, preferred_element_type=_F32)
    ui = jnp.dot(fi_ref[...], zb, preferred_element_type=_F32)
    kr = kr_ref[...]
    ki = ki_ref[...]
    row0 = jnp.logical_and(lax.broadcasted_iota(jnp.int32, ur.shape, 0) == 0, fk == 0)
    yr = jnp.where(row0, ur * kr, ur * kr - ui * ki)
    yi = jnp.where(row0, ui * ki, ur * ki + ui * kr)
    sc = jnp.where(row0, 1.0 / FFT_N, 2.0 / FFT_N)
    part = (jnp.dot(ftr_ref[...], (yr * sc).astype(_BF), preferred_element_type=_F32)
            + jnp.dot(fti_ref[...], (yi * sc).astype(_BF), preferred_element_type=_F32))

    @pl.when(fk == 0)
    def _():
        o_ref[0] = part

    @pl.when(fk > 0)
    def _():
        o_ref[0] += part

    @pl.when(fk == pl.num_programs(1) - 1)
    def _():
        o_ref[0] = xn_ref[0] * (o_ref[0] + z * skip_ref[...])


def _long_conv_gate(hbc, zsrc, z_col, xn_col, order, fmat, fmat_t, kf, skip):
    b = hbc.shape[0]
    nf = SEQ // HY_FB
    w = HY_WIDTH
    return pl.pallas_call(
        _hconv_kernel, grid=(b, nf),
        in_specs=[pl.BlockSpec((1, SEQ, w), lambda i, f: (i, 0, z_col)),
                  pl.BlockSpec((HY_FB, SEQ), lambda i, f: (f, 0)),
                  pl.BlockSpec((HY_FB, SEQ), lambda i, f: (nf + f, 0)),
                  pl.BlockSpec((SEQ, HY_FB), lambda i, f: (0, f)),
                  pl.BlockSpec((SEQ, HY_FB), lambda i, f: (0, nf + f)),
                  pl.BlockSpec((HY_FB, w), lambda i, f: (f, order)),
                  pl.BlockSpec((HY_FB, w), lambda i, f: (nf + f, order)),
                  pl.BlockSpec((1, w), lambda i, f: (0, 0)),
                  pl.BlockSpec((1, SEQ, w), lambda i, f: (i, 0, xn_col))],
        out_specs=pl.BlockSpec((1, SEQ, w), lambda i, f: (i, 0, 0)),
        out_shape=jax.ShapeDtypeStruct((b, SEQ, w), _F32),
        compiler_params=_cp(("parallel", "arbitrary"), 48), name=f"hy_long_conv{order}")(
            zsrc, fmat, fmat, fmat_t, fmat_t, kf, kf, skip[order].reshape(1, w), hbc)


def _outproj_ln_kernel(n_a, *refs):
    a_refs = refs[:n_a]
    w_refs = refs[n_a:2 * n_a]
    x_ref, g_ref, b_ref, o_ref, ob_ref = refs[2 * n_a:]
    m = _dot(a_refs[0][...], w_refs[0][...])
    for a_ref, w_ref in zip(a_refs[1:], w_refs[1:]):
        m = m + _dot(a_ref[...], w_ref[...])
    y = _layer_norm(DN_ALPHA * x_ref[...] + m, g_ref[...], b_ref[...])
    o_ref[...] = y
    ob_ref[...] = y.astype(_BF)


def _outproj_ln(a_list, w, x2d, g, b):
    m = x2d.shape[0]
    tm = 512
    in_specs, w_args, row0 = [], [], 0
    for a in a_list:
        in_specs.append(pl.BlockSpec((tm, a.shape[1]), lambda i: (i, 0)))
    for a in a_list:
        ka = a.shape[1]
        in_specs.append(pl.BlockSpec((ka, D_MODEL), lambda i, r=row0 // ka: (r, 0)))
        w_args.append(w)
        row0 += ka
    in_specs += [pl.BlockSpec((tm, D_MODEL), lambda i: (i, 0)),
                 pl.BlockSpec((1, D_MODEL), lambda i: (0, 0)),
                 pl.BlockSpec((1, D_MODEL), lambda i: (0, 0))]
    return pl.pallas_call(
        functools.partial(_outproj_ln_kernel, len(a_list)), grid=(m // tm,),
        in_specs=in_specs,
        out_specs=[pl.BlockSpec((tm, D_MODEL), lambda i: (i, 0))] * 2,
        out_shape=[jax.ShapeDtypeStruct((m, D_MODEL), _F32), jax.ShapeDtypeStruct((m, D_MODEL), _BF)],
        compiler_params=_cp(("parallel",), 40), name="outproj_ln")(
            *a_list, *w_args, x2d, g.reshape(1, -1), b.reshape(1, -1))


def _router_kernel(x_ref, wt_ref, a_ref):
    logits = lax.dot_general(wt_ref[...], x_ref[0], (((1,), (1,)), ((), ())),
                             precision=_HP, preferred_element_type=_F32)
    mx = jnp.max(logits, axis=0, keepdims=True)
    ex = jnp.exp(logits - mx)
    a_ref[0] = ex / jnp.sum(ex, axis=0, keepdims=True)


def _router(x3d, w_router):
    b = x3d.shape[0]
    return pl.pallas_call(
        _router_kernel, grid=(b,),
        in_specs=[pl.BlockSpec((1, SEQ, D_MODEL), lambda i: (i, 0, 0)),
                  pl.BlockSpec((N_EXPERTS, D_MODEL), lambda i: (0, 0))],
        out_specs=pl.BlockSpec((1, N_EXPERTS, SEQ), lambda i: (i, 0, 0)),
        out_shape=jax.ShapeDtypeStruct((b, N_EXPERTS, SEQ), _F32),
        compiler_params=_cp(("parallel",), 32), name="moe_router")(x3d, w_router.T)


BISECT_STEPS = 160


def _select_kernel(a_ref, tri_ref, slot_ref):
    a = a_ref[...]
    rows = a.shape[0]
    cap = float(CAP)

    def body(_, c):
        lo, hi = c
        mid = 0.5 * (lo + hi)
        cnt = jnp.sum(jnp.where(a > mid, 1.0, 0.0), axis=1, keepdims=True)
        ge = cnt >= cap
        return jnp.where(ge, mid, lo), jnp.where(ge, hi, mid)

    lo, hi = lax.fori_loop(0, BISECT_STEPS, body,
                           (jnp.full((rows, 1), -1.0, _F32), jnp.full((rows, 1), 1.0, _F32)))
    vstar = jnp.max(jnp.where(a <= hi, a, -1.0), axis=1, keepdims=True)
    gt = a > vstar
    eq = a == vstar
    need = cap - jnp.sum(jnp.where(gt, 1.0, 0.0), axis=1, keepdims=True)
    tri = tri_ref[...]
    eq_before = jnp.dot(jnp.where(eq, 1.0, 0.0).astype(_BF), tri, preferred_element_type=_F32)
    sel = jnp.where(gt, 1.0, jnp.where(eq, jnp.where(eq_before < need, 1.0, 0.0), 0.0))
    pos = jnp.dot(sel.astype(_BF), tri, preferred_element_type=_F32)
    slot_ref[...] = jnp.where(sel > 0.5, pos, -1.0).astype(jnp.int32)


def _select(aff2d):
    rows = aff2d.shape[0]
    idx = jnp.arange(SEQ, dtype=jnp.int32)
    tri = (idx[:, None] < idx[None, :]).astype(_BF)
    return pl.pallas_call(
        _select_kernel, grid=(1,),
        in_specs=[pl.BlockSpec((rows, SEQ), lambda i: (0, 0)),
                  pl.BlockSpec((SEQ, SEQ), lambda i: (0, 0))],
        out_specs=pl.BlockSpec((rows, SEQ), lambda i: (0, 0)),
        out_shape=jax.ShapeDtypeStruct((rows, SEQ), jnp.int32),
        compiler_params=_cp(("arbitrary",), 40), name="moe_select")(aff2d, tri)


def _onehot(slot_row):
    return slot_row == lax.broadcasted_iota(jnp.int32, (CAP, SEQ), 0)


def _gather_kernel(slot_ref, a_ref, xb_ref, xe_ref, g_ref):
    hit = _onehot(slot_ref[0, 0])
    p = jnp.where(hit, 1.0, 0.0).astype(_BF)
    xe_ref[0] = jnp.dot(p, xb_ref[0], preferred_element_type=_F32).astype(_BF)
    g_ref[0] = jnp.sum(jnp.where(hit, a_ref[0, 0], 0.0), axis=1, keepdims=True)


def _gather(slot4, aff4, xb3d):
    b = xb3d.shape[0]
    row = pl.BlockSpec((1, 1, 1, SEQ), lambda i, e: (i, e, 0, 0))
    return pl.pallas_call(
        _gather_kernel, grid=(b, N_EXPERTS),
        in_specs=[row, row, pl.BlockSpec((1, SEQ, D_MODEL), lambda i, e: (i, 0, 0))],
        out_specs=[pl.BlockSpec((1, CAP, D_MODEL), lambda i, e: (e, i, 0)),
                   pl.BlockSpec((1, CAP, 1), lambda i, e: (e, i, 0))],
        out_shape=[jax.ShapeDtypeStruct((N_EXPERTS, b * CAP, D_MODEL), _BF),
                   jax.ShapeDtypeStruct((N_EXPERTS, b * CAP, 1), _F32)],
        compiler_params=_cp(("parallel", "parallel"), 32), name="moe_gather")(slot4, aff4, xb3d)


FFN_TF = 512
FFN_TM = 512


def _ffn_kernel(xe_ref, wg_ref, wu_ref, wd_ref, g_ref, ye_ref, acc_ref):
    f = pl.program_id(1)
    wg = wg_ref[0].astype(_BF)
    wu = wu_ref[0].astype(_BF)
    wd = wd_ref[0].astype(_BF)
    for c in range(xe_ref.shape[1] // FFN_TM):
        rows = slice(c * FFN_TM, (c + 1) * FFN_TM)
        xe = xe_ref[0, rows, :]
        hg = _dot(xe, wg)
        hu = _dot(xe, wu)
        h = (hg * (1.0 / (1.0 + jnp.exp(-hg)))) * hu
        part = _dot(h, wd)

        @pl.when(f == 0)
        def _():
            acc_ref[rows, :] = part

        @pl.when(f > 0)
        def _():
            acc_ref[rows, :] += part

    @pl.when(f == pl.num_programs(1) - 1)
    def _():
        ye_ref[0] = (acc_ref[...] * g_ref[0]).astype(_BF)


def _expert_ffn(xe, gates, wg, wu, wd):
    rows = xe.shape[1]
    return pl.pallas_call(
        _ffn_kernel, grid=(N_EXPERTS, D_FF_EXPERT // FFN_TF),
        in_specs=[pl.BlockSpec((1, rows, D_MODEL), lambda e, f: (e, 0, 0)),
                  pl.BlockSpec((1, D_MODEL, FFN_TF), lambda e, f: (e, 0, f)),
                  pl.BlockSpec((1, D_MODEL, FFN_TF), lambda e, f: (e, 0, f)),
                  pl.BlockSpec((1, FFN_TF, D_MODEL), lambda e, f: (e, f, 0)),
                  pl.BlockSpec((1, rows, 1), lambda e, f: (e, 0, 0))],
        out_specs=pl.BlockSpec((1, rows, D_MODEL), lambda e, f: (e, 0, 0)),
        out_shape=jax.ShapeDtypeStruct((N_EXPERTS, rows, D_MODEL), _BF),
        scratch_shapes=[pltpu.VMEM((rows, D_MODEL), _F32)],
        compiler_params=_cp(("parallel", "arbitrary"), V7X_VMEM_SCOPED_MIB), name="moe_ffn")(
            xe, wg, wu, wd, gates)


SCAT_TC = 256
SCAT_TM = 512


def _scatter_ln_kernel(slot_ref, ye_ref, x_ref, g_ref, b_ref, o_ref, ob_ref):
    e = pl.program_id(1)
    p = jnp.where(_onehot(slot_ref[0, 0]), 1.0, 0.0).astype(_BF)
    for c in range(D_MODEL // SCAT_TC):
        cols = slice(c * SCAT_TC, (c + 1) * SCAT_TC)
        part = _dot_tn(p, ye_ref[0, :, cols])

        @pl.when(e == 0)
        def _():
            o_ref[0, :, cols] = part

        @pl.when(e > 0)
        def _():
            o_ref[0, :, cols] += part

    @pl.when(e == pl.num_programs(1) - 1)
    def _():
        for r in range(SEQ // SCAT_TM):
            rows = slice(r * SCAT_TM, (r + 1) * SCAT_TM)
            y = _layer_norm(DN_ALPHA * x_ref[0, rows, :] + o_ref[0, rows, :], g_ref[...], b_ref[...])
            o_ref[0, rows, :] = y
            ob_ref[0, rows, :] = y.astype(_BF)


def _scatter_ln(slot4, ye, x3d, g, b):
    bsz = x3d.shape[0]
    vec = pl.BlockSpec((1, D_MODEL), lambda i, e: (0, 0))
    seq = pl.BlockSpec((1, SEQ, D_MODEL), lambda i, e: (i, 0, 0))
    return pl.pallas_call(
        _scatter_ln_kernel, grid=(bsz, N_EXPERTS),
        in_specs=[pl.BlockSpec((1, 1, 1, SEQ), lambda i, e: (i, e, 0, 0)),
                  pl.BlockSpec((1, CAP, D_MODEL), lambda i, e: (e, i, 0)),
                  seq, vec, vec],
        out_specs=[seq, seq],
        out_shape=[jax.ShapeDtypeStruct((bsz, SEQ, D_MODEL), _F32),
                   jax.ShapeDtypeStruct((bsz, SEQ, D_MODEL), _BF)],
        compiler_params=_cp(("parallel", "arbitrary"), V7X_VMEM_SCOPED_MIB), name="moe_scatter_ln")(
            slot4, ye, x3d, g.reshape(1, -1), b.reshape(1, -1))


def _moe_block(x1, x1b, w_router, wg, wu, wd, g, b):
    bsz = x1.shape[0]
    aff = _router(x1, w_router)
    slot = _select(aff.reshape(bsz * N_EXPERTS, SEQ))
    slot4 = slot.reshape(bsz, N_EXPERTS, 1, SEQ)
    aff4 = aff.reshape(bsz, N_EXPERTS, 1, SEQ)
    xe, gates = _gather(slot4, aff4, x1b)
    ye = _expert_ffn(xe, gates, wg, wu, wd)
    return _scatter_ln(slot4, ye, x1, g, b)


def _ple_kernel(x_ref, xb_ref, p_ref, wg_ref, wp_ref, o_ref):
    gate = _dot(xb_ref[...], wg_ref[...])
    emb = _dot(p_ref[...], wp_ref[...])
    o_ref[...] = x_ref[...] + (1.0 / (1.0 + jnp.exp(-gate))) * emb


def _ple(x2d, xb2d, p2d, wg, wp):
    m = x2d.shape[0]
    tm = 512
    return pl.pallas_call(
        _ple_kernel, grid=(m // tm,),
        in_specs=[pl.BlockSpec((tm, D_MODEL), lambda i: (i, 0)),
                  pl.BlockSpec((tm, D_MODEL), lambda i: (i, 0)),
                  pl.BlockSpec((tm, PLE_DIM), lambda i: (i, 0)),
                  pl.BlockSpec((D_MODEL, D_MODEL), lambda i: (0, 0)),
                  pl.BlockSpec((PLE_DIM, D_MODEL), lambda i: (0, 0))],
        out_specs=pl.BlockSpec((tm, D_MODEL), lambda i: (i, 0)),
        out_shape=jax.ShapeDtypeStruct((m, D_MODEL), _F32),
        compiler_params=_cp(("parallel",), 32), name="ple")(x2d, xb2d, p2d, wg, wp)


MLA_HB = LANE
MLA_IN_PAD = 768
MLA_KR_COL = MLA_Q_LORA + MLA_KV_LORA


def _rms(x, g):
    return x * lax.rsqrt(jnp.mean(x * x, axis=-1, keepdims=True) + NORM_EPS) * g


def _mla_pre_kernel(x_ref, win_ref, qg_ref, wq_ref, kvg_ref, wkv_ref, qa_ref, qb_ref, ka_ref, kb_ref,
                    q_ref, kv_ref, kr_ref):
    h = _dot(x_ref[...], win_ref[...])
    q = _dot(_rms(h[:, :MLA_Q_LORA], qg_ref[...]), wq_ref[...])
    swap = pltpu.roll(q, q.shape[1] - MLA_ROPE, axis=1)
    qa = jnp.concatenate([qa_ref[...]] * MLA_HEADS, axis=1)
    qb = jnp.concatenate([qb_ref[...]] * MLA_HEADS, axis=1)
    q_ref[...] = (q * qa + swap * qb).astype(_BF)
    kv_ref[...] = _dot(_rms(h[:, MLA_Q_LORA:MLA_KR_COL], kvg_ref[...]), wkv_ref[...]).astype(_BF)
    kr = h[:, MLA_KR_COL:MLA_KR_COL + MLA_HB]
    kr_ref[...] = (kr * ka_ref[...] + pltpu.roll(kr, MLA_HB - MLA_ROPE, axis=1) * kb_ref[...]).astype(_BF)


def _rope_lane_tables():
    inv = 1.0 / (ROPE_THETA ** (jnp.arange(0, MLA_ROPE, 2, dtype=_F32) / MLA_ROPE))
    ang = jnp.arange(SEQ, dtype=_F32)[:, None] * inv[None, :]
    cos, sin = jnp.cos(ang), jnp.sin(ang)
    ones = jnp.ones((SEQ, MLA_NOPE), _F32)
    zeros_n = jnp.zeros((SEQ, MLA_NOPE), _F32)
    zeros_r = jnp.zeros((SEQ, MLA_ROPE), _F32)
    cc = jnp.concatenate([cos, cos], axis=1)
    ss = jnp.concatenate([-sin, sin], axis=1)
    keep_q = jnp.concatenate([ones, cc, zeros_r], axis=1)
    keep_k = jnp.concatenate([zeros_n, cc, zeros_r], axis=1)
    swp = jnp.concatenate([zeros_n, ss, zeros_r], axis=1)
    return keep_q, keep_k, swp


def _mla_pre(x2d, w_in, q_norm, w_q_up, kv_norm, w_kv_up):
    m = x2d.shape[0]
    tm = 512
    half = MLA_ROPE // 2
    kr_w = w_in[:, MLA_KR_COL:]
    kr_swapped = jnp.concatenate([kr_w[:, half:], kr_w[:, :half]], axis=1)
    win = jnp.concatenate([w_in[:, :MLA_KR_COL], jnp.zeros((D_MODEL, MLA_NOPE), _F32), kr_w, kr_swapped], axis=1)
    wq = w_q_up.reshape(MLA_Q_LORA, MLA_HEADS, MLA_NOPE + MLA_ROPE)
    rope_w = wq[:, :, MLA_NOPE:]
    wq = jnp.concatenate([wq, rope_w[:, :, half:], rope_w[:, :, :half]], axis=2).reshape(MLA_Q_LORA, MLA_HEADS * MLA_HB)
    keep_q, keep_k, swp = _rope_lane_tables()
    qscale = (MLA_NOPE + MLA_ROPE) ** -0.5
    nq = MLA_HEADS * MLA_HB
    pos = lambda: pl.BlockSpec((tm, MLA_HB), lambda i: (i % (SEQ // tm), 0))
    full = lambda a: pl.BlockSpec(a.shape, lambda i: (0,) * a.ndim)
    qg, kvg = q_norm.reshape(1, -1), kv_norm.reshape(1, -1)
    return pl.pallas_call(
        _mla_pre_kernel, grid=(m // tm,),
        in_specs=[pl.BlockSpec((tm, D_MODEL), lambda i: (i, 0)), full(win), full(qg), full(wq), full(kvg),
                  full(w_kv_up), pos(), pos(), pos(), pos()],
        out_specs=[pl.BlockSpec((tm, nq), lambda i: (i, 0)),
                   pl.BlockSpec((tm, nq), lambda i: (i, 0)),
                   pl.BlockSpec((tm, MLA_HB), lambda i: (i, 0))],
        out_shape=[jax.ShapeDtypeStruct((m, nq), _BF), jax.ShapeDtypeStruct((m, nq), _BF),
                   jax.ShapeDtypeStruct((m, MLA_HB), _BF)],
        compiler_params=_cp(("parallel",), 48), name="mla_pre")(
            x2d, win, qg, wq, kvg, w_kv_up, keep_q * qscale, swp * qscale, keep_k, swp)


MLA_TQ = 256
MLA_PAIR = 2


def _mla_attn_kernel(q_ref, kv_ref, kr_ref, o_ref, k_scr):
    lane = lax.broadcasted_iota(jnp.int32, (SEQ, MLA_HB), 1)
    kr = kr_ref[0].astype(_F32)
    for j in range(MLA_PAIR):
        kvh = kv_ref[0, :, j * MLA_HB:(j + 1) * MLA_HB].astype(_F32)
        k_scr[j] = jnp.where(lane < MLA_NOPE, kvh, kr).astype(_BF)
    lane_q = lax.broadcasted_iota(jnp.int32, (MLA_TQ, MLA_HB), 1)

    def body(i, carry):
        rows = pl.ds(pl.multiple_of(i * MLA_TQ, MLA_TQ), MLA_TQ)
        outs = []
        for j in range(MLA_PAIR):
            qb = q_ref[0, rows, j * MLA_HB:(j + 1) * MLA_HB]
            s = _dot_nt(qb, k_scr[j])
            mx = jnp.max(s, axis=1, keepdims=True)
            p = jnp.exp(s - mx)
            den = jnp.sum(p, axis=1, keepdims=True)
            outs.append(_dot(p, kv_ref[0, :, j * MLA_HB:(j + 1) * MLA_HB]) / den)
        even = pltpu.roll(outs[0], MLA_V, axis=1)
        o_ref[0, rows, :] = jnp.where(lane_q < MLA_V, even, outs[1]).astype(_BF)
        return carry

    lax.fori_loop(0, SEQ // MLA_TQ, body, 0)


def _mla_attention(q, kv, kr):
    b = q.shape[0]
    w = MLA_PAIR * MLA_HB
    return pl.pallas_call(
        _mla_attn_kernel, grid=(b, MLA_HEADS // MLA_PAIR),
        in_specs=[pl.BlockSpec((1, SEQ, w), lambda i, h: (i, 0, h)),
                  pl.BlockSpec((1, SEQ, w), lambda i, h: (i, 0, h)),
                  pl.BlockSpec((1, SEQ, MLA_HB), lambda i, h: (i, 0, 0))],
        out_specs=pl.BlockSpec((1, SEQ, MLA_PAIR * MLA_V), lambda i, h: (i, 0, h)),
        out_shape=jax.ShapeDtypeStruct((b, SEQ, MLA_HEADS * MLA_V), _BF),
        scratch_shapes=[pltpu.VMEM((MLA_PAIR, SEQ, MLA_HB), _BF)],
        compiler_params=_cp(("parallel", "parallel"), 40), name="mla_attention")(q, kv, kr)


def _na_hyena_mixer(x, w_in, rpb, conv_w, conv_b, f_w1, f_b1, f_freq, f_w2, f_b2, f_w3, skip):
    b = x.shape[0]
    qkv, hb = _inproj(x.reshape(b * SEQ, D_MODEL), w_in)
    y_a = _na_attention(qkv.reshape(b, SEQ, 3 * NA_WIDTH), rpb)
    hbc = _short_conv(hb.reshape(b, SEQ, (HY_ORDER + 1) * HY_WIDTH), conv_w, conv_b)
    fmat = _dft_matrix()
    kfilt = _hyena_filters(f_w1, f_b1, f_freq, f_w2, f_b2, f_w3)
    kf = _filter_spectrum(fmat, kfilt)
    fmat_t = fmat.T
    z = _long_conv_gate(hbc, hbc, 0, 1, 0, fmat, fmat_t, kf, skip)
    for o in range(1, HY_ORDER):
        z = _long_conv_gate(hbc, z, 0, o + 1, o, fmat, fmat_t, kf, skip)
    return y_a.reshape(b * SEQ, NA_WIDTH), z.reshape(b * SEQ, HY_WIDTH)


def kernel(x, p, ab_w_in, na_rpb, hy_conv_w, hy_conv_b, hy_f_w1, hy_f_b1, hy_f_freq, hy_f_w2, hy_f_b2, hy_f_w3, hy_skip, ab_w_out, mla_w_in, mla_q_norm, mla_w_q_up, mla_kv_norm, mla_w_kv_up, mla_w_out, ln1_g, ln1_b, ln2_g, ln2_b, moe_router, moe_w_gate, moe_w_up, moe_w_down, ple_gate, ple_proj):
    b = x.shape[0]
    m = b * SEQ
    for i in range(DEPTH):
        j = i // 2
        x2d = x.reshape(m, D_MODEL)
        if i % 2 == 0:
            y_a, z = _na_hyena_mixer(x, ab_w_in[j], na_rpb[j], hy_conv_w[j], hy_conv_b[j], hy_f_w1[j],
                                     hy_f_b1[j], hy_f_freq[j], hy_f_w2[j], hy_f_b2[j], hy_f_w3[j], hy_skip[j])
            x1, x1b = _outproj_ln([y_a, z], ab_w_out[j], x2d, ln1_g[i], ln1_b[i])
        else:
            q, kv, kr = _mla_pre(x2d, mla_w_in[j], mla_q_norm[j], mla_w_q_up[j], mla_kv_norm[j], mla_w_kv_up[j])
            att = _mla_attention(q.reshape(b, SEQ, -1), kv.reshape(b, SEQ, -1), kr.reshape(b, SEQ, -1))
            x1, x1b = _outproj_ln([att.reshape(m, MLA_HEADS * MLA_V)], mla_w_out[j], x2d, ln1_g[i], ln1_b[i])
        x2, x2b = _moe_block(x1.reshape(b, SEQ, D_MODEL), x1b.reshape(b, SEQ, D_MODEL), moe_router[i],
                             moe_w_gate[i], moe_w_up[i], moe_w_down[i], ln2_g[i], ln2_b[i])
        x = _ple(x2.reshape(m, D_MODEL), x2b.reshape(m, D_MODEL), p[i].reshape(m, PLE_DIM),
                 ple_gate[i], ple_proj[i]).reshape(b, SEQ, D_MODEL)
    return x
```

```python
import functools
import math

import numpy as np
import jax
import jax.numpy as jnp
from jax import lax
from jax.experimental import pallas as pl
from jax.experimental.pallas import tpu as pltpu

D_MODEL = 1024
BATCH = 8
SEQ = 2048
DEPTH = 2
GRID_W = 64
PLE_DIM = 256
NA_HEADS = 8
NA_HEAD_DIM = 64
NA_WIDTH = NA_HEADS * NA_HEAD_DIM
NA_WIN_ROWS_MAX = 8
NA_WIN_COLS = 16
HY_WIDTH = D_MODEL - NA_WIDTH
HY_ORDER = 2
HY_SHORT_K = 3
HY_EMB_DIM = 33
HY_FILTER_HIDDEN = 64
HY_FAST_DECAY_PCT = 0.3
HY_SLOW_DECAY_PCT = 1.5
HY_DECAY_TARGET = 1e-2
AB_IN_WIDTH = 3 * NA_WIDTH + (HY_ORDER + 1) * HY_WIDTH
MLA_HEADS = 16
MLA_Q_LORA = 384
MLA_KV_LORA = 256
MLA_NOPE = 64
MLA_ROPE = 32
MLA_V = 64
ROPE_THETA = 10000.0
N_EXPERTS = 16
EC_CAPACITY_FACTOR = 2
D_FF_EXPERT = 2048
DN_ALPHA = (2 * DEPTH) ** 0.25
NORM_EPS = 1e-5
NEG_INF = -1e30

CAP = EC_CAPACITY_FACTOR * SEQ // N_EXPERTS
NA_ROWS = SEQ // GRID_W
NA_WIN_ROWS = min(NA_WIN_ROWS_MAX, NA_ROWS)
NA_SLAB = NA_WIN_ROWS * GRID_W
FFT_N = 2 * SEQ

LANE = 128
MIB = 1 << 20
V7X_VMEM_SCOPED_MIB = 56

_BF = jnp.bfloat16
_F32 = jnp.float32
_HP = lax.Precision.HIGHEST


def _cp(sem, vmem_mib):
    return pltpu.CompilerParams(dimension_semantics=sem, vmem_limit_bytes=vmem_mib * MIB)


def _dot(a, b):
    return jnp.dot(a.astype(_BF), b.astype(_BF), preferred_element_type=_F32)


def _dot_nt(a, b):
    return lax.dot_general(a.astype(_BF), b.astype(_BF), (((1,), (1,)), ((), ())),
                           preferred_element_type=_F32)


def _layer_norm(y, g, b):
    mu = jnp.mean(y, axis=-1, keepdims=True)
    d = y - mu
    var = jnp.mean(d * d, axis=-1, keepdims=True)
    return d * lax.rsqrt(var + NORM_EPS) * g + b


def _inproj_kernel(x_ref, w_ref, qkv_ref, hb_ref):
    h = _dot(x_ref[...], w_ref[...])
    qkv_ref[...] = h[:, :3 * NA_WIDTH].astype(_BF)
    hb_ref[...] = h[:, 3 * NA_WIDTH:]


def _inproj(x2d, w):
    m = x2d.shape[0]
    tm = 512
    nq, nh = 3 * NA_WIDTH, (HY_ORDER + 1) * HY_WIDTH
    return pl.pallas_call(
        _inproj_kernel, grid=(m // tm,),
        in_specs=[pl.BlockSpec((tm, D_MODEL), lambda i: (i, 0)),
                  pl.BlockSpec((D_MODEL, AB_IN_WIDTH), lambda i: (0, 0))],
        out_specs=[pl.BlockSpec((tm, nq), lambda i: (i, 0)),
                   pl.BlockSpec((tm, nh), lambda i: (i, 0))],
        out_shape=[jax.ShapeDtypeStruct((m, nq), _BF), jax.ShapeDtypeStruct((m, nh), _F32)],
        compiler_params=_cp(("parallel",), 48), name="ab_inproj")(x2d, w)


NA_GROUP = 4
NA_GW = NA_GROUP * NA_HEAD_DIM


def _na_kernel(q_ref, k_ref, v_ref, tbl_ref, o_ref):
    lane_head = lax.broadcasted_iota(jnp.int32, (GRID_W, NA_GW), 1) >> int(math.log2(NA_HEAD_DIM))
    scale = NA_HEAD_DIM ** -0.5

    def body(r, carry):
        r0 = jnp.clip(r - NA_WIN_ROWS // 2, 0, NA_ROWS - NA_WIN_ROWS)
        off = r0 - r + (NA_WIN_ROWS_MAX - 1)
        qr = q_ref[0, pl.ds(pl.multiple_of(r * GRID_W, GRID_W), GRID_W), :].astype(_F32)
        q4 = jnp.concatenate([jnp.where(lane_head == h, qr, 0.0) for h in range(NA_GROUP)], axis=0)
        ks = k_ref[0, pl.ds(pl.multiple_of(r0 * GRID_W, GRID_W), NA_SLAB), :]
        vs = v_ref[0, pl.ds(pl.multiple_of(r0 * GRID_W, GRID_W), NA_SLAB), :]
        bias = jnp.concatenate([tbl_ref[h, off] for h in range(NA_GROUP)], axis=0)
        s = _dot_nt(q4, ks) * scale + bias
        mx = jnp.max(s, axis=1, keepdims=True)
        p = jnp.exp(s - mx)
        den = jnp.sum(p, axis=1, keepdims=True)
        o4 = _dot(p, vs) / den
        out = jnp.zeros((GRID_W, NA_GW), _F32)
        for h in range(NA_GROUP):
            out = out + jnp.where(lane_head == h, o4[h * GRID_W:(h + 1) * GRID_W], 0.0)
        o_ref[0, pl.ds(pl.multiple_of(r * GRID_W, GRID_W), GRID_W), :] = out.astype(_BF)
        return carry

    lax.fori_loop(0, NA_ROWS, body, 0, unroll=2)


def _na_bias_table(rpb):
    c = np.arange(GRID_W)
    c0 = np.clip(c - NA_WIN_COLS // 2, 0, GRID_W - NA_WIN_COLS)
    kc = np.arange(GRID_W)
    col_ok = (kc[None, :] >= c0[:, None]) & (kc[None, :] < c0[:, None] + NA_WIN_COLS)
    dc_idx = np.clip(kc[None, :] - c[:, None], -(NA_WIN_COLS - 1), NA_WIN_COLS - 1) + (NA_WIN_COLS - 1)
    pick = (dc_idx[None, :, :] == np.arange(2 * NA_WIN_COLS - 1)[:, None, None]).astype(np.float32)
    per_dr = jnp.einsum("hdk,kqc->hdqc", rpb.astype(_F32), pick, precision=_HP)
    per_dr = jnp.where(col_ok[None, None], per_dr, NEG_INF)
    slabs = jnp.stack([per_dr[:, off:off + NA_WIN_ROWS] for off in range(NA_WIN_ROWS_MAX)], axis=1)
    return slabs.transpose(0, 1, 3, 2, 4).reshape(NA_HEADS, NA_WIN_ROWS_MAX, GRID_W, NA_SLAB)


def _na_attention(qkv, rpb):
    b = qkv.shape[0]
    tbl = _na_bias_table(rpb)
    ng = NA_HEADS // NA_GROUP
    blk = lambda col0: pl.BlockSpec((1, SEQ, NA_GW), lambda i, g, c=col0: (i, 0, c + g))
    return pl.pallas_call(
        _na_kernel, grid=(b, ng),
        in_specs=[blk(0), blk(ng), blk(2 * ng),
                  pl.BlockSpec((NA_GROUP, NA_WIN_ROWS_MAX, GRID_W, NA_SLAB), lambda i, g: (g, 0, 0, 0))],
        out_specs=pl.BlockSpec((1, SEQ, NA_GW), lambda i, g: (i, 0, g)),
        out_shape=jax.ShapeDtypeStruct((b, SEQ, NA_WIDTH), _BF),
        compiler_params=_cp(("parallel", "parallel"), 32), name="na_attention")(qkv, qkv, qkv, tbl)


def _sconv_kernel(x_ref, w_ref, b_ref, o_ref):
    x = x_ref[0]
    n = x.shape[0]
    row = lax.broadcasted_iota(jnp.int32, x.shape, 0)
    xm = jnp.where(row == 0, 0.0, pltpu.roll(x, 1, axis=0))
    xp = jnp.where(row == n - 1, 0.0, pltpu.roll(x, n - 1, axis=0))
    o_ref[0] = b_ref[...] + xm * w_ref[0:1, :] + x * w_ref[1:2, :] + xp * w_ref[2:3, :]


def _short_conv(hb, w, bias):
    b, n, c = hb.shape
    tc = HY_WIDTH
    return pl.pallas_call(
        _sconv_kernel, grid=(b, c // tc),
        in_specs=[pl.BlockSpec((1, n, tc), lambda i, j: (i, 0, j)),
                  pl.BlockSpec((HY_SHORT_K, tc), lambda i, j: (0, j)),
                  pl.BlockSpec((1, tc), lambda i, j: (0, j))],
        out_specs=pl.BlockSpec((1, n, tc), lambda i, j: (i, 0, j)),
        out_shape=jax.ShapeDtypeStruct((b, n, c), _F32),
        compiler_params=_cp(("parallel", "parallel"), 32), name="hy_short_conv")(hb, w, bias.reshape(1, c))


HY_HID_PAD = LANE
HY_FILT_TC = 256


def _filter_kernel(z_ref, w1_ref, b1_ref, fr_ref, w2_ref, b2_ref, w3f_ref, w3b_ref, dec_ref, k_ref):
    fr = fr_ref[...]
    h = jnp.sin(fr * (jnp.dot(z_ref[...], w1_ref[...], precision=_HP, preferred_element_type=_F32) + b1_ref[...]))
    h = jnp.sin(fr * (jnp.dot(h, w2_ref[...], precision=_HP, preferred_element_type=_F32) + b2_ref[...]))
    fwd = jnp.dot(h[:SEQ], w3f_ref[...], precision=_HP, preferred_element_type=_F32)
    bwd = jnp.dot(h[SEQ:], w3b_ref[...], precision=_HP, preferred_element_type=_F32)
    k = jnp.concatenate([fwd, bwd], axis=0) * dec_ref[...]
    ss = jnp.sum(k * k, axis=0, keepdims=True)
    k_ref[...] = (k * lax.rsqrt(ss + 1e-12)).astype(_BF)


def _filter_tables():
    bands = (HY_EMB_DIM - 1) // 2
    t = jnp.linspace(0.0, 1.0, SEQ, dtype=_F32)[:, None]
    w = 2.0 * math.pi * jnp.arange(SEQ, dtype=_F32)[:, None] / SEQ
    f = jnp.linspace(1e-4, bands - 1, bands, dtype=_F32)[None, :]
    z = jnp.concatenate([t, jnp.cos(f * w), -jnp.sin(f * w)], axis=-1)
    min_decay = math.log(HY_DECAY_TARGET) / HY_SLOW_DECAY_PCT
    max_decay = math.log(HY_DECAY_TARGET) / HY_FAST_DECAY_PCT
    deltas = jnp.abs(jnp.linspace(min_decay, max_decay, HY_WIDTH, dtype=_F32))
    dec = jnp.exp(-t * deltas)
    src = np.concatenate([np.arange(SEQ), [0], np.arange(SEQ - 1, 0, -1)])
    live = np.ones((FFT_N, 1), np.float32)
    live[SEQ] = 0.0
    z2 = jnp.pad(z[src], ((0, 0), (0, HY_HID_PAD - HY_EMB_DIM)))
    dec2 = dec[src] * live
    return z2, dec2


def _hyena_filters(w1, b1, freq, w2, b2, w3):
    z2, dec2 = _filter_tables()
    hp = HY_HID_PAD - HY_FILTER_HIDDEN
    w1p = jnp.pad(w1, ((0, HY_HID_PAD - HY_EMB_DIM), (0, hp)))
    w2p = jnp.pad(w2, ((0, hp), (0, hp)))
    w3p = jnp.pad(w3, ((0, hp), (0, 0)))
    row = lambda v: jnp.pad(v, (0, hp)).reshape(1, HY_HID_PAD)
    nc = HY_WIDTH // HY_FILT_TC
    per_order = 2 * nc
    full = lambda shape: pl.BlockSpec(shape, lambda o, j: (0, 0))
    return pl.pallas_call(
        _filter_kernel, grid=(HY_ORDER, nc),
        in_specs=[full((FFT_N, HY_HID_PAD)), full((HY_HID_PAD, HY_HID_PAD)), full((1, HY_HID_PAD)),
                  full((1, HY_HID_PAD)), full((HY_HID_PAD, HY_HID_PAD)), full((1, HY_HID_PAD)),
                  pl.BlockSpec((HY_HID_PAD, HY_FILT_TC), lambda o, j: (0, o * per_order + j)),
                  pl.BlockSpec((HY_HID_PAD, HY_FILT_TC), lambda o, j: (0, o * per_order + nc + j)),
                  pl.BlockSpec((FFT_N, HY_FILT_TC), lambda o, j: (0, j))],
        out_specs=pl.BlockSpec((FFT_N, HY_FILT_TC), lambda o, j: (0, o * nc + j)),
        out_shape=jax.ShapeDtypeStruct((FFT_N, HY_ORDER * HY_WIDTH), _BF),
        compiler_params=_cp(("parallel", "parallel"), 40), name="hy_filters")(
            z2, w1p, row(b1), row(freq), w2p, row(b2), w3p, w3p, dec2)


def _dft_matrix():
    hi, lo = SEQ // 64, 64
    t = np.arange(SEQ)
    m_hi = (np.arange(hi)[:, None] * lo * t[None, :]) % FFT_N
    m_lo = (np.arange(lo)[:, None] * t[None, :]) % FFT_N
    ang = lambda m: jnp.asarray(m, _F32) * (2.0 * math.pi / FFT_N)
    c1, s1 = jnp.cos(ang(m_hi))[:, None, :], jnp.sin(ang(m_hi))[:, None, :]
    c0, s0 = jnp.cos(ang(m_lo))[None, :, :], jnp.sin(ang(m_lo))[None, :, :]
    re = (c1 * c0 - s1 * s0).reshape(SEQ, SEQ)
    im = -(s1 * c0 + c1 * s0).reshape(SEQ, SEQ)
    nyq = jnp.asarray(1.0 - 2.0 * (t % 2), _F32)
    im = im.at[0].set(nyq)
    return jnp.concatenate([re, im], axis=0).astype(_BF)


def _kf_kernel(f_ref, k_ref, o_ref):
    i = pl.program_id(0)
    f = f_ref[...]
    tm = f.shape[0]
    p1 = jnp.dot(f, k_ref[:SEQ, :], preferred_element_type=_F32)
    p2 = jnp.dot(f, k_ref[SEQ:, :], preferred_element_type=_F32)
    row = lax.broadcasted_iota(jnp.int32, (tm, 1), 0) + i * tm
    sign = (1 - 2 * (row & 1)).astype(_F32)
    o_ref[...] = p1 + sign * p2


def _filter_spectrum(fmat, kfilt):
    tm = 512
    nw = kfilt.shape[1]
    return pl.pallas_call(
        _kf_kernel, grid=(FFT_N // tm,),
        in_specs=[pl.BlockSpec((tm, SEQ), lambda i: (i, 0)),
                  pl.BlockSpec((FFT_N, nw), lambda i: (0, 0))],
        out_specs=pl.BlockSpec((tm, nw), lambda i: (i, 0)),
        out_shape=jax.ShapeDtypeStruct((FFT_N, nw), _F32),
        compiler_params=_cp(("parallel",), 40), name="hy_filter_spectrum")(fmat, kfilt)


HY_FB = 256


def _hconv_kernel(z_ref, fr_ref, fi_ref, ftr_ref, fti_ref, kr_ref, ki_ref, skip_ref, xn_ref, o_ref):
    fk = pl.program_id(1)
    z = z_ref[0]
    zb = z.astype(_BF)
    ur = jnp.dot(fr_ref[...], zb, preferred_element_type=_F32)
    ui = jnp.dot(fi_ref[...], zb, preferred_element_type=_F32)
    kr = kr_ref[...]
    ki = ki_ref[...]
    row0 = jnp.logical_and(lax.broadcasted_iota(jnp.int32, ur.shape, 0) == 0, fk == 0)
    yr = jnp.where(row0, ur * kr, ur * kr - ui * ki)
    yi = jnp.where(row0, ui * ki, ur * ki + ui * kr)
    sc = jnp.where(row0, 1.0 / FFT_N, 2.0 / FFT_N)
    part = (jnp.dot(ftr_ref[...], (yr * sc).astype(_BF), preferred_element_type=_F32)
            + jnp.dot(fti_ref[...], (yi * sc).astype(_BF), preferred_element_type=_F32))

    @pl.when(fk == 0)
    def _():
        o_ref[0] = part

    @pl.when(fk > 0)
    def _():
        o_ref[0] += part

    @pl.when(fk == pl.num_programs(1) - 1)
    def _():
        o_ref[0] = xn_ref[0] * (o_ref[0] + z * skip_ref[...])


def _long_conv_gate(hbc, zsrc, z_col, xn_col, order, fmat, fmat_t, kf, skip):
    b = hbc.shape[0]
    nf = SEQ // HY_FB
    w = HY_WIDTH
    return pl.pallas_call(
        _hconv_kernel, grid=(b, nf),
        in_specs=[pl.BlockSpec((1, SEQ, w), lambda i, f: (i, 0, z_col)),
                  pl.BlockSpec((HY_FB, SEQ), lambda i, f: (f, 0)),
                  pl.BlockSpec((HY_FB, SEQ), lambda i, f: (nf + f, 0)),
                  pl.BlockSpec((SEQ, HY_FB), lambda i, f: (0, f)),
                  pl.BlockSpec((SEQ, HY_FB), lambda i, f: (0, nf + f)),
                  pl.BlockSpec((HY_FB, w), lambda i, f: (f, order)),
                  pl.BlockSpec((HY_FB, w), lambda i, f: (nf + f, order)),
                  pl.BlockSpec((1, w), lambda i, f: (0, 0)),
                  pl.BlockSpec((1, SEQ, w), lambda i, f: (i, 0, xn_col))],
        out_specs=pl.BlockSpec((1, SEQ, w), lambda i, f: (i, 0, 0)),
        out_shape=jax.ShapeDtypeStruct((b, SEQ, w), _F32),
        compiler_params=_cp(("parallel", "arbitrary"), 48), name=f"hy_long_conv{order}")(
            zsrc, fmat, fmat, fmat_t, fmat_t, kf, kf, skip[order].reshape(1, w), hbc)


def _outproj_ln_kernel(n_a, *refs):
    a_refs = refs[:n_a]
    w_refs = refs[n_a:2 * n_a]
    x_ref, g_ref, b_ref, o_ref, ob_ref = refs[2 * n_a:]
    m = _dot(a_refs[0][...], w_refs[0][...])
    for a_ref, w_ref in zip(a_refs[1:], w_refs[1:]):
        m = m + _dot(a_ref[...], w_ref[...])
    y = _layer_norm(DN_ALPHA * x_ref[...] + m, g_ref[...], b_ref[...])
    o_ref[...] = y
    ob_ref[...] = y.astype(_BF)


def _outproj_ln(a_list, w, x2d, g, b):
    m = x2d.shape[0]
    tm = 512
    in_specs, w_args, row0 = [], [], 0
    for a in a_list:
        in_specs.append(pl.BlockSpec((tm, a.shape[1]), lambda i: (i, 0)))
    for a in a_list:
        ka = a.shape[1]
        in_specs.append(pl.BlockSpec((ka, D_MODEL), lambda i, r=row0 // ka: (r, 0)))
        w_args.append(w)
        row0 += ka
    in_specs += [pl.BlockSpec((tm, D_MODEL), lambda i: (i, 0)),
                 pl.BlockSpec((1, D_MODEL), lambda i: (0, 0)),
                 pl.BlockSpec((1, D_MODEL), lambda i: (0, 0))]
    return pl.pallas_call(
        functools.partial(_outproj_ln_kernel, len(a_list)), grid=(m // tm,),
        in_specs=in_specs,
        out_specs=[pl.BlockSpec((tm, D_MODEL), lambda i: (i, 0))] * 2,
        out_shape=[jax.ShapeDtypeStruct((m, D_MODEL), _F32), jax.ShapeDtypeStruct((m, D_MODEL), _BF)],
        compiler_params=_cp(("parallel",), 40), name="outproj_ln")(
            *a_list, *w_args, x2d, g.reshape(1, -1), b.reshape(1, -1))


def _router_kernel(x_ref, wt_ref, a_ref):
    logits = lax.dot_general(wt_ref[...], x_ref[0], (((1,), (1,)), ((), ())),
                             precision=_HP, preferred_element_type=_F32)
    mx = jnp.max(logits, axis=0, keepdims=True)
    ex = jnp.exp(logits - mx)
    a_ref[0] = ex / jnp.sum(ex, axis=0, keepdims=True)


def _router(x3d, w_router):
    b = x3d.shape[0]
    return pl.pallas_call(
        _router_kernel, grid=(b,),
        in_specs=[pl.BlockSpec((1, SEQ, D_MODEL), lambda i: (i, 0, 0)),
                  pl.BlockSpec((N_EXPERTS, D_MODEL), lambda i: (0, 0))],
        out_specs=pl.BlockSpec((1, N_EXPERTS, SEQ), lambda i: (i, 0, 0)),
        out_shape=jax.ShapeDtypeStruct((b, N_EXPERTS, SEQ), _F32),
        compiler_params=_cp(("parallel",), 32), name="moe_router")(x3d, w_router.T)


BISECT_STEPS = 160


def _select_kernel(a_ref, tri_ref, slot_ref, slot_t_ref):
    a = a_ref[...]
    rows = a.shape[0]
    cap = float(CAP)

    def body(_, c):
        lo, hi = c
        mid = 0.5 * (lo + hi)
        cnt = jnp.sum(jnp.where(a > mid, 1.0, 0.0), axis=1, keepdims=True)
        ge = cnt >= cap
        return jnp.where(ge, mid, lo), jnp.where(ge, hi, mid)

    lo, hi = lax.fori_loop(0, BISECT_STEPS, body,
                           (jnp.full((rows, 1), -1.0, _F32), jnp.full((rows, 1), 1.0, _F32)))
    vstar = jnp.max(jnp.where(a <= hi, a, -1.0), axis=1, keepdims=True)
    gt = a > vstar
    eq = a == vstar
    need = cap - jnp.sum(jnp.where(gt, 1.0, 0.0), axis=1, keepdims=True)
    tri = tri_ref[...]
    eq_before = jnp.dot(jnp.where(eq, 1.0, 0.0).astype(_BF), tri, preferred_element_type=_F32)
    sel = jnp.where(gt, 1.0, jnp.where(eq, jnp.where(eq_before < need, 1.0, 0.0), 0.0))
    pos = jnp.dot(sel.astype(_BF), tri, preferred_element_type=_F32)
    slot = jnp.where(sel > 0.5, pos, -1.0)
    slot_ref[...] = slot.astype(jnp.int32)
    if rows < LANE:
        slot = jnp.concatenate([slot, jnp.full((LANE - rows, SEQ), -1.0, _F32)], axis=0)
    slot_tm = slot.T
    for b in range(rows // N_EXPERTS):
        shifted = slot_tm if b == 0 else pltpu.roll(slot_tm, LANE - N_EXPERTS * b, axis=1)
        slot_t_ref[b] = shifted.astype(jnp.int32)


def _select(aff2d):
    rows = aff2d.shape[0]
    assert rows <= LANE and rows % N_EXPERTS == 0
    bsz = rows // N_EXPERTS
    idx = jnp.arange(SEQ, dtype=jnp.int32)
    tri = (idx[:, None] < idx[None, :]).astype(_BF)
    return pl.pallas_call(
        _select_kernel, grid=(1,),
        in_specs=[pl.BlockSpec((rows, SEQ), lambda i: (0, 0)),
                  pl.BlockSpec((SEQ, SEQ), lambda i: (0, 0))],
        out_specs=[pl.BlockSpec((rows, SEQ), lambda i: (0, 0)),
                   pl.BlockSpec((bsz, SEQ, LANE), lambda i: (0, 0, 0))],
        out_shape=[jax.ShapeDtypeStruct((rows, SEQ), jnp.int32),
                   jax.ShapeDtypeStruct((bsz, SEQ, LANE), jnp.int32)],
        compiler_params=_cp(("arbitrary",), 48), name="moe_select")(aff2d, tri)


def _onehot(slot_row):
    return slot_row == lax.broadcasted_iota(jnp.int32, (CAP, SEQ), 0)


def _gather_kernel(slot_ref, a_ref, xb_ref, xe_ref, g_ref):
    hit = _onehot(slot_ref[0, 0])
    p = jnp.where(hit, 1.0, 0.0).astype(_BF)
    xe_ref[0] = jnp.dot(p, xb_ref[0], preferred_element_type=_F32).astype(_BF)
    g_ref[0] = jnp.sum(jnp.where(hit, a_ref[0, 0], 0.0), axis=1, keepdims=True)


def _gather(slot4, aff4, xb3d):
    b = xb3d.shape[0]
    row = pl.BlockSpec((1, 1, 1, SEQ), lambda i, e: (i, e, 0, 0))
    return pl.pallas_call(
        _gather_kernel, grid=(b, N_EXPERTS),
        in_specs=[row, row, pl.BlockSpec((1, SEQ, D_MODEL), lambda i, e: (i, 0, 0))],
        out_specs=[pl.BlockSpec((1, CAP, D_MODEL), lambda i, e: (e, i, 0)),
                   pl.BlockSpec((1, CAP, 1), lambda i, e: (e, i, 0))],
        out_shape=[jax.ShapeDtypeStruct((N_EXPERTS, b * CAP, D_MODEL), _BF),
                   jax.ShapeDtypeStruct((N_EXPERTS, b * CAP, 1), _F32)],
        compiler_params=_cp(("parallel", "parallel"), 32), name="moe_gather")(slot4, aff4, xb3d)


FFN_TF = 512
FFN_TM = 512


def _ffn_kernel(xe_ref, wg_ref, wu_ref, wd_ref, g_ref, ye_ref, acc_ref):
    f = pl.program_id(1)
    wg = wg_ref[0].astype(_BF)
    wu = wu_ref[0].astype(_BF)
    wd = wd_ref[0].astype(_BF)
    for c in range(xe_ref.shape[1] // FFN_TM):
        rows = slice(c * FFN_TM, (c + 1) * FFN_TM)
        xe = xe_ref[0, rows, :]
        hg = _dot(xe, wg)
        hu = _dot(xe, wu)
        h = (hg * (1.0 / (1.0 + jnp.exp(-hg)))) * hu
        part = _dot(h, wd)

        @pl.when(f == 0)
        def _():
            acc_ref[rows, :] = part

        @pl.when(f > 0)
        def _():
            acc_ref[rows, :] += part

    @pl.when(f == pl.num_programs(1) - 1)
    def _():
        ye_ref[0] = (acc_ref[...] * g_ref[0]).astype(_BF)


def _expert_ffn(xe, gates, wg, wu, wd, layer):
    rows = xe.shape[1]
    return pl.pallas_call(
        _ffn_kernel, grid=(N_EXPERTS, D_FF_EXPERT // FFN_TF),
        in_specs=[pl.BlockSpec((1, rows, D_MODEL), lambda e, f: (e, 0, 0)),
                  pl.BlockSpec((None, 1, D_MODEL, FFN_TF), lambda e, f: (layer, e, 0, f)),
                  pl.BlockSpec((None, 1, D_MODEL, FFN_TF), lambda e, f: (layer, e, 0, f)),
                  pl.BlockSpec((None, 1, FFN_TF, D_MODEL), lambda e, f: (layer, e, f, 0)),
                  pl.BlockSpec((1, rows, 1), lambda e, f: (e, 0, 0))],
        out_specs=pl.BlockSpec((1, rows, D_MODEL), lambda e, f: (e, 0, 0)),
        out_shape=jax.ShapeDtypeStruct((N_EXPERTS, rows, D_MODEL), _BF),
        scratch_shapes=[pltpu.VMEM((rows, D_MODEL), _F32)],
        compiler_params=_cp(("parallel", "arbitrary"), V7X_VMEM_SCOPED_MIB), name="moe_ffn")(
            xe, wg, wu, wd, gates)


SCAT_TM = 512


def _scatter_ln_kernel(slot_t_ref, ye_ref, x_ref, g_ref, b_ref, o_ref, ob_ref, pt_ref):
    slot_t = slot_t_ref[0]
    lane_c = lax.broadcasted_iota(jnp.int32, (SCAT_TM, CAP), 1)
    for e in range(N_EXPERTS):
        pt_ref[:, e * CAP:(e + 1) * CAP] = jnp.where(slot_t[:, e:e + 1] == lane_c, 1.0, 0.0).astype(_BF)
    f = jnp.dot(pt_ref[...], ye_ref[...].reshape(N_EXPERTS * CAP, D_MODEL), preferred_element_type=_F32)
    y = _layer_norm(DN_ALPHA * x_ref[0] + f, g_ref[...], b_ref[...])
    o_ref[0] = y
    ob_ref[0] = y.astype(_BF)


def _scatter_ln(slot_t, ye, x3d, g, b):
    bsz = x3d.shape[0]
    vec = pl.BlockSpec((1, D_MODEL), lambda i, r: (0, 0))
    seq = pl.BlockSpec((1, SCAT_TM, D_MODEL), lambda i, r: (i, r, 0))
    return pl.pallas_call(
        _scatter_ln_kernel, grid=(bsz, SEQ // SCAT_TM),
        in_specs=[pl.BlockSpec((1, SCAT_TM, LANE), lambda i, r: (i, r, 0)),
                  pl.BlockSpec((N_EXPERTS, CAP, D_MODEL), lambda i, r: (0, i, 0)),
                  seq, vec, vec],
        out_specs=[seq, seq],
        out_shape=[jax.ShapeDtypeStruct((bsz, SEQ, D_MODEL), _F32),
                   jax.ShapeDtypeStruct((bsz, SEQ, D_MODEL), _BF)],
        scratch_shapes=[pltpu.VMEM((SCAT_TM, N_EXPERTS * CAP), _BF)],
        compiler_params=_cp(("parallel", "arbitrary"), 48), name="moe_scatter_ln")(
            slot_t, ye, x3d, g.reshape(1, -1), b.reshape(1, -1))


def _moe_block(x1, x1b, w_router, wg, wu, wd, layer, g, b):
    bsz = x1.shape[0]
    aff = _router(x1, w_router)
    slot, slot_t = _select(aff.reshape(bsz * N_EXPERTS, SEQ))
    slot4 = slot.reshape(bsz, N_EXPERTS, 1, SEQ)
    aff4 = aff.reshape(bsz, N_EXPERTS, 1, SEQ)
    xe, gates = _gather(slot4, aff4, x1b)
    ye = _expert_ffn(xe, gates, wg, wu, wd, layer)
    return _scatter_ln(slot_t, ye, x1, g, b)


def _ple_kernel(x_ref, xb_ref, p_ref, wg_ref, wp_ref, o_ref):
    gate = _dot(xb_ref[...], wg_ref[...])
    emb = _dot(p_ref[...], wp_ref[...])
    o_ref[...] = x_ref[...] + (1.0 / (1.0 + jnp.exp(-gate))) * emb


def _ple(x2d, xb2d, p3d, wg, wp, layer):
    m = x2d.shape[0]
    tm = 512
    return pl.pallas_call(
        _ple_kernel, grid=(m // tm,),
        in_specs=[pl.BlockSpec((tm, D_MODEL), lambda i: (i, 0)),
                  pl.BlockSpec((tm, D_MODEL), lambda i: (i, 0)),
                  pl.BlockSpec((None, tm, PLE_DIM), lambda i: (layer, i, 0)),
                  pl.BlockSpec((None, D_MODEL, D_MODEL), lambda i: (layer, 0, 0)),
                  pl.BlockSpec((None, PLE_DIM, D_MODEL), lambda i: (layer, 0, 0))],
        out_specs=pl.BlockSpec((tm, D_MODEL), lambda i: (i, 0)),
        out_shape=jax.ShapeDtypeStruct((m, D_MODEL), _F32),
        compiler_params=_cp(("parallel",), 32), name="ple")(x2d, xb2d, p3d, wg, wp)


MLA_HB = LANE
MLA_IN_PAD = 768
MLA_KR_COL = MLA_Q_LORA + MLA_KV_LORA


def _rms(x, g):
    return x * lax.rsqrt(jnp.mean(x * x, axis=-1, keepdims=True) + NORM_EPS) * g


def _mla_pre_kernel(x_ref, win_ref, qg_ref, wq_ref, kvg_ref, wkv_ref, qa_ref, qb_ref, ka_ref, kb_ref,
                    q_ref, kv_ref, kr_ref):
    h = _dot(x_ref[...], win_ref[...])
    q = _dot(_rms(h[:, :MLA_Q_LORA], qg_ref[...]), wq_ref[...])
    swap = pltpu.roll(q, q.shape[1] - MLA_ROPE, axis=1)
    qa = jnp.concatenate([qa_ref[...]] * MLA_HEADS, axis=1)
    qb = jnp.concatenate([qb_ref[...]] * MLA_HEADS, axis=1)
    q_ref[...] = (q * qa + swap * qb).astype(_BF)
    kv_ref[...] = _dot(_rms(h[:, MLA_Q_LORA:MLA_KR_COL], kvg_ref[...]), wkv_ref[...]).astype(_BF)
    kr = h[:, MLA_KR_COL:MLA_KR_COL + MLA_HB]
    kr_ref[...] = (kr * ka_ref[...] + pltpu.roll(kr, MLA_HB - MLA_ROPE, axis=1) * kb_ref[...]).astype(_BF)


def _rope_lane_tables():
    inv = 1.0 / (ROPE_THETA ** (jnp.arange(0, MLA_ROPE, 2, dtype=_F32) / MLA_ROPE))
    ang = jnp.arange(SEQ, dtype=_F32)[:, None] * inv[None, :]
    cos, sin = jnp.cos(ang), jnp.sin(ang)
    ones = jnp.ones((SEQ, MLA_NOPE), _F32)
    zeros_n = jnp.zeros((SEQ, MLA_NOPE), _F32)
    zeros_r = jnp.zeros((SEQ, MLA_ROPE), _F32)
    cc = jnp.concatenate([cos, cos], axis=1)
    ss = jnp.concatenate([-sin, sin], axis=1)
    keep_q = jnp.concatenate([ones, cc, zeros_r], axis=1)
    keep_k = jnp.concatenate([zeros_n, cc, zeros_r], axis=1)
    swp = jnp.concatenate([zeros_n, ss, zeros_r], axis=1)
    return keep_q, keep_k, swp


def _mla_pre(x2d, w_in, q_norm, w_q_up, kv_norm, w_kv_up):
    m = x2d.shape[0]
    tm = 512
    half = MLA_ROPE // 2
    kr_w = w_in[:, MLA_KR_COL:]
    kr_swapped = jnp.concatenate([kr_w[:, half:], kr_w[:, :half]], axis=1)
    win = jnp.concatenate([w_in[:, :MLA_KR_COL], jnp.zeros((D_MODEL, MLA_NOPE), _F32), kr_w, kr_swapped], axis=1)
    wq = w_q_up.reshape(MLA_Q_LORA, MLA_HEADS, MLA_NOPE + MLA_ROPE)
    rope_w = wq[:, :, MLA_NOPE:]
    wq = jnp.concatenate([wq, rope_w[:, :, half:], rope_w[:, :, :half]], axis=2).reshape(MLA_Q_LORA, MLA_HEADS * MLA_HB)
    keep_q, keep_k, swp = _rope_lane_tables()
    qscale = (MLA_NOPE + MLA_ROPE) ** -0.5
    nq = MLA_HEADS * MLA_HB
    pos = lambda: pl.BlockSpec((tm, MLA_HB), lambda i: (i % (SEQ // tm), 0))
    full = lambda a: pl.BlockSpec(a.shape, lambda i: (0,) * a.ndim)
    qg, kvg = q_norm.reshape(1, -1), kv_norm.reshape(1, -1)
    return pl.pallas_call(
        _mla_pre_kernel, grid=(m // tm,),
        in_specs=[pl.BlockSpec((tm, D_MODEL), lambda i: (i, 0)), full(win), full(qg), full(wq), full(kvg),
                  full(w_kv_up), pos(), pos(), pos(), pos()],
        out_specs=[pl.BlockSpec((tm, nq), lambda i: (i, 0)),
                   pl.BlockSpec((tm, nq), lambda i: (i, 0)),
                   pl.BlockSpec((tm, MLA_HB), lambda i: (i, 0))],
        out_shape=[jax.ShapeDtypeStruct((m, nq), _BF), jax.ShapeDtypeStruct((m, nq), _BF),
                   jax.ShapeDtypeStruct((m, MLA_HB), _BF)],
        compiler_params=_cp(("parallel",), 48), name="mla_pre")(
            x2d, win, qg, wq, kvg, w_kv_up, keep_q * qscale, swp * qscale, keep_k, swp)


MLA_TQ = 256
MLA_PAIR = 2


def _mla_attn_kernel(q_ref, kv_ref, kr_ref, o_ref, k_scr):
    lane = lax.broadcasted_iota(jnp.int32, (SEQ, MLA_HB), 1)
    kr = kr_ref[0].astype(_F32)
    for j in range(MLA_PAIR):
        kvh = kv_ref[0, :, j * MLA_HB:(j + 1) * MLA_HB].astype(_F32)
        k_scr[j] = jnp.where(lane < MLA_NOPE, kvh, kr).astype(_BF)
    lane_q = lax.broadcasted_iota(jnp.int32, (MLA_TQ, MLA_HB), 1)

    def body(i, carry):
        rows = pl.ds(pl.multiple_of(i * MLA_TQ, MLA_TQ), MLA_TQ)
        outs = []
        for j in range(MLA_PAIR):
            qb = q_ref[0, rows, j * MLA_HB:(j + 1) * MLA_HB]
            s = _dot_nt(qb, k_scr[j])
            mx = jnp.max(s, axis=1, keepdims=True)
            p = jnp.exp(s - mx)
            den = jnp.sum(p, axis=1, keepdims=True)
            outs.append(_dot(p, kv_ref[0, :, j * MLA_HB:(j + 1) * MLA_HB]) / den)
        even = pltpu.roll(outs[0], MLA_V, axis=1)
        o_ref[0, rows, :] = jnp.where(lane_q < MLA_V, even, outs[1]).astype(_BF)
        return carry

    lax.fori_loop(0, SEQ // MLA_TQ, body, 0)


def _mla_attention(q, kv, kr):
    b = q.shape[0]
    w = MLA_PAIR * MLA_HB
    return pl.pallas_call(
        _mla_attn_kernel, grid=(b, MLA_HEADS // MLA_PAIR),
        in_specs=[pl.BlockSpec((1, SEQ, w), lambda i, h: (i, 0, h)),
                  pl.BlockSpec((1, SEQ, w), lambda i, h: (i, 0, h)),
                  pl.BlockSpec((1, SEQ, MLA_HB), lambda i, h: (i, 0, 0))],
        out_specs=pl.BlockSpec((1, SEQ, MLA_PAIR * MLA_V), lambda i, h: (i, 0, h)),
        out_shape=jax.ShapeDtypeStruct((b, SEQ, MLA_HEADS * MLA_V), _BF),
        scratch_shapes=[pltpu.VMEM((MLA_PAIR, SEQ, MLA_HB), _BF)],
        compiler_params=_cp(("parallel", "parallel"), 40), name="mla_attention")(q, kv, kr)


def _na_hyena_mixer(x, w_in, rpb, conv_w, conv_b, f_w1, f_b1, f_freq, f_w2, f_b2, f_w3, skip):
    b = x.shape[0]
    qkv, hb = _inproj(x.reshape(b * SEQ, D_MODEL), w_in)
    y_a = _na_attention(qkv.reshape(b, SEQ, 3 * NA_WIDTH), rpb)
    hbc = _short_conv(hb.reshape(b, SEQ, (HY_ORDER + 1) * HY_WIDTH), conv_w, conv_b)
    fmat = _dft_matrix()
    kfilt = _hyena_filters(f_w1, f_b1, f_freq, f_w2, f_b2, f_w3)
    kf = _filter_spectrum(fmat, kfilt)
    fmat_t = fmat.T
    z = _long_conv_gate(hbc, hbc, 0, 1, 0, fmat, fmat_t, kf, skip)
    for o in range(1, HY_ORDER):
        z = _long_conv_gate(hbc, z, 0, o + 1, o, fmat, fmat_t, kf, skip)
    return y_a.reshape(b * SEQ, NA_WIDTH), z.reshape(b * SEQ, HY_WIDTH)


def kernel(x, p, ab_w_in, na_rpb, hy_conv_w, hy_conv_b, hy_f_w1, hy_f_b1, hy_f_freq, hy_f_w2, hy_f_b2, hy_f_w3, hy_skip, ab_w_out, mla_w_in, mla_q_norm, mla_w_q_up, mla_kv_norm, mla_w_kv_up, mla_w_out, ln1_g, ln1_b, ln2_g, ln2_b, moe_router, moe_w_gate, moe_w_up, moe_w_down, ple_gate, ple_proj):
    b = x.shape[0]
    m = b * SEQ
    for i in range(DEPTH):
        j = i // 2
        x2d = x.reshape(m, D_MODEL)
        if i % 2 == 0:
            y_a, z = _na_hyena_mixer(x, ab_w_in[j], na_rpb[j], hy_conv_w[j], hy_conv_b[j], hy_f_w1[j],
                                     hy_f_b1[j], hy_f_freq[j], hy_f_w2[j], hy_f_b2[j], hy_f_w3[j], hy_skip[j])
            x1, x1b = _outproj_ln([y_a, z], ab_w_out[j], x2d, ln1_g[i], ln1_b[i])
        else:
            q, kv, kr = _mla_pre(x2d, mla_w_in[j], mla_q_norm[j], mla_w_q_up[j], mla_kv_norm[j], mla_w_kv_up[j])
            att = _mla_attention(q.reshape(b, SEQ, -1), kv.reshape(b, SEQ, -1), kr.reshape(b, SEQ, -1))
            x1, x1b = _outproj_ln([att.reshape(m, MLA_HEADS * MLA_V)], mla_w_out[j], x2d, ln1_g[i], ln1_b[i])
        x2, x2b = _moe_block(x1.reshape(b, SEQ, D_MODEL), x1b.reshape(b, SEQ, D_MODEL), moe_router[i],
                             moe_w_gate, moe_w_up, moe_w_down, i, ln2_g[i], ln2_b[i])
        x = _ple(x2.reshape(m, D_MODEL), x2b.reshape(m, D_MODEL), p.reshape(DEPTH, m, PLE_DIM),
                 ple_gate, ple_proj, i).reshape(b, SEQ, D_MODEL)
    return x
```

```python
import functools
import math

import numpy as np
import jax
import jax.numpy as jnp
from jax import lax
from jax.experimental import pallas as pl
from jax.experimental.pallas import tpu as pltpu

D_MODEL = 1024
BATCH = 8
SEQ = 2048
DEPTH = 2
GRID_W = 64
PLE_DIM = 256
NA_HEADS = 8
NA_HEAD_DIM = 64
NA_WIDTH = NA_HEADS * NA_HEAD_DIM
NA_WIN_ROWS_MAX = 8
NA_WIN_COLS = 16
HY_WIDTH = D_MODEL - NA_WIDTH
HY_ORDER = 2
HY_SHORT_K = 3
HY_EMB_DIM = 33
HY_FILTER_HIDDEN = 64
HY_FAST_DECAY_PCT = 0.3
HY_SLOW_DECAY_PCT = 1.5
HY_DECAY_TARGET = 1e-2
AB_IN_WIDTH = 3 * NA_WIDTH + (HY_ORDER + 1) * HY_WIDTH
MLA_HEADS = 16
MLA_Q_LORA = 384
MLA_KV_LORA = 256
MLA_NOPE = 64
MLA_ROPE = 32
MLA_V = 64
ROPE_THETA = 10000.0
N_EXPERTS = 16
EC_CAPACITY_FACTOR = 2
D_FF_EXPERT = 2048
DN_ALPHA = (2 * DEPTH) ** 0.25
NORM_EPS = 1e-5
NEG_INF = -1e30

CAP = EC_CAPACITY_FACTOR * SEQ // N_EXPERTS
NA_ROWS = SEQ // GRID_W
NA_WIN_ROWS = min(NA_WIN_ROWS_MAX, NA_ROWS)
NA_SLAB = NA_WIN_ROWS * GRID_W
FFT_N = 2 * SEQ
LOG2E = math.log2(math.e)

LANE = 128
MIB = 1 << 20

_BF = jnp.bfloat16
_F32 = jnp.float32
_HP = lax.Precision.HIGHEST


def _cp(sem, vmem_mib):
    return pltpu.CompilerParams(dimension_semantics=sem, vmem_limit_bytes=vmem_mib * MIB)


def _dot(a, b):
    return jnp.dot(a.astype(_BF), b.astype(_BF), preferred_element_type=_F32)


def _dot_nt(a, b):
    return lax.dot_general(a.astype(_BF), b.astype(_BF), (((1,), (1,)), ((), ())),
                           preferred_element_type=_F32)


def _layer_norm(y, g, b):
    mu = jnp.mean(y, axis=-1, keepdims=True)
    d = y - mu
    var = jnp.mean(d * d, axis=-1, keepdims=True)
    return d * lax.rsqrt(var + NORM_EPS) * g + b


def _inproj_kernel(x_ref, w_ref, qkv_ref, hb_ref):
    h = _dot(x_ref[...], w_ref[...])
    qkv_ref[...] = h[:, :3 * NA_WIDTH].astype(_BF)
    hb_ref[...] = h[:, 3 * NA_WIDTH:]


def _inproj(x2d, w):
    m = x2d.shape[0]
    tm = 512
    nq, nh = 3 * NA_WIDTH, (HY_ORDER + 1) * HY_WIDTH
    return pl.pallas_call(
        _inproj_kernel, grid=(m // tm,),
        in_specs=[pl.BlockSpec((tm, D_MODEL), lambda i: (i, 0)),
                  pl.BlockSpec((D_MODEL, AB_IN_WIDTH), lambda i: (0, 0))],
        out_specs=[pl.BlockSpec((tm, nq), lambda i: (i, 0)),
                   pl.BlockSpec((tm, nh), lambda i: (i, 0))],
        out_shape=[jax.ShapeDtypeStruct((m, nq), _BF), jax.ShapeDtypeStruct((m, nh), _F32)],
        compiler_params=_cp(("parallel",), 48), name="ab_inproj")(x2d, w)


NA_GROUP = 4
NA_GW = NA_GROUP * NA_HEAD_DIM


def _na_kernel(q_ref, k_ref, v_ref, tbl_ref, o_ref):
    lane_head = lax.broadcasted_iota(jnp.int32, (GRID_W, NA_GW), 1) >> int(math.log2(NA_HEAD_DIM))
    scale = NA_HEAD_DIM ** -0.5 * LOG2E

    def body(r, carry):
        r0 = jnp.clip(r - NA_WIN_ROWS // 2, 0, NA_ROWS - NA_WIN_ROWS)
        off = r0 - r + (NA_WIN_ROWS_MAX - 1)
        qr = q_ref[0, pl.ds(pl.multiple_of(r * GRID_W, GRID_W), GRID_W), :].astype(_F32)
        q4 = jnp.concatenate([jnp.where(lane_head == h, qr, 0.0) for h in range(NA_GROUP)], axis=0)
        ks = k_ref[0, pl.ds(pl.multiple_of(r0 * GRID_W, GRID_W), NA_SLAB), :]
        vs = v_ref[0, pl.ds(pl.multiple_of(r0 * GRID_W, GRID_W), NA_SLAB), :]
        bias = jnp.concatenate([tbl_ref[h, off] for h in range(NA_GROUP)], axis=0)
        s = _dot_nt(q4, ks) * scale + bias
        mx = jnp.max(s, axis=1, keepdims=True)
        p = jnp.exp2(s - mx)
        den = jnp.sum(p, axis=1, keepdims=True)
        o4 = _dot(p, vs) / den
        out = jnp.zeros((GRID_W, NA_GW), _F32)
        for h in range(NA_GROUP):
            out = out + jnp.where(lane_head == h, o4[h * GRID_W:(h + 1) * GRID_W], 0.0)
        o_ref[0, pl.ds(pl.multiple_of(r * GRID_W, GRID_W), GRID_W), :] = out.astype(_BF)
        return carry

    lax.fori_loop(0, NA_ROWS, body, 0, unroll=4)


def _na_bias_table(rpb):
    c = np.arange(GRID_W)
    c0 = np.clip(c - NA_WIN_COLS // 2, 0, GRID_W - NA_WIN_COLS)
    kc = np.arange(GRID_W)
    col_ok = (kc[None, :] >= c0[:, None]) & (kc[None, :] < c0[:, None] + NA_WIN_COLS)
    dc_idx = np.clip(kc[None, :] - c[:, None], -(NA_WIN_COLS - 1), NA_WIN_COLS - 1) + (NA_WIN_COLS - 1)
    pick = (dc_idx[None, :, :] == np.arange(2 * NA_WIN_COLS - 1)[:, None, None]).astype(np.float32)
    per_dr = jnp.einsum("hdk,kqc->hdqc", rpb.astype(_F32), pick, precision=_HP)
    per_dr = jnp.where(col_ok[None, None], per_dr * LOG2E, NEG_INF)
    slabs = jnp.stack([per_dr[:, off:off + NA_WIN_ROWS] for off in range(NA_WIN_ROWS_MAX)], axis=1)
    return slabs.transpose(0, 1, 3, 2, 4).reshape(NA_HEADS, NA_WIN_ROWS_MAX, GRID_W, NA_SLAB)


def _na_attention(qkv, rpb):
    b = qkv.shape[0]
    tbl = _na_bias_table(rpb)
    ng = NA_HEADS // NA_GROUP
    blk = lambda col0: pl.BlockSpec((1, SEQ, NA_GW), lambda i, g, c=col0: (i, 0, c + g))
    return pl.pallas_call(
        _na_kernel, grid=(b, ng),
        in_specs=[blk(0), blk(ng), blk(2 * ng),
                  pl.BlockSpec((NA_GROUP, NA_WIN_ROWS_MAX, GRID_W, NA_SLAB), lambda i, g: (g, 0, 0, 0))],
        out_specs=pl.BlockSpec((1, SEQ, NA_GW), lambda i, g: (i, 0, g)),
        out_shape=jax.ShapeDtypeStruct((b, SEQ, NA_WIDTH), _BF),
        compiler_params=_cp(("parallel", "parallel"), 32), name="na_attention")(qkv, qkv, qkv, tbl)


def _sconv_kernel(x_ref, w_ref, b_ref, o_ref, vb_ref):
    x = x_ref[0]
    n = x.shape[0]
    row = lax.broadcasted_iota(jnp.int32, x.shape, 0)
    xm = jnp.where(row == 0, 0.0, pltpu.roll(x, 1, axis=0))
    xp = jnp.where(row == n - 1, 0.0, pltpu.roll(x, n - 1, axis=0))
    y = b_ref[...] + xm * w_ref[0:1, :] + x * w_ref[1:2, :] + xp * w_ref[2:3, :]
    o_ref[0] = y

    @pl.when(pl.program_id(1) == 0)
    def _():
        vb_ref[0] = y.astype(_BF)


def _short_conv(hb, w, bias):
    b, n, c = hb.shape
    tc = HY_WIDTH
    return pl.pallas_call(
        _sconv_kernel, grid=(b, c // tc),
        in_specs=[pl.BlockSpec((1, n, tc), lambda i, j: (i, 0, j)),
                  pl.BlockSpec((HY_SHORT_K, tc), lambda i, j: (0, j)),
                  pl.BlockSpec((1, tc), lambda i, j: (0, j))],
        out_specs=[pl.BlockSpec((1, n, tc), lambda i, j: (i, 0, j)),
                   pl.BlockSpec((1, n, tc), lambda i, j: (i, 0, 0))],
        out_shape=[jax.ShapeDtypeStruct((b, n, c), _F32), jax.ShapeDtypeStruct((b, n, tc), _BF)],
        compiler_params=_cp(("parallel", "arbitrary"), 32), name="hy_short_conv")(hb, w, bias.reshape(1, c))


HY_HID_PAD = LANE
HY_FILT_TC = 256


def _filter_kernel(z_ref, w1_ref, b1_ref, fr_ref, w2_ref, b2_ref, w3f_ref, w3b_ref, dec_ref, k_ref):
    fr = fr_ref[...]
    h = jnp.sin(fr * (jnp.dot(z_ref[...], w1_ref[...], precision=_HP, preferred_element_type=_F32) + b1_ref[...]))
    h = jnp.sin(fr * (jnp.dot(h, w2_ref[...], precision=_HP, preferred_element_type=_F32) + b2_ref[...]))
    fwd = jnp.dot(h[:SEQ], w3f_ref[...], precision=_HP, preferred_element_type=_F32)
    bwd = jnp.dot(h[SEQ:], w3b_ref[...], precision=_HP, preferred_element_type=_F32)
    k = jnp.concatenate([fwd, bwd], axis=0) * dec_ref[...]
    ss = jnp.sum(k * k, axis=0, keepdims=True)
    k_ref[...] = (k * lax.rsqrt(ss + 1e-12)).astype(_BF)


def _filter_tables():
    bands = (HY_EMB_DIM - 1) // 2
    t = jnp.linspace(0.0, 1.0, SEQ, dtype=_F32)[:, None]
    w = 2.0 * math.pi * jnp.arange(SEQ, dtype=_F32)[:, None] / SEQ
    f = jnp.linspace(1e-4, bands - 1, bands, dtype=_F32)[None, :]
    z = jnp.concatenate([t, jnp.cos(f * w), -jnp.sin(f * w)], axis=-1)
    min_decay = math.log(HY_DECAY_TARGET) / HY_SLOW_DECAY_PCT
    max_decay = math.log(HY_DECAY_TARGET) / HY_FAST_DECAY_PCT
    deltas = jnp.abs(jnp.linspace(min_decay, max_decay, HY_WIDTH, dtype=_F32))
    dec = jnp.exp(-t * deltas)
    src = np.concatenate([np.arange(SEQ), [0], np.arange(SEQ - 1, 0, -1)])
    live = np.ones((FFT_N, 1), np.float32)
    live[SEQ] = 0.0
    z2 = jnp.pad(z[src], ((0, 0), (0, HY_HID_PAD - HY_EMB_DIM)))
    dec2 = dec[src] * live
    return z2, dec2


def _hyena_filters(w1, b1, freq, w2, b2, w3):
    z2, dec2 = _filter_tables()
    hp = HY_HID_PAD - HY_FILTER_HIDDEN
    w1p = jnp.pad(w1, ((0, HY_HID_PAD - HY_EMB_DIM), (0, hp)))
    w2p = jnp.pad(w2, ((0, hp), (0, hp)))
    w3p = jnp.pad(w3, ((0, hp), (0, 0)))
    row = lambda v: jnp.pad(v, (0, hp)).reshape(1, HY_HID_PAD)
    nc = HY_WIDTH // HY_FILT_TC
    per_order = 2 * nc
    full = lambda shape: pl.BlockSpec(shape, lambda o, j: (0, 0))
    return pl.pallas_call(
        _filter_kernel, grid=(HY_ORDER, nc),
        in_specs=[full((FFT_N, HY_HID_PAD)), full((HY_HID_PAD, HY_HID_PAD)), full((1, HY_HID_PAD)),
                  full((1, HY_HID_PAD)), full((HY_HID_PAD, HY_HID_PAD)), full((1, HY_HID_PAD)),
                  pl.BlockSpec((HY_HID_PAD, HY_FILT_TC), lambda o, j: (0, o * per_order + j)),
                  pl.BlockSpec((HY_HID_PAD, HY_FILT_TC), lambda o, j: (0, o * per_order + nc + j)),
                  pl.BlockSpec((FFT_N, HY_FILT_TC), lambda o, j: (0, j))],
        out_specs=pl.BlockSpec((FFT_N, HY_FILT_TC), lambda o, j: (0, o * nc + j)),
        out_shape=jax.ShapeDtypeStruct((FFT_N, HY_ORDER * HY_WIDTH), _BF),
        compiler_params=_cp(("parallel", "parallel"), 40), name="hy_filters")(
            z2, w1p, row(b1), row(freq), w2p, row(b2), w3p, w3p, dec2)


def _dft_matrix():
    hi, lo = SEQ // 64, 64
    t = np.arange(SEQ)
    m_hi = (np.arange(hi)[:, None] * lo * t[None, :]) % FFT_N
    m_lo = (np.arange(lo)[:, None] * t[None, :]) % FFT_N
    ang = lambda m: jnp.asarray(m, _F32) * (2.0 * math.pi / FFT_N)
    c1, s1 = jnp.cos(ang(m_hi))[:, None, :], jnp.sin(ang(m_hi))[:, None, :]
    c0, s0 = jnp.cos(ang(m_lo))[None, :, :], jnp.sin(ang(m_lo))[None, :, :]
    re = (c1 * c0 - s1 * s0).reshape(SEQ, SEQ)
    im = -(s1 * c0 + c1 * s0).reshape(SEQ, SEQ)
    nyq = jnp.asarray(1.0 - 2.0 * (t % 2), _F32)
    im = im.at[0].set(nyq)
    return jnp.concatenate([re, im], axis=0).astype(_BF)


def _kf_kernel(f_ref, k_ref, o_ref):
    i = pl.program_id(0)
    f = f_ref[...]
    tm = f.shape[0]
    p1 = jnp.dot(f, k_ref[:SEQ, :], preferred_element_type=_F32)
    p2 = jnp.dot(f, k_ref[SEQ:, :], preferred_element_type=_F32)
    row = lax.broadcasted_iota(jnp.int32, (tm, 1), 0) + i * tm
    sign = (1 - 2 * (row & 1)).astype(_F32)
    o_ref[...] = p1 + sign * p2


def _filter_spectrum(fmat, kfilt):
    tm = 512
    nw = kfilt.shape[1]
    return pl.pallas_call(
        _kf_kernel, grid=(FFT_N // tm,),
        in_specs=[pl.BlockSpec((tm, SEQ), lambda i: (i, 0)),
                  pl.BlockSpec((FFT_N, nw), lambda i: (0, 0))],
        out_specs=pl.BlockSpec((tm, nw), lambda i: (i, 0)),
        out_shape=jax.ShapeDtypeStruct((FFT_N, nw), _F32),
        compiler_params=_cp(("parallel",), 40), name="hy_filter_spectrum")(fmat, kfilt)


HY_FB = 256


HY_TM = 512


def _hconv_fwd_kernel(zb_ref, fr_ref, fi_ref, kr_ref, ki_ref, y_ref):
    fk = pl.program_id(1)
    zb = zb_ref[0]
    ur = jnp.dot(fr_ref[...], zb, preferred_element_type=_F32)
    ui = jnp.dot(fi_ref[...], zb, preferred_element_type=_F32)
    kr = kr_ref[...]
    ki = ki_ref[...]
    row0 = jnp.logical_and(lax.broadcasted_iota(jnp.int32, ur.shape, 0) == 0, fk == 0)
    yr = jnp.where(row0, ur * kr, ur * kr - ui * ki)
    yi = jnp.where(row0, ui * ki, ur * ki + ui * kr)
    sc = jnp.where(row0, 1.0 / FFT_N, 2.0 / FFT_N)
    y_ref[0, 0] = (yr * sc).astype(_BF)
    y_ref[0, 1] = (yi * sc).astype(_BF)


def _hconv_inv_kernel(ft_ref, y_ref, z_ref, skip_ref, xn_ref, o_ref, ob_ref):
    conv = jnp.dot(ft_ref[...], y_ref[0], preferred_element_type=_F32)
    out = xn_ref[0] * (conv + z_ref[0] * skip_ref[...])
    o_ref[0] = out
    ob_ref[0] = out.astype(_BF)


def _long_conv_gate(hbc, zsrc, zb, xn_col, order, fmat, fmat_t, kf, skip):
    b = hbc.shape[0]
    nf = SEQ // HY_FB
    w = HY_WIDTH
    y = pl.pallas_call(
        _hconv_fwd_kernel, grid=(b, nf),
        in_specs=[pl.BlockSpec((1, SEQ, w), lambda i, f: (i, 0, 0)),
                  pl.BlockSpec((HY_FB, SEQ), lambda i, f: (f, 0)),
                  pl.BlockSpec((HY_FB, SEQ), lambda i, f: (nf + f, 0)),
                  pl.BlockSpec((HY_FB, w), lambda i, f: (f, order)),
                  pl.BlockSpec((HY_FB, w), lambda i, f: (nf + f, order))],
        out_specs=pl.BlockSpec((1, 2, HY_FB, w), lambda i, f: (i, 0, f, 0)),
        out_shape=jax.ShapeDtypeStruct((b, 2, SEQ, w), _BF),
        compiler_params=_cp(("parallel", "parallel"), 32), name=f"hy_conv_fwd{order}")(zb, fmat, fmat, kf, kf)
    blk = lambda col: pl.BlockSpec((1, HY_TM, w), lambda i, m, c=col: (i, m, c))
    return pl.pallas_call(
        _hconv_inv_kernel, grid=(b, SEQ // HY_TM),
        in_specs=[pl.BlockSpec((HY_TM, FFT_N), lambda i, m: (m, 0)),
                  pl.BlockSpec((1, FFT_N, w), lambda i, m: (i, 0, 0)),
                  blk(0),
                  pl.BlockSpec((1, w), lambda i, m: (0, 0)),
                  blk(xn_col)],
        out_specs=[blk(0), blk(0)],
        out_shape=[jax.ShapeDtypeStruct((b, SEQ, w), _F32), jax.ShapeDtypeStruct((b, SEQ, w), _BF)],
        compiler_params=_cp(("parallel", "parallel"), 32), name=f"hy_conv_inv{order}")(
            fmat_t, y.reshape(b, FFT_N, w), zsrc, skip[order].reshape(1, w), hbc)


def _outproj_ln_kernel(n_a, *refs):
    a_refs = refs[:n_a]
    w_refs = refs[n_a:2 * n_a]
    x_ref, g_ref, b_ref, o_ref, ob_ref = refs[2 * n_a:]
    m = _dot(a_refs[0][...], w_refs[0][...])
    for a_ref, w_ref in zip(a_refs[1:], w_refs[1:]):
        m = m + _dot(a_ref[...], w_ref[...])
    y = _layer_norm(DN_ALPHA * x_ref[...] + m, g_ref[...], b_ref[...])
    o_ref[...] = y
    ob_ref[...] = y.astype(_BF)


def _outproj_ln(a_list, w, x2d, g, b):
    m = x2d.shape[0]
    tm = 512
    in_specs, w_args, row0 = [], [], 0
    for a in a_list:
        in_specs.append(pl.BlockSpec((tm, a.shape[1]), lambda i: (i, 0)))
    for a in a_list:
        ka = a.shape[1]
        in_specs.append(pl.BlockSpec((ka, D_MODEL), lambda i, r=row0 // ka: (r, 0)))
        w_args.append(w)
        row0 += ka
    in_specs += [pl.BlockSpec((tm, D_MODEL), lambda i: (i, 0)),
                 pl.BlockSpec((1, D_MODEL), lambda i: (0, 0)),
                 pl.BlockSpec((1, D_MODEL), lambda i: (0, 0))]
    return pl.pallas_call(
        functools.partial(_outproj_ln_kernel, len(a_list)), grid=(m // tm,),
        in_specs=in_specs,
        out_specs=[pl.BlockSpec((tm, D_MODEL), lambda i: (i, 0))] * 2,
        out_shape=[jax.ShapeDtypeStruct((m, D_MODEL), _F32), jax.ShapeDtypeStruct((m, D_MODEL), _BF)],
        compiler_params=_cp(("parallel",), 40), name="outproj_ln")(
            *a_list, *w_args, x2d, g.reshape(1, -1), b.reshape(1, -1))


def _router_kernel(x_ref, wt_ref, a_ref):
    logits = lax.dot_general(wt_ref[...], x_ref[0], (((1,), (1,)), ((), ())),
                             precision=_HP, preferred_element_type=_F32)
    mx = jnp.max(logits, axis=0, keepdims=True)
    ex = jnp.exp(logits - mx)
    a_ref[0] = ex / jnp.sum(ex, axis=0, keepdims=True)


def _router(x3d, w_router):
    b = x3d.shape[0]
    return pl.pallas_call(
        _router_kernel, grid=(b,),
        in_specs=[pl.BlockSpec((1, SEQ, D_MODEL), lambda i: (i, 0, 0)),
                  pl.BlockSpec((N_EXPERTS, D_MODEL), lambda i: (0, 0))],
        out_specs=pl.BlockSpec((1, N_EXPERTS, SEQ), lambda i: (i, 0, 0)),
        out_shape=jax.ShapeDtypeStruct((b, N_EXPERTS, SEQ), _F32),
        compiler_params=_cp(("parallel",), 32), name="moe_router")(x3d, w_router.T)


BISECT_STEPS = 160


def _select_kernel(a_ref, tri_ref, slot_ref, slot_t_ref):
    a = a_ref[...]
    rows = a.shape[0]
    cap = float(CAP)

    def body(_, c):
        lo, hi = c
        mid = 0.5 * (lo + hi)
        cnt = jnp.sum(jnp.where(a > mid, 1.0, 0.0), axis=1, keepdims=True)
        ge = cnt >= cap
        return jnp.where(ge, mid, lo), jnp.where(ge, hi, mid)

    lo, hi = lax.fori_loop(0, BISECT_STEPS, body,
                           (jnp.full((rows, 1), -1.0, _F32), jnp.full((rows, 1), 1.0, _F32)))
    vstar = jnp.max(jnp.where(a <= hi, a, -1.0), axis=1, keepdims=True)
    gt = a > vstar
    eq = a == vstar
    need = cap - jnp.sum(jnp.where(gt, 1.0, 0.0), axis=1, keepdims=True)
    tri = tri_ref[...]
    eq_before = jnp.dot(jnp.where(eq, 1.0, 0.0).astype(_BF), tri, preferred_element_type=_F32)
    sel = jnp.where(gt, 1.0, jnp.where(eq, jnp.where(eq_before < need, 1.0, 0.0), 0.0))
    pos = jnp.dot(sel.astype(_BF), tri, preferred_element_type=_F32)
    slot = jnp.where(sel > 0.5, pos, -1.0)
    slot_ref[...] = slot.astype(jnp.int32)
    if rows < LANE:
        slot = jnp.concatenate([slot, jnp.full((LANE - rows, SEQ), -1.0, _F32)], axis=0)
    slot_tm = slot.T
    for b in range(rows // N_EXPERTS):
        shifted = slot_tm if b == 0 else pltpu.roll(slot_tm, LANE - N_EXPERTS * b, axis=1)
        slot_t_ref[b] = shifted.astype(jnp.int32)


def _select(aff2d):
    rows = aff2d.shape[0]
    assert rows <= LANE and rows % N_EXPERTS == 0
    bsz = rows // N_EXPERTS
    idx = jnp.arange(SEQ, dtype=jnp.int32)
    tri = (idx[:, None] < idx[None, :]).astype(_BF)
    return pl.pallas_call(
        _select_kernel, grid=(1,),
        in_specs=[pl.BlockSpec((rows, SEQ), lambda i: (0, 0)),
                  pl.BlockSpec((SEQ, SEQ), lambda i: (0, 0))],
        out_specs=[pl.BlockSpec((rows, SEQ), lambda i: (0, 0)),
                   pl.BlockSpec((bsz, SEQ, LANE), lambda i: (0, 0, 0))],
        out_shape=[jax.ShapeDtypeStruct((rows, SEQ), jnp.int32),
                   jax.ShapeDtypeStruct((bsz, SEQ, LANE), jnp.int32)],
        compiler_params=_cp(("arbitrary",), 48), name="moe_select")(aff2d, tri)


def _onehot(slot_row):
    return slot_row == lax.broadcasted_iota(jnp.int32, (CAP, SEQ), 0)


def _gather_kernel(slot_ref, a_ref, xb_ref, xe_ref, g_ref):
    hit = _onehot(slot_ref[0, 0])
    p = jnp.where(hit, 1.0, 0.0).astype(_BF)
    xe_ref[0] = jnp.dot(p, xb_ref[0], preferred_element_type=_F32).astype(_BF)
    g_ref[0] = jnp.sum(jnp.where(hit, a_ref[0, 0], 0.0), axis=1, keepdims=True)


def _gather(slot4, aff4, xb3d):
    b = xb3d.shape[0]
    row = pl.BlockSpec((1, 1, 1, SEQ), lambda i, e: (i, e, 0, 0))
    return pl.pallas_call(
        _gather_kernel, grid=(b, N_EXPERTS),
        in_specs=[row, row, pl.BlockSpec((1, SEQ, D_MODEL), lambda i, e: (i, 0, 0))],
        out_specs=[pl.BlockSpec((1, CAP, D_MODEL), lambda i, e: (e, i, 0)),
                   pl.BlockSpec((1, CAP, 1), lambda i, e: (e, i, 0))],
        out_shape=[jax.ShapeDtypeStruct((N_EXPERTS, b * CAP, D_MODEL), _BF),
                   jax.ShapeDtypeStruct((N_EXPERTS, b * CAP, 1), _F32)],
        compiler_params=_cp(("parallel", "parallel"), 32), name="moe_gather")(slot4, aff4, xb3d)


FFN_TF = 512
FFN_TM = 512


def _ffn_up_kernel(xe_ref, wg_ref, wu_ref, h_ref):
    wg = wg_ref[0].astype(_BF)
    wu = wu_ref[0].astype(_BF)
    for c in range(xe_ref.shape[1] // FFN_TM):
        rows = slice(c * FFN_TM, (c + 1) * FFN_TM)
        xe = xe_ref[0, rows, :]
        hg = _dot(xe, wg)
        hu = _dot(xe, wu)
        h_ref[0, rows, :] = ((hg * (1.0 / (1.0 + jnp.exp(-hg)))) * hu).astype(_BF)


def _ffn_down_kernel(h_ref, wd_ref, g_ref, ye_ref):
    wd = wd_ref[0].astype(_BF)
    for c in range(h_ref.shape[1] // FFN_TM):
        rows = slice(c * FFN_TM, (c + 1) * FFN_TM)
        ye_ref[0, rows, :] = (_dot(h_ref[0, rows, :], wd) * g_ref[0, rows, :]).astype(_BF)


def _expert_ffn(xe, gates, wg, wu, wd, layer):
    rows = xe.shape[1]
    h = pl.pallas_call(
        _ffn_up_kernel, grid=(N_EXPERTS, D_FF_EXPERT // FFN_TF),
        in_specs=[pl.BlockSpec((1, rows, D_MODEL), lambda e, f: (e, 0, 0)),
                  pl.BlockSpec((None, 1, D_MODEL, FFN_TF), lambda e, f: (layer, e, 0, f)),
                  pl.BlockSpec((None, 1, D_MODEL, FFN_TF), lambda e, f: (layer, e, 0, f))],
        out_specs=pl.BlockSpec((1, rows, FFN_TF), lambda e, f: (e, 0, f)),
        out_shape=jax.ShapeDtypeStruct((N_EXPERTS, rows, D_FF_EXPERT), _BF),
        compiler_params=_cp(("parallel", "parallel"), 40), name="moe_ffn_up")(xe, wg, wu)
    return pl.pallas_call(
        _ffn_down_kernel, grid=(N_EXPERTS, D_MODEL // FFN_TF),
        in_specs=[pl.BlockSpec((1, rows, D_FF_EXPERT), lambda e, n: (e, 0, 0)),
                  pl.BlockSpec((None, 1, D_FF_EXPERT, FFN_TF), lambda e, n: (layer, e, 0, n)),
                  pl.BlockSpec((1, rows, 1), lambda e, n: (e, 0, 0))],
        out_specs=pl.BlockSpec((1, rows, FFN_TF), lambda e, n: (e, 0, n)),
        out_shape=jax.ShapeDtypeStruct((N_EXPERTS, rows, D_MODEL), _BF),
        compiler_params=_cp(("parallel", "parallel"), 48), name="moe_ffn_down")(h, wd, gates)


SCAT_TM = 512


def _scatter_ln_kernel(slot_t_ref, ye_ref, x_ref, g_ref, b_ref, o_ref, ob_ref, pt_ref):
    slot_t = slot_t_ref[0]
    lane_c = lax.broadcasted_iota(jnp.int32, (SCAT_TM, CAP), 1)
    for e in range(N_EXPERTS):
        pt_ref[:, e * CAP:(e + 1) * CAP] = jnp.where(slot_t[:, e:e + 1] == lane_c, 1.0, 0.0).astype(_BF)
    f = jnp.dot(pt_ref[...], ye_ref[...].reshape(N_EXPERTS * CAP, D_MODEL), preferred_element_type=_F32)
    y = _layer_norm(DN_ALPHA * x_ref[0] + f, g_ref[...], b_ref[...])
    o_ref[0] = y
    ob_ref[0] = y.astype(_BF)


def _scatter_ln(slot_t, ye, x3d, g, b):
    bsz = x3d.shape[0]
    vec = pl.BlockSpec((1, D_MODEL), lambda i, r: (0, 0))
    seq = pl.BlockSpec((1, SCAT_TM, D_MODEL), lambda i, r: (i, r, 0))
    return pl.pallas_call(
        _scatter_ln_kernel, grid=(bsz, SEQ // SCAT_TM),
        in_specs=[pl.BlockSpec((1, SCAT_TM, LANE), lambda i, r: (i, r, 0)),
                  pl.BlockSpec((N_EXPERTS, CAP, D_MODEL), lambda i, r: (0, i, 0)),
                  seq, vec, vec],
        out_specs=[seq, seq],
        out_shape=[jax.ShapeDtypeStruct((bsz, SEQ, D_MODEL), _F32),
                   jax.ShapeDtypeStruct((bsz, SEQ, D_MODEL), _BF)],
        scratch_shapes=[pltpu.VMEM((SCAT_TM, N_EXPERTS * CAP), _BF)],
        compiler_params=_cp(("parallel", "arbitrary"), 48), name="moe_scatter_ln")(
            slot_t, ye, x3d, g.reshape(1, -1), b.reshape(1, -1))


def _moe_block(x1, x1b, w_router, wg, wu, wd, layer, g, b):
    bsz = x1.shape[0]
    aff = _router(x1, w_router)
    slot, slot_t = _select(aff.reshape(bsz * N_EXPERTS, SEQ))
    slot4 = slot.reshape(bsz, N_EXPERTS, 1, SEQ)
    aff4 = aff.reshape(bsz, N_EXPERTS, 1, SEQ)
    xe, gates = _gather(slot4, aff4, x1b)
    ye = _expert_ffn(xe, gates, wg, wu, wd, layer)
    return _scatter_ln(slot_t, ye, x1, g, b)


def _ple_kernel(x_ref, xb_ref, p_ref, wg_ref, wp_ref, o_ref):
    gate = _dot(xb_ref[...], wg_ref[...])
    emb = _dot(p_ref[...], wp_ref[...])
    o_ref[...] = x_ref[...] + (1.0 / (1.0 + jnp.exp(-gate))) * emb


def _ple(x2d, xb2d, p3d, wg, wp, layer):
    m = x2d.shape[0]
    tm = 512
    return pl.pallas_call(
        _ple_kernel, grid=(m // tm,),
        in_specs=[pl.BlockSpec((tm, D_MODEL), lambda i: (i, 0)),
                  pl.BlockSpec((tm, D_MODEL), lambda i: (i, 0)),
                  pl.BlockSpec((None, tm, PLE_DIM), lambda i: (layer, i, 0)),
                  pl.BlockSpec((None, D_MODEL, D_MODEL), lambda i: (layer, 0, 0)),
                  pl.BlockSpec((None, PLE_DIM, D_MODEL), lambda i: (layer, 0, 0))],
        out_specs=pl.BlockSpec((tm, D_MODEL), lambda i: (i, 0)),
        out_shape=jax.ShapeDtypeStruct((m, D_MODEL), _F32),
        compiler_params=_cp(("parallel",), 32), name="ple")(x2d, xb2d, p3d, wg, wp)


MLA_HB = LANE
MLA_IN_PAD = 768
MLA_KR_COL = MLA_Q_LORA + MLA_KV_LORA


def _rms(x, g):
    return x * lax.rsqrt(jnp.mean(x * x, axis=-1, keepdims=True) + NORM_EPS) * g


def _mla_pre_kernel(x_ref, win_ref, qg_ref, wq_ref, kvg_ref, wkv_ref, qa_ref, qb_ref, ka_ref, kb_ref,
                    q_ref, kv_ref, kr_ref):
    h = _dot(x_ref[...], win_ref[...])
    q = _dot(_rms(h[:, :MLA_Q_LORA], qg_ref[...]), wq_ref[...])
    swap = pltpu.roll(q, q.shape[1] - MLA_ROPE, axis=1)
    qa = jnp.concatenate([qa_ref[...]] * MLA_HEADS, axis=1)
    qb = jnp.concatenate([qb_ref[...]] * MLA_HEADS, axis=1)
    q_ref[...] = (q * qa + swap * qb).astype(_BF)
    kv_ref[...] = _dot(_rms(h[:, MLA_Q_LORA:MLA_KR_COL], kvg_ref[...]), wkv_ref[...]).astype(_BF)
    kr = h[:, MLA_KR_COL:MLA_KR_COL + MLA_HB]
    kr_ref[...] = (kr * ka_ref[...] + pltpu.roll(kr, MLA_HB - MLA_ROPE, axis=1) * kb_ref[...]).astype(_BF)


def _rope_lane_tables():
    inv = 1.0 / (ROPE_THETA ** (jnp.arange(0, MLA_ROPE, 2, dtype=_F32) / MLA_ROPE))
    ang = jnp.arange(SEQ, dtype=_F32)[:, None] * inv[None, :]
    cos, sin = jnp.cos(ang), jnp.sin(ang)
    ones = jnp.ones((SEQ, MLA_NOPE), _F32)
    zeros_n = jnp.zeros((SEQ, MLA_NOPE), _F32)
    zeros_r = jnp.zeros((SEQ, MLA_ROPE), _F32)
    cc = jnp.concatenate([cos, cos], axis=1)
    ss = jnp.concatenate([-sin, sin], axis=1)
    keep_q = jnp.concatenate([ones, cc, zeros_r], axis=1)
    keep_k = jnp.concatenate([zeros_n, cc, zeros_r], axis=1)
    swp = jnp.concatenate([zeros_n, ss, zeros_r], axis=1)
    return keep_q, keep_k, swp


def _mla_pre(x2d, w_in, q_norm, w_q_up, kv_norm, w_kv_up):
    m = x2d.shape[0]
    tm = 512
    half = MLA_ROPE // 2
    kr_w = w_in[:, MLA_KR_COL:]
    kr_swapped = jnp.concatenate([kr_w[:, half:], kr_w[:, :half]], axis=1)
    win = jnp.concatenate([w_in[:, :MLA_KR_COL], jnp.zeros((D_MODEL, MLA_NOPE), _F32), kr_w, kr_swapped], axis=1)
    wq = w_q_up.reshape(MLA_Q_LORA, MLA_HEADS, MLA_NOPE + MLA_ROPE)
    rope_w = wq[:, :, MLA_NOPE:]
    wq = jnp.concatenate([wq, rope_w[:, :, half:], rope_w[:, :, :half]], axis=2).reshape(MLA_Q_LORA, MLA_HEADS * MLA_HB)
    keep_q, keep_k, swp = _rope_lane_tables()
    qscale = (MLA_NOPE + MLA_ROPE) ** -0.5 * LOG2E
    nq = MLA_HEADS * MLA_HB
    pos = lambda: pl.BlockSpec((tm, MLA_HB), lambda i: (i % (SEQ // tm), 0))
    full = lambda a: pl.BlockSpec(a.shape, lambda i: (0,) * a.ndim)
    qg, kvg = q_norm.reshape(1, -1), kv_norm.reshape(1, -1)
    return pl.pallas_call(
        _mla_pre_kernel, grid=(m // tm,),
        in_specs=[pl.BlockSpec((tm, D_MODEL), lambda i: (i, 0)), full(win), full(qg), full(wq), full(kvg),
                  full(w_kv_up), pos(), pos(), pos(), pos()],
        out_specs=[pl.BlockSpec((tm, nq), lambda i: (i, 0)),
                   pl.BlockSpec((tm, nq), lambda i: (i, 0)),
                   pl.BlockSpec((tm, MLA_HB), lambda i: (i, 0))],
        out_shape=[jax.ShapeDtypeStruct((m, nq), _BF), jax.ShapeDtypeStruct((m, nq), _BF),
                   jax.ShapeDtypeStruct((m, MLA_HB), _BF)],
        compiler_params=_cp(("parallel",), 48), name="mla_pre")(
            x2d, win, qg, wq, kvg, w_kv_up, keep_q * qscale, swp * qscale, keep_k, swp)


MLA_TQ = 256
MLA_PAIR = 2


def _mla_attn_kernel(q_ref, kv_ref, kr_ref, o_ref, k_scr, v_scr):
    lane = lax.broadcasted_iota(jnp.int32, (SEQ, MLA_HB), 1)
    kr = kr_ref[0].astype(_F32)
    for j in range(MLA_PAIR):
        kvh = kv_ref[0, :, j * MLA_HB:(j + 1) * MLA_HB].astype(_F32)
        k_scr[j] = jnp.where(lane < MLA_NOPE, kvh, kr).astype(_BF)
        v_scr[j] = jnp.where(lane < MLA_NOPE, 1.0, kvh).astype(_BF)
    lane_q = lax.broadcasted_iota(jnp.int32, (MLA_TQ, MLA_HB), 1)

    def body(i, carry):
        rows = pl.ds(pl.multiple_of(i * MLA_TQ, MLA_TQ), MLA_TQ)
        outs = []
        for j in range(MLA_PAIR):
            qb = q_ref[0, rows, j * MLA_HB:(j + 1) * MLA_HB]
            s = _dot_nt(qb, k_scr[j])
            p = jnp.exp2(s - jnp.max(s, axis=1, keepdims=True))
            o = _dot(p, v_scr[j])
            outs.append(o / o[:, 0:1])
        even = pltpu.roll(outs[0], MLA_V, axis=1)
        o_ref[0, rows, :] = jnp.where(lane_q < MLA_V, even, outs[1]).astype(_BF)
        return carry

    lax.fori_loop(0, SEQ // MLA_TQ, body, 0, unroll=2)


def _mla_attention(q, kv, kr):
    b = q.shape[0]
    w = MLA_PAIR * MLA_HB
    return pl.pallas_call(
        _mla_attn_kernel, grid=(b, MLA_HEADS // MLA_PAIR),
        in_specs=[pl.BlockSpec((1, SEQ, w), lambda i, h: (i, 0, h)),
                  pl.BlockSpec((1, SEQ, w), lambda i, h: (i, 0, h)),
                  pl.BlockSpec((1, SEQ, MLA_HB), lambda i, h: (i, 0, 0))],
        out_specs=pl.BlockSpec((1, SEQ, MLA_PAIR * MLA_V), lambda i, h: (i, 0, h)),
        out_shape=jax.ShapeDtypeStruct((b, SEQ, MLA_HEADS * MLA_V), _BF),
        scratch_shapes=[pltpu.VMEM((MLA_PAIR, SEQ, MLA_HB), _BF)] * 2,
        compiler_params=_cp(("parallel", "parallel"), 40), name="mla_attention")(q, kv, kr)


def _na_hyena_mixer(x, w_in, rpb, conv_w, conv_b, f_w1, f_b1, f_freq, f_w2, f_b2, f_w3, skip):
    b = x.shape[0]
    qkv, hb = _inproj(x.reshape(b * SEQ, D_MODEL), w_in)
    y_a = _na_attention(qkv.reshape(b, SEQ, 3 * NA_WIDTH), rpb)
    hbc, zb = _short_conv(hb.reshape(b, SEQ, (HY_ORDER + 1) * HY_WIDTH), conv_w, conv_b)
    fmat = _dft_matrix()
    kfilt = _hyena_filters(f_w1, f_b1, f_freq, f_w2, f_b2, f_w3)
    kf = _filter_spectrum(fmat, kfilt)
    fmat_t = fmat.T
    z = hbc
    for o in range(HY_ORDER):
        z, zb = _long_conv_gate(hbc, z, zb, o + 1, o, fmat, fmat_t, kf, skip)
    return y_a.reshape(b * SEQ, NA_WIDTH), zb.reshape(b * SEQ, HY_WIDTH)


def kernel(x, p, ab_w_in, na_rpb, hy_conv_w, hy_conv_b, hy_f_w1, hy_f_b1, hy_f_freq, hy_f_w2, hy_f_b2, hy_f_w3, hy_skip, ab_w_out, mla_w_in, mla_q_norm, mla_w_q_up, mla_kv_norm, mla_w_kv_up, mla_w_out, ln1_g, ln1_b, ln2_g, ln2_b, moe_router, moe_w_gate, moe_w_up, moe_w_down, ple_gate, ple_proj):
    b = x.shape[0]
    m = b * SEQ
    for i in range(DEPTH):
        j = i // 2
        x2d = x.reshape(m, D_MODEL)
        if i % 2 == 0:
            y_a, z = _na_hyena_mixer(x, ab_w_in[j], na_rpb[j], hy_conv_w[j], hy_conv_b[j], hy_f_w1[j],
                                     hy_f_b1[j], hy_f_freq[j], hy_f_w2[j], hy_f_b2[j], hy_f_w3[j], hy_skip[j])
            x1, x1b = _outproj_ln([y_a, z], ab_w_out[j], x2d, ln1_g[i], ln1_b[i])
        else:
            q, kv, kr = _mla_pre(x2d, mla_w_in[j], mla_q_norm[j], mla_w_q_up[j], mla_kv_norm[j], mla_w_kv_up[j])
            att = _mla_attention(q.reshape(b, SEQ, -1), kv.reshape(b, SEQ, -1), kr.reshape(b, SEQ, -1))
            x1, x1b = _outproj_ln([att.reshape(m, MLA_HEADS * MLA_V)], mla_w_out[j], x2d, ln1_g[i], ln1_b[i])
        x2, x2b = _moe_block(x1.reshape(b, SEQ, D_MODEL), x1b.reshape(b, SEQ, D_MODEL), moe_router[i],
                             moe_w_gate, moe_w_up, moe_w_down, i, ln2_g[i], ln2_b[i])
        x = _ple(x2.reshape(m, D_MODEL), x2b.reshape(m, D_MODEL), p.reshape(DEPTH, m, PLE_DIM),
                 ple_gate, ple_proj, i).reshape(b, SEQ, D_MODEL)
    return x
```

```python
import functools
import math

import numpy as np
import jax
import jax.numpy as jnp
from jax import lax
from jax.experimental import pallas as pl
from jax.experimental.pallas import tpu as pltpu

D_MODEL = 1024
BATCH = 8
SEQ = 2048
DEPTH = 2
GRID_W = 64
PLE_DIM = 256
NA_HEADS = 8
NA_HEAD_DIM = 64
NA_WIDTH = NA_HEADS * NA_HEAD_DIM
NA_WIN_ROWS_MAX = 8
NA_WIN_COLS = 16
HY_WIDTH = D_MODEL - NA_WIDTH
HY_ORDER = 2
HY_SHORT_K = 3
HY_EMB_DIM = 33
HY_FILTER_HIDDEN = 64
HY_FAST_DECAY_PCT = 0.3
HY_SLOW_DECAY_PCT = 1.5
HY_DECAY_TARGET = 1e-2
AB_IN_WIDTH = 3 * NA_WIDTH + (HY_ORDER + 1) * HY_WIDTH
MLA_HEADS = 16
MLA_Q_LORA = 384
MLA_KV_LORA = 256
MLA_NOPE = 64
MLA_ROPE = 32
MLA_V = 64
ROPE_THETA = 10000.0
N_EXPERTS = 16
EC_CAPACITY_FACTOR = 2
D_FF_EXPERT = 2048
DN_ALPHA = (2 * DEPTH) ** 0.25
NORM_EPS = 1e-5
NEG_INF = -1e30

CAP = EC_CAPACITY_FACTOR * SEQ // N_EXPERTS
NA_ROWS = SEQ // GRID_W
NA_WIN_ROWS = min(NA_WIN_ROWS_MAX, NA_ROWS)
NA_SLAB = NA_WIN_ROWS * GRID_W
FFT_N = 2 * SEQ
LOG2E = math.log2(math.e)

LANE = 128
MIB = 1 << 20

_BF = jnp.bfloat16
_F32 = jnp.float32
_HP = lax.Precision.HIGHEST


def _cp(sem, vmem_mib):
    return pltpu.CompilerParams(dimension_semantics=sem, vmem_limit_bytes=vmem_mib * MIB)


def _dot(a, b):
    return jnp.dot(a.astype(_BF), b.astype(_BF), preferred_element_type=_F32)


def _dot_nt(a, b):
    return lax.dot_general(a.astype(_BF), b.astype(_BF), (((1,), (1,)), ((), ())),
                           preferred_element_type=_F32)


def _layer_norm(y, g, b):
    mu = jnp.mean(y, axis=-1, keepdims=True)
    d = y - mu
    var = jnp.mean(d * d, axis=-1, keepdims=True)
    return d * lax.rsqrt(var + NORM_EPS) * g + b


def _inproj_kernel(x_ref, w_ref, qkv_ref, hb_ref):
    h = _dot(x_ref[...], w_ref[...])
    qkv_ref[...] = h[:, :3 * NA_WIDTH].astype(_BF)
    hb_ref[...] = h[:, 3 * NA_WIDTH:]


def _inproj(x2d, w):
    m = x2d.shape[0]
    tm = 512
    nq, nh = 3 * NA_WIDTH, (HY_ORDER + 1) * HY_WIDTH
    return pl.pallas_call(
        _inproj_kernel, grid=(m // tm,),
        in_specs=[pl.BlockSpec((tm, D_MODEL), lambda i: (i, 0)),
                  pl.BlockSpec((D_MODEL, AB_IN_WIDTH), lambda i: (0, 0))],
        out_specs=[pl.BlockSpec((tm, nq), lambda i: (i, 0)),
                   pl.BlockSpec((tm, nh), lambda i: (i, 0))],
        out_shape=[jax.ShapeDtypeStruct((m, nq), _BF), jax.ShapeDtypeStruct((m, nh), _F32)],
        compiler_params=_cp(("parallel",), 48), name="ab_inproj")(x2d, w)


NA_GROUP = 4
NA_GW = NA_GROUP * NA_HEAD_DIM


def _na_kernel(q_ref, k_ref, v_ref, tbl_ref, o_ref):
    lane_head = lax.broadcasted_iota(jnp.int32, (GRID_W, NA_GW), 1) >> int(math.log2(NA_HEAD_DIM))
    scale = NA_HEAD_DIM ** -0.5 * LOG2E

    def body(r, carry):
        r0 = jnp.clip(r - NA_WIN_ROWS // 2, 0, NA_ROWS - NA_WIN_ROWS)
        off = r0 - r + (NA_WIN_ROWS_MAX - 1)
        qr = q_ref[0, pl.ds(pl.multiple_of(r * GRID_W, GRID_W), GRID_W), :].astype(_F32)
        q4 = jnp.concatenate([jnp.where(lane_head == h, qr, 0.0) for h in range(NA_GROUP)], axis=0)
        ks = k_ref[0, pl.ds(pl.multiple_of(r0 * GRID_W, GRID_W), NA_SLAB), :]
        vs = v_ref[0, pl.ds(pl.multiple_of(r0 * GRID_W, GRID_W), NA_SLAB), :]
        bias = jnp.concatenate([tbl_ref[h, off] for h in range(NA_GROUP)], axis=0)
        s = _dot_nt(q4, ks) * scale + bias
        mx = jnp.max(s, axis=1, keepdims=True)
        p = jnp.exp2(s - mx)
        den = jnp.sum(p, axis=1, keepdims=True)
        o4 = _dot(p, vs) / den
        out = jnp.zeros((GRID_W, NA_GW), _F32)
        for h in range(NA_GROUP):
            out = out + jnp.where(lane_head == h, o4[h * GRID_W:(h + 1) * GRID_W], 0.0)
        o_ref[0, pl.ds(pl.multiple_of(r * GRID_W, GRID_W), GRID_W), :] = out.astype(_BF)
        return carry

    lax.fori_loop(0, NA_ROWS, body, 0, unroll=4)


def _na_bias_table(rpb):
    c = np.arange(GRID_W)
    c0 = np.clip(c - NA_WIN_COLS // 2, 0, GRID_W - NA_WIN_COLS)
    kc = np.arange(GRID_W)
    col_ok = (kc[None, :] >= c0[:, None]) & (kc[None, :] < c0[:, None] + NA_WIN_COLS)
    dc_idx = np.clip(kc[None, :] - c[:, None], -(NA_WIN_COLS - 1), NA_WIN_COLS - 1) + (NA_WIN_COLS - 1)
    pick = (dc_idx[None, :, :] == np.arange(2 * NA_WIN_COLS - 1)[:, None, None]).astype(np.float32)
    per_dr = jnp.einsum("hdk,kqc->hdqc", rpb.astype(_F32), pick, precision=_HP)
    per_dr = jnp.where(col_ok[None, None], per_dr * LOG2E, NEG_INF)
    slabs = jnp.stack([per_dr[:, off:off + NA_WIN_ROWS] for off in range(NA_WIN_ROWS_MAX)], axis=1)
    return slabs.transpose(0, 1, 3, 2, 4).reshape(NA_HEADS, NA_WIN_ROWS_MAX, GRID_W, NA_SLAB)


def _na_attention(qkv, rpb):
    b = qkv.shape[0]
    tbl = _na_bias_table(rpb)
    ng = NA_HEADS // NA_GROUP
    blk = lambda col0: pl.BlockSpec((1, SEQ, NA_GW), lambda i, g, c=col0: (i, 0, c + g))
    return pl.pallas_call(
        _na_kernel, grid=(b, ng),
        in_specs=[blk(0), blk(ng), blk(2 * ng),
                  pl.BlockSpec((NA_GROUP, NA_WIN_ROWS_MAX, GRID_W, NA_SLAB), lambda i, g: (g, 0, 0, 0))],
        out_specs=pl.BlockSpec((1, SEQ, NA_GW), lambda i, g: (i, 0, g)),
        out_shape=jax.ShapeDtypeStruct((b, SEQ, NA_WIDTH), _BF),
        compiler_params=_cp(("parallel", "parallel"), 32), name="na_attention")(qkv, qkv, qkv, tbl)


def _sconv_kernel(x_ref, w_ref, b_ref, o_ref, vb_ref):
    x = x_ref[0]
    n = x.shape[0]
    row = lax.broadcasted_iota(jnp.int32, x.shape, 0)
    xm = jnp.where(row == 0, 0.0, pltpu.roll(x, 1, axis=0))
    xp = jnp.where(row == n - 1, 0.0, pltpu.roll(x, n - 1, axis=0))
    y = b_ref[...] + xm * w_ref[0:1, :] + x * w_ref[1:2, :] + xp * w_ref[2:3, :]
    o_ref[0] = y

    @pl.when(pl.program_id(1) == 0)
    def _():
        vb_ref[0] = y.astype(_BF)


def _short_conv(hb, w, bias):
    b, n, c = hb.shape
    tc = HY_WIDTH
    return pl.pallas_call(
        _sconv_kernel, grid=(b, c // tc),
        in_specs=[pl.BlockSpec((1, n, tc), lambda i, j: (i, 0, j)),
                  pl.BlockSpec((HY_SHORT_K, tc), lambda i, j: (0, j)),
                  pl.BlockSpec((1, tc), lambda i, j: (0, j))],
        out_specs=[pl.BlockSpec((1, n, tc), lambda i, j: (i, 0, j)),
                   pl.BlockSpec((1, n, tc), lambda i, j: (i, 0, 0))],
        out_shape=[jax.ShapeDtypeStruct((b, n, c), _F32), jax.ShapeDtypeStruct((b, n, tc), _BF)],
        compiler_params=_cp(("parallel", "arbitrary"), 32), name="hy_short_conv")(hb, w, bias.reshape(1, c))


HY_HID_PAD = LANE
HY_FILT_TC = 256


def _filter_kernel(z_ref, w1_ref, b1_ref, fr_ref, w2_ref, b2_ref, w3f_ref, w3b_ref, dec_ref, k_ref, h_ref):
    @pl.when(jnp.logical_and(pl.program_id(0) == 0, pl.program_id(1) == 0))
    def _():
        fr = fr_ref[...]
        h1 = jnp.sin(fr * (jnp.dot(z_ref[...], w1_ref[...], precision=_HP, preferred_element_type=_F32) + b1_ref[...]))
        h_ref[...] = jnp.sin(fr * (jnp.dot(h1, w2_ref[...], precision=_HP, preferred_element_type=_F32) + b2_ref[...]))

    fwd = jnp.dot(h_ref[:SEQ, :], w3f_ref[...], precision=_HP, preferred_element_type=_F32)
    bwd = jnp.dot(h_ref[SEQ:, :], w3b_ref[...], precision=_HP, preferred_element_type=_F32)
    k = jnp.concatenate([fwd, bwd], axis=0) * dec_ref[...]
    ss = jnp.sum(k * k, axis=0, keepdims=True)
    k_ref[...] = (k * lax.rsqrt(ss + 1e-12)).astype(_BF)


def _filter_tables():
    bands = (HY_EMB_DIM - 1) // 2
    t = jnp.linspace(0.0, 1.0, SEQ, dtype=_F32)[:, None]
    w = 2.0 * math.pi * jnp.arange(SEQ, dtype=_F32)[:, None] / SEQ
    f = jnp.linspace(1e-4, bands - 1, bands, dtype=_F32)[None, :]
    z = jnp.concatenate([t, jnp.cos(f * w), -jnp.sin(f * w)], axis=-1)
    min_decay = math.log(HY_DECAY_TARGET) / HY_SLOW_DECAY_PCT
    max_decay = math.log(HY_DECAY_TARGET) / HY_FAST_DECAY_PCT
    deltas = jnp.abs(jnp.linspace(min_decay, max_decay, HY_WIDTH, dtype=_F32))
    dec = jnp.exp(-t * deltas)
    src = np.concatenate([np.arange(SEQ), [0], np.arange(SEQ - 1, 0, -1)])
    live = np.ones((FFT_N, 1), np.float32)
    live[SEQ] = 0.0
    z2 = jnp.pad(z[src], ((0, 0), (0, HY_HID_PAD - HY_EMB_DIM)))
    dec2 = dec[src] * live
    return z2, dec2


def _hyena_filters(w1, b1, freq, w2, b2, w3):
    z2, dec2 = _filter_tables()
    hp = HY_HID_PAD - HY_FILTER_HIDDEN
    w1p = jnp.pad(w1, ((0, HY_HID_PAD - HY_EMB_DIM), (0, hp)))
    w2p = jnp.pad(w2, ((0, hp), (0, hp)))
    w3p = jnp.pad(w3, ((0, hp), (0, 0)))
    row = lambda v: jnp.pad(v, (0, hp)).reshape(1, HY_HID_PAD)
    nc = HY_WIDTH // HY_FILT_TC
    per_order = 2 * nc
    full = lambda shape: pl.BlockSpec(shape, lambda o, j: (0, 0))
    return pl.pallas_call(
        _filter_kernel, grid=(HY_ORDER, nc),
        in_specs=[full((FFT_N, HY_HID_PAD)), full((HY_HID_PAD, HY_HID_PAD)), full((1, HY_HID_PAD)),
                  full((1, HY_HID_PAD)), full((HY_HID_PAD, HY_HID_PAD)), full((1, HY_HID_PAD)),
                  pl.BlockSpec((HY_HID_PAD, HY_FILT_TC), lambda o, j: (0, o * per_order + j)),
                  pl.BlockSpec((HY_HID_PAD, HY_FILT_TC), lambda o, j: (0, o * per_order + nc + j)),
                  pl.BlockSpec((FFT_N, HY_FILT_TC), lambda o, j: (0, j))],
        out_specs=pl.BlockSpec((FFT_N, HY_FILT_TC), lambda o, j: (0, o * nc + j)),
        out_shape=jax.ShapeDtypeStruct((FFT_N, HY_ORDER * HY_WIDTH), _BF),
        scratch_shapes=[pltpu.VMEM((FFT_N, HY_HID_PAD), _F32)],
        compiler_params=_cp(("arbitrary", "arbitrary"), 40), name="hy_filters")(
            z2, w1p, row(b1), row(freq), w2p, row(b2), w3p, w3p, dec2)


def _dft_matrix():
    hi, lo = SEQ // 64, 64
    t = np.arange(SEQ)
    m_hi = (np.arange(hi)[:, None] * lo * t[None, :]) % FFT_N
    m_lo = (np.arange(lo)[:, None] * t[None, :]) % FFT_N
    ang = lambda m: jnp.asarray(m, _F32) * (2.0 * math.pi / FFT_N)
    c1, s1 = jnp.cos(ang(m_hi))[:, None, :], jnp.sin(ang(m_hi))[:, None, :]
    c0, s0 = jnp.cos(ang(m_lo))[None, :, :], jnp.sin(ang(m_lo))[None, :, :]
    re = (c1 * c0 - s1 * s0).reshape(SEQ, SEQ)
    im = -(s1 * c0 + c1 * s0).reshape(SEQ, SEQ)
    nyq = jnp.asarray(1.0 - 2.0 * (t % 2), _F32)
    im = im.at[0].set(nyq)
    packed = jnp.stack([re.reshape(SEQ // HY_FB, HY_FB, SEQ), im.reshape(SEQ // HY_FB, HY_FB, SEQ)], axis=1)
    return packed.reshape(FFT_N, SEQ).astype(_BF)


def _kf_kernel(f_ref, k_ref, o_ref):
    i = pl.program_id(0)
    f = f_ref[...]
    tm = f.shape[0]
    p1 = jnp.dot(f, k_ref[:SEQ, :], preferred_element_type=_F32)
    p2 = jnp.dot(f, k_ref[SEQ:, :], preferred_element_type=_F32)
    row = lax.broadcasted_iota(jnp.int32, (tm, 1), 0) + i * tm
    sign = (1 - 2 * (row & 1)).astype(_F32)
    o_ref[...] = p1 + sign * p2


def _filter_spectrum(fmat, kfilt):
    tm = 512
    nw = kfilt.shape[1]
    return pl.pallas_call(
        _kf_kernel, grid=(FFT_N // tm,),
        in_specs=[pl.BlockSpec((tm, SEQ), lambda i: (i, 0)),
                  pl.BlockSpec((FFT_N, nw), lambda i: (0, 0))],
        out_specs=pl.BlockSpec((tm, nw), lambda i: (i, 0)),
        out_shape=jax.ShapeDtypeStruct((FFT_N, nw), _F32),
        compiler_params=_cp(("parallel",), 40), name="hy_filter_spectrum")(fmat, kfilt)


HY_FB = 512
HY_NB = 2


HY_TM = 512


def _hconv_fwd_kernel(zb_ref, f_ref, k_ref, y_ref):
    fk = pl.program_id(1)
    kr = k_ref[:HY_FB, :]
    ki = k_ref[HY_FB:, :]
    row0 = jnp.logical_and(lax.broadcasted_iota(jnp.int32, kr.shape, 0) == 0, fk == 0)
    sc = jnp.where(row0, 1.0 / FFT_N, 2.0 / FFT_N)
    for j in range(HY_NB):
        u = jnp.dot(f_ref[...], zb_ref[j], preferred_element_type=_F32)
        ur = u[:HY_FB]
        ui = u[HY_FB:]
        yr = jnp.where(row0, ur * kr, ur * kr - ui * ki)
        yi = jnp.where(row0, ui * ki, ur * ki + ui * kr)
        y_ref[j, 0, 0] = (yr * sc).astype(_BF)
        y_ref[j, 0, 1] = (yi * sc).astype(_BF)


def _hconv_inv_kernel(ft_ref, y_ref, z_ref, skip_ref, xn_ref, o_ref, ob_ref):
    for j in range(HY_NB):
        conv = jnp.dot(ft_ref[...], y_ref[j], preferred_element_type=_F32)
        out = xn_ref[j] * (conv + z_ref[j] * skip_ref[...])
        o_ref[j] = out
        ob_ref[j] = out.astype(_BF)


def _long_conv_gate(hbc, zsrc, zb, xn_col, order, fmat, fmat_t, kf, skip):
    b = hbc.shape[0]
    nf = SEQ // HY_FB
    w = HY_WIDTH
    nb = HY_NB
    y = pl.pallas_call(
        _hconv_fwd_kernel, grid=(b // nb, nf),
        in_specs=[pl.BlockSpec((nb, SEQ, w), lambda i, f: (i, 0, 0)),
                  pl.BlockSpec((2 * HY_FB, SEQ), lambda i, f: (f, 0)),
                  pl.BlockSpec((2 * HY_FB, w), lambda i, f: (f, order))],
        out_specs=pl.BlockSpec((nb, 1, 2, HY_FB, w), lambda i, f: (i, f, 0, 0, 0)),
        out_shape=jax.ShapeDtypeStruct((b, nf, 2, HY_FB, w), _BF),
        compiler_params=_cp(("parallel", "parallel"), 40), name=f"hy_conv_fwd{order}")(zb, fmat, kf)
    blk = lambda col: pl.BlockSpec((nb, HY_TM, w), lambda i, m, c=col: (i, m, c))
    return pl.pallas_call(
        _hconv_inv_kernel, grid=(b // nb, SEQ // HY_TM),
        in_specs=[pl.BlockSpec((HY_TM, FFT_N), lambda i, m: (m, 0)),
                  pl.BlockSpec((nb, FFT_N, w), lambda i, m: (i, 0, 0)),
                  blk(0),
                  pl.BlockSpec((1, w), lambda i, m: (0, 0)),
                  blk(xn_col)],
        out_specs=[blk(0), blk(0)],
        out_shape=[jax.ShapeDtypeStruct((b, SEQ, w), _F32), jax.ShapeDtypeStruct((b, SEQ, w), _BF)],
        compiler_params=_cp(("parallel", "parallel"), 48), name=f"hy_conv_inv{order}")(
            fmat_t, y.reshape(b, FFT_N, w), zsrc, skip[order].reshape(1, w), hbc)


def _outproj_ln_kernel(n_a, *refs):
    a_refs = refs[:n_a]
    w_refs = refs[n_a:2 * n_a]
    x_ref, g_ref, b_ref, wr_ref, o_ref, ob_ref, aff_ref = refs[2 * n_a:]
    m = _dot(a_refs[0][...], w_refs[0][...])
    for a_ref, w_ref in zip(a_refs[1:], w_refs[1:]):
        m = m + _dot(a_ref[...], w_ref[...])
    y = _layer_norm(DN_ALPHA * x_ref[...] + m, g_ref[...], b_ref[...])
    o_ref[...] = y
    ob_ref[...] = y.astype(_BF)
    logits = lax.dot_general(wr_ref[...], y, (((1,), (1,)), ((), ())),
                             precision=_HP, preferred_element_type=_F32)
    ex = jnp.exp(logits - jnp.max(logits, axis=0, keepdims=True))
    aff_ref[0] = ex / jnp.sum(ex, axis=0, keepdims=True)


def _outproj_ln(a_list, w, x2d, g, b, w_router):
    m = x2d.shape[0]
    tm = 512
    per_seq = SEQ // tm
    in_specs, w_args, row0 = [], [], 0
    for a in a_list:
        in_specs.append(pl.BlockSpec((tm, a.shape[1]), lambda i: (i, 0)))
    for a in a_list:
        ka = a.shape[1]
        in_specs.append(pl.BlockSpec((ka, D_MODEL), lambda i, r=row0 // ka: (r, 0)))
        w_args.append(w)
        row0 += ka
    in_specs += [pl.BlockSpec((tm, D_MODEL), lambda i: (i, 0)),
                 pl.BlockSpec((1, D_MODEL), lambda i: (0, 0)),
                 pl.BlockSpec((1, D_MODEL), lambda i: (0, 0)),
                 pl.BlockSpec((N_EXPERTS, D_MODEL), lambda i: (0, 0))]
    return pl.pallas_call(
        functools.partial(_outproj_ln_kernel, len(a_list)), grid=(m // tm,),
        in_specs=in_specs,
        out_specs=[pl.BlockSpec((tm, D_MODEL), lambda i: (i, 0)),
                   pl.BlockSpec((tm, D_MODEL), lambda i: (i, 0)),
                   pl.BlockSpec((1, N_EXPERTS, tm), lambda i: (i // per_seq, 0, i % per_seq))],
        out_shape=[jax.ShapeDtypeStruct((m, D_MODEL), _F32), jax.ShapeDtypeStruct((m, D_MODEL), _BF),
                   jax.ShapeDtypeStruct((m // SEQ, N_EXPERTS, SEQ), _F32)],
        compiler_params=_cp(("parallel",), 40), name="outproj_ln_router")(
            *a_list, *w_args, x2d, g.reshape(1, -1), b.reshape(1, -1), w_router.T)


BISECT_STEPS = 160


def _select_kernel(a_ref, tri_ref, slot_ref, slot_t_ref):
    a = a_ref[...]
    rows = a.shape[0]
    cap = float(CAP)

    def body(_, c):
        lo, hi = c
        mid = 0.5 * (lo + hi)
        cnt = jnp.sum(jnp.where(a > mid, 1.0, 0.0), axis=1, keepdims=True)
        ge = cnt >= cap
        return jnp.where(ge, mid, lo), jnp.where(ge, hi, mid)

    lo, hi = lax.fori_loop(0, BISECT_STEPS, body,
                           (jnp.full((rows, 1), -1.0, _F32), jnp.full((rows, 1), 1.0, _F32)))
    vstar = jnp.max(jnp.where(a <= hi, a, -1.0), axis=1, keepdims=True)
    gt = a > vstar
    eq = a == vstar
    need = cap - jnp.sum(jnp.where(gt, 1.0, 0.0), axis=1, keepdims=True)
    tri = tri_ref[...]
    eq_before = jnp.dot(jnp.where(eq, 1.0, 0.0).astype(_BF), tri, preferred_element_type=_F32)
    sel = jnp.where(gt, 1.0, jnp.where(eq, jnp.where(eq_before < need, 1.0, 0.0), 0.0))
    pos = jnp.dot(sel.astype(_BF), tri, preferred_element_type=_F32)
    slot = jnp.where(sel > 0.5, pos, -1.0)
    slot_ref[...] = slot.astype(jnp.int32)
    if rows < LANE:
        slot = jnp.concatenate([slot, jnp.full((LANE - rows, SEQ), -1.0, _F32)], axis=0)
    slot_tm = slot.T
    for b in range(rows // N_EXPERTS):
        shifted = slot_tm if b == 0 else pltpu.roll(slot_tm, LANE - N_EXPERTS * b, axis=1)
        slot_t_ref[b] = shifted.astype(jnp.int32)


def _select(aff2d):
    rows = aff2d.shape[0]
    assert rows <= LANE and rows % N_EXPERTS == 0
    bsz = rows // N_EXPERTS
    idx = jnp.arange(SEQ, dtype=jnp.int32)
    tri = (idx[:, None] < idx[None, :]).astype(_BF)
    return pl.pallas_call(
        _select_kernel, grid=(1,),
        in_specs=[pl.BlockSpec((rows, SEQ), lambda i: (0, 0)),
                  pl.BlockSpec((SEQ, SEQ), lambda i: (0, 0))],
        out_specs=[pl.BlockSpec((rows, SEQ), lambda i: (0, 0)),
                   pl.BlockSpec((bsz, SEQ, LANE), lambda i: (0, 0, 0))],
        out_shape=[jax.ShapeDtypeStruct((rows, SEQ), jnp.int32),
                   jax.ShapeDtypeStruct((bsz, SEQ, LANE), jnp.int32)],
        compiler_params=_cp(("arbitrary",), 48), name="moe_select")(aff2d, tri)


def _onehot(slot_row):
    return slot_row == lax.broadcasted_iota(jnp.int32, (CAP, SEQ), 0)


GATHER_NE = 4


def _gather_kernel(slot_ref, a_ref, xb_ref, xe_ref, g_ref):
    hits = [_onehot(slot_ref[0, e]) for e in range(GATHER_NE)]
    p = jnp.concatenate([jnp.where(h, 1.0, 0.0).astype(_BF) for h in hits], axis=0)
    xe = jnp.dot(p, xb_ref[0], preferred_element_type=_F32).astype(_BF)
    xe_ref[...] = xe.reshape(GATHER_NE, CAP, D_MODEL)
    for e in range(GATHER_NE):
        g_ref[e] = jnp.sum(jnp.where(hits[e], a_ref[0, e], 0.0), axis=1, keepdims=True)


def _gather(slot4, aff4, xb3d):
    b = xb3d.shape[0]
    row = pl.BlockSpec((1, GATHER_NE, 1, SEQ), lambda i, e: (i, e, 0, 0))
    return pl.pallas_call(
        _gather_kernel, grid=(b, N_EXPERTS // GATHER_NE),
        in_specs=[row, row, pl.BlockSpec((1, SEQ, D_MODEL), lambda i, e: (i, 0, 0))],
        out_specs=[pl.BlockSpec((GATHER_NE, CAP, D_MODEL), lambda i, e: (e, i, 0)),
                   pl.BlockSpec((GATHER_NE, CAP, 1), lambda i, e: (e, i, 0))],
        out_shape=[jax.ShapeDtypeStruct((N_EXPERTS, b * CAP, D_MODEL), _BF),
                   jax.ShapeDtypeStruct((N_EXPERTS, b * CAP, 1), _F32)],
        compiler_params=_cp(("parallel", "parallel"), 40), name="moe_gather")(slot4, aff4, xb3d)


FFN_TF = 512
FFN_TM = 512


def _ffn_up_kernel(xe_ref, wg_ref, wu_ref, h_ref):
    wg = wg_ref[0].astype(_BF)
    wu = wu_ref[0].astype(_BF)
    for c in range(xe_ref.shape[1] // FFN_TM):
        rows = slice(c * FFN_TM, (c + 1) * FFN_TM)
        xe = xe_ref[0, rows, :]
        hg = _dot(xe, wg)
        hu = _dot(xe, wu)
        h_ref[0, rows, :] = ((hg * (1.0 / (1.0 + jnp.exp(-hg)))) * hu).astype(_BF)


def _ffn_down_kernel(h_ref, wd_ref, g_ref, ye_ref):
    wd = wd_ref[0].astype(_BF)
    for c in range(h_ref.shape[1] // FFN_TM):
        rows = slice(c * FFN_TM, (c + 1) * FFN_TM)
        ye_ref[0, rows, :] = (_dot(h_ref[0, rows, :], wd) * g_ref[0, rows, :]).astype(_BF)


def _expert_ffn(xe, gates, wg, wu, wd, layer):
    rows = xe.shape[1]
    h = pl.pallas_call(
        _ffn_up_kernel, grid=(N_EXPERTS, D_FF_EXPERT // FFN_TF),
        in_specs=[pl.BlockSpec((1, rows, D_MODEL), lambda e, f: (e, 0, 0)),
                  pl.BlockSpec((None, 1, D_MODEL, FFN_TF), lambda e, f: (layer, e, 0, f)),
                  pl.BlockSpec((None, 1, D_MODEL, FFN_TF), lambda e, f: (layer, e, 0, f))],
        out_specs=pl.BlockSpec((1, rows, FFN_TF), lambda e, f: (e, 0, f)),
        out_shape=jax.ShapeDtypeStruct((N_EXPERTS, rows, D_FF_EXPERT), _BF),
        compiler_params=_cp(("parallel", "parallel"), 40), name="moe_ffn_up")(xe, wg, wu)
    return pl.pallas_call(
        _ffn_down_kernel, grid=(N_EXPERTS, D_MODEL // FFN_TF),
        in_specs=[pl.BlockSpec((1, rows, D_FF_EXPERT), lambda e, n: (e, 0, 0)),
                  pl.BlockSpec((None, 1, D_FF_EXPERT, FFN_TF), lambda e, n: (layer, e, 0, n)),
                  pl.BlockSpec((1, rows, 1), lambda e, n: (e, 0, 0))],
        out_specs=pl.BlockSpec((1, rows, FFN_TF), lambda e, n: (e, 0, n)),
        out_shape=jax.ShapeDtypeStruct((N_EXPERTS, rows, D_MODEL), _BF),
        compiler_params=_cp(("parallel", "parallel"), 48), name="moe_ffn_down")(h, wd, gates)


SCAT_TM = 512


def _scatter_ln_ple_kernel(slot_t_ref, ye_ref, x_ref, g_ref, b_ref, p_ref, wg_ref, wp_ref, o_ref, pt_ref):
    slot_t = slot_t_ref[0]
    lane_c = lax.broadcasted_iota(jnp.int32, (SCAT_TM, CAP), 1)
    for e in range(N_EXPERTS):
        pt_ref[:, e * CAP:(e + 1) * CAP] = jnp.where(slot_t[:, e:e + 1] == lane_c, 1.0, 0.0).astype(_BF)
    f = jnp.dot(pt_ref[...], ye_ref[...].reshape(N_EXPERTS * CAP, D_MODEL), preferred_element_type=_F32)
    y = _layer_norm(DN_ALPHA * x_ref[0] + f, g_ref[...], b_ref[...])
    gate = _dot(y, wg_ref[...])
    emb = _dot(p_ref[0], wp_ref[...])
    o_ref[0] = y + (1.0 / (1.0 + jnp.exp(-gate))) * emb


def _scatter_ln_ple(slot_t, ye, x3d, g, b, p, ple_gate, ple_proj, layer):
    bsz = x3d.shape[0]
    vec = pl.BlockSpec((1, D_MODEL), lambda i, r: (0, 0))
    seq = pl.BlockSpec((1, SCAT_TM, D_MODEL), lambda i, r: (i, r, 0))
    return pl.pallas_call(
        _scatter_ln_ple_kernel, grid=(bsz, SEQ // SCAT_TM),
        in_specs=[pl.BlockSpec((1, SCAT_TM, LANE), lambda i, r: (i, r, 0)),
                  pl.BlockSpec((N_EXPERTS, CAP, D_MODEL), lambda i, r: (0, i, 0)),
                  seq, vec, vec,
                  pl.BlockSpec((None, 1, SCAT_TM, PLE_DIM), lambda i, r: (layer, i, r, 0)),
                  pl.BlockSpec((None, D_MODEL, D_MODEL), lambda i, r: (layer, 0, 0)),
                  pl.BlockSpec((None, PLE_DIM, D_MODEL), lambda i, r: (layer, 0, 0))],
        out_specs=seq,
        out_shape=jax.ShapeDtypeStruct((bsz, SEQ, D_MODEL), _F32),
        scratch_shapes=[pltpu.VMEM((SCAT_TM, N_EXPERTS * CAP), _BF)],
        compiler_params=_cp(("parallel", "arbitrary"), 56), name="moe_scatter_ln_ple")(
            slot_t, ye, x3d, g.reshape(1, -1), b.reshape(1, -1), p, ple_gate, ple_proj)


def _moe_ple_block(x1, x1b, aff, wg, wu, wd, g, b, p, ple_gate, ple_proj, layer):
    bsz = x1.shape[0]
    slot, slot_t = _select(aff.reshape(bsz * N_EXPERTS, SEQ))
    slot4 = slot.reshape(bsz, N_EXPERTS, 1, SEQ)
    aff4 = aff.reshape(bsz, N_EXPERTS, 1, SEQ)
    xe, gates = _gather(slot4, aff4, x1b)
    ye = _expert_ffn(xe, gates, wg, wu, wd, layer)
    return _scatter_ln_ple(slot_t, ye, x1, g, b, p, ple_gate, ple_proj, layer)


MLA_HB = LANE
MLA_IN_PAD = 768
MLA_KR_COL = MLA_Q_LORA + MLA_KV_LORA


def _rms(x, g):
    return x * lax.rsqrt(jnp.mean(x * x, axis=-1, keepdims=True) + NORM_EPS) * g


def _mla_pre_kernel(x_ref, win_ref, qg_ref, wq_ref, kvg_ref, wkv_ref, qa_ref, qb_ref, ka_ref, kb_ref,
                    q_ref, kv_ref, kr_ref):
    h = _dot(x_ref[...], win_ref[...])
    q = _dot(_rms(h[:, :MLA_Q_LORA], qg_ref[...]), wq_ref[...])
    swap = pltpu.roll(q, q.shape[1] - MLA_ROPE, axis=1)
    qa = jnp.concatenate([qa_ref[...]] * MLA_HEADS, axis=1)
    qb = jnp.concatenate([qb_ref[...]] * MLA_HEADS, axis=1)
    q_ref[...] = (q * qa + swap * qb).astype(_BF)
    kv_ref[...] = _dot(_rms(h[:, MLA_Q_LORA:MLA_KR_COL], kvg_ref[...]), wkv_ref[...]).astype(_BF)
    kr = h[:, MLA_KR_COL:MLA_KR_COL + MLA_HB]
    kr_ref[...] = (kr * ka_ref[...] + pltpu.roll(kr, MLA_HB - MLA_ROPE, axis=1) * kb_ref[...]).astype(_BF)


def _rope_lane_tables():
    inv = 1.0 / (ROPE_THETA ** (jnp.arange(0, MLA_ROPE, 2, dtype=_F32) / MLA_ROPE))
    ang = jnp.arange(SEQ, dtype=_F32)[:, None] * inv[None, :]
    cos, sin = jnp.cos(ang), jnp.sin(ang)
    ones = jnp.ones((SEQ, MLA_NOPE), _F32)
    zeros_n = jnp.zeros((SEQ, MLA_NOPE), _F32)
    zeros_r = jnp.zeros((SEQ, MLA_ROPE), _F32)
    cc = jnp.concatenate([cos, cos], axis=1)
    ss = jnp.concatenate([-sin, sin], axis=1)
    keep_q = jnp.concatenate([ones, cc, zeros_r], axis=1)
    keep_k = jnp.concatenate([zeros_n, cc, zeros_r], axis=1)
    swp = jnp.concatenate([zeros_n, ss, zeros_r], axis=1)
    return keep_q, keep_k, swp


def _mla_pre(x2d, w_in, q_norm, w_q_up, kv_norm, w_kv_up):
    m = x2d.shape[0]
    tm = 512
    half = MLA_ROPE // 2
    kr_w = w_in[:, MLA_KR_COL:]
    kr_swapped = jnp.concatenate([kr_w[:, half:], kr_w[:, :half]], axis=1)
    win = jnp.concatenate([w_in[:, :MLA_KR_COL], jnp.zeros((D_MODEL, MLA_NOPE), _F32), kr_w, kr_swapped], axis=1)
    wq = w_q_up.reshape(MLA_Q_LORA, MLA_HEADS, MLA_NOPE + MLA_ROPE)
    rope_w = wq[:, :, MLA_NOPE:]
    wq = jnp.concatenate([wq, rope_w[:, :, half:], rope_w[:, :, :half]], axis=2).reshape(MLA_Q_LORA, MLA_HEADS * MLA_HB)
    keep_q, keep_k, swp = _rope_lane_tables()
    qscale = (MLA_NOPE + MLA_ROPE) ** -0.5 * LOG2E
    nq = MLA_HEADS * MLA_HB
    pos = lambda: pl.BlockSpec((tm, MLA_HB), lambda i: (i % (SEQ // tm), 0))
    full = lambda a: pl.BlockSpec(a.shape, lambda i: (0,) * a.ndim)
    qg, kvg = q_norm.reshape(1, -1), kv_norm.reshape(1, -1)
    return pl.pallas_call(
        _mla_pre_kernel, grid=(m // tm,),
        in_specs=[pl.BlockSpec((tm, D_MODEL), lambda i: (i, 0)), full(win), full(qg), full(wq), full(kvg),
                  full(w_kv_up), pos(), pos(), pos(), pos()],
        out_specs=[pl.BlockSpec((tm, nq), lambda i: (i, 0)),
                   pl.BlockSpec((tm, nq), lambda i: (i, 0)),
                   pl.BlockSpec((tm, MLA_HB), lambda i: (i, 0))],
        out_shape=[jax.ShapeDtypeStruct((m, nq), _BF), jax.ShapeDtypeStruct((m, nq), _BF),
                   jax.ShapeDtypeStruct((m, MLA_HB), _BF)],
        compiler_params=_cp(("parallel",), 48), name="mla_pre")(
            x2d, win, qg, wq, kvg, w_kv_up, keep_q * qscale, swp * qscale, keep_k, swp)


MLA_TQ = 256
MLA_PAIR = 2


def _mla_attn_kernel(q_ref, kv_ref, kr_ref, o_ref, k_scr, v_scr):
    lane = lax.broadcasted_iota(jnp.int32, (SEQ, MLA_HB), 1)
    kr = kr_ref[0].astype(_F32)
    for j in range(MLA_PAIR):
        kvh = kv_ref[0, :, j * MLA_HB:(j + 1) * MLA_HB].astype(_F32)
        k_scr[j] = jnp.where(lane < MLA_NOPE, kvh, kr).astype(_BF)
        v_scr[j] = jnp.where(lane < MLA_NOPE, 1.0, kvh).astype(_BF)
    lane_q = lax.broadcasted_iota(jnp.int32, (MLA_TQ, MLA_HB), 1)

    def body(i, carry):
        rows = pl.ds(pl.multiple_of(i * MLA_TQ, MLA_TQ), MLA_TQ)
        outs = []
        for j in range(MLA_PAIR):
            qb = q_ref[0, rows, j * MLA_HB:(j + 1) * MLA_HB]
            s = _dot_nt(qb, k_scr[j])
            p = jnp.exp2(s - jnp.max(s, axis=1, keepdims=True))
            o = _dot(p, v_scr[j])
            outs.append(o / o[:, 0:1])
        even = pltpu.roll(outs[0], MLA_V, axis=1)
        o_ref[0, rows, :] = jnp.where(lane_q < MLA_V, even, outs[1]).astype(_BF)
        return carry

    lax.fori_loop(0, SEQ // MLA_TQ, body, 0, unroll=4)


def _mla_attention(q, kv, kr):
    b = q.shape[0]
    w = MLA_PAIR * MLA_HB
    return pl.pallas_call(
        _mla_attn_kernel, grid=(b, MLA_HEADS // MLA_PAIR),
        in_specs=[pl.BlockSpec((1, SEQ, w), lambda i, h: (i, 0, h)),
                  pl.BlockSpec((1, SEQ, w), lambda i, h: (i, 0, h)),
                  pl.BlockSpec((1, SEQ, MLA_HB), lambda i, h: (i, 0, 0))],
        out_specs=pl.BlockSpec((1, SEQ, MLA_PAIR * MLA_V), lambda i, h: (i, 0, h)),
        out_shape=jax.ShapeDtypeStruct((b, SEQ, MLA_HEADS * MLA_V), _BF),
        scratch_shapes=[pltpu.VMEM((MLA_PAIR, SEQ, MLA_HB), _BF)] * 2,
        compiler_params=_cp(("parallel", "parallel"), 40), name="mla_attention")(q, kv, kr)


def _na_hyena_mixer(x, w_in, rpb, conv_w, conv_b, f_w1, f_b1, f_freq, f_w2, f_b2, f_w3, skip):
    b = x.shape[0]
    qkv, hb = _inproj(x.reshape(b * SEQ, D_MODEL), w_in)
    y_a = _na_attention(qkv.reshape(b, SEQ, 3 * NA_WIDTH), rpb)
    hbc, zb = _short_conv(hb.reshape(b, SEQ, (HY_ORDER + 1) * HY_WIDTH), conv_w, conv_b)
    fmat = _dft_matrix()
    kfilt = _hyena_filters(f_w1, f_b1, f_freq, f_w2, f_b2, f_w3)
    kf = _filter_spectrum(fmat, kfilt)
    fmat_t = fmat.T
    z = hbc
    for o in range(HY_ORDER):
        z, zb = _long_conv_gate(hbc, z, zb, o + 1, o, fmat, fmat_t, kf, skip)
    return y_a.reshape(b * SEQ, NA_WIDTH), zb.reshape(b * SEQ, HY_WIDTH)


def kernel(x, p, ab_w_in, na_rpb, hy_conv_w, hy_conv_b, hy_f_w1, hy_f_b1, hy_f_freq, hy_f_w2, hy_f_b2, hy_f_w3, hy_skip, ab_w_out, mla_w_in, mla_q_norm, mla_w_q_up, mla_kv_norm, mla_w_kv_up, mla_w_out, ln1_g, ln1_b, ln2_g, ln2_b, moe_router, moe_w_gate, moe_w_up, moe_w_down, ple_gate, ple_proj):
    b = x.shape[0]
    m = b * SEQ
    for i in range(DEPTH):
        j = i // 2
        x2d = x.reshape(m, D_MODEL)
        if i % 2 == 0:
            y_a, z = _na_hyena_mixer(x, ab_w_in[j], na_rpb[j], hy_conv_w[j], hy_conv_b[j], hy_f_w1[j],
                                     hy_f_b1[j], hy_f_freq[j], hy_f_w2[j], hy_f_b2[j], hy_f_w3[j], hy_skip[j])
            x1, x1b, aff = _outproj_ln([y_a, z], ab_w_out[j], x2d, ln1_g[i], ln1_b[i], moe_router[i])
        else:
            q, kv, kr = _mla_pre(x2d, mla_w_in[j], mla_q_norm[j], mla_w_q_up[j], mla_kv_norm[j], mla_w_kv_up[j])
            att = _mla_attention(q.reshape(b, SEQ, -1), kv.reshape(b, SEQ, -1), kr.reshape(b, SEQ, -1))
            x1, x1b, aff = _outproj_ln([att.reshape(m, MLA_HEADS * MLA_V)], mla_w_out[j], x2d, ln1_g[i], ln1_b[i],
                                       moe_router[i])
        x = _moe_ple_block(x1.reshape(b, SEQ, D_MODEL), x1b.reshape(b, SEQ, D_MODEL), aff,
                           moe_w_gate, moe_w_up, moe_w_down, ln2_g[i], ln2_b[i], p, ple_gate, ple_proj, i)
    return x
```

```python
import functools
import math

import numpy as np
import jax
import jax.numpy as jnp
from jax import lax
from jax.experimental import pallas as pl
from jax.experimental.pallas import tpu as pltpu

D_MODEL = 1024
BATCH = 8
SEQ = 2048
DEPTH = 2
GRID_W = 64
PLE_DIM = 256
NA_HEADS = 8
NA_HEAD_DIM = 64
NA_WIDTH = NA_HEADS * NA_HEAD_DIM
NA_WIN_ROWS_MAX = 8
NA_WIN_COLS = 16
HY_WIDTH = D_MODEL - NA_WIDTH
HY_ORDER = 2
HY_SHORT_K = 3
HY_EMB_DIM = 33
HY_FILTER_HIDDEN = 64
HY_FAST_DECAY_PCT = 0.3
HY_SLOW_DECAY_PCT = 1.5
HY_DECAY_TARGET = 1e-2
AB_IN_WIDTH = 3 * NA_WIDTH + (HY_ORDER + 1) * HY_WIDTH
MLA_HEADS = 16
MLA_Q_LORA = 384
MLA_KV_LORA = 256
MLA_NOPE = 64
MLA_ROPE = 32
MLA_V = 64
ROPE_THETA = 10000.0
N_EXPERTS = 16
EC_CAPACITY_FACTOR = 2
D_FF_EXPERT = 2048
DN_ALPHA = (2 * DEPTH) ** 0.25
NORM_EPS = 1e-5
NEG_INF = -1e30

CAP = EC_CAPACITY_FACTOR * SEQ // N_EXPERTS
NA_ROWS = SEQ // GRID_W
NA_WIN_ROWS = min(NA_WIN_ROWS_MAX, NA_ROWS)
NA_SLAB = NA_WIN_ROWS * GRID_W
FFT_N = 2 * SEQ
LOG2E = math.log2(math.e)

LANE = 128
MIB = 1 << 20

_BF = jnp.bfloat16
_F32 = jnp.float32
_HP = lax.Precision.HIGHEST


def _cp(sem, vmem_mib):
    return pltpu.CompilerParams(dimension_semantics=sem, vmem_limit_bytes=vmem_mib * MIB)


def _dot(a, b):
    return jnp.dot(a.astype(_BF), b.astype(_BF), preferred_element_type=_F32)


def _dot_nt(a, b):
    return lax.dot_general(a.astype(_BF), b.astype(_BF), (((1,), (1,)), ((), ())),
                           preferred_element_type=_F32)


def _layer_norm(y, g, b):
    mu = jnp.mean(y, axis=-1, keepdims=True)
    d = y - mu
    var = jnp.mean(d * d, axis=-1, keepdims=True)
    return d * lax.rsqrt(var + NORM_EPS) * g + b


def _inproj_kernel(x_ref, w_ref, qkv_ref, hb_ref):
    h = _dot(x_ref[...], w_ref[...])
    qkv_ref[...] = h[:, :3 * NA_WIDTH].astype(_BF)
    hb_ref[...] = h[:, 3 * NA_WIDTH:]


def _inproj(x2d, w):
    m = x2d.shape[0]
    tm = 512
    nq, nh = 3 * NA_WIDTH, (HY_ORDER + 1) * HY_WIDTH
    return pl.pallas_call(
        _inproj_kernel, grid=(m // tm,),
        in_specs=[pl.BlockSpec((tm, D_MODEL), lambda i: (i, 0)),
                  pl.BlockSpec((D_MODEL, AB_IN_WIDTH), lambda i: (0, 0))],
        out_specs=[pl.BlockSpec((tm, nq), lambda i: (i, 0)),
                   pl.BlockSpec((tm, nh), lambda i: (i, 0))],
        out_shape=[jax.ShapeDtypeStruct((m, nq), _BF), jax.ShapeDtypeStruct((m, nh), _F32)],
        compiler_params=_cp(("parallel",), 48), name="ab_inproj")(x2d, w)


NA_GROUP = 4
NA_GW = NA_GROUP * NA_HEAD_DIM


def _na_kernel(q_ref, k_ref, v_ref, tbl_ref, o_ref):
    lane_head = lax.broadcasted_iota(jnp.int32, (GRID_W, NA_GW), 1) >> int(math.log2(NA_HEAD_DIM))
    scale = NA_HEAD_DIM ** -0.5 * LOG2E

    def body(r, carry):
        r0 = jnp.clip(r - NA_WIN_ROWS // 2, 0, NA_ROWS - NA_WIN_ROWS)
        off = r0 - r + (NA_WIN_ROWS_MAX - 1)
        qr = q_ref[0, pl.ds(pl.multiple_of(r * GRID_W, GRID_W), GRID_W), :].astype(_F32)
        q4 = jnp.concatenate([jnp.where(lane_head == h, qr, 0.0) for h in range(NA_GROUP)], axis=0)
        ks = k_ref[0, pl.ds(pl.multiple_of(r0 * GRID_W, GRID_W), NA_SLAB), :]
        vs = v_ref[0, pl.ds(pl.multiple_of(r0 * GRID_W, GRID_W), NA_SLAB), :]
        bias = jnp.concatenate([tbl_ref[h, off] for h in range(NA_GROUP)], axis=0)
        s = _dot_nt(q4, ks) * scale + bias
        mx = jnp.max(s, axis=1, keepdims=True)
        p = jnp.exp2(s - mx)
        den = jnp.sum(p, axis=1, keepdims=True)
        o4 = _dot(p, vs) / den
        out = jnp.zeros((GRID_W, NA_GW), _F32)
        for h in range(NA_GROUP):
            out = out + jnp.where(lane_head == h, o4[h * GRID_W:(h + 1) * GRID_W], 0.0)
        o_ref[0, pl.ds(pl.multiple_of(r * GRID_W, GRID_W), GRID_W), :] = out.astype(_BF)
        return carry

    lax.fori_loop(0, NA_ROWS, body, 0, unroll=4)


def _na_bias_table(rpb):
    c = np.arange(GRID_W)
    c0 = np.clip(c - NA_WIN_COLS // 2, 0, GRID_W - NA_WIN_COLS)
    kc = np.arange(GRID_W)
    col_ok = (kc[None, :] >= c0[:, None]) & (kc[None, :] < c0[:, None] + NA_WIN_COLS)
    dc_idx = np.clip(kc[None, :] - c[:, None], -(NA_WIN_COLS - 1), NA_WIN_COLS - 1) + (NA_WIN_COLS - 1)
    pick = (dc_idx[None, :, :] == np.arange(2 * NA_WIN_COLS - 1)[:, None, None]).astype(np.float32)
    per_dr = jnp.einsum("hdk,kqc->hdqc", rpb.astype(_F32), pick, precision=_HP)
    per_dr = jnp.where(col_ok[None, None], per_dr * LOG2E, NEG_INF)
    slabs = jnp.stack([per_dr[:, off:off + NA_WIN_ROWS] for off in range(NA_WIN_ROWS_MAX)], axis=1)
    return slabs.transpose(0, 1, 3, 2, 4).reshape(NA_HEADS, NA_WIN_ROWS_MAX, GRID_W, NA_SLAB)


def _na_attention(qkv, rpb):
    b = qkv.shape[0]
    tbl = _na_bias_table(rpb)
    ng = NA_HEADS // NA_GROUP
    blk = lambda col0: pl.BlockSpec((1, SEQ, NA_GW), lambda i, g, c=col0: (i, 0, c + g))
    return pl.pallas_call(
        _na_kernel, grid=(b, ng),
        in_specs=[blk(0), blk(ng), blk(2 * ng),
                  pl.BlockSpec((NA_GROUP, NA_WIN_ROWS_MAX, GRID_W, NA_SLAB), lambda i, g: (g, 0, 0, 0))],
        out_specs=pl.BlockSpec((1, SEQ, NA_GW), lambda i, g: (i, 0, g)),
        out_shape=jax.ShapeDtypeStruct((b, SEQ, NA_WIDTH), _BF),
        compiler_params=_cp(("parallel", "parallel"), 32), name="na_attention")(qkv, qkv, qkv, tbl)


def _sconv_kernel(x_ref, w_ref, b_ref, o_ref, vb_ref):
    x = x_ref[0]
    n = x.shape[0]
    row = lax.broadcasted_iota(jnp.int32, x.shape, 0)
    xm = jnp.where(row == 0, 0.0, pltpu.roll(x, 1, axis=0))
    xp = jnp.where(row == n - 1, 0.0, pltpu.roll(x, n - 1, axis=0))
    y = b_ref[...] + xm * w_ref[0:1, :] + x * w_ref[1:2, :] + xp * w_ref[2:3, :]
    o_ref[0] = y

    @pl.when(pl.program_id(1) == 0)
    def _():
        vb_ref[0] = y.astype(_BF)


def _short_conv(hb, w, bias):
    b, n, c = hb.shape
    tc = HY_WIDTH
    return pl.pallas_call(
        _sconv_kernel, grid=(b, c // tc),
        in_specs=[pl.BlockSpec((1, n, tc), lambda i, j: (i, 0, j)),
                  pl.BlockSpec((HY_SHORT_K, tc), lambda i, j: (0, j)),
                  pl.BlockSpec((1, tc), lambda i, j: (0, j))],
        out_specs=[pl.BlockSpec((1, n, tc), lambda i, j: (i, 0, j)),
                   pl.BlockSpec((1, n, tc), lambda i, j: (i, 0, 0))],
        out_shape=[jax.ShapeDtypeStruct((b, n, c), _F32), jax.ShapeDtypeStruct((b, n, tc), _BF)],
        compiler_params=_cp(("parallel", "arbitrary"), 32), name="hy_short_conv")(hb, w, bias.reshape(1, c))


HY_HID_PAD = LANE
HY_FILT_TC = 256


def _filter_kernel(z_ref, w1_ref, b1_ref, fr_ref, w2_ref, b2_ref, w3f_ref, w3b_ref, dec_ref, k_ref, h_ref):
    @pl.when(jnp.logical_and(pl.program_id(0) == 0, pl.program_id(1) == 0))
    def _():
        fr = fr_ref[...]
        h1 = jnp.sin(fr * (jnp.dot(z_ref[...], w1_ref[...], precision=_HP, preferred_element_type=_F32) + b1_ref[...]))
        h_ref[...] = jnp.sin(fr * (jnp.dot(h1, w2_ref[...], precision=_HP, preferred_element_type=_F32) + b2_ref[...]))

    fwd = jnp.dot(h_ref[:SEQ, :], w3f_ref[...], precision=_HP, preferred_element_type=_F32)
    bwd = jnp.dot(h_ref[SEQ:, :], w3b_ref[...], precision=_HP, preferred_element_type=_F32)
    k = jnp.concatenate([fwd, bwd], axis=0) * dec_ref[...]
    ss = jnp.sum(k * k, axis=0, keepdims=True)
    k_ref[...] = (k * lax.rsqrt(ss + 1e-12)).astype(_BF)


@functools.lru_cache(maxsize=None)
def _filter_tables():
    bands = (HY_EMB_DIM - 1) // 2
    t = np.linspace(0.0, 1.0, SEQ)[:, None]
    w = 2.0 * math.pi * np.arange(SEQ)[:, None] / SEQ
    f = np.linspace(1e-4, bands - 1, bands)[None, :]
    z = np.concatenate([t, np.cos(f * w), -np.sin(f * w)], axis=-1)
    min_decay = math.log(HY_DECAY_TARGET) / HY_SLOW_DECAY_PCT
    max_decay = math.log(HY_DECAY_TARGET) / HY_FAST_DECAY_PCT
    deltas = np.abs(np.linspace(min_decay, max_decay, HY_WIDTH))
    dec = np.exp(-t * deltas)
    src = np.concatenate([np.arange(SEQ), [0], np.arange(SEQ - 1, 0, -1)])
    live = np.ones((FFT_N, 1))
    live[SEQ] = 0.0
    z2 = np.pad(z[src], ((0, 0), (0, HY_HID_PAD - HY_EMB_DIM))).astype(np.float32)
    dec2 = (dec[src] * live).astype(np.float32)
    return z2, dec2


def _hyena_filters(w1, b1, freq, w2, b2, w3):
    z2, dec2 = _filter_tables()
    hp = HY_HID_PAD - HY_FILTER_HIDDEN
    w1p = jnp.pad(w1, ((0, HY_HID_PAD - HY_EMB_DIM), (0, hp)))
    w2p = jnp.pad(w2, ((0, hp), (0, hp)))
    w3p = jnp.pad(w3, ((0, hp), (0, 0)))
    row = lambda v: jnp.pad(v, (0, hp)).reshape(1, HY_HID_PAD)
    nc = HY_WIDTH // HY_FILT_TC
    per_order = 2 * nc
    full = lambda shape: pl.BlockSpec(shape, lambda o, j: (0, 0))
    return pl.pallas_call(
        _filter_kernel, grid=(HY_ORDER, nc),
        in_specs=[full((FFT_N, HY_HID_PAD)), full((HY_HID_PAD, HY_HID_PAD)), full((1, HY_HID_PAD)),
                  full((1, HY_HID_PAD)), full((HY_HID_PAD, HY_HID_PAD)), full((1, HY_HID_PAD)),
                  pl.BlockSpec((HY_HID_PAD, HY_FILT_TC), lambda o, j: (0, o * per_order + j)),
                  pl.BlockSpec((HY_HID_PAD, HY_FILT_TC), lambda o, j: (0, o * per_order + nc + j)),
                  pl.BlockSpec((FFT_N, HY_FILT_TC), lambda o, j: (0, j))],
        out_specs=pl.BlockSpec((FFT_N, HY_FILT_TC), lambda o, j: (0, o * nc + j)),
        out_shape=jax.ShapeDtypeStruct((FFT_N, HY_ORDER * HY_WIDTH), _BF),
        scratch_shapes=[pltpu.VMEM((FFT_N, HY_HID_PAD), _F32)],
        compiler_params=_cp(("arbitrary", "arbitrary"), 40), name="hy_filters")(
            z2, w1p, row(b1), row(freq), w2p, row(b2), w3p, w3p, dec2)


@functools.lru_cache(maxsize=None)
def _dft_matrices():
    t = np.arange(SEQ)
    ang = ((t[:, None] * t[None, :]) % FFT_N) * (2.0 * math.pi / FFT_N)
    re = np.cos(ang)
    im = -np.sin(ang)
    im[0] = 1.0 - 2.0 * (t % 2)
    nf = SEQ // HY_FB
    packed = np.stack([re.reshape(nf, HY_FB, SEQ), im.reshape(nf, HY_FB, SEQ)], axis=1).reshape(FFT_N, SEQ)
    return packed.astype(np.float32), np.ascontiguousarray(packed.T).astype(np.float32)


def _kf_kernel(f_ref, k_ref, o_ref):
    i = pl.program_id(0)
    f = f_ref[...]
    tm = f.shape[0]
    p1 = jnp.dot(f, k_ref[:SEQ, :], preferred_element_type=_F32)
    p2 = jnp.dot(f, k_ref[SEQ:, :], preferred_element_type=_F32)
    row = lax.broadcasted_iota(jnp.int32, (tm, 1), 0) + i * tm
    sign = (1 - 2 * (row & 1)).astype(_F32)
    o_ref[...] = p1 + sign * p2


def _filter_spectrum(fmat, kfilt):
    tm = 512
    nw = kfilt.shape[1]
    return pl.pallas_call(
        _kf_kernel, grid=(FFT_N // tm,),
        in_specs=[pl.BlockSpec((tm, SEQ), lambda i: (i, 0)),
                  pl.BlockSpec((FFT_N, nw), lambda i: (0, 0))],
        out_specs=pl.BlockSpec((tm, nw), lambda i: (i, 0)),
        out_shape=jax.ShapeDtypeStruct((FFT_N, nw), _F32),
        compiler_params=_cp(("parallel",), 40), name="hy_filter_spectrum")(fmat, kfilt)


HY_FB = 512
HY_NB = 2


HY_TM = 512


def _hconv_fwd_kernel(zb_ref, f_ref, k_ref, y_ref):
    fk = pl.program_id(1)
    kr = k_ref[:HY_FB, :]
    ki = k_ref[HY_FB:, :]
    row0 = jnp.logical_and(lax.broadcasted_iota(jnp.int32, kr.shape, 0) == 0, fk == 0)
    sc = jnp.where(row0, 1.0 / FFT_N, 2.0 / FFT_N)
    for j in range(HY_NB):
        u = jnp.dot(f_ref[...], zb_ref[j], preferred_element_type=_F32)
        ur = u[:HY_FB]
        ui = u[HY_FB:]
        yr = jnp.where(row0, ur * kr, ur * kr - ui * ki)
        yi = jnp.where(row0, ui * ki, ur * ki + ui * kr)
        y_ref[j, 0, 0] = (yr * sc).astype(_BF)
        y_ref[j, 0, 1] = (yi * sc).astype(_BF)


def _hconv_inv_kernel(ft_ref, y_ref, z_ref, skip_ref, xn_ref, o_ref, ob_ref):
    for j in range(HY_NB):
        conv = jnp.dot(ft_ref[...], y_ref[j], preferred_element_type=_F32)
        out = xn_ref[j] * (conv + z_ref[j] * skip_ref[...])
        o_ref[j] = out
        ob_ref[j] = out.astype(_BF)


def _long_conv_gate(hbc, zsrc, zb, xn_col, order, fmat, fmat_t, kf, skip):
    b = hbc.shape[0]
    nf = SEQ // HY_FB
    w = HY_WIDTH
    nb = HY_NB
    y = pl.pallas_call(
        _hconv_fwd_kernel, grid=(b // nb, nf),
        in_specs=[pl.BlockSpec((nb, SEQ, w), lambda i, f: (i, 0, 0)),
                  pl.BlockSpec((2 * HY_FB, SEQ), lambda i, f: (f, 0)),
                  pl.BlockSpec((2 * HY_FB, w), lambda i, f: (f, order))],
        out_specs=pl.BlockSpec((nb, 1, 2, HY_FB, w), lambda i, f: (i, f, 0, 0, 0)),
        out_shape=jax.ShapeDtypeStruct((b, nf, 2, HY_FB, w), _BF),
        compiler_params=_cp(("parallel", "parallel"), 40), name=f"hy_conv_fwd{order}")(zb, fmat, kf)
    blk = lambda col: pl.BlockSpec((nb, HY_TM, w), lambda i, m, c=col: (i, m, c))
    return pl.pallas_call(
        _hconv_inv_kernel, grid=(b // nb, SEQ // HY_TM),
        in_specs=[pl.BlockSpec((HY_TM, FFT_N), lambda i, m: (m, 0)),
                  pl.BlockSpec((nb, FFT_N, w), lambda i, m: (i, 0, 0)),
                  blk(0),
                  pl.BlockSpec((1, w), lambda i, m: (0, 0)),
                  blk(xn_col)],
        out_specs=[blk(0), blk(0)],
        out_shape=[jax.ShapeDtypeStruct((b, SEQ, w), _F32), jax.ShapeDtypeStruct((b, SEQ, w), _BF)],
        compiler_params=_cp(("parallel", "parallel"), 48), name=f"hy_conv_inv{order}")(
            fmat_t, y.reshape(b, FFT_N, w), zsrc, skip[order].reshape(1, w), hbc)


def _outproj_ln_kernel(n_a, *refs):
    a_refs = refs[:n_a]
    w_refs = refs[n_a:2 * n_a]
    x_ref, g_ref, b_ref, wr_ref, o_ref, ob_ref, aff_ref = refs[2 * n_a:]
    m = _dot(a_refs[0][...], w_refs[0][...])
    for a_ref, w_ref in zip(a_refs[1:], w_refs[1:]):
        m = m + _dot(a_ref[...], w_ref[...])
    y = _layer_norm(DN_ALPHA * x_ref[...] + m, g_ref[...], b_ref[...])
    o_ref[...] = y
    ob_ref[...] = y.astype(_BF)
    wr = wr_ref[...]
    wr_hi = wr.astype(_BF)
    wr_lo = (wr - wr_hi.astype(_F32)).astype(_BF)
    y_hi = y.astype(_BF)
    y_lo = (y - y_hi.astype(_F32)).astype(_BF)
    by_hi = _dot_nt(jnp.concatenate([wr_hi, wr_lo], axis=0), y_hi)
    logits = by_hi[:N_EXPERTS] + by_hi[N_EXPERTS:] + _dot_nt(wr_hi, y_lo)
    ex = jnp.exp(logits - jnp.max(logits, axis=0, keepdims=True))
    aff_ref[0] = ex / jnp.sum(ex, axis=0, keepdims=True)


def _outproj_ln(a_list, w, x2d, g, b, w_router):
    m = x2d.shape[0]
    tm = 512
    per_seq = SEQ // tm
    in_specs, w_args, row0 = [], [], 0
    for a in a_list:
        in_specs.append(pl.BlockSpec((tm, a.shape[1]), lambda i: (i, 0)))
    for a in a_list:
        ka = a.shape[1]
        in_specs.append(pl.BlockSpec((ka, D_MODEL), lambda i, r=row0 // ka: (r, 0)))
        w_args.append(w)
        row0 += ka
    in_specs += [pl.BlockSpec((tm, D_MODEL), lambda i: (i, 0)),
                 pl.BlockSpec((1, D_MODEL), lambda i: (0, 0)),
                 pl.BlockSpec((1, D_MODEL), lambda i: (0, 0)),
                 pl.BlockSpec((N_EXPERTS, D_MODEL), lambda i: (0, 0))]
    return pl.pallas_call(
        functools.partial(_outproj_ln_kernel, len(a_list)), grid=(m // tm,),
        in_specs=in_specs,
        out_specs=[pl.BlockSpec((tm, D_MODEL), lambda i: (i, 0)),
                   pl.BlockSpec((tm, D_MODEL), lambda i: (i, 0)),
                   pl.BlockSpec((1, N_EXPERTS, tm), lambda i: (i // per_seq, 0, i % per_seq))],
        out_shape=[jax.ShapeDtypeStruct((m, D_MODEL), _F32), jax.ShapeDtypeStruct((m, D_MODEL), _BF),
                   jax.ShapeDtypeStruct((m // SEQ, N_EXPERTS, SEQ), _F32)],
        compiler_params=_cp(("parallel",), 40), name="outproj_ln_router")(
            *a_list, *w_args, x2d, g.reshape(1, -1), b.reshape(1, -1), w_router.T)


BISECT_STEPS = 160


def _select_kernel(a_ref, tri_ref, slot_ref, slot_t_ref):
    a = a_ref[...]
    rows = a.shape[0]
    cap = float(CAP)

    def body(_, c):
        lo, hi = c
        mid = 0.5 * (lo + hi)
        cnt = jnp.sum(jnp.where(a > mid, 1.0, 0.0), axis=1, keepdims=True)
        ge = cnt >= cap
        return jnp.where(ge, mid, lo), jnp.where(ge, hi, mid)

    lo, hi = lax.fori_loop(0, BISECT_STEPS, body,
                           (jnp.full((rows, 1), -1.0, _F32), jnp.full((rows, 1), 1.0, _F32)))
    vstar = jnp.max(jnp.where(a <= hi, a, -1.0), axis=1, keepdims=True)
    gt = a > vstar
    eq = a == vstar
    need = cap - jnp.sum(jnp.where(gt, 1.0, 0.0), axis=1, keepdims=True)
    tri = tri_ref[...]
    eq_before = jnp.dot(jnp.where(eq, 1.0, 0.0).astype(_BF), tri, preferred_element_type=_F32)
    sel = jnp.where(gt, 1.0, jnp.where(eq, jnp.where(eq_before < need, 1.0, 0.0), 0.0))
    pos = jnp.dot(sel.astype(_BF), tri, preferred_element_type=_F32)
    slot = jnp.where(sel > 0.5, pos, -1.0)
    slot_ref[...] = slot.astype(jnp.int32)
    if rows < LANE:
        slot = jnp.concatenate([slot, jnp.full((LANE - rows, SEQ), -1.0, _F32)], axis=0)
    slot_tm = slot.T
    for b in range(rows // N_EXPERTS):
        shifted = slot_tm if b == 0 else pltpu.roll(slot_tm, LANE - N_EXPERTS * b, axis=1)
        slot_t_ref[b] = shifted.astype(jnp.int32)


@functools.lru_cache(maxsize=None)
def _prefix_matrix():
    idx = np.arange(SEQ)
    return (idx[:, None] < idx[None, :]).astype(_BF)


def _select(aff2d):
    rows = aff2d.shape[0]
    assert rows <= LANE and rows % N_EXPERTS == 0
    bsz = rows // N_EXPERTS
    tri = _prefix_matrix()
    return pl.pallas_call(
        _select_kernel, grid=(1,),
        in_specs=[pl.BlockSpec((rows, SEQ), lambda i: (0, 0)),
                  pl.BlockSpec((SEQ, SEQ), lambda i: (0, 0))],
        out_specs=[pl.BlockSpec((rows, SEQ), lambda i: (0, 0)),
                   pl.BlockSpec((bsz, SEQ, LANE), lambda i: (0, 0, 0))],
        out_shape=[jax.ShapeDtypeStruct((rows, SEQ), jnp.int32),
                   jax.ShapeDtypeStruct((bsz, SEQ, LANE), jnp.int32)],
        compiler_params=_cp(("arbitrary",), 48), name="moe_select")(aff2d, tri)


def _onehot(slot_row):
    return slot_row == lax.broadcasted_iota(jnp.int32, (CAP, SEQ), 0)


GATHER_NE = 4


def _gather_kernel(slot_ref, a_ref, xb_ref, xe_ref, g_ref):
    hits = [_onehot(slot_ref[0, e]) for e in range(GATHER_NE)]
    p = jnp.concatenate([jnp.where(h, 1.0, 0.0).astype(_BF) for h in hits], axis=0)
    xe = jnp.dot(p, xb_ref[0], preferred_element_type=_F32).astype(_BF)
    xe_ref[...] = xe.reshape(GATHER_NE, CAP, D_MODEL)
    for e in range(GATHER_NE):
        g_ref[e] = jnp.sum(jnp.where(hits[e], a_ref[0, e], 0.0), axis=1, keepdims=True)


def _gather(slot4, aff4, xb3d):
    b = xb3d.shape[0]
    row = pl.BlockSpec((1, GATHER_NE, 1, SEQ), lambda i, e: (i, e, 0, 0))
    return pl.pallas_call(
        _gather_kernel, grid=(b, N_EXPERTS // GATHER_NE),
        in_specs=[row, row, pl.BlockSpec((1, SEQ, D_MODEL), lambda i, e: (i, 0, 0))],
        out_specs=[pl.BlockSpec((GATHER_NE, CAP, D_MODEL), lambda i, e: (e, i, 0)),
                   pl.BlockSpec((GATHER_NE, CAP, 1), lambda i, e: (e, i, 0))],
        out_shape=[jax.ShapeDtypeStruct((N_EXPERTS, b * CAP, D_MODEL), _BF),
                   jax.ShapeDtypeStruct((N_EXPERTS, b * CAP, 1), _F32)],
        compiler_params=_cp(("parallel", "parallel"), 40), name="moe_gather")(slot4, aff4, xb3d)


FFN_TF = 512
FFN_TM = 512


def _ffn_up_kernel(xe_ref, wg_ref, wu_ref, h_ref):
    wg = wg_ref[0].astype(_BF)
    wu = wu_ref[0].astype(_BF)
    for c in range(xe_ref.shape[1] // FFN_TM):
        rows = slice(c * FFN_TM, (c + 1) * FFN_TM)
        xe = xe_ref[0, rows, :]
        hg = _dot(xe, wg)
        hu = _dot(xe, wu)
        h_ref[0, rows, :] = ((hg * (1.0 / (1.0 + jnp.exp(-hg)))) * hu).astype(_BF)


def _ffn_down_kernel(h_ref, wd_ref, g_ref, ye_ref):
    wd = wd_ref[0].astype(_BF)
    for c in range(h_ref.shape[1] // FFN_TM):
        rows = slice(c * FFN_TM, (c + 1) * FFN_TM)
        ye_ref[0, rows, :] = (_dot(h_ref[0, rows, :], wd) * g_ref[0, rows, :]).astype(_BF)


def _expert_ffn(xe, gates, wg, wu, wd, layer):
    rows = xe.shape[1]
    h = pl.pallas_call(
        _ffn_up_kernel, grid=(N_EXPERTS, D_FF_EXPERT // FFN_TF),
        in_specs=[pl.BlockSpec((1, rows, D_MODEL), lambda e, f: (e, 0, 0)),
                  pl.BlockSpec((None, 1, D_MODEL, FFN_TF), lambda e, f: (layer, e, 0, f)),
                  pl.BlockSpec((None, 1, D_MODEL, FFN_TF), lambda e, f: (layer, e, 0, f))],
        out_specs=pl.BlockSpec((1, rows, FFN_TF), lambda e, f: (e, 0, f)),
        out_shape=jax.ShapeDtypeStruct((N_EXPERTS, rows, D_FF_EXPERT), _BF),
        compiler_params=_cp(("parallel", "parallel"), 40), name="moe_ffn_up")(xe, wg, wu)
    return pl.pallas_call(
        _ffn_down_kernel, grid=(N_EXPERTS,),
        in_specs=[pl.BlockSpec((1, rows, D_FF_EXPERT), lambda e: (e, 0, 0)),
                  pl.BlockSpec((None, 1, D_FF_EXPERT, D_MODEL), lambda e: (layer, e, 0, 0)),
                  pl.BlockSpec((1, rows, 1), lambda e: (e, 0, 0))],
        out_specs=pl.BlockSpec((1, rows, D_MODEL), lambda e: (e, 0, 0)),
        out_shape=jax.ShapeDtypeStruct((N_EXPERTS, rows, D_MODEL), _BF),
        compiler_params=_cp(("parallel",), 56), name="moe_ffn_down")(h, wd, gates)


SCAT_TM = 512


def _scatter_ln_ple_kernel(slot_t_ref, ye_ref, x_ref, g_ref, b_ref, p_ref, wg_ref, wp_ref, o_ref, pt_ref):
    slot_t = slot_t_ref[0]
    lane_c = lax.broadcasted_iota(jnp.int32, (SCAT_TM, CAP), 1)
    for e in range(N_EXPERTS):
        pt_ref[:, e * CAP:(e + 1) * CAP] = jnp.where(slot_t[:, e:e + 1] == lane_c, 1.0, 0.0).astype(_BF)
    f = jnp.dot(pt_ref[...], ye_ref[...].reshape(N_EXPERTS * CAP, D_MODEL), preferred_element_type=_F32)
    y = _layer_norm(DN_ALPHA * x_ref[0] + f, g_ref[...], b_ref[...])
    gate = _dot(y, wg_ref[...])
    emb = _dot(p_ref[0], wp_ref[...])
    o_ref[0] = y + (1.0 / (1.0 + jnp.exp(-gate))) * emb


def _scatter_ln_ple(slot_t, ye, x3d, g, b, p, ple_gate, ple_proj, layer):
    bsz = x3d.shape[0]
    vec = pl.BlockSpec((1, D_MODEL), lambda i, r: (0, 0))
    seq = pl.BlockSpec((1, SCAT_TM, D_MODEL), lambda i, r: (i, r, 0))
    return pl.pallas_call(
        _scatter_ln_ple_kernel, grid=(bsz, SEQ // SCAT_TM),
        in_specs=[pl.BlockSpec((1, SCAT_TM, LANE), lambda i, r: (i, r, 0)),
                  pl.BlockSpec((N_EXPERTS, CAP, D_MODEL), lambda i, r: (0, i, 0)),
                  seq, vec, vec,
                  pl.BlockSpec((None, 1, SCAT_TM, PLE_DIM), lambda i, r: (layer, i, r, 0)),
                  pl.BlockSpec((None, D_MODEL, D_MODEL), lambda i, r: (layer, 0, 0)),
                  pl.BlockSpec((None, PLE_DIM, D_MODEL), lambda i, r: (layer, 0, 0))],
        out_specs=seq,
        out_shape=jax.ShapeDtypeStruct((bsz, SEQ, D_MODEL), _F32),
        scratch_shapes=[pltpu.VMEM((SCAT_TM, N_EXPERTS * CAP), _BF)],
        compiler_params=_cp(("parallel", "arbitrary"), 56), name="moe_scatter_ln_ple")(
            slot_t, ye, x3d, g.reshape(1, -1), b.reshape(1, -1), p, ple_gate, ple_proj)


def _moe_ple_block(x1, x1b, aff, wg, wu, wd, g, b, p, ple_gate, ple_proj, layer):
    bsz = x1.shape[0]
    slot, slot_t = _select(aff.reshape(bsz * N_EXPERTS, SEQ))
    slot4 = slot.reshape(bsz, N_EXPERTS, 1, SEQ)
    aff4 = aff.reshape(bsz, N_EXPERTS, 1, SEQ)
    xe, gates = _gather(slot4, aff4, x1b)
    ye = _expert_ffn(xe, gates, wg, wu, wd, layer)
    return _scatter_ln_ple(slot_t, ye, x1, g, b, p, ple_gate, ple_proj, layer)


MLA_HB = LANE
MLA_IN_PAD = 768
MLA_KR_COL = MLA_Q_LORA + MLA_KV_LORA


def _rms(x, g):
    return x * lax.rsqrt(jnp.mean(x * x, axis=-1, keepdims=True) + NORM_EPS) * g


def _mla_pre_kernel(x_ref, win_ref, qg_ref, wq_ref, kvg_ref, wkv_ref, qa_ref, qb_ref, ka_ref, kb_ref,
                    q_ref, kv_ref, kr_ref):
    h = _dot(x_ref[...], win_ref[...])
    q = _dot(_rms(h[:, :MLA_Q_LORA], qg_ref[...]), wq_ref[...])
    swap = pltpu.roll(q, q.shape[1] - MLA_ROPE, axis=1)
    qa = jnp.concatenate([qa_ref[...]] * MLA_HEADS, axis=1)
    qb = jnp.concatenate([qb_ref[...]] * MLA_HEADS, axis=1)
    q_ref[...] = (q * qa + swap * qb).astype(_BF)
    kv_ref[...] = _dot(_rms(h[:, MLA_Q_LORA:MLA_KR_COL], kvg_ref[...]), wkv_ref[...]).astype(_BF)
    kr = h[:, MLA_KR_COL:MLA_KR_COL + MLA_HB]
    kr_ref[...] = (kr * ka_ref[...] + pltpu.roll(kr, MLA_HB - MLA_ROPE, axis=1) * kb_ref[...]).astype(_BF)


@functools.lru_cache(maxsize=None)
def _rope_lane_tables():
    inv = 1.0 / (ROPE_THETA ** (np.arange(0, MLA_ROPE, 2) / MLA_ROPE))
    ang = np.arange(SEQ)[:, None] * inv[None, :]
    cos, sin = np.cos(ang), np.sin(ang)
    ones = np.ones((SEQ, MLA_NOPE))
    zeros_n = np.zeros((SEQ, MLA_NOPE))
    zeros_r = np.zeros((SEQ, MLA_ROPE))
    cc = np.concatenate([cos, cos], axis=1)
    ss = np.concatenate([-sin, sin], axis=1)
    keep_q = np.concatenate([ones, cc, zeros_r], axis=1)
    keep_k = np.concatenate([zeros_n, cc, zeros_r], axis=1)
    swp = np.concatenate([zeros_n, ss, zeros_r], axis=1)
    qscale = (MLA_NOPE + MLA_ROPE) ** -0.5 * LOG2E
    f32 = lambda a: a.astype(np.float32)
    return f32(keep_q * qscale), f32(swp * qscale), f32(keep_k), f32(swp)


def _mla_pre(x2d, w_in, q_norm, w_q_up, kv_norm, w_kv_up):
    m = x2d.shape[0]
    tm = 512
    half = MLA_ROPE // 2
    kr_w = w_in[:, MLA_KR_COL:]
    kr_swapped = jnp.concatenate([kr_w[:, half:], kr_w[:, :half]], axis=1)
    win = jnp.concatenate([w_in[:, :MLA_KR_COL], jnp.zeros((D_MODEL, MLA_NOPE), _F32), kr_w, kr_swapped], axis=1)
    wq = w_q_up.reshape(MLA_Q_LORA, MLA_HEADS, MLA_NOPE + MLA_ROPE)
    rope_w = wq[:, :, MLA_NOPE:]
    wq = jnp.concatenate([wq, rope_w[:, :, half:], rope_w[:, :, :half]], axis=2).reshape(MLA_Q_LORA, MLA_HEADS * MLA_HB)
    q_keep, q_swap, k_keep, k_swap = _rope_lane_tables()
    nq = MLA_HEADS * MLA_HB
    pos = lambda: pl.BlockSpec((tm, MLA_HB), lambda i: (i % (SEQ // tm), 0))
    full = lambda a: pl.BlockSpec(a.shape, lambda i: (0,) * a.ndim)
    qg, kvg = q_norm.reshape(1, -1), kv_norm.reshape(1, -1)
    return pl.pallas_call(
        _mla_pre_kernel, grid=(m // tm,),
        in_specs=[pl.BlockSpec((tm, D_MODEL), lambda i: (i, 0)), full(win), full(qg), full(wq), full(kvg),
                  full(w_kv_up), pos(), pos(), pos(), pos()],
        out_specs=[pl.BlockSpec((tm, nq), lambda i: (i, 0)),
                   pl.BlockSpec((tm, nq), lambda i: (i, 0)),
                   pl.BlockSpec((tm, MLA_HB), lambda i: (i, 0))],
        out_shape=[jax.ShapeDtypeStruct((m, nq), _BF), jax.ShapeDtypeStruct((m, nq), _BF),
                   jax.ShapeDtypeStruct((m, MLA_HB), _BF)],
        compiler_params=_cp(("parallel",), 48), name="mla_pre")(
            x2d, win, qg, wq, kvg, w_kv_up, q_keep, q_swap, k_keep, k_swap)


MLA_TQ = 256
MLA_PAIR = 2


def _mla_attn_kernel(q_ref, kv_ref, kr_ref, o_ref, kt_scr, v_scr, s_scr):
    lane = lax.broadcasted_iota(jnp.int32, (SEQ, MLA_HB), 1)
    kr = kr_ref[0].astype(_F32)
    for j in range(MLA_PAIR):
        kvh = kv_ref[0, :, j * MLA_HB:(j + 1) * MLA_HB].astype(_F32)
        kt_scr[j] = jnp.where(lane < MLA_NOPE, kvh, kr).T.astype(_BF)
        v_scr[j] = jnp.where(lane < MLA_NOPE, 1.0, kvh).astype(_BF)
    lane_q = lax.broadcasted_iota(jnp.int32, (MLA_TQ, MLA_HB), 1)

    def block_rows(i):
        return pl.ds(pl.multiple_of(i * MLA_TQ, MLA_TQ), MLA_TQ)

    def scores(i, slot):
        for j in range(MLA_PAIR):
            s_scr[slot, j] = _dot(q_ref[0, block_rows(i), j * MLA_HB:(j + 1) * MLA_HB], kt_scr[j])

    def finish(i, slot):
        outs = []
        for j in range(MLA_PAIR):
            s = s_scr[slot, j]
            p = jnp.exp2(s - jnp.max(s, axis=1, keepdims=True))
            o = _dot(p, v_scr[j])
            outs.append(o / o[:, 0:1])
        even = pltpu.roll(outs[0], MLA_V, axis=1)
        o_ref[0, block_rows(i), :] = jnp.where(lane_q < MLA_V, even, outs[1]).astype(_BF)

    nblk = SEQ // MLA_TQ
    scores(0, 0)

    def body(k, carry):
        scores(2 * k + 1, 1)
        finish(2 * k, 0)
        scores(2 * k + 2, 0)
        finish(2 * k + 1, 1)
        return carry

    lax.fori_loop(0, nblk // 2 - 1, body, 0)
    scores(nblk - 1, 1)
    finish(nblk - 2, 0)
    finish(nblk - 1, 1)


def _mla_attention(q, kv, kr):
    b = q.shape[0]
    w = MLA_PAIR * MLA_HB
    return pl.pallas_call(
        _mla_attn_kernel, grid=(b, MLA_HEADS // MLA_PAIR),
        in_specs=[pl.BlockSpec((1, SEQ, w), lambda i, h: (i, 0, h)),
                  pl.BlockSpec((1, SEQ, w), lambda i, h: (i, 0, h)),
                  pl.BlockSpec((1, SEQ, MLA_HB), lambda i, h: (i, 0, 0))],
        out_specs=pl.BlockSpec((1, SEQ, MLA_PAIR * MLA_V), lambda i, h: (i, 0, h)),
        out_shape=jax.ShapeDtypeStruct((b, SEQ, MLA_HEADS * MLA_V), _BF),
        scratch_shapes=[pltpu.VMEM((MLA_PAIR, MLA_HB, SEQ), _BF), pltpu.VMEM((MLA_PAIR, SEQ, MLA_HB), _BF),
                        pltpu.VMEM((2, MLA_PAIR, MLA_TQ, SEQ), _F32)],
        compiler_params=_cp(("parallel", "parallel"), 40), name="mla_attention")(q, kv, kr)


def _na_hyena_mixer(x, w_in, rpb, conv_w, conv_b, f_w1, f_b1, f_freq, f_w2, f_b2, f_w3, skip):
    b = x.shape[0]
    qkv, hb = _inproj(x.reshape(b * SEQ, D_MODEL), w_in)
    y_a = _na_attention(qkv.reshape(b, SEQ, 3 * NA_WIDTH), rpb)
    hbc, zb = _short_conv(hb.reshape(b, SEQ, (HY_ORDER + 1) * HY_WIDTH), conv_w, conv_b)
    fmat, fmat_t = (jnp.asarray(f).astype(_BF) for f in _dft_matrices())
    kfilt = _hyena_filters(f_w1, f_b1, f_freq, f_w2, f_b2, f_w3)
    kf = _filter_spectrum(fmat, kfilt)
    z = hbc
    for o in range(HY_ORDER):
        z, zb = _long_conv_gate(hbc, z, zb, o + 1, o, fmat, fmat_t, kf, skip)
    return y_a.reshape(b * SEQ, NA_WIDTH), zb.reshape(b * SEQ, HY_WIDTH)


def kernel(x, p, ab_w_in, na_rpb, hy_conv_w, hy_conv_b, hy_f_w1, hy_f_b1, hy_f_freq, hy_f_w2, hy_f_b2, hy_f_w3, hy_skip, ab_w_out, mla_w_in, mla_q_norm, mla_w_q_up, mla_kv_norm, mla_w_kv_up, mla_w_out, ln1_g, ln1_b, ln2_g, ln2_b, moe_router, moe_w_gate, moe_w_up, moe_w_down, ple_gate, ple_proj):
    b = x.shape[0]
    m = b * SEQ
    for i in range(DEPTH):
        j = i // 2
        x2d = x.reshape(m, D_MODEL)
        if i % 2 == 0:
            y_a, z = _na_hyena_mixer(x, ab_w_in[j], na_rpb[j], hy_conv_w[j], hy_conv_b[j], hy_f_w1[j],
                                     hy_f_b1[j], hy_f_freq[j], hy_f_w2[j], hy_f_b2[j], hy_f_w3[j], hy_skip[j])
            x1, x1b, aff = _outproj_ln([y_a, z], ab_w_out[j], x2d, ln1_g[i], ln1_b[i], moe_router[i])
        else:
            q, kv, kr = _mla_pre(x2d, mla_w_in[j], mla_q_norm[j], mla_w_q_up[j], mla_kv_norm[j], mla_w_kv_up[j])
            att = _mla_attention(q.reshape(b, SEQ, -1), kv.reshape(b, SEQ, -1), kr.reshape(b, SEQ, -1))
            x1, x1b, aff = _outproj_ln([att.reshape(m, MLA_HEADS * MLA_V)], mla_w_out[j], x2d, ln1_g[i], ln1_b[i],
                                       moe_router[i])
        x = _moe_ple_block(x1.reshape(b, SEQ, D_MODEL), x1b.reshape(b, SEQ, D_MODEL), aff,
                           moe_w_gate, moe_w_up, moe_w_down, ln2_g[i], ln2_b[i], p, ple_gate, ple_proj, i)
    return x
```

```python
import functools
import math

import numpy as np
import jax
import jax.numpy as jnp
from jax import lax
from jax.experimental import pallas as pl
from jax.experimental.pallas import tpu as pltpu

D_MODEL = 1024
BATCH = 8
SEQ = 2048
DEPTH = 2
GRID_W = 64
PLE_DIM = 256
NA_HEADS = 8
NA_HEAD_DIM = 64
NA_WIDTH = NA_HEADS * NA_HEAD_DIM
NA_WIN_ROWS_MAX = 8
NA_WIN_COLS = 16
HY_WIDTH = D_MODEL - NA_WIDTH
HY_ORDER = 2
HY_SHORT_K = 3
HY_EMB_DIM = 33
HY_FILTER_HIDDEN = 64
HY_FAST_DECAY_PCT = 0.3
HY_SLOW_DECAY_PCT = 1.5
HY_DECAY_TARGET = 1e-2
AB_IN_WIDTH = 3 * NA_WIDTH + (HY_ORDER + 1) * HY_WIDTH
MLA_HEADS = 16
MLA_Q_LORA = 384
MLA_KV_LORA = 256
MLA_NOPE = 64
MLA_ROPE = 32
MLA_V = 64
ROPE_THETA = 10000.0
N_EXPERTS = 16
EC_CAPACITY_FACTOR = 2
D_FF_EXPERT = 2048
DN_ALPHA = (2 * DEPTH) ** 0.25
NORM_EPS = 1e-5
NEG_INF = -1e30

CAP = EC_CAPACITY_FACTOR * SEQ // N_EXPERTS
NA_ROWS = SEQ // GRID_W
NA_WIN_ROWS = min(NA_WIN_ROWS_MAX, NA_ROWS)
NA_SLAB = NA_WIN_ROWS * GRID_W
FFT_N = 2 * SEQ
LOG2E = math.log2(math.e)

LANE = 128
MIB = 1 << 20

_BF = jnp.bfloat16
_F32 = jnp.float32
_HP = lax.Precision.HIGHEST


def _cp(sem, vmem_mib):
    return pltpu.CompilerParams(dimension_semantics=sem, vmem_limit_bytes=vmem_mib * MIB)


def _dot(a, b):
    return jnp.dot(a.astype(_BF), b.astype(_BF), preferred_element_type=_F32)


def _dot_nt(a, b):
    return lax.dot_general(a.astype(_BF), b.astype(_BF), (((1,), (1,)), ((), ())),
                           preferred_element_type=_F32)


def _layer_norm(y, g, b):
    mu = jnp.mean(y, axis=-1, keepdims=True)
    d = y - mu
    var = jnp.mean(d * d, axis=-1, keepdims=True)
    return d * lax.rsqrt(var + NORM_EPS) * g + b


def _inproj_kernel(x_ref, w_ref, qkv_ref, hb_ref):
    h = _dot(x_ref[...], w_ref[...])
    qkv_ref[...] = h[:, :3 * NA_WIDTH].astype(_BF)
    hb_ref[...] = h[:, 3 * NA_WIDTH:]


def _inproj(x2d, w):
    m = x2d.shape[0]
    tm = 512
    nq, nh = 3 * NA_WIDTH, (HY_ORDER + 1) * HY_WIDTH
    return pl.pallas_call(
        _inproj_kernel, grid=(m // tm,),
        in_specs=[pl.BlockSpec((tm, D_MODEL), lambda i: (i, 0)),
                  pl.BlockSpec((D_MODEL, AB_IN_WIDTH), lambda i: (0, 0))],
        out_specs=[pl.BlockSpec((tm, nq), lambda i: (i, 0)),
                   pl.BlockSpec((tm, nh), lambda i: (i, 0))],
        out_shape=[jax.ShapeDtypeStruct((m, nq), _BF), jax.ShapeDtypeStruct((m, nh), _F32)],
        compiler_params=_cp(("parallel",), 48), name="ab_inproj")(x2d, w)


NA_GROUP = 4
NA_GW = NA_GROUP * NA_HEAD_DIM


def _na_kernel(q_ref, k_ref, v_ref, tbl_ref, o_ref):
    lane_head = lax.broadcasted_iota(jnp.int32, (GRID_W, NA_GW), 1) >> int(math.log2(NA_HEAD_DIM))
    scale = NA_HEAD_DIM ** -0.5 * LOG2E

    def body(r, carry):
        r0 = jnp.clip(r - NA_WIN_ROWS // 2, 0, NA_ROWS - NA_WIN_ROWS)
        off = r0 - r + (NA_WIN_ROWS_MAX - 1)
        qr = q_ref[0, pl.ds(pl.multiple_of(r * GRID_W, GRID_W), GRID_W), :].astype(_F32)
        q4 = jnp.concatenate([jnp.where(lane_head == h, qr, 0.0) for h in range(NA_GROUP)], axis=0)
        ks = k_ref[0, pl.ds(pl.multiple_of(r0 * GRID_W, GRID_W), NA_SLAB), :]
        vs = v_ref[0, pl.ds(pl.multiple_of(r0 * GRID_W, GRID_W), NA_SLAB), :]
        bias = jnp.concatenate([tbl_ref[h, off] for h in range(NA_GROUP)], axis=0)
        s = _dot_nt(q4, ks) * scale + bias
        mx = jnp.max(s, axis=1, keepdims=True)
        p = jnp.exp2(s - mx)
        den = jnp.sum(p, axis=1, keepdims=True)
        o4 = _dot(p, vs) / den
        out = jnp.zeros((GRID_W, NA_GW), _F32)
        for h in range(NA_GROUP):
            out = out + jnp.where(lane_head == h, o4[h * GRID_W:(h + 1) * GRID_W], 0.0)
        o_ref[0, pl.ds(pl.multiple_of(r * GRID_W, GRID_W), GRID_W), :] = out.astype(_BF)
        return carry

    lax.fori_loop(0, NA_ROWS, body, 0, unroll=4)


def _na_bias_table(rpb):
    c = np.arange(GRID_W)
    c0 = np.clip(c - NA_WIN_COLS // 2, 0, GRID_W - NA_WIN_COLS)
    kc = np.arange(GRID_W)
    col_ok = (kc[None, :] >= c0[:, None]) & (kc[None, :] < c0[:, None] + NA_WIN_COLS)
    dc_idx = np.clip(kc[None, :] - c[:, None], -(NA_WIN_COLS - 1), NA_WIN_COLS - 1) + (NA_WIN_COLS - 1)
    pick = (dc_idx[None, :, :] == np.arange(2 * NA_WIN_COLS - 1)[:, None, None]).astype(np.float32)
    per_dr = jnp.einsum("hdk,kqc->hdqc", rpb.astype(_F32), pick, precision=_HP)
    per_dr = jnp.where(col_ok[None, None], per_dr * LOG2E, NEG_INF)
    slabs = jnp.stack([per_dr[:, off:off + NA_WIN_ROWS] for off in range(NA_WIN_ROWS_MAX)], axis=1)
    return slabs.transpose(0, 1, 3, 2, 4).reshape(NA_HEADS, NA_WIN_ROWS_MAX, GRID_W, NA_SLAB)


def _na_attention(qkv, rpb):
    b = qkv.shape[0]
    tbl = _na_bias_table(rpb)
    ng = NA_HEADS // NA_GROUP
    blk = lambda col0: pl.BlockSpec((1, SEQ, NA_GW), lambda i, g, c=col0: (i, 0, c + g))
    return pl.pallas_call(
        _na_kernel, grid=(b, ng),
        in_specs=[blk(0), blk(ng), blk(2 * ng),
                  pl.BlockSpec((NA_GROUP, NA_WIN_ROWS_MAX, GRID_W, NA_SLAB), lambda i, g: (g, 0, 0, 0))],
        out_specs=pl.BlockSpec((1, SEQ, NA_GW), lambda i, g: (i, 0, g)),
        out_shape=jax.ShapeDtypeStruct((b, SEQ, NA_WIDTH), _BF),
        compiler_params=_cp(("parallel", "parallel"), 32), name="na_attention")(qkv, qkv, qkv, tbl)


def _sconv_kernel(x_ref, w_ref, b_ref, o_ref, vb_ref):
    x = x_ref[0]
    n = x.shape[0]
    row = lax.broadcasted_iota(jnp.int32, x.shape, 0)
    xm = jnp.where(row == 0, 0.0, pltpu.roll(x, 1, axis=0))
    xp = jnp.where(row == n - 1, 0.0, pltpu.roll(x, n - 1, axis=0))
    y = b_ref[...] + xm * w_ref[0:1, :] + x * w_ref[1:2, :] + xp * w_ref[2:3, :]
    o_ref[0] = y

    @pl.when(pl.program_id(1) == 0)
    def _():
        vb_ref[0] = y.astype(_BF)


def _short_conv(hb, w, bias):
    b, n, c = hb.shape
    tc = HY_WIDTH
    return pl.pallas_call(
        _sconv_kernel, grid=(b, c // tc),
        in_specs=[pl.BlockSpec((1, n, tc), lambda i, j: (i, 0, j)),
                  pl.BlockSpec((HY_SHORT_K, tc), lambda i, j: (0, j)),
                  pl.BlockSpec((1, tc), lambda i, j: (0, j))],
        out_specs=[pl.BlockSpec((1, n, tc), lambda i, j: (i, 0, j)),
                   pl.BlockSpec((1, n, tc), lambda i, j: (i, 0, 0))],
        out_shape=[jax.ShapeDtypeStruct((b, n, c), _F32), jax.ShapeDtypeStruct((b, n, tc), _BF)],
        compiler_params=_cp(("parallel", "arbitrary"), 32), name="hy_short_conv")(hb, w, bias.reshape(1, c))


HY_HID_PAD = LANE
HY_FILT_TC = 256


def _filter_kernel(z_ref, w1_ref, b1_ref, fr_ref, w2_ref, b2_ref, w3f_ref, w3b_ref, dec_ref, k_ref, h_ref):
    @pl.when(jnp.logical_and(pl.program_id(0) == 0, pl.program_id(1) == 0))
    def _():
        fr = fr_ref[...]
        h1 = jnp.sin(fr * (jnp.dot(z_ref[...], w1_ref[...], precision=_HP, preferred_element_type=_F32) + b1_ref[...]))
        h_ref[...] = jnp.sin(fr * (jnp.dot(h1, w2_ref[...], precision=_HP, preferred_element_type=_F32) + b2_ref[...]))

    fwd = jnp.dot(h_ref[:SEQ, :], w3f_ref[...], precision=_HP, preferred_element_type=_F32)
    bwd = jnp.dot(h_ref[SEQ:, :], w3b_ref[...], precision=_HP, preferred_element_type=_F32)
    k = jnp.concatenate([fwd, bwd], axis=0) * dec_ref[...]
    ss = jnp.sum(k * k, axis=0, keepdims=True)
    k_ref[...] = (k * lax.rsqrt(ss + 1e-12)).astype(_BF)


@functools.lru_cache(maxsize=None)
def _filter_tables():
    bands = (HY_EMB_DIM - 1) // 2
    t = np.linspace(0.0, 1.0, SEQ)[:, None]
    w = 2.0 * math.pi * np.arange(SEQ)[:, None] / SEQ
    f = np.linspace(1e-4, bands - 1, bands)[None, :]
    z = np.concatenate([t, np.cos(f * w), -np.sin(f * w)], axis=-1)
    min_decay = math.log(HY_DECAY_TARGET) / HY_SLOW_DECAY_PCT
    max_decay = math.log(HY_DECAY_TARGET) / HY_FAST_DECAY_PCT
    deltas = np.abs(np.linspace(min_decay, max_decay, HY_WIDTH))
    dec = np.exp(-t * deltas)
    src = np.concatenate([np.arange(SEQ), [0], np.arange(SEQ - 1, 0, -1)])
    live = np.ones((FFT_N, 1))
    live[SEQ] = 0.0
    z2 = np.pad(z[src], ((0, 0), (0, HY_HID_PAD - HY_EMB_DIM))).astype(np.float32)
    dec2 = (dec[src] * live).astype(np.float32)
    return z2, dec2


def _hyena_filters(w1, b1, freq, w2, b2, w3):
    z2, dec2 = _filter_tables()
    hp = HY_HID_PAD - HY_FILTER_HIDDEN
    w1p = jnp.pad(w1, ((0, HY_HID_PAD - HY_EMB_DIM), (0, hp)))
    w2p = jnp.pad(w2, ((0, hp), (0, hp)))
    w3p = jnp.pad(w3, ((0, hp), (0, 0)))
    row = lambda v: jnp.pad(v, (0, hp)).reshape(1, HY_HID_PAD)
    nc = HY_WIDTH // HY_FILT_TC
    per_order = 2 * nc
    full = lambda shape: pl.BlockSpec(shape, lambda o, j: (0, 0))
    return pl.pallas_call(
        _filter_kernel, grid=(HY_ORDER, nc),
        in_specs=[full((FFT_N, HY_HID_PAD)), full((HY_HID_PAD, HY_HID_PAD)), full((1, HY_HID_PAD)),
                  full((1, HY_HID_PAD)), full((HY_HID_PAD, HY_HID_PAD)), full((1, HY_HID_PAD)),
                  pl.BlockSpec((HY_HID_PAD, HY_FILT_TC), lambda o, j: (0, o * per_order + j)),
                  pl.BlockSpec((HY_HID_PAD, HY_FILT_TC), lambda o, j: (0, o * per_order + nc + j)),
                  pl.BlockSpec((FFT_N, HY_FILT_TC), lambda o, j: (0, j))],
        out_specs=pl.BlockSpec((FFT_N, HY_FILT_TC), lambda o, j: (0, o * nc + j)),
        out_shape=jax.ShapeDtypeStruct((FFT_N, HY_ORDER * HY_WIDTH), _BF),
        scratch_shapes=[pltpu.VMEM((FFT_N, HY_HID_PAD), _F32)],
        compiler_params=_cp(("arbitrary", "arbitrary"), 40), name="hy_filters")(
            z2, w1p, row(b1), row(freq), w2p, row(b2), w3p, w3p, dec2)


@functools.lru_cache(maxsize=None)
def _dft_matrices():
    t = np.arange(SEQ)
    ang = ((t[:, None] * t[None, :]) % FFT_N) * (2.0 * math.pi / FFT_N)
    re = np.cos(ang)
    im = -np.sin(ang)
    im[0] = 1.0 - 2.0 * (t % 2)
    nf = SEQ // HY_FB
    packed = np.stack([re.reshape(nf, HY_FB, SEQ), im.reshape(nf, HY_FB, SEQ)], axis=1).reshape(FFT_N, SEQ)
    return packed.astype(np.float32), np.ascontiguousarray(packed.T).astype(np.float32)


def _kf_kernel(f_ref, k_ref, o_ref):
    i = pl.program_id(0)
    f = f_ref[...]
    tm = f.shape[0]
    p1 = jnp.dot(f, k_ref[:SEQ, :], preferred_element_type=_F32)
    p2 = jnp.dot(f, k_ref[SEQ:, :], preferred_element_type=_F32)
    row = lax.broadcasted_iota(jnp.int32, (tm, 1), 0) + i * tm
    sign = (1 - 2 * (row & 1)).astype(_F32)
    o_ref[...] = p1 + sign * p2


def _filter_spectrum(fmat, kfilt):
    tm = 512
    nw = kfilt.shape[1]
    return pl.pallas_call(
        _kf_kernel, grid=(FFT_N // tm,),
        in_specs=[pl.BlockSpec((tm, SEQ), lambda i: (i, 0)),
                  pl.BlockSpec((FFT_N, nw), lambda i: (0, 0))],
        out_specs=pl.BlockSpec((tm, nw), lambda i: (i, 0)),
        out_shape=jax.ShapeDtypeStruct((FFT_N, nw), _F32),
        compiler_params=_cp(("parallel",), 40), name="hy_filter_spectrum")(fmat, kfilt)


HY_FB = 512
HY_NB = 2


HY_TM = 512


def _hconv_fwd_kernel(zb_ref, f_ref, k_ref, y_ref):
    fk = pl.program_id(1)
    kr = k_ref[:HY_FB, :]
    ki = k_ref[HY_FB:, :]
    row0 = jnp.logical_and(lax.broadcasted_iota(jnp.int32, kr.shape, 0) == 0, fk == 0)
    sc = jnp.where(row0, 1.0 / FFT_N, 2.0 / FFT_N)
    for j in range(HY_NB):
        u = jnp.dot(f_ref[...], zb_ref[j], preferred_element_type=_F32)
        ur = u[:HY_FB]
        ui = u[HY_FB:]
        yr = jnp.where(row0, ur * kr, ur * kr - ui * ki)
        yi = jnp.where(row0, ui * ki, ur * ki + ui * kr)
        y_ref[j, 0, 0] = (yr * sc).astype(_BF)
        y_ref[j, 0, 1] = (yi * sc).astype(_BF)


def _hconv_inv_kernel(ft_ref, y_ref, z_ref, skip_ref, xn_ref, o_ref, ob_ref):
    for j in range(HY_NB):
        conv = jnp.dot(ft_ref[...], y_ref[j], preferred_element_type=_F32)
        out = xn_ref[j] * (conv + z_ref[j] * skip_ref[...])
        o_ref[j] = out
        ob_ref[j] = out.astype(_BF)


def _long_conv_gate(hbc, zsrc, zb, xn_col, order, fmat, fmat_t, kf, skip):
    b = hbc.shape[0]
    nf = SEQ // HY_FB
    w = HY_WIDTH
    nb = HY_NB
    y = pl.pallas_call(
        _hconv_fwd_kernel, grid=(b // nb, nf),
        in_specs=[pl.BlockSpec((nb, SEQ, w), lambda i, f: (i, 0, 0)),
                  pl.BlockSpec((2 * HY_FB, SEQ), lambda i, f: (f, 0)),
                  pl.BlockSpec((2 * HY_FB, w), lambda i, f: (f, order))],
        out_specs=pl.BlockSpec((nb, 1, 2, HY_FB, w), lambda i, f: (i, f, 0, 0, 0)),
        out_shape=jax.ShapeDtypeStruct((b, nf, 2, HY_FB, w), _BF),
        compiler_params=_cp(("parallel", "parallel"), 40), name=f"hy_conv_fwd{order}")(zb, fmat, kf)
    blk = lambda col: pl.BlockSpec((nb, HY_TM, w), lambda i, m, c=col: (i, m, c))
    return pl.pallas_call(
        _hconv_inv_kernel, grid=(b // nb, SEQ // HY_TM),
        in_specs=[pl.BlockSpec((HY_TM, FFT_N), lambda i, m: (m, 0)),
                  pl.BlockSpec((nb, FFT_N, w), lambda i, m: (i, 0, 0)),
                  blk(0),
                  pl.BlockSpec((1, w), lambda i, m: (0, 0)),
                  blk(xn_col)],
        out_specs=[blk(0), blk(0)],
        out_shape=[jax.ShapeDtypeStruct((b, SEQ, w), _F32), jax.ShapeDtypeStruct((b, SEQ, w), _BF)],
        compiler_params=_cp(("parallel", "parallel"), 48), name=f"hy_conv_inv{order}")(
            fmat_t, y.reshape(b, FFT_N, w), zsrc, skip[order].reshape(1, w), hbc)


def _outproj_ln_kernel(n_a, *refs):
    a_refs = refs[:n_a]
    w_refs = refs[n_a:2 * n_a]
    x_ref, g_ref, b_ref, wr_ref, o_ref, ob_ref, aff_ref = refs[2 * n_a:]
    m = _dot(a_refs[0][...], w_refs[0][...])
    for a_ref, w_ref in zip(a_refs[1:], w_refs[1:]):
        m = m + _dot(a_ref[...], w_ref[...])
    y = _layer_norm(DN_ALPHA * x_ref[...] + m, g_ref[...], b_ref[...])
    o_ref[...] = y
    ob_ref[...] = y.astype(_BF)
    wr = wr_ref[...]
    wr_hi = wr.astype(_BF)
    wr_lo = (wr - wr_hi.astype(_F32)).astype(_BF)
    y_hi = y.astype(_BF)
    y_lo = (y - y_hi.astype(_F32)).astype(_BF)
    by_hi = _dot_nt(jnp.concatenate([wr_hi, wr_lo], axis=0), y_hi)
    logits = by_hi[:N_EXPERTS] + by_hi[N_EXPERTS:] + _dot_nt(wr_hi, y_lo)
    ex = jnp.exp(logits - jnp.max(logits, axis=0, keepdims=True))
    aff_ref[0] = ex / jnp.sum(ex, axis=0, keepdims=True)


def _outproj_ln(a_list, w, x2d, g, b, w_router):
    m = x2d.shape[0]
    tm = 512
    per_seq = SEQ // tm
    in_specs, w_args, row0 = [], [], 0
    for a in a_list:
        in_specs.append(pl.BlockSpec((tm, a.shape[1]), lambda i: (i, 0)))
    for a in a_list:
        ka = a.shape[1]
        in_specs.append(pl.BlockSpec((ka, D_MODEL), lambda i, r=row0 // ka: (r, 0)))
        w_args.append(w)
        row0 += ka
    in_specs += [pl.BlockSpec((tm, D_MODEL), lambda i: (i, 0)),
                 pl.BlockSpec((1, D_MODEL), lambda i: (0, 0)),
                 pl.BlockSpec((1, D_MODEL), lambda i: (0, 0)),
                 pl.BlockSpec((N_EXPERTS, D_MODEL), lambda i: (0, 0))]
    return pl.pallas_call(
        functools.partial(_outproj_ln_kernel, len(a_list)), grid=(m // tm,),
        in_specs=in_specs,
        out_specs=[pl.BlockSpec((tm, D_MODEL), lambda i: (i, 0)),
                   pl.BlockSpec((tm, D_MODEL), lambda i: (i, 0)),
                   pl.BlockSpec((1, N_EXPERTS, tm), lambda i: (i // per_seq, 0, i % per_seq))],
        out_shape=[jax.ShapeDtypeStruct((m, D_MODEL), _F32), jax.ShapeDtypeStruct((m, D_MODEL), _BF),
                   jax.ShapeDtypeStruct((m // SEQ, N_EXPERTS, SEQ), _F32)],
        compiler_params=_cp(("parallel",), 40), name="outproj_ln_router")(
            *a_list, *w_args, x2d, g.reshape(1, -1), b.reshape(1, -1), w_router.T)


BISECT_STEPS = 160


def _select_kernel(a_ref, tri_ref, edge_ref, slot_ref, slot_t_ref, cnt_ref):
    a = a_ref[...]
    rows = a.shape[0]
    cap = float(CAP)

    def body(_, c):
        lo, hi = c
        mid = 0.5 * (lo + hi)
        cnt = jnp.sum(jnp.where(a > mid, 1.0, 0.0), axis=1, keepdims=True)
        ge = cnt >= cap
        return jnp.where(ge, mid, lo), jnp.where(ge, hi, mid)

    lo, hi = lax.fori_loop(0, BISECT_STEPS, body,
                           (jnp.full((rows, 1), -1.0, _F32), jnp.full((rows, 1), 1.0, _F32)))
    vstar = jnp.max(jnp.where(a <= hi, a, -1.0), axis=1, keepdims=True)
    gt = a > vstar
    eq = a == vstar
    need = cap - jnp.sum(jnp.where(gt, 1.0, 0.0), axis=1, keepdims=True)
    tri = tri_ref[...]
    eq_before = jnp.dot(jnp.where(eq, 1.0, 0.0).astype(_BF), tri, preferred_element_type=_F32)
    sel = jnp.where(gt, 1.0, jnp.where(eq, jnp.where(eq_before < need, 1.0, 0.0), 0.0))
    pos = jnp.dot(sel.astype(_BF), tri, preferred_element_type=_F32)
    slot = jnp.where(sel > 0.5, pos, -1.0)
    slot_ref[...] = slot.astype(jnp.int32)
    cnt_ref[...] = jnp.dot(sel.astype(_BF), edge_ref[...], preferred_element_type=_F32).astype(jnp.int32)
    if rows < LANE:
        slot = jnp.concatenate([slot, jnp.full((LANE - rows, SEQ), -1.0, _F32)], axis=0)
    slot_tm = slot.T
    for b in range(rows // N_EXPERTS):
        shifted = slot_tm if b == 0 else pltpu.roll(slot_tm, LANE - N_EXPERTS * b, axis=1)
        slot_t_ref[b] = shifted.astype(jnp.int32)


@functools.lru_cache(maxsize=None)
def _prefix_matrix():
    idx = np.arange(SEQ)
    return (idx[:, None] < idx[None, :]).astype(_BF)


WIN_TT = 256
WIN_W = 128
WIN_EDGES = 16


@functools.lru_cache(maxsize=None)
def _edge_matrix():
    return (np.arange(SEQ)[:, None] < np.arange(LANE)[None, :] * WIN_TT).astype(_BF)


def _select(aff2d):
    rows = aff2d.shape[0]
    assert rows <= LANE and rows % N_EXPERTS == 0
    bsz = rows // N_EXPERTS
    slot, slot_t, cnt = pl.pallas_call(
        _select_kernel, grid=(1,),
        in_specs=[pl.BlockSpec((rows, SEQ), lambda i: (0, 0)),
                  pl.BlockSpec((SEQ, SEQ), lambda i: (0, 0)),
                  pl.BlockSpec((SEQ, LANE), lambda i: (0, 0))],
        out_specs=[pl.BlockSpec((rows, SEQ), lambda i: (0, 0)),
                   pl.BlockSpec((bsz, SEQ, LANE), lambda i: (0, 0, 0)),
                   pl.BlockSpec((rows, LANE), lambda i: (0, 0))],
        out_shape=[jax.ShapeDtypeStruct((rows, SEQ), jnp.int32),
                   jax.ShapeDtypeStruct((bsz, SEQ, LANE), jnp.int32),
                   jax.ShapeDtypeStruct((rows, LANE), jnp.int32)],
        compiler_params=_cp(("arbitrary",), 48), name="moe_select")(aff2d, _prefix_matrix(), _edge_matrix())
    return slot, slot_t, cnt[:, :WIN_EDGES].reshape(rows * WIN_EDGES)


def _onehot(slot_row):
    return slot_row == lax.broadcasted_iota(jnp.int32, (CAP, SEQ), 0)


GATHER_NE = 4
GATHER_TC = 512


def _window_start(cnt_ref, seq, e, tile):
    base = (seq * N_EXPERTS + e) * WIN_EDGES + tile
    start = jnp.minimum((cnt_ref[base] >> 4) << 4, CAP - WIN_W)
    return start, cnt_ref[base + 1] - start <= WIN_W


def _gather_kernel(cnt_ref, slot_ref, a_ref, xb_ref, xe_ref, g_ref):
    i = pl.program_id(0)
    n_tiles = SEQ // WIN_TT
    starts, fits = {}, None
    for t in range(n_tiles):
        for e in range(N_EXPERTS):
            starts[e, t], ok = _window_start(cnt_ref, i, e, t)
            fits = ok if fits is None else jnp.logical_and(fits, ok)

    @pl.when(fits)
    def _():
        xe_ref[...] = jnp.zeros(xe_ref.shape, _BF)
        g_ref[...] = jnp.zeros(g_ref.shape, _F32)
        sub_w = lax.broadcasted_iota(jnp.int32, (WIN_W, WIN_TT), 0)
        for t in range(n_tiles):
            toks = slice(t * WIN_TT, (t + 1) * WIN_TT)
            ps = []
            for e in range(N_EXPERTS):
                hit = slot_ref[0, e, :, toks] - starts[e, t] == sub_w
                ps.append(jnp.where(hit, 1.0, 0.0).astype(_BF))
                rows = pl.ds(pl.multiple_of(starts[e, t], 16), WIN_W)
                g_ref[e, rows, :] += jnp.sum(jnp.where(hit, a_ref[0, e, :, toks], 0.0), axis=1, keepdims=True)
            pcat = jnp.concatenate(ps, axis=0)
            for c in range(D_MODEL // GATHER_TC):
                cols = slice(c * GATHER_TC, (c + 1) * GATHER_TC)
                res = jnp.dot(pcat, xb_ref[0, toks, cols], preferred_element_type=_F32).astype(_BF)
                for e in range(N_EXPERTS):
                    rows = pl.ds(pl.multiple_of(starts[e, t], 16), WIN_W)
                    xe_ref[e, rows, cols] += res[e * WIN_W:(e + 1) * WIN_W]

    @pl.when(jnp.logical_not(fits))
    def _():
        for grp in range(N_EXPERTS // GATHER_NE):
            es = range(grp * GATHER_NE, (grp + 1) * GATHER_NE)
            hits = [_onehot(slot_ref[0, e]) for e in es]
            p = jnp.concatenate([jnp.where(h, 1.0, 0.0).astype(_BF) for h in hits], axis=0)
            xe = jnp.dot(p, xb_ref[0], preferred_element_type=_F32).astype(_BF)
            xe_ref[grp * GATHER_NE:(grp + 1) * GATHER_NE] = xe.reshape(GATHER_NE, CAP, D_MODEL)
            for k, e in enumerate(es):
                g_ref[e] = jnp.sum(jnp.where(hits[k], a_ref[0, e], 0.0), axis=1, keepdims=True)


def _gather(cnt, slot4, aff4, xb3d):
    b = xb3d.shape[0]
    row = pl.BlockSpec((1, N_EXPERTS, 1, SEQ), lambda i, c: (i, 0, 0, 0))
    grid_spec = pltpu.PrefetchScalarGridSpec(
        num_scalar_prefetch=1, grid=(b,),
        in_specs=[row, row, pl.BlockSpec((1, SEQ, D_MODEL), lambda i, c: (i, 0, 0))],
        out_specs=[pl.BlockSpec((N_EXPERTS, CAP, D_MODEL), lambda i, c: (0, i, 0)),
                   pl.BlockSpec((N_EXPERTS, CAP, 1), lambda i, c: (0, i, 0))])
    return pl.pallas_call(
        _gather_kernel, grid_spec=grid_spec,
        out_shape=[jax.ShapeDtypeStruct((N_EXPERTS, b * CAP, D_MODEL), _BF),
                   jax.ShapeDtypeStruct((N_EXPERTS, b * CAP, 1), _F32)],
        compiler_params=_cp(("parallel",), 56), name="moe_gather")(cnt, slot4, aff4, xb3d)


FFN_TF = 512
FFN_TM = 512


def _ffn_up_kernel(xe_ref, wg_ref, wu_ref, h_ref):
    wg = wg_ref[0].astype(_BF)
    wu = wu_ref[0].astype(_BF)
    for c in range(xe_ref.shape[1] // FFN_TM):
        rows = slice(c * FFN_TM, (c + 1) * FFN_TM)
        xe = xe_ref[0, rows, :]
        hg = _dot(xe, wg)
        hu = _dot(xe, wu)
        h_ref[0, rows, :] = ((hg * (1.0 / (1.0 + jnp.exp(-hg)))) * hu).astype(_BF)


def _ffn_down_kernel(h_ref, wd_ref, g_ref, ye_ref):
    wd = wd_ref[0].astype(_BF)
    for c in range(h_ref.shape[1] // FFN_TM):
        rows = slice(c * FFN_TM, (c + 1) * FFN_TM)
        ye_ref[0, rows, :] = (_dot(h_ref[0, rows, :], wd) * g_ref[0, rows, :]).astype(_BF)


def _expert_ffn(xe, gates, wg, wu, wd, layer):
    rows = xe.shape[1]
    h = pl.pallas_call(
        _ffn_up_kernel, grid=(N_EXPERTS, D_FF_EXPERT // FFN_TF),
        in_specs=[pl.BlockSpec((1, rows, D_MODEL), lambda e, f: (e, 0, 0)),
                  pl.BlockSpec((None, 1, D_MODEL, FFN_TF), lambda e, f: (layer, e, 0, f)),
                  pl.BlockSpec((None, 1, D_MODEL, FFN_TF), lambda e, f: (layer, e, 0, f))],
        out_specs=pl.BlockSpec((1, rows, FFN_TF), lambda e, f: (e, 0, f)),
        out_shape=jax.ShapeDtypeStruct((N_EXPERTS, rows, D_FF_EXPERT), _BF),
        compiler_params=_cp(("parallel", "parallel"), 40), name="moe_ffn_up")(xe, wg, wu)
    return pl.pallas_call(
        _ffn_down_kernel, grid=(N_EXPERTS,),
        in_specs=[pl.BlockSpec((1, rows, D_FF_EXPERT), lambda e: (e, 0, 0)),
                  pl.BlockSpec((None, 1, D_FF_EXPERT, D_MODEL), lambda e: (layer, e, 0, 0)),
                  pl.BlockSpec((1, rows, 1), lambda e: (e, 0, 0))],
        out_specs=pl.BlockSpec((1, rows, D_MODEL), lambda e: (e, 0, 0)),
        out_shape=jax.ShapeDtypeStruct((N_EXPERTS, rows, D_MODEL), _BF),
        compiler_params=_cp(("parallel",), 56), name="moe_ffn_down")(h, wd, gates)


def _scatter_ln_ple_kernel(cnt_ref, slot_t_ref, ye_ref, x_ref, g_ref, b_ref, p_ref, wg_ref, wp_ref, o_ref,
                           pt_ref, yw_ref, f_ref):
    i = pl.program_id(0)
    r = pl.program_id(1)
    slot_t = slot_t_ref[0]
    starts = []
    fits = None
    for e in range(N_EXPERTS):
        start, ok = _window_start(cnt_ref, i, e, r)
        starts.append(start)
        fits = ok if fits is None else jnp.logical_and(fits, ok)

    @pl.when(fits)
    def _():
        lane_w = lax.broadcasted_iota(jnp.int32, (WIN_TT, WIN_W), 1)
        for e in range(N_EXPERTS):
            cols = slice(e * WIN_W, (e + 1) * WIN_W)
            pt_ref[:, cols] = jnp.where(slot_t[:, e:e + 1] - starts[e] == lane_w, 1.0, 0.0).astype(_BF)
            yw_ref[cols, :] = ye_ref[e, pl.ds(pl.multiple_of(starts[e], 16), WIN_W), :]
        f_ref[...] = jnp.dot(pt_ref[:, :N_EXPERTS * WIN_W], yw_ref[...], preferred_element_type=_F32)

    @pl.when(jnp.logical_not(fits))
    def _():
        lane_c = lax.broadcasted_iota(jnp.int32, (WIN_TT, CAP), 1)
        for e in range(N_EXPERTS):
            pt_ref[:, e * CAP:(e + 1) * CAP] = jnp.where(slot_t[:, e:e + 1] == lane_c, 1.0, 0.0).astype(_BF)
        f_ref[...] = jnp.dot(pt_ref[...], ye_ref[...].reshape(N_EXPERTS * CAP, D_MODEL),
                             preferred_element_type=_F32)

    y = _layer_norm(DN_ALPHA * x_ref[0] + f_ref[...], g_ref[...], b_ref[...])
    gate = _dot(y, wg_ref[...])
    emb = _dot(p_ref[0], wp_ref[...])
    o_ref[0] = y + (1.0 / (1.0 + jnp.exp(-gate))) * emb


def _scatter_ln_ple(cnt, slot_t, ye, x3d, g, b, p, ple_gate, ple_proj, layer):
    bsz = x3d.shape[0]
    vec = pl.BlockSpec((1, D_MODEL), lambda i, r, c: (0, 0))
    seq = pl.BlockSpec((1, WIN_TT, D_MODEL), lambda i, r, c: (i, r, 0))
    grid_spec = pltpu.PrefetchScalarGridSpec(
        num_scalar_prefetch=1, grid=(bsz, SEQ // WIN_TT),
        in_specs=[pl.BlockSpec((1, WIN_TT, LANE), lambda i, r, c: (i, r, 0)),
                  pl.BlockSpec((N_EXPERTS, CAP, D_MODEL), lambda i, r, c: (0, i, 0)),
                  seq, vec, vec,
                  pl.BlockSpec((None, 1, WIN_TT, PLE_DIM), lambda i, r, c: (layer, i, r, 0)),
                  pl.BlockSpec((None, D_MODEL, D_MODEL), lambda i, r, c: (layer, 0, 0)),
                  pl.BlockSpec((None, PLE_DIM, D_MODEL), lambda i, r, c: (layer, 0, 0))],
        out_specs=seq,
        scratch_shapes=[pltpu.VMEM((WIN_TT, N_EXPERTS * CAP), _BF),
                        pltpu.VMEM((N_EXPERTS * WIN_W, D_MODEL), _BF),
                        pltpu.VMEM((WIN_TT, D_MODEL), _F32)])
    return pl.pallas_call(
        _scatter_ln_ple_kernel, grid_spec=grid_spec,
        out_shape=jax.ShapeDtypeStruct((bsz, SEQ, D_MODEL), _F32),
        compiler_params=_cp(("parallel", "arbitrary"), 56), name="moe_scatter_ln_ple")(
            cnt, slot_t, ye, x3d, g.reshape(1, -1), b.reshape(1, -1), p, ple_gate, ple_proj)


def _moe_ple_block(x1, x1b, aff, wg, wu, wd, g, b, p, ple_gate, ple_proj, layer):
    bsz = x1.shape[0]
    slot, slot_t, cnt = _select(aff.reshape(bsz * N_EXPERTS, SEQ))
    slot4 = slot.reshape(bsz, N_EXPERTS, 1, SEQ)
    aff4 = aff.reshape(bsz, N_EXPERTS, 1, SEQ)
    xe, gates = _gather(cnt, slot4, aff4, x1b)
    ye = _expert_ffn(xe, gates, wg, wu, wd, layer)
    return _scatter_ln_ple(cnt, slot_t, ye, x1, g, b, p, ple_gate, ple_proj, layer)


MLA_HB = LANE
MLA_IN_PAD = 768
MLA_KR_COL = MLA_Q_LORA + MLA_KV_LORA


def _rms(x, g):
    return x * lax.rsqrt(jnp.mean(x * x, axis=-1, keepdims=True) + NORM_EPS) * g


def _mla_pre_kernel(x_ref, win_ref, qg_ref, wq_ref, kvg_ref, wkv_ref, qa_ref, qb_ref, ka_ref, kb_ref,
                    q_ref, kv_ref, kr_ref):
    h = _dot(x_ref[...], win_ref[...])
    q = _dot(_rms(h[:, :MLA_Q_LORA], qg_ref[...]), wq_ref[...])
    swap = pltpu.roll(q, q.shape[1] - MLA_ROPE, axis=1)
    qa = jnp.concatenate([qa_ref[...]] * MLA_HEADS, axis=1)
    qb = jnp.concatenate([qb_ref[...]] * MLA_HEADS, axis=1)
    q_ref[...] = (q * qa + swap * qb).astype(_BF)
    kv_ref[...] = _dot(_rms(h[:, MLA_Q_LORA:MLA_KR_COL], kvg_ref[...]), wkv_ref[...]).astype(_BF)
    kr = h[:, MLA_KR_COL:MLA_KR_COL + MLA_HB]
    kr_ref[...] = (kr * ka_ref[...] + pltpu.roll(kr, MLA_HB - MLA_ROPE, axis=1) * kb_ref[...]).astype(_BF)


@functools.lru_cache(maxsize=None)
def _rope_lane_tables():
    inv = 1.0 / (ROPE_THETA ** (np.arange(0, MLA_ROPE, 2) / MLA_ROPE))
    ang = np.arange(SEQ)[:, None] * inv[None, :]
    cos, sin = np.cos(ang), np.sin(ang)
    ones = np.ones((SEQ, MLA_NOPE))
    zeros_n = np.zeros((SEQ, MLA_NOPE))
    zeros_r = np.zeros((SEQ, MLA_ROPE))
    cc = np.concatenate([cos, cos], axis=1)
    ss = np.concatenate([-sin, sin], axis=1)
    keep_q = np.concatenate([ones, cc, zeros_r], axis=1)
    keep_k = np.concatenate([zeros_n, cc, zeros_r], axis=1)
    swp = np.concatenate([zeros_n, ss, zeros_r], axis=1)
    qscale = (MLA_NOPE + MLA_ROPE) ** -0.5 * LOG2E
    f32 = lambda a: a.astype(np.float32)
    return f32(keep_q * qscale), f32(swp * qscale), f32(keep_k), f32(swp)


def _mla_pre(x2d, w_in, q_norm, w_q_up, kv_norm, w_kv_up):
    m = x2d.shape[0]
    tm = 512
    half = MLA_ROPE // 2
    kr_w = w_in[:, MLA_KR_COL:]
    kr_swapped = jnp.concatenate([kr_w[:, half:], kr_w[:, :half]], axis=1)
    win = jnp.concatenate([w_in[:, :MLA_KR_COL], jnp.zeros((D_MODEL, MLA_NOPE), _F32), kr_w, kr_swapped], axis=1)
    wq = w_q_up.reshape(MLA_Q_LORA, MLA_HEADS, MLA_NOPE + MLA_ROPE)
    rope_w = wq[:, :, MLA_NOPE:]
    wq = jnp.concatenate([wq, rope_w[:, :, half:], rope_w[:, :, :half]], axis=2).reshape(MLA_Q_LORA, MLA_HEADS * MLA_HB)
    q_keep, q_swap, k_keep, k_swap = _rope_lane_tables()
    nq = MLA_HEADS * MLA_HB
    pos = lambda: pl.BlockSpec((tm, MLA_HB), lambda i: (i % (SEQ // tm), 0))
    full = lambda a: pl.BlockSpec(a.shape, lambda i: (0,) * a.ndim)
    qg, kvg = q_norm.reshape(1, -1), kv_norm.reshape(1, -1)
    return pl.pallas_call(
        _mla_pre_kernel, grid=(m // tm,),
        in_specs=[pl.BlockSpec((tm, D_MODEL), lambda i: (i, 0)), full(win), full(qg), full(wq), full(kvg),
                  full(w_kv_up), pos(), pos(), pos(), pos()],
        out_specs=[pl.BlockSpec((tm, nq), lambda i: (i, 0)),
                   pl.BlockSpec((tm, nq), lambda i: (i, 0)),
                   pl.BlockSpec((tm, MLA_HB), lambda i: (i, 0))],
        out_shape=[jax.ShapeDtypeStruct((m, nq), _BF), jax.ShapeDtypeStruct((m, nq), _BF),
                   jax.ShapeDtypeStruct((m, MLA_HB), _BF)],
        compiler_params=_cp(("parallel",), 48), name="mla_pre")(
            x2d, win, qg, wq, kvg, w_kv_up, q_keep, q_swap, k_keep, k_swap)


MLA_TQ = 256
MLA_PAIR = 2


def _mla_attn_kernel(q_ref, kv_ref, kr_ref, o_ref, kt_scr, v_scr, s_scr):
    lane = lax.broadcasted_iota(jnp.int32, (SEQ, MLA_HB), 1)
    kr = kr_ref[0].astype(_F32)
    for j in range(MLA_PAIR):
        kvh = kv_ref[0, :, j * MLA_HB:(j + 1) * MLA_HB].astype(_F32)
        kt_scr[j] = jnp.where(lane < MLA_NOPE, kvh, kr).T.astype(_BF)
        v_scr[j] = jnp.where(lane < MLA_NOPE, 1.0, kvh).astype(_BF)
    lane_q = lax.broadcasted_iota(jnp.int32, (MLA_TQ, MLA_HB), 1)

    def block_rows(i):
        return pl.ds(pl.multiple_of(i * MLA_TQ, MLA_TQ), MLA_TQ)

    def scores(i, slot):
        for j in range(MLA_PAIR):
            s_scr[slot, j] = _dot(q_ref[0, block_rows(i), j * MLA_HB:(j + 1) * MLA_HB], kt_scr[j])

    def finish(i, slot):
        outs = []
        for j in range(MLA_PAIR):
            s = s_scr[slot, j]
            p = jnp.exp2(s - jnp.max(s, axis=1, keepdims=True))
            o = _dot(p, v_scr[j])
            outs.append(o / o[:, 0:1])
        even = pltpu.roll(outs[0], MLA_V, axis=1)
        o_ref[0, block_rows(i), :] = jnp.where(lane_q < MLA_V, even, outs[1]).astype(_BF)

    nblk = SEQ // MLA_TQ
    scores(0, 0)

    def body(k, carry):
        scores(2 * k + 1, 1)
        finish(2 * k, 0)
        scores(2 * k + 2, 0)
        finish(2 * k + 1, 1)
        return carry

    lax.fori_loop(0, nblk // 2 - 1, body, 0)
    scores(nblk - 1, 1)
    finish(nblk - 2, 0)
    finish(nblk - 1, 1)


def _mla_attention(q, kv, kr):
    b = q.shape[0]
    w = MLA_PAIR * MLA_HB
    return pl.pallas_call(
        _mla_attn_kernel, grid=(b, MLA_HEADS // MLA_PAIR),
        in_specs=[pl.BlockSpec((1, SEQ, w), lambda i, h: (i, 0, h)),
                  pl.BlockSpec((1, SEQ, w), lambda i, h: (i, 0, h)),
                  pl.BlockSpec((1, SEQ, MLA_HB), lambda i, h: (i, 0, 0))],
        out_specs=pl.BlockSpec((1, SEQ, MLA_PAIR * MLA_V), lambda i, h: (i, 0, h)),
        out_shape=jax.ShapeDtypeStruct((b, SEQ, MLA_HEADS * MLA_V), _BF),
        scratch_shapes=[pltpu.VMEM((MLA_PAIR, MLA_HB, SEQ), _BF), pltpu.VMEM((MLA_PAIR, SEQ, MLA_HB), _BF),
                        pltpu.VMEM((2, MLA_PAIR, MLA_TQ, SEQ), _F32)],
        compiler_params=_cp(("parallel", "parallel"), 40), name="mla_attention")(q, kv, kr)


def _na_hyena_mixer(x, w_in, rpb, conv_w, conv_b, f_w1, f_b1, f_freq, f_w2, f_b2, f_w3, skip):
    b = x.shape[0]
    qkv, hb = _inproj(x.reshape(b * SEQ, D_MODEL), w_in)
    y_a = _na_attention(qkv.reshape(b, SEQ, 3 * NA_WIDTH), rpb)
    hbc, zb = _short_conv(hb.reshape(b, SEQ, (HY_ORDER + 1) * HY_WIDTH), conv_w, conv_b)
    fmat, fmat_t = (jnp.asarray(f).astype(_BF) for f in _dft_matrices())
    kfilt = _hyena_filters(f_w1, f_b1, f_freq, f_w2, f_b2, f_w3)
    kf = _filter_spectrum(fmat, kfilt)
    z = hbc
    for o in range(HY_ORDER):
        z, zb = _long_conv_gate(hbc, z, zb, o + 1, o, fmat, fmat_t, kf, skip)
    return y_a.reshape(b * SEQ, NA_WIDTH), zb.reshape(b * SEQ, HY_WIDTH)


def kernel(x, p, ab_w_in, na_rpb, hy_conv_w, hy_conv_b, hy_f_w1, hy_f_b1, hy_f_freq, hy_f_w2, hy_f_b2, hy_f_w3, hy_skip, ab_w_out, mla_w_in, mla_q_norm, mla_w_q_up, mla_kv_norm, mla_w_kv_up, mla_w_out, ln1_g, ln1_b, ln2_g, ln2_b, moe_router, moe_w_gate, moe_w_up, moe_w_down, ple_gate, ple_proj):
    b = x.shape[0]
    m = b * SEQ
    for i in range(DEPTH):
        j = i // 2
        x2d = x.reshape(m, D_MODEL)
        if i % 2 == 0:
            y_a, z = _na_hyena_mixer(x, ab_w_in[j], na_rpb[j], hy_conv_w[j], hy_conv_b[j], hy_f_w1[j],
                                     hy_f_b1[j], hy_f_freq[j], hy_f_w2[j], hy_f_b2[j], hy_f_w3[j], hy_skip[j])
            x1, x1b, aff = _outproj_ln([y_a, z], ab_w_out[j], x2d, ln1_g[i], ln1_b[i], moe_router[i])
        else:
            q, kv, kr = _mla_pre(x2d, mla_w_in[j], mla_q_norm[j], mla_w_q_up[j], mla_kv_norm[j], mla_w_kv_up[j])
            att = _mla_attention(q.reshape(b, SEQ, -1), kv.reshape(b, SEQ, -1), kr.reshape(b, SEQ, -1))
            x1, x1b, aff = _outproj_ln([att.reshape(m, MLA_HEADS * MLA_V)], mla_w_out[j], x2d, ln1_g[i], ln1_b[i],
                                       moe_router[i])
        x = _moe_ple_block(x1.reshape(b, SEQ, D_MODEL), x1b.reshape(b, SEQ, D_MODEL), aff,
                           moe_w_gate, moe_w_up, moe_w_down, ln2_g[i], ln2_b[i], p, ple_gate, ple_proj, i)
    return x
```

```python
import functools
import math

import numpy as np
import jax
import jax.numpy as jnp
from jax import lax
from jax.experimental import pallas as pl
from jax.experimental.pallas import tpu as pltpu

D_MODEL = 1024
BATCH = 8
SEQ = 2048
DEPTH = 2
GRID_W = 64
PLE_DIM = 256
NA_HEADS = 8
NA_HEAD_DIM = 64
NA_WIDTH = NA_HEADS * NA_HEAD_DIM
NA_WIN_ROWS_MAX = 8
NA_WIN_COLS = 16
HY_WIDTH = D_MODEL - NA_WIDTH
HY_ORDER = 2
HY_SHORT_K = 3
HY_EMB_DIM = 33
HY_FILTER_HIDDEN = 64
HY_FAST_DECAY_PCT = 0.3
HY_SLOW_DECAY_PCT = 1.5
HY_DECAY_TARGET = 1e-2
AB_IN_WIDTH = 3 * NA_WIDTH + (HY_ORDER + 1) * HY_WIDTH
MLA_HEADS = 16
MLA_Q_LORA = 384
MLA_KV_LORA = 256
MLA_NOPE = 64
MLA_ROPE = 32
MLA_V = 64
ROPE_THETA = 10000.0
N_EXPERTS = 16
EC_CAPACITY_FACTOR = 2
D_FF_EXPERT = 2048
DN_ALPHA = (2 * DEPTH) ** 0.25
NORM_EPS = 1e-5
NEG_INF = -1e30

CAP = EC_CAPACITY_FACTOR * SEQ // N_EXPERTS
NA_ROWS = SEQ // GRID_W
NA_WIN_ROWS = min(NA_WIN_ROWS_MAX, NA_ROWS)
NA_SLAB = NA_WIN_ROWS * GRID_W
FFT_N = 2 * SEQ
LOG2E = math.log2(math.e)

LANE = 128
MIB = 1 << 20

_BF = jnp.bfloat16
_F32 = jnp.float32
_HP = lax.Precision.HIGHEST


def _cp(sem, vmem_mib):
    return pltpu.CompilerParams(dimension_semantics=sem, vmem_limit_bytes=vmem_mib * MIB)


def _dot(a, b):
    return jnp.dot(a.astype(_BF), b.astype(_BF), preferred_element_type=_F32)


def _dot_nt(a, b):
    return lax.dot_general(a.astype(_BF), b.astype(_BF), (((1,), (1,)), ((), ())),
                           preferred_element_type=_F32)


def _layer_norm(y, g, b):
    mu = jnp.mean(y, axis=-1, keepdims=True)
    d = y - mu
    var = jnp.mean(d * d, axis=-1, keepdims=True)
    return d * lax.rsqrt(var + NORM_EPS) * g + b


def _inproj_kernel(x_ref, w_ref, qkv_ref, hb_ref):
    h = _dot(x_ref[...], w_ref[...])
    qkv_ref[...] = h[:, :3 * NA_WIDTH].astype(_BF)
    hb_ref[...] = h[:, 3 * NA_WIDTH:]


def _inproj(x2d, w):
    m = x2d.shape[0]
    tm = 512
    nq, nh = 3 * NA_WIDTH, (HY_ORDER + 1) * HY_WIDTH
    return pl.pallas_call(
        _inproj_kernel, grid=(m // tm,),
        in_specs=[pl.BlockSpec((tm, D_MODEL), lambda i: (i, 0)),
                  pl.BlockSpec((D_MODEL, AB_IN_WIDTH), lambda i: (0, 0))],
        out_specs=[pl.BlockSpec((tm, nq), lambda i: (i, 0)),
                   pl.BlockSpec((tm, nh), lambda i: (i, 0))],
        out_shape=[jax.ShapeDtypeStruct((m, nq), _BF), jax.ShapeDtypeStruct((m, nh), _F32)],
        compiler_params=_cp(("parallel",), 48), name="ab_inproj")(x2d, w)


NA_GROUP = 4
NA_GW = NA_GROUP * NA_HEAD_DIM


def _na_kernel(q_ref, k_ref, v_ref, tbl_ref, o_ref):
    lane_head = lax.broadcasted_iota(jnp.int32, (GRID_W, NA_GW), 1) >> int(math.log2(NA_HEAD_DIM))
    scale = NA_HEAD_DIM ** -0.5 * LOG2E

    def body(r, carry):
        r0 = jnp.clip(r - NA_WIN_ROWS // 2, 0, NA_ROWS - NA_WIN_ROWS)
        off = r0 - r + (NA_WIN_ROWS_MAX - 1)
        qr = q_ref[0, pl.ds(pl.multiple_of(r * GRID_W, GRID_W), GRID_W), :].astype(_F32)
        q4 = jnp.concatenate([jnp.where(lane_head == h, qr, 0.0) for h in range(NA_GROUP)], axis=0)
        ks = k_ref[0, pl.ds(pl.multiple_of(r0 * GRID_W, GRID_W), NA_SLAB), :]
        vs = v_ref[0, pl.ds(pl.multiple_of(r0 * GRID_W, GRID_W), NA_SLAB), :]
        bias = jnp.concatenate([tbl_ref[h, off] for h in range(NA_GROUP)], axis=0)
        s = _dot_nt(q4, ks) * scale + bias
        mx = jnp.max(s, axis=1, keepdims=True)
        p = jnp.exp2(s - mx)
        den = jnp.sum(p, axis=1, keepdims=True)
        o4 = _dot(p, vs) / den
        out = jnp.zeros((GRID_W, NA_GW), _F32)
        for h in range(NA_GROUP):
            out = out + jnp.where(lane_head == h, o4[h * GRID_W:(h + 1) * GRID_W], 0.0)
        o_ref[0, pl.ds(pl.multiple_of(r * GRID_W, GRID_W), GRID_W), :] = out.astype(_BF)
        return carry

    lax.fori_loop(0, NA_ROWS, body, 0, unroll=4)


def _na_bias_table(rpb):
    c = np.arange(GRID_W)
    c0 = np.clip(c - NA_WIN_COLS // 2, 0, GRID_W - NA_WIN_COLS)
    kc = np.arange(GRID_W)
    col_ok = (kc[None, :] >= c0[:, None]) & (kc[None, :] < c0[:, None] + NA_WIN_COLS)
    dc_idx = np.clip(kc[None, :] - c[:, None], -(NA_WIN_COLS - 1), NA_WIN_COLS - 1) + (NA_WIN_COLS - 1)
    pick = (dc_idx[None, :, :] == np.arange(2 * NA_WIN_COLS - 1)[:, None, None]).astype(np.float32)
    per_dr = jnp.einsum("hdk,kqc->hdqc", rpb.astype(_F32), pick, precision=_HP)
    per_dr = jnp.where(col_ok[None, None], per_dr * LOG2E, NEG_INF)
    slabs = jnp.stack([per_dr[:, off:off + NA_WIN_ROWS] for off in range(NA_WIN_ROWS_MAX)], axis=1)
    return slabs.transpose(0, 1, 3, 2, 4).reshape(NA_HEADS, NA_WIN_ROWS_MAX, GRID_W, NA_SLAB)


def _na_attention(qkv, rpb):
    b = qkv.shape[0]
    tbl = _na_bias_table(rpb)
    ng = NA_HEADS // NA_GROUP
    blk = lambda col0: pl.BlockSpec((1, SEQ, NA_GW), lambda i, g, c=col0: (i, 0, c + g))
    return pl.pallas_call(
        _na_kernel, grid=(b, ng),
        in_specs=[blk(0), blk(ng), blk(2 * ng),
                  pl.BlockSpec((NA_GROUP, NA_WIN_ROWS_MAX, GRID_W, NA_SLAB), lambda i, g: (g, 0, 0, 0))],
        out_specs=pl.BlockSpec((1, SEQ, NA_GW), lambda i, g: (i, 0, g)),
        out_shape=jax.ShapeDtypeStruct((b, SEQ, NA_WIDTH), _BF),
        compiler_params=_cp(("parallel", "parallel"), 32), name="na_attention")(qkv, qkv, qkv, tbl)


def _sconv_kernel(x_ref, w_ref, b_ref, o_ref, vb_ref):
    x = x_ref[0]
    n = x.shape[0]
    row = lax.broadcasted_iota(jnp.int32, x.shape, 0)
    xm = jnp.where(row == 0, 0.0, pltpu.roll(x, 1, axis=0))
    xp = jnp.where(row == n - 1, 0.0, pltpu.roll(x, n - 1, axis=0))
    y = b_ref[...] + xm * w_ref[0:1, :] + x * w_ref[1:2, :] + xp * w_ref[2:3, :]
    o_ref[0] = y

    @pl.when(pl.program_id(1) == 0)
    def _():
        vb_ref[0] = y.astype(_BF)


def _short_conv(hb, w, bias):
    b, n, c = hb.shape
    tc = HY_WIDTH
    return pl.pallas_call(
        _sconv_kernel, grid=(b, c // tc),
        in_specs=[pl.BlockSpec((1, n, tc), lambda i, j: (i, 0, j)),
                  pl.BlockSpec((HY_SHORT_K, tc), lambda i, j: (0, j)),
                  pl.BlockSpec((1, tc), lambda i, j: (0, j))],
        out_specs=[pl.BlockSpec((1, n, tc), lambda i, j: (i, 0, j)),
                   pl.BlockSpec((1, n, tc), lambda i, j: (i, 0, 0))],
        out_shape=[jax.ShapeDtypeStruct((b, n, c), _F32), jax.ShapeDtypeStruct((b, n, tc), _BF)],
        compiler_params=_cp(("parallel", "arbitrary"), 32), name="hy_short_conv")(hb, w, bias.reshape(1, c))


HY_HID_PAD = LANE
HY_FILT_TC = 256


def _filter_kernel(z_ref, w1_ref, b1_ref, fr_ref, w2_ref, b2_ref, w3f_ref, w3b_ref, dec_ref, k_ref, h_ref):
    @pl.when(jnp.logical_and(pl.program_id(0) == 0, pl.program_id(1) == 0))
    def _():
        fr = fr_ref[...]
        h1 = jnp.sin(fr * (jnp.dot(z_ref[...], w1_ref[...], precision=_HP, preferred_element_type=_F32) + b1_ref[...]))
        h_ref[...] = jnp.sin(fr * (jnp.dot(h1, w2_ref[...], precision=_HP, preferred_element_type=_F32) + b2_ref[...]))

    fwd = jnp.dot(h_ref[:SEQ, :], w3f_ref[...], precision=_HP, preferred_element_type=_F32)
    bwd = jnp.dot(h_ref[SEQ:, :], w3b_ref[...], precision=_HP, preferred_element_type=_F32)
    k = jnp.concatenate([fwd, bwd], axis=0) * dec_ref[...]
    ss = jnp.sum(k * k, axis=0, keepdims=True)
    k_ref[...] = (k * lax.rsqrt(ss + 1e-12)).astype(_BF)


@functools.lru_cache(maxsize=None)
def _filter_tables():
    bands = (HY_EMB_DIM - 1) // 2
    t = np.linspace(0.0, 1.0, SEQ)[:, None]
    w = 2.0 * math.pi * np.arange(SEQ)[:, None] / SEQ
    f = np.linspace(1e-4, bands - 1, bands)[None, :]
    z = np.concatenate([t, np.cos(f * w), -np.sin(f * w)], axis=-1)
    min_decay = math.log(HY_DECAY_TARGET) / HY_SLOW_DECAY_PCT
    max_decay = math.log(HY_DECAY_TARGET) / HY_FAST_DECAY_PCT
    deltas = np.abs(np.linspace(min_decay, max_decay, HY_WIDTH))
    dec = np.exp(-t * deltas)
    src = np.concatenate([np.arange(SEQ), [0], np.arange(SEQ - 1, 0, -1)])
    live = np.ones((FFT_N, 1))
    live[SEQ] = 0.0
    z2 = np.pad(z[src], ((0, 0), (0, HY_HID_PAD - HY_EMB_DIM))).astype(np.float32)
    dec2 = (dec[src] * live).astype(np.float32)
    return z2, dec2


def _hyena_filters(w1, b1, freq, w2, b2, w3):
    z2, dec2 = _filter_tables()
    hp = HY_HID_PAD - HY_FILTER_HIDDEN
    w1p = jnp.pad(w1, ((0, HY_HID_PAD - HY_EMB_DIM), (0, hp)))
    w2p = jnp.pad(w2, ((0, hp), (0, hp)))
    w3p = jnp.pad(w3, ((0, hp), (0, 0)))
    row = lambda v: jnp.pad(v, (0, hp)).reshape(1, HY_HID_PAD)
    nc = HY_WIDTH // HY_FILT_TC
    per_order = 2 * nc
    full = lambda shape: pl.BlockSpec(shape, lambda o, j: (0, 0))
    return pl.pallas_call(
        _filter_kernel, grid=(HY_ORDER, nc),
        in_specs=[full((FFT_N, HY_HID_PAD)), full((HY_HID_PAD, HY_HID_PAD)), full((1, HY_HID_PAD)),
                  full((1, HY_HID_PAD)), full((HY_HID_PAD, HY_HID_PAD)), full((1, HY_HID_PAD)),
                  pl.BlockSpec((HY_HID_PAD, HY_FILT_TC), lambda o, j: (0, o * per_order + j)),
                  pl.BlockSpec((HY_HID_PAD, HY_FILT_TC), lambda o, j: (0, o * per_order + nc + j)),
                  pl.BlockSpec((FFT_N, HY_FILT_TC), lambda o, j: (0, j))],
        out_specs=pl.BlockSpec((FFT_N, HY_FILT_TC), lambda o, j: (0, o * nc + j)),
        out_shape=jax.ShapeDtypeStruct((FFT_N, HY_ORDER * HY_WIDTH), _BF),
        scratch_shapes=[pltpu.VMEM((FFT_N, HY_HID_PAD), _F32)],
        compiler_params=_cp(("arbitrary", "arbitrary"), 40), name="hy_filters")(
            z2, w1p, row(b1), row(freq), w2p, row(b2), w3p, w3p, dec2)


@functools.lru_cache(maxsize=None)
def _dft_matrices():
    t = np.arange(SEQ)
    ang = ((t[:, None] * t[None, :]) % FFT_N) * (2.0 * math.pi / FFT_N)
    re = np.cos(ang)
    im = -np.sin(ang)
    im[0] = 1.0 - 2.0 * (t % 2)
    nf = SEQ // HY_FB
    packed = np.stack([re.reshape(nf, HY_FB, SEQ), im.reshape(nf, HY_FB, SEQ)], axis=1).reshape(FFT_N, SEQ)
    return packed.astype(np.float32), np.ascontiguousarray(packed.T).astype(np.float32)


def _kf_kernel(f_ref, k_ref, o_ref):
    i = pl.program_id(0)
    f = f_ref[...]
    tm = f.shape[0]
    p1 = jnp.dot(f, k_ref[:SEQ, :], preferred_element_type=_F32)
    p2 = jnp.dot(f, k_ref[SEQ:, :], preferred_element_type=_F32)
    row = lax.broadcasted_iota(jnp.int32, (tm, 1), 0) + i * tm
    sign = (1 - 2 * (row & 1)).astype(_F32)
    o_ref[...] = p1 + sign * p2


def _filter_spectrum(fmat, kfilt):
    tm = 512
    nw = kfilt.shape[1]
    return pl.pallas_call(
        _kf_kernel, grid=(FFT_N // tm,),
        in_specs=[pl.BlockSpec((tm, SEQ), lambda i: (i, 0)),
                  pl.BlockSpec((FFT_N, nw), lambda i: (0, 0))],
        out_specs=pl.BlockSpec((tm, nw), lambda i: (i, 0)),
        out_shape=jax.ShapeDtypeStruct((FFT_N, nw), _F32),
        compiler_params=_cp(("parallel",), 40), name="hy_filter_spectrum")(fmat, kfilt)


HY_FB = 512
HY_NB = 2


HY_TM = 512


def _hconv_fwd_kernel(zb_ref, f_ref, k_ref, y_ref):
    fk = pl.program_id(1)
    kr = k_ref[:HY_FB, :]
    ki = k_ref[HY_FB:, :]
    row0 = jnp.logical_and(lax.broadcasted_iota(jnp.int32, kr.shape, 0) == 0, fk == 0)
    sc = jnp.where(row0, 1.0 / FFT_N, 2.0 / FFT_N)
    for j in range(HY_NB):
        u = jnp.dot(f_ref[...], zb_ref[j], preferred_element_type=_F32)
        ur = u[:HY_FB]
        ui = u[HY_FB:]
        yr = jnp.where(row0, ur * kr, ur * kr - ui * ki)
        yi = jnp.where(row0, ui * ki, ur * ki + ui * kr)
        y_ref[j, 0, 0] = (yr * sc).astype(_BF)
        y_ref[j, 0, 1] = (yi * sc).astype(_BF)


def _hconv_inv_kernel(ft_ref, y_ref, z_ref, skip_ref, xn_ref, o_ref, ob_ref):
    for j in range(HY_NB):
        conv = jnp.dot(ft_ref[...], y_ref[j], preferred_element_type=_F32)
        out = xn_ref[j] * (conv + z_ref[j] * skip_ref[...])
        o_ref[j] = out
        ob_ref[j] = out.astype(_BF)


def _long_conv_gate(hbc, zsrc, zb, xn_col, order, fmat, fmat_t, kf, skip):
    b = hbc.shape[0]
    nf = SEQ // HY_FB
    w = HY_WIDTH
    nb = HY_NB
    y = pl.pallas_call(
        _hconv_fwd_kernel, grid=(b // nb, nf),
        in_specs=[pl.BlockSpec((nb, SEQ, w), lambda i, f: (i, 0, 0)),
                  pl.BlockSpec((2 * HY_FB, SEQ), lambda i, f: (f, 0)),
                  pl.BlockSpec((2 * HY_FB, w), lambda i, f: (f, order))],
        out_specs=pl.BlockSpec((nb, 1, 2, HY_FB, w), lambda i, f: (i, f, 0, 0, 0)),
        out_shape=jax.ShapeDtypeStruct((b, nf, 2, HY_FB, w), _BF),
        compiler_params=_cp(("parallel", "parallel"), 40), name=f"hy_conv_fwd{order}")(zb, fmat, kf)
    blk = lambda col: pl.BlockSpec((nb, HY_TM, w), lambda i, m, c=col: (i, m, c))
    return pl.pallas_call(
        _hconv_inv_kernel, grid=(b // nb, SEQ // HY_TM),
        in_specs=[pl.BlockSpec((HY_TM, FFT_N), lambda i, m: (m, 0)),
                  pl.BlockSpec((nb, FFT_N, w), lambda i, m: (i, 0, 0)),
                  blk(0),
                  pl.BlockSpec((1, w), lambda i, m: (0, 0)),
                  blk(xn_col)],
        out_specs=[blk(0), blk(0)],
        out_shape=[jax.ShapeDtypeStruct((b, SEQ, w), _F32), jax.ShapeDtypeStruct((b, SEQ, w), _BF)],
        compiler_params=_cp(("parallel", "parallel"), 48), name=f"hy_conv_inv{order}")(
            fmat_t, y.reshape(b, FFT_N, w), zsrc, skip[order].reshape(1, w), hbc)


def _outproj_ln_kernel(n_a, *refs):
    a_refs = refs[:n_a]
    w_refs = refs[n_a:2 * n_a]
    x_ref, g_ref, b_ref, wr_ref, o_ref, ob_ref, aff_ref = refs[2 * n_a:]
    m = _dot(a_refs[0][...], w_refs[0][...])
    for a_ref, w_ref in zip(a_refs[1:], w_refs[1:]):
        m = m + _dot(a_ref[...], w_ref[...])
    y = _layer_norm(DN_ALPHA * x_ref[...] + m, g_ref[...], b_ref[...])
    o_ref[...] = y
    ob_ref[...] = y.astype(_BF)
    wr = wr_ref[...]
    wr_hi = wr.astype(_BF)
    wr_lo = (wr - wr_hi.astype(_F32)).astype(_BF)
    y_hi = y.astype(_BF)
    y_lo = (y - y_hi.astype(_F32)).astype(_BF)
    by_hi = _dot_nt(jnp.concatenate([wr_hi, wr_lo], axis=0), y_hi)
    logits = by_hi[:N_EXPERTS] + by_hi[N_EXPERTS:] + _dot_nt(wr_hi, y_lo)
    ex = jnp.exp(logits - jnp.max(logits, axis=0, keepdims=True))
    aff_ref[0] = ex / jnp.sum(ex, axis=0, keepdims=True)


def _outproj_ln(a_list, w, x2d, g, b, w_router):
    m = x2d.shape[0]
    tm = 512
    per_seq = SEQ // tm
    in_specs, w_args, row0 = [], [], 0
    for a in a_list:
        in_specs.append(pl.BlockSpec((tm, a.shape[1]), lambda i: (i, 0)))
    for a in a_list:
        ka = a.shape[1]
        in_specs.append(pl.BlockSpec((ka, D_MODEL), lambda i, r=row0 // ka: (r, 0)))
        w_args.append(w)
        row0 += ka
    in_specs += [pl.BlockSpec((tm, D_MODEL), lambda i: (i, 0)),
                 pl.BlockSpec((1, D_MODEL), lambda i: (0, 0)),
                 pl.BlockSpec((1, D_MODEL), lambda i: (0, 0)),
                 pl.BlockSpec((N_EXPERTS, D_MODEL), lambda i: (0, 0))]
    return pl.pallas_call(
        functools.partial(_outproj_ln_kernel, len(a_list)), grid=(m // tm,),
        in_specs=in_specs,
        out_specs=[pl.BlockSpec((tm, D_MODEL), lambda i: (i, 0)),
                   pl.BlockSpec((tm, D_MODEL), lambda i: (i, 0)),
                   pl.BlockSpec((1, N_EXPERTS, tm), lambda i: (i // per_seq, 0, i % per_seq))],
        out_shape=[jax.ShapeDtypeStruct((m, D_MODEL), _F32), jax.ShapeDtypeStruct((m, D_MODEL), _BF),
                   jax.ShapeDtypeStruct((m // SEQ, N_EXPERTS, SEQ), _F32)],
        compiler_params=_cp(("parallel",), 40), name="outproj_ln_router")(
            *a_list, *w_args, x2d, g.reshape(1, -1), b.reshape(1, -1), w_router.T)


BISECT_STEPS = 160


def _select_kernel(a_ref, tri_ref, edge_ref, slot_ref, slot_t_ref, cnt_ref):
    a = a_ref[...]
    rows = a.shape[0]
    cap = float(CAP)

    def body(_, c):
        lo, hi = c
        mid = 0.5 * (lo + hi)
        cnt = jnp.sum(jnp.where(a > mid, 1.0, 0.0), axis=1, keepdims=True)
        ge = cnt >= cap
        return jnp.where(ge, mid, lo), jnp.where(ge, hi, mid)

    lo, hi = lax.fori_loop(0, BISECT_STEPS, body,
                           (jnp.full((rows, 1), -1.0, _F32), jnp.full((rows, 1), 1.0, _F32)))
    vstar = jnp.max(jnp.where(a <= hi, a, -1.0), axis=1, keepdims=True)
    gt = a > vstar
    eq = a == vstar
    need = cap - jnp.sum(jnp.where(gt, 1.0, 0.0), axis=1, keepdims=True)
    tri = tri_ref[...]
    eq_before = jnp.dot(jnp.where(eq, 1.0, 0.0).astype(_BF), tri, preferred_element_type=_F32)
    sel = jnp.where(gt, 1.0, jnp.where(eq, jnp.where(eq_before < need, 1.0, 0.0), 0.0))
    pos = jnp.dot(sel.astype(_BF), tri, preferred_element_type=_F32)
    slot = jnp.where(sel > 0.5, pos, -1.0)
    slot_ref[...] = slot.astype(jnp.int32)
    cnt_ref[...] = jnp.dot(sel.astype(_BF), edge_ref[...], preferred_element_type=_F32).astype(jnp.int32)
    if rows < LANE:
        slot = jnp.concatenate([slot, jnp.full((LANE - rows, SEQ), -1.0, _F32)], axis=0)
    slot_tm = slot.T
    for b in range(rows // N_EXPERTS):
        shifted = slot_tm if b == 0 else pltpu.roll(slot_tm, LANE - N_EXPERTS * b, axis=1)
        slot_t_ref[b] = shifted.astype(jnp.int32)


@functools.lru_cache(maxsize=None)
def _prefix_matrix():
    idx = np.arange(SEQ)
    return (idx[:, None] < idx[None, :]).astype(_BF)


WIN_TT = 256
WIN_W = 128
WIN_EDGES = 16


@functools.lru_cache(maxsize=None)
def _edge_matrix():
    return (np.arange(SEQ)[:, None] < np.arange(LANE)[None, :] * WIN_TT).astype(_BF)


def _select(aff2d):
    rows = aff2d.shape[0]
    assert rows <= LANE and rows % N_EXPERTS == 0
    bsz = rows // N_EXPERTS
    slot, slot_t, cnt = pl.pallas_call(
        _select_kernel, grid=(1,),
        in_specs=[pl.BlockSpec((rows, SEQ), lambda i: (0, 0)),
                  pl.BlockSpec((SEQ, SEQ), lambda i: (0, 0)),
                  pl.BlockSpec((SEQ, LANE), lambda i: (0, 0))],
        out_specs=[pl.BlockSpec((rows, SEQ), lambda i: (0, 0)),
                   pl.BlockSpec((bsz, SEQ, LANE), lambda i: (0, 0, 0)),
                   pl.BlockSpec((rows, LANE), lambda i: (0, 0))],
        out_shape=[jax.ShapeDtypeStruct((rows, SEQ), jnp.int32),
                   jax.ShapeDtypeStruct((bsz, SEQ, LANE), jnp.int32),
                   jax.ShapeDtypeStruct((rows, LANE), jnp.int32)],
        compiler_params=_cp(("arbitrary",), 48), name="moe_select")(aff2d, _prefix_matrix(), _edge_matrix())
    return slot, slot_t, cnt[:, :WIN_EDGES].reshape(rows * WIN_EDGES)


def _onehot(slot_row):
    return slot_row == lax.broadcasted_iota(jnp.int32, (CAP, SEQ), 0)


GATHER_NE = 4
GATHER_TC = 512
GATHER_W = 64


def _window_start(cnt_ref, seq, e, tile, width):
    base = (seq * N_EXPERTS + e) * WIN_EDGES + tile
    start = jnp.minimum((cnt_ref[base] >> 4) << 4, CAP - width)
    return start, cnt_ref[base + 1] - start <= width


def _gather_kernel(cnt_ref, slot_ref, a_ref, xb_ref, xe_ref, g_ref):
    i = pl.program_id(0)
    n_tiles = SEQ // WIN_TT
    wd = GATHER_W
    starts, fits = {}, None
    for t in range(n_tiles):
        for e in range(N_EXPERTS):
            starts[e, t], ok = _window_start(cnt_ref, i, e, t, wd)
            fits = ok if fits is None else jnp.logical_and(fits, ok)

    @pl.when(fits)
    def _():
        xe_ref[...] = jnp.zeros(xe_ref.shape, _BF)
        g_ref[...] = jnp.zeros(g_ref.shape, _F32)
        sub_w = lax.broadcasted_iota(jnp.int32, (wd, WIN_TT), 0)
        for t in range(n_tiles):
            toks = slice(t * WIN_TT, (t + 1) * WIN_TT)
            ps = []
            for e in range(N_EXPERTS):
                hit = slot_ref[0, e, :, toks] - starts[e, t] == sub_w
                ps.append(jnp.where(hit, 1.0, 0.0).astype(_BF))
                rows = pl.ds(pl.multiple_of(starts[e, t], 16), wd)
                g_ref[e, rows, :] += jnp.sum(jnp.where(hit, a_ref[0, e, :, toks], 0.0), axis=1, keepdims=True)
            pcat = jnp.concatenate(ps, axis=0)
            for c in range(D_MODEL // GATHER_TC):
                cols = slice(c * GATHER_TC, (c + 1) * GATHER_TC)
                res = jnp.dot(pcat, xb_ref[0, toks, cols], preferred_element_type=_F32).astype(_BF)
                for e in range(N_EXPERTS):
                    rows = pl.ds(pl.multiple_of(starts[e, t], 16), wd)
                    xe_ref[e, rows, cols] += res[e * wd:(e + 1) * wd]

    @pl.when(jnp.logical_not(fits))
    def _():
        for grp in range(N_EXPERTS // GATHER_NE):
            es = range(grp * GATHER_NE, (grp + 1) * GATHER_NE)
            hits = [_onehot(slot_ref[0, e]) for e in es]
            p = jnp.concatenate([jnp.where(h, 1.0, 0.0).astype(_BF) for h in hits], axis=0)
            xe = jnp.dot(p, xb_ref[0], preferred_element_type=_F32).astype(_BF)
            xe_ref[grp * GATHER_NE:(grp + 1) * GATHER_NE] = xe.reshape(GATHER_NE, CAP, D_MODEL)
            for k, e in enumerate(es):
                g_ref[e] = jnp.sum(jnp.where(hits[k], a_ref[0, e], 0.0), axis=1, keepdims=True)


def _gather(cnt, slot4, aff4, xb3d):
    b = xb3d.shape[0]
    row = pl.BlockSpec((1, N_EXPERTS, 1, SEQ), lambda i, c: (i, 0, 0, 0))
    grid_spec = pltpu.PrefetchScalarGridSpec(
        num_scalar_prefetch=1, grid=(b,),
        in_specs=[row, row, pl.BlockSpec((1, SEQ, D_MODEL), lambda i, c: (i, 0, 0))],
        out_specs=[pl.BlockSpec((N_EXPERTS, CAP, D_MODEL), lambda i, c: (0, i, 0)),
                   pl.BlockSpec((N_EXPERTS, CAP, 1), lambda i, c: (0, i, 0))])
    return pl.pallas_call(
        _gather_kernel, grid_spec=grid_spec,
        out_shape=[jax.ShapeDtypeStruct((N_EXPERTS, b * CAP, D_MODEL), _BF),
                   jax.ShapeDtypeStruct((N_EXPERTS, b * CAP, 1), _F32)],
        compiler_params=_cp(("parallel",), 56), name="moe_gather")(cnt, slot4, aff4, xb3d)


FFN_TF = 512
FFN_TM = 512


def _ffn_up_kernel(xe_ref, wg_ref, wu_ref, h_ref):
    wg = wg_ref[0].astype(_BF)
    wu = wu_ref[0].astype(_BF)
    for c in range(xe_ref.shape[1] // FFN_TM):
        rows = slice(c * FFN_TM, (c + 1) * FFN_TM)
        xe = xe_ref[0, rows, :]
        hg = _dot(xe, wg)
        hu = _dot(xe, wu)
        h_ref[0, rows, :] = ((hg * (1.0 / (1.0 + jnp.exp(-hg)))) * hu).astype(_BF)


def _ffn_down_kernel(h_ref, wd_ref, g_ref, ye_ref):
    wd = wd_ref[0].astype(_BF)
    for c in range(h_ref.shape[1] // FFN_TM):
        rows = slice(c * FFN_TM, (c + 1) * FFN_TM)
        ye_ref[0, rows, :] = (_dot(h_ref[0, rows, :], wd) * g_ref[0, rows, :]).astype(_BF)


def _expert_ffn(xe, gates, wg, wu, wd, layer):
    rows = xe.shape[1]
    h = pl.pallas_call(
        _ffn_up_kernel, grid=(N_EXPERTS, D_FF_EXPERT // FFN_TF),
        in_specs=[pl.BlockSpec((1, rows, D_MODEL), lambda e, f: (e, 0, 0)),
                  pl.BlockSpec((None, 1, D_MODEL, FFN_TF), lambda e, f: (layer, e, 0, f)),
                  pl.BlockSpec((None, 1, D_MODEL, FFN_TF), lambda e, f: (layer, e, 0, f))],
        out_specs=pl.BlockSpec((1, rows, FFN_TF), lambda e, f: (e, 0, f)),
        out_shape=jax.ShapeDtypeStruct((N_EXPERTS, rows, D_FF_EXPERT), _BF),
        compiler_params=_cp(("parallel", "parallel"), 40), name="moe_ffn_up")(xe, wg, wu)
    return pl.pallas_call(
        _ffn_down_kernel, grid=(N_EXPERTS,),
        in_specs=[pl.BlockSpec((1, rows, D_FF_EXPERT), lambda e: (e, 0, 0)),
                  pl.BlockSpec((None, 1, D_FF_EXPERT, D_MODEL), lambda e: (layer, e, 0, 0)),
                  pl.BlockSpec((1, rows, 1), lambda e: (e, 0, 0))],
        out_specs=pl.BlockSpec((1, rows, D_MODEL), lambda e: (e, 0, 0)),
        out_shape=jax.ShapeDtypeStruct((N_EXPERTS, rows, D_MODEL), _BF),
        compiler_params=_cp(("parallel",), 56), name="moe_ffn_down")(h, wd, gates)


def _scatter_ln_ple_kernel(cnt_ref, slot_t_ref, ye_ref, x_ref, g_ref, b_ref, p_ref, wg_ref, wp_ref, o_ref,
                           pt_ref, yw_ref, f_ref):
    i = pl.program_id(0)
    r = pl.program_id(1)
    slot_t = slot_t_ref[0]
    starts = []
    fits = None
    for e in range(N_EXPERTS):
        start, ok = _window_start(cnt_ref, i, e, r, WIN_W)
        starts.append(start)
        fits = ok if fits is None else jnp.logical_and(fits, ok)

    @pl.when(fits)
    def _():
        lane_w = lax.broadcasted_iota(jnp.int32, (WIN_TT, WIN_W), 1)
        for e in range(N_EXPERTS):
            cols = slice(e * WIN_W, (e + 1) * WIN_W)
            pt_ref[:, cols] = jnp.where(slot_t[:, e:e + 1] - starts[e] == lane_w, 1.0, 0.0).astype(_BF)
            yw_ref[cols, :] = ye_ref[e, pl.ds(pl.multiple_of(starts[e], 16), WIN_W), :]
        f_ref[...] = jnp.dot(pt_ref[:, :N_EXPERTS * WIN_W], yw_ref[...], preferred_element_type=_F32)

    @pl.when(jnp.logical_not(fits))
    def _():
        lane_c = lax.broadcasted_iota(jnp.int32, (WIN_TT, CAP), 1)
        for e in range(N_EXPERTS):
            pt_ref[:, e * CAP:(e + 1) * CAP] = jnp.where(slot_t[:, e:e + 1] == lane_c, 1.0, 0.0).astype(_BF)
        f_ref[...] = jnp.dot(pt_ref[...], ye_ref[...].reshape(N_EXPERTS * CAP, D_MODEL),
                             preferred_element_type=_F32)

    y = _layer_norm(DN_ALPHA * x_ref[0] + f_ref[...], g_ref[...], b_ref[...])
    gate = _dot(y, wg_ref[...])
    emb = _dot(p_ref[0], wp_ref[...])
    o_ref[0] = y + (1.0 / (1.0 + jnp.exp(-gate))) * emb


def _scatter_ln_ple(cnt, slot_t, ye, x3d, g, b, p, ple_gate, ple_proj, layer):
    bsz = x3d.shape[0]
    vec = pl.BlockSpec((1, D_MODEL), lambda i, r, c: (0, 0))
    seq = pl.BlockSpec((1, WIN_TT, D_MODEL), lambda i, r, c: (i, r, 0))
    grid_spec = pltpu.PrefetchScalarGridSpec(
        num_scalar_prefetch=1, grid=(bsz, SEQ // WIN_TT),
        in_specs=[pl.BlockSpec((1, WIN_TT, LANE), lambda i, r, c: (i, r, 0)),
                  pl.BlockSpec((N_EXPERTS, CAP, D_MODEL), lambda i, r, c: (0, i, 0)),
                  seq, vec, vec,
                  pl.BlockSpec((None, 1, WIN_TT, PLE_DIM), lambda i, r, c: (layer, i, r, 0)),
                  pl.BlockSpec((None, D_MODEL, D_MODEL), lambda i, r, c: (layer, 0, 0)),
                  pl.BlockSpec((None, PLE_DIM, D_MODEL), lambda i, r, c: (layer, 0, 0))],
        out_specs=seq,
        scratch_shapes=[pltpu.VMEM((WIN_TT, N_EXPERTS * CAP), _BF),
                        pltpu.VMEM((N_EXPERTS * WIN_W, D_MODEL), _BF),
                        pltpu.VMEM((WIN_TT, D_MODEL), _F32)])
    return pl.pallas_call(
        _scatter_ln_ple_kernel, grid_spec=grid_spec,
        out_shape=jax.ShapeDtypeStruct((bsz, SEQ, D_MODEL), _F32),
        compiler_params=_cp(("parallel", "arbitrary"), 56), name="moe_scatter_ln_ple")(
            cnt, slot_t, ye, x3d, g.reshape(1, -1), b.reshape(1, -1), p, ple_gate, ple_proj)


def _moe_ple_block(x1, x1b, aff, wg, wu, wd, g, b, p, ple_gate, ple_proj, layer):
    bsz = x1.shape[0]
    slot, slot_t, cnt = _select(aff.reshape(bsz * N_EXPERTS, SEQ))
    slot4 = slot.reshape(bsz, N_EXPERTS, 1, SEQ)
    aff4 = aff.reshape(bsz, N_EXPERTS, 1, SEQ)
    xe, gates = _gather(cnt, slot4, aff4, x1b)
    ye = _expert_ffn(xe, gates, wg, wu, wd, layer)
    return _scatter_ln_ple(cnt, slot_t, ye, x1, g, b, p, ple_gate, ple_proj, layer)


MLA_HB = LANE
MLA_IN_PAD = 768
MLA_KR_COL = MLA_Q_LORA + MLA_KV_LORA


def _rms(x, g):
    return x * lax.rsqrt(jnp.mean(x * x, axis=-1, keepdims=True) + NORM_EPS) * g


def _mla_pre_kernel(x_ref, win_ref, qg_ref, wq_ref, kvg_ref, wkv_ref, qa_ref, qb_ref, ka_ref, kb_ref,
                    q_ref, kv_ref, kr_ref):
    h = _dot(x_ref[...], win_ref[...])
    q = _dot(_rms(h[:, :MLA_Q_LORA], qg_ref[...]), wq_ref[...])
    swap = pltpu.roll(q, q.shape[1] - MLA_ROPE, axis=1)
    qa = jnp.concatenate([qa_ref[...]] * MLA_HEADS, axis=1)
    qb = jnp.concatenate([qb_ref[...]] * MLA_HEADS, axis=1)
    q_ref[...] = (q * qa + swap * qb).astype(_BF)
    kv_ref[...] = _dot(_rms(h[:, MLA_Q_LORA:MLA_KR_COL], kvg_ref[...]), wkv_ref[...]).astype(_BF)
    kr = h[:, MLA_KR_COL:MLA_KR_COL + MLA_HB]
    kr_ref[...] = (kr * ka_ref[...] + pltpu.roll(kr, MLA_HB - MLA_ROPE, axis=1) * kb_ref[...]).astype(_BF)


@functools.lru_cache(maxsize=None)
def _rope_lane_tables():
    inv = 1.0 / (ROPE_THETA ** (np.arange(0, MLA_ROPE, 2) / MLA_ROPE))
    ang = np.arange(SEQ)[:, None] * inv[None, :]
    cos, sin = np.cos(ang), np.sin(ang)
    ones = np.ones((SEQ, MLA_NOPE))
    zeros_n = np.zeros((SEQ, MLA_NOPE))
    zeros_r = np.zeros((SEQ, MLA_ROPE))
    cc = np.concatenate([cos, cos], axis=1)
    ss = np.concatenate([-sin, sin], axis=1)
    keep_q = np.concatenate([ones, cc, zeros_r], axis=1)
    keep_k = np.concatenate([zeros_n, cc, zeros_r], axis=1)
    swp = np.concatenate([zeros_n, ss, zeros_r], axis=1)
    qscale = (MLA_NOPE + MLA_ROPE) ** -0.5 * LOG2E
    f32 = lambda a: a.astype(np.float32)
    return f32(keep_q * qscale), f32(swp * qscale), f32(keep_k), f32(swp)


def _mla_pre(x2d, w_in, q_norm, w_q_up, kv_norm, w_kv_up):
    m = x2d.shape[0]
    tm = 512
    half = MLA_ROPE // 2
    kr_w = w_in[:, MLA_KR_COL:]
    kr_swapped = jnp.concatenate([kr_w[:, half:], kr_w[:, :half]], axis=1)
    win = jnp.concatenate([w_in[:, :MLA_KR_COL], jnp.zeros((D_MODEL, MLA_NOPE), _F32), kr_w, kr_swapped],
                          axis=1).astype(_BF)
    wq = w_q_up.reshape(MLA_Q_LORA, MLA_HEADS, MLA_NOPE + MLA_ROPE)
    rope_w = wq[:, :, MLA_NOPE:]
    wq = jnp.concatenate([wq, rope_w[:, :, half:], rope_w[:, :, :half]], axis=2).reshape(
        MLA_Q_LORA, MLA_HEADS * MLA_HB).astype(_BF)
    w_kv_up = w_kv_up.astype(_BF)
    q_keep, q_swap, k_keep, k_swap = _rope_lane_tables()
    nq = MLA_HEADS * MLA_HB
    pos = lambda: pl.BlockSpec((tm, MLA_HB), lambda i: (i % (SEQ // tm), 0))
    full = lambda a: pl.BlockSpec(a.shape, lambda i: (0,) * a.ndim)
    qg, kvg = q_norm.reshape(1, -1), kv_norm.reshape(1, -1)
    return pl.pallas_call(
        _mla_pre_kernel, grid=(m // tm,),
        in_specs=[pl.BlockSpec((tm, D_MODEL), lambda i: (i, 0)), full(win), full(qg), full(wq), full(kvg),
                  full(w_kv_up), pos(), pos(), pos(), pos()],
        out_specs=[pl.BlockSpec((tm, nq), lambda i: (i, 0)),
                   pl.BlockSpec((tm, nq), lambda i: (i, 0)),
                   pl.BlockSpec((tm, MLA_HB), lambda i: (i, 0))],
        out_shape=[jax.ShapeDtypeStruct((m, nq), _BF), jax.ShapeDtypeStruct((m, nq), _BF),
                   jax.ShapeDtypeStruct((m, MLA_HB), _BF)],
        compiler_params=_cp(("parallel",), 48), name="mla_pre")(
            x2d, win, qg, wq, kvg, w_kv_up, q_keep, q_swap, k_keep, k_swap)


MLA_TQ = 256
MLA_PAIR = 2


def _mla_attn_kernel(q_ref, kv_ref, kr_ref, o_ref, kt_scr, v_scr, s_scr):
    lane = lax.broadcasted_iota(jnp.int32, (SEQ, MLA_HB), 1)
    kr = kr_ref[0].astype(_F32)
    for j in range(MLA_PAIR):
        kvh = kv_ref[0, :, j * MLA_HB:(j + 1) * MLA_HB].astype(_F32)
        kt_scr[j] = jnp.where(lane < MLA_NOPE, kvh, kr).T.astype(_BF)
        v_scr[j] = jnp.where(lane < MLA_NOPE, 1.0, kvh).astype(_BF)
    lane_q = lax.broadcasted_iota(jnp.int32, (MLA_TQ, MLA_HB), 1)

    def block_rows(i):
        return pl.ds(pl.multiple_of(i * MLA_TQ, MLA_TQ), MLA_TQ)

    def scores(i, slot):
        for j in range(MLA_PAIR):
            s_scr[slot, j] = _dot(q_ref[0, block_rows(i), j * MLA_HB:(j + 1) * MLA_HB], kt_scr[j])

    def finish(i, slot):
        outs = []
        for j in range(MLA_PAIR):
            s = s_scr[slot, j]
            p = jnp.exp2(s - jnp.max(s, axis=1, keepdims=True))
            o = _dot(p, v_scr[j])
            outs.append(o / o[:, 0:1])
        even = pltpu.roll(outs[0], MLA_V, axis=1)
        o_ref[0, block_rows(i), :] = jnp.where(lane_q < MLA_V, even, outs[1]).astype(_BF)

    nblk = SEQ // MLA_TQ
    scores(0, 0)

    def body(k, carry):
        scores(2 * k + 1, 1)
        finish(2 * k, 0)
        scores(2 * k + 2, 0)
        finish(2 * k + 1, 1)
        return carry

    lax.fori_loop(0, nblk // 2 - 1, body, 0)
    scores(nblk - 1, 1)
    finish(nblk - 2, 0)
    finish(nblk - 1, 1)


def _mla_attention(q, kv, kr):
    b = q.shape[0]
    w = MLA_PAIR * MLA_HB
    return pl.pallas_call(
        _mla_attn_kernel, grid=(b, MLA_HEADS // MLA_PAIR),
        in_specs=[pl.BlockSpec((1, SEQ, w), lambda i, h: (i, 0, h)),
                  pl.BlockSpec((1, SEQ, w), lambda i, h: (i, 0, h)),
                  pl.BlockSpec((1, SEQ, MLA_HB), lambda i, h: (i, 0, 0))],
        out_specs=pl.BlockSpec((1, SEQ, MLA_PAIR * MLA_V), lambda i, h: (i, 0, h)),
        out_shape=jax.ShapeDtypeStruct((b, SEQ, MLA_HEADS * MLA_V), _BF),
        scratch_shapes=[pltpu.VMEM((MLA_PAIR, MLA_HB, SEQ), _BF), pltpu.VMEM((MLA_PAIR, SEQ, MLA_HB), _BF),
                        pltpu.VMEM((2, MLA_PAIR, MLA_TQ, SEQ), _F32)],
        compiler_params=_cp(("parallel", "parallel"), 40), name="mla_attention")(q, kv, kr)


def _na_hyena_mixer(x, w_in, rpb, conv_w, conv_b, f_w1, f_b1, f_freq, f_w2, f_b2, f_w3, skip):
    b = x.shape[0]
    qkv, hb = _inproj(x.reshape(b * SEQ, D_MODEL), w_in)
    y_a = _na_attention(qkv.reshape(b, SEQ, 3 * NA_WIDTH), rpb)
    hbc, zb = _short_conv(hb.reshape(b, SEQ, (HY_ORDER + 1) * HY_WIDTH), conv_w, conv_b)
    fmat, fmat_t = (jnp.asarray(f).astype(_BF) for f in _dft_matrices())
    kfilt = _hyena_filters(f_w1, f_b1, f_freq, f_w2, f_b2, f_w3)
    kf = _filter_spectrum(fmat, kfilt)
    z = hbc
    for o in range(HY_ORDER):
        z, zb = _long_conv_gate(hbc, z, zb, o + 1, o, fmat, fmat_t, kf, skip)
    return y_a.reshape(b * SEQ, NA_WIDTH), zb.reshape(b * SEQ, HY_WIDTH)


def kernel(x, p, ab_w_in, na_rpb, hy_conv_w, hy_conv_b, hy_f_w1, hy_f_b1, hy_f_freq, hy_f_w2, hy_f_b2, hy_f_w3, hy_skip, ab_w_out, mla_w_in, mla_q_norm, mla_w_q_up, mla_kv_norm, mla_w_kv_up, mla_w_out, ln1_g, ln1_b, ln2_g, ln2_b, moe_router, moe_w_gate, moe_w_up, moe_w_down, ple_gate, ple_proj):
    b = x.shape[0]
    m = b * SEQ
    ple_gate_b, ple_proj_b = ple_gate.astype(_BF), ple_proj.astype(_BF)
    for i in range(DEPTH):
        j = i // 2
        x2d = x.reshape(m, D_MODEL)
        if i % 2 == 0:
            y_a, z = _na_hyena_mixer(x, ab_w_in[j].astype(_BF), na_rpb[j], hy_conv_w[j], hy_conv_b[j], hy_f_w1[j],
                                     hy_f_b1[j], hy_f_freq[j], hy_f_w2[j], hy_f_b2[j], hy_f_w3[j], hy_skip[j])
            x1, x1b, aff = _outproj_ln([y_a, z], ab_w_out[j].astype(_BF), x2d, ln1_g[i], ln1_b[i], moe_router[i])
        else:
            q, kv, kr = _mla_pre(x2d, mla_w_in[j], mla_q_norm[j], mla_w_q_up[j], mla_kv_norm[j], mla_w_kv_up[j])
            att = _mla_attention(q.reshape(b, SEQ, -1), kv.reshape(b, SEQ, -1), kr.reshape(b, SEQ, -1))
            x1, x1b, aff = _outproj_ln([att.reshape(m, MLA_HEADS * MLA_V)], mla_w_out[j].astype(_BF), x2d,
                                       ln1_g[i], ln1_b[i], moe_router[i])
        x = _moe_ple_block(x1.reshape(b, SEQ, D_MODEL), x1b.reshape(b, SEQ, D_MODEL), aff,
                           moe_w_gate, moe_w_up, moe_w_down, ln2_g[i], ln2_b[i], p, ple_gate_b, ple_proj_b, i)
    return x
```

```python
import functools
import math

import numpy as np
import jax
import jax.numpy as jnp
from jax import lax
from jax.experimental import pallas as pl
from jax.experimental.pallas import tpu as pltpu

D_MODEL = 1024
BATCH = 8
SEQ = 2048
DEPTH = 2
GRID_W = 64
PLE_DIM = 256
NA_HEADS = 8
NA_HEAD_DIM = 64
NA_WIDTH = NA_HEADS * NA_HEAD_DIM
NA_WIN_ROWS_MAX = 8
NA_WIN_COLS = 16
HY_WIDTH = D_MODEL - NA_WIDTH
HY_ORDER = 2
HY_SHORT_K = 3
HY_EMB_DIM = 33
HY_FILTER_HIDDEN = 64
HY_FAST_DECAY_PCT = 0.3
HY_SLOW_DECAY_PCT = 1.5
HY_DECAY_TARGET = 1e-2
AB_IN_WIDTH = 3 * NA_WIDTH + (HY_ORDER + 1) * HY_WIDTH
MLA_HEADS = 16
MLA_Q_LORA = 384
MLA_KV_LORA = 256
MLA_NOPE = 64
MLA_ROPE = 32
MLA_V = 64
ROPE_THETA = 10000.0
N_EXPERTS = 16
EC_CAPACITY_FACTOR = 2
D_FF_EXPERT = 2048
DN_ALPHA = (2 * DEPTH) ** 0.25
NORM_EPS = 1e-5
NEG_INF = -1e30

CAP = EC_CAPACITY_FACTOR * SEQ // N_EXPERTS
NA_ROWS = SEQ // GRID_W
NA_WIN_ROWS = min(NA_WIN_ROWS_MAX, NA_ROWS)
NA_SLAB = NA_WIN_ROWS * GRID_W
FFT_N = 2 * SEQ
LOG2E = math.log2(math.e)

LANE = 128
MIB = 1 << 20

_BF = jnp.bfloat16
_F32 = jnp.float32
_HP = lax.Precision.HIGHEST


def _cp(sem, vmem_mib):
    return pltpu.CompilerParams(dimension_semantics=sem, vmem_limit_bytes=vmem_mib * MIB)


def _dot(a, b):
    return jnp.dot(a.astype(_BF), b.astype(_BF), preferred_element_type=_F32)


def _dot_nt(a, b):
    return lax.dot_general(a.astype(_BF), b.astype(_BF), (((1,), (1,)), ((), ())),
                           preferred_element_type=_F32)


def _layer_norm(y, g, b):
    mu = jnp.mean(y, axis=-1, keepdims=True)
    d = y - mu
    var = jnp.mean(d * d, axis=-1, keepdims=True)
    return d * lax.rsqrt(var + NORM_EPS) * g + b


INPROJ_TM = 512
HALO = 8


def _inproj_kernel(x_ref, xp_ref, xn_ref, w_ref, cw_ref, cb_ref, qkv_ref, hbc_ref, vb_ref):
    tile_in_seq = pl.program_id(0) % (SEQ // INPROJ_TM)
    x_ext = jnp.concatenate([xp_ref[...], x_ref[...], xn_ref[...]], axis=0)
    h = _dot(x_ext, w_ref[...])
    qkv_ref[...] = h[HALO:HALO + INPROJ_TM, :3 * NA_WIDTH].astype(_BF)
    hb = h[:, 3 * NA_WIDTH:]
    n = hb.shape[0]
    row = lax.broadcasted_iota(jnp.int32, hb.shape, 0)
    pad_row = jnp.logical_or(jnp.logical_and(row == HALO - 1, tile_in_seq == 0),
                             jnp.logical_and(row == HALO + INPROJ_TM, tile_in_seq == SEQ // INPROJ_TM - 1))
    hb = jnp.where(pad_row, 0.0, hb)
    y = (cb_ref[...] + pltpu.roll(hb, 1, axis=0) * cw_ref[0:1, :] + hb * cw_ref[1:2, :]
         + pltpu.roll(hb, n - 1, axis=0) * cw_ref[2:3, :])[HALO:HALO + INPROJ_TM]
    hbc_ref[...] = y
    vb_ref[...] = y[:, :HY_WIDTH].astype(_BF)


def _inproj_conv(x2d, w, conv_w, conv_b):
    m = x2d.shape[0]
    tm = INPROJ_TM
    nq, nh = 3 * NA_WIDTH, (HY_ORDER + 1) * HY_WIDTH
    per_tile = tm // HALO
    last = m // HALO - 1
    return pl.pallas_call(
        _inproj_kernel, grid=(m // tm,),
        in_specs=[pl.BlockSpec((tm, D_MODEL), lambda i: (i, 0)),
                  pl.BlockSpec((HALO, D_MODEL), lambda i: (jnp.maximum(i * per_tile - 1, 0), 0)),
                  pl.BlockSpec((HALO, D_MODEL), lambda i: (jnp.minimum((i + 1) * per_tile, last), 0)),
                  pl.BlockSpec((D_MODEL, AB_IN_WIDTH), lambda i: (0, 0)),
                  pl.BlockSpec((HY_SHORT_K, nh), lambda i: (0, 0)),
                  pl.BlockSpec((1, nh), lambda i: (0, 0))],
        out_specs=[pl.BlockSpec((tm, nq), lambda i: (i, 0)),
                   pl.BlockSpec((tm, nh), lambda i: (i, 0)),
                   pl.BlockSpec((tm, HY_WIDTH), lambda i: (i, 0))],
        out_shape=[jax.ShapeDtypeStruct((m, nq), _BF), jax.ShapeDtypeStruct((m, nh), _F32),
                   jax.ShapeDtypeStruct((m, HY_WIDTH), _BF)],
        compiler_params=_cp(("parallel",), 56), name="ab_inproj_conv")(
            x2d, x2d, x2d, w, conv_w, conv_b.reshape(1, nh))


NA_GROUP = 4
NA_GW = NA_GROUP * NA_HEAD_DIM


def _na_kernel(q_ref, k_ref, v_ref, tbl_ref, o_ref):
    lane_head = lax.broadcasted_iota(jnp.int32, (GRID_W, NA_GW), 1) >> int(math.log2(NA_HEAD_DIM))
    scale = NA_HEAD_DIM ** -0.5 * LOG2E

    def body(r, carry):
        r0 = jnp.clip(r - NA_WIN_ROWS // 2, 0, NA_ROWS - NA_WIN_ROWS)
        off = r0 - r + (NA_WIN_ROWS_MAX - 1)
        qr = q_ref[0, pl.ds(pl.multiple_of(r * GRID_W, GRID_W), GRID_W), :].astype(_F32)
        q4 = jnp.concatenate([jnp.where(lane_head == h, qr, 0.0) for h in range(NA_GROUP)], axis=0)
        ks = k_ref[0, pl.ds(pl.multiple_of(r0 * GRID_W, GRID_W), NA_SLAB), :]
        vs = v_ref[0, pl.ds(pl.multiple_of(r0 * GRID_W, GRID_W), NA_SLAB), :]
        bias = jnp.concatenate([tbl_ref[h, off] for h in range(NA_GROUP)], axis=0)
        s = _dot_nt(q4, ks) * scale + bias
        mx = jnp.max(s, axis=1, keepdims=True)
        p = jnp.exp2(s - mx)
        den = jnp.sum(p, axis=1, keepdims=True)
        o4 = _dot(p, vs) / den
        out = jnp.zeros((GRID_W, NA_GW), _F32)
        for h in range(NA_GROUP):
            out = out + jnp.where(lane_head == h, o4[h * GRID_W:(h + 1) * GRID_W], 0.0)
        o_ref[0, pl.ds(pl.multiple_of(r * GRID_W, GRID_W), GRID_W), :] = out.astype(_BF)
        return carry

    lax.fori_loop(0, NA_ROWS, body, 0, unroll=8)


def _na_bias_table(rpb):
    c = np.arange(GRID_W)
    c0 = np.clip(c - NA_WIN_COLS // 2, 0, GRID_W - NA_WIN_COLS)
    kc = np.arange(GRID_W)
    col_ok = (kc[None, :] >= c0[:, None]) & (kc[None, :] < c0[:, None] + NA_WIN_COLS)
    dc_idx = np.clip(kc[None, :] - c[:, None], -(NA_WIN_COLS - 1), NA_WIN_COLS - 1) + (NA_WIN_COLS - 1)
    pick = (dc_idx[None, :, :] == np.arange(2 * NA_WIN_COLS - 1)[:, None, None]).astype(np.float32)
    per_dr = jnp.einsum("hdk,kqc->hdqc", rpb.astype(_F32), pick, precision=_HP)
    per_dr = jnp.where(col_ok[None, None], per_dr * LOG2E, NEG_INF)
    slabs = jnp.stack([per_dr[:, off:off + NA_WIN_ROWS] for off in range(NA_WIN_ROWS_MAX)], axis=1)
    return slabs.transpose(0, 1, 3, 2, 4).reshape(NA_HEADS, NA_WIN_ROWS_MAX, GRID_W, NA_SLAB)


def _na_attention(qkv, rpb):
    b = qkv.shape[0]
    tbl = _na_bias_table(rpb)
    ng = NA_HEADS // NA_GROUP
    blk = lambda col0: pl.BlockSpec((1, SEQ, NA_GW), lambda i, g, c=col0: (i, 0, c + g))
    return pl.pallas_call(
        _na_kernel, grid=(b, ng),
        in_specs=[blk(0), blk(ng), blk(2 * ng),
                  pl.BlockSpec((NA_GROUP, NA_WIN_ROWS_MAX, GRID_W, NA_SLAB), lambda i, g: (g, 0, 0, 0))],
        out_specs=pl.BlockSpec((1, SEQ, NA_GW), lambda i, g: (i, 0, g)),
        out_shape=jax.ShapeDtypeStruct((b, SEQ, NA_WIDTH), _BF),
        compiler_params=_cp(("parallel", "parallel"), 32), name="na_attention")(qkv, qkv, qkv, tbl)


HY_HID_PAD = LANE
HY_FILT_TC = 256


def _filter_kernel(z_ref, w1_ref, b1_ref, fr_ref, w2_ref, b2_ref, w3f_ref, w3b_ref, dec_ref, k_ref, h_ref):
    @pl.when(jnp.logical_and(pl.program_id(0) == 0, pl.program_id(1) == 0))
    def _():
        fr = fr_ref[...]
        h1 = jnp.sin(fr * (jnp.dot(z_ref[...], w1_ref[...], precision=_HP, preferred_element_type=_F32) + b1_ref[...]))
        h_ref[...] = jnp.sin(fr * (jnp.dot(h1, w2_ref[...], precision=_HP, preferred_element_type=_F32) + b2_ref[...]))

    fwd = jnp.dot(h_ref[:SEQ, :], w3f_ref[...], precision=_HP, preferred_element_type=_F32)
    bwd = jnp.dot(h_ref[SEQ:, :], w3b_ref[...], precision=_HP, preferred_element_type=_F32)
    k = jnp.concatenate([fwd, bwd], axis=0) * dec_ref[...]
    ss = jnp.sum(k * k, axis=0, keepdims=True)
    k_ref[...] = (k * lax.rsqrt(ss + 1e-12)).astype(_BF)


@functools.lru_cache(maxsize=None)
def _filter_tables():
    bands = (HY_EMB_DIM - 1) // 2
    t = np.linspace(0.0, 1.0, SEQ)[:, None]
    w = 2.0 * math.pi * np.arange(SEQ)[:, None] / SEQ
    f = np.linspace(1e-4, bands - 1, bands)[None, :]
    z = np.concatenate([t, np.cos(f * w), -np.sin(f * w)], axis=-1)
    min_decay = math.log(HY_DECAY_TARGET) / HY_SLOW_DECAY_PCT
    max_decay = math.log(HY_DECAY_TARGET) / HY_FAST_DECAY_PCT
    deltas = np.abs(np.linspace(min_decay, max_decay, HY_WIDTH))
    dec = np.exp(-t * deltas)
    src = np.concatenate([np.arange(SEQ), [0], np.arange(SEQ - 1, 0, -1)])
    live = np.ones((FFT_N, 1))
    live[SEQ] = 0.0
    z2 = np.pad(z[src], ((0, 0), (0, HY_HID_PAD - HY_EMB_DIM))).astype(np.float32)
    dec2 = (dec[src] * live).astype(np.float32)
    return z2, dec2


def _hyena_filters(w1, b1, freq, w2, b2, w3):
    z2, dec2 = _filter_tables()
    hp = HY_HID_PAD - HY_FILTER_HIDDEN
    w1p = jnp.pad(w1, ((0, HY_HID_PAD - HY_EMB_DIM), (0, hp)))
    w2p = jnp.pad(w2, ((0, hp), (0, hp)))
    w3p = jnp.pad(w3, ((0, hp), (0, 0)))
    row = lambda v: jnp.pad(v, (0, hp)).reshape(1, HY_HID_PAD)
    nc = HY_WIDTH // HY_FILT_TC
    per_order = 2 * nc
    full = lambda shape: pl.BlockSpec(shape, lambda o, j: (0, 0))
    return pl.pallas_call(
        _filter_kernel, grid=(HY_ORDER, nc),
        in_specs=[full((FFT_N, HY_HID_PAD)), full((HY_HID_PAD, HY_HID_PAD)), full((1, HY_HID_PAD)),
                  full((1, HY_HID_PAD)), full((HY_HID_PAD, HY_HID_PAD)), full((1, HY_HID_PAD)),
                  pl.BlockSpec((HY_HID_PAD, HY_FILT_TC), lambda o, j: (0, o * per_order + j)),
                  pl.BlockSpec((HY_HID_PAD, HY_FILT_TC), lambda o, j: (0, o * per_order + nc + j)),
                  pl.BlockSpec((FFT_N, HY_FILT_TC), lambda o, j: (0, j))],
        out_specs=pl.BlockSpec((FFT_N, HY_FILT_TC), lambda o, j: (0, o * nc + j)),
        out_shape=jax.ShapeDtypeStruct((FFT_N, HY_ORDER * HY_WIDTH), _BF),
        scratch_shapes=[pltpu.VMEM((FFT_N, HY_HID_PAD), _F32)],
        compiler_params=_cp(("arbitrary", "arbitrary"), 40), name="hy_filters")(
            z2, w1p, row(b1), row(freq), w2p, row(b2), w3p, w3p, dec2)


@functools.lru_cache(maxsize=None)
def _dft_matrices():
    t = np.arange(SEQ)
    ang = ((t[:, None] * t[None, :]) % FFT_N) * (2.0 * math.pi / FFT_N)
    re = np.cos(ang)
    im = -np.sin(ang)
    im[0] = 1.0 - 2.0 * (t % 2)
    nf = SEQ // HY_FB
    packed = np.stack([re.reshape(nf, HY_FB, SEQ), im.reshape(nf, HY_FB, SEQ)], axis=1).reshape(FFT_N, SEQ)
    return packed.astype(np.float32), np.ascontiguousarray(packed.T).astype(np.float32)


def _kf_kernel(f_ref, k_ref, o_ref):
    i = pl.program_id(0)
    f = f_ref[...]
    tm = f.shape[0]
    p1 = jnp.dot(f, k_ref[:SEQ, :], preferred_element_type=_F32)
    p2 = jnp.dot(f, k_ref[SEQ:, :], preferred_element_type=_F32)
    row = lax.broadcasted_iota(jnp.int32, (tm, 1), 0) + i * tm
    sign = (1 - 2 * (row & 1)).astype(_F32)
    o_ref[...] = p1 + sign * p2


def _filter_spectrum(fmat, kfilt):
    tm = 512
    nw = kfilt.shape[1]
    return pl.pallas_call(
        _kf_kernel, grid=(FFT_N // tm,),
        in_specs=[pl.BlockSpec((tm, SEQ), lambda i: (i, 0)),
                  pl.BlockSpec((FFT_N, nw), lambda i: (0, 0))],
        out_specs=pl.BlockSpec((tm, nw), lambda i: (i, 0)),
        out_shape=jax.ShapeDtypeStruct((FFT_N, nw), _F32),
        compiler_params=_cp(("parallel",), 40), name="hy_filter_spectrum")(fmat, kfilt)


HY_FB = 512
HY_NB = 2


HY_TM = 512


def _hconv_fwd_kernel(zb_ref, f_ref, k_ref, y_ref):
    fk = pl.program_id(1)
    kr = k_ref[:HY_FB, :]
    ki = k_ref[HY_FB:, :]
    row0 = jnp.logical_and(lax.broadcasted_iota(jnp.int32, kr.shape, 0) == 0, fk == 0)
    sc = jnp.where(row0, 1.0 / FFT_N, 2.0 / FFT_N)
    for j in range(HY_NB):
        u = jnp.dot(f_ref[...], zb_ref[j], preferred_element_type=_F32)
        ur = u[:HY_FB]
        ui = u[HY_FB:]
        yr = jnp.where(row0, ur * kr, ur * kr - ui * ki)
        yi = jnp.where(row0, ui * ki, ur * ki + ui * kr)
        y_ref[j, 0, 0] = (yr * sc).astype(_BF)
        y_ref[j, 0, 1] = (yi * sc).astype(_BF)


def _hconv_inv_kernel(ft_ref, y_ref, z_ref, skip_ref, xn_ref, o_ref, ob_ref):
    for j in range(HY_NB):
        conv = jnp.dot(ft_ref[...], y_ref[j], preferred_element_type=_F32)
        out = xn_ref[j] * (conv + z_ref[j] * skip_ref[...])
        o_ref[j] = out
        ob_ref[j] = out.astype(_BF)


def _long_conv_gate(hbc, zsrc, zb, xn_col, order, fmat, fmat_t, kf, skip):
    b = hbc.shape[0]
    nf = SEQ // HY_FB
    w = HY_WIDTH
    nb = HY_NB
    y = pl.pallas_call(
        _hconv_fwd_kernel, grid=(b // nb, nf),
        in_specs=[pl.BlockSpec((nb, SEQ, w), lambda i, f: (i, 0, 0)),
                  pl.BlockSpec((2 * HY_FB, SEQ), lambda i, f: (f, 0)),
                  pl.BlockSpec((2 * HY_FB, w), lambda i, f: (f, order))],
        out_specs=pl.BlockSpec((nb, 1, 2, HY_FB, w), lambda i, f: (i, f, 0, 0, 0)),
        out_shape=jax.ShapeDtypeStruct((b, nf, 2, HY_FB, w), _BF),
        compiler_params=_cp(("parallel", "parallel"), 40), name=f"hy_conv_fwd{order}")(zb, fmat, kf)
    blk = lambda col: pl.BlockSpec((nb, HY_TM, w), lambda i, m, c=col: (i, m, c))
    return pl.pallas_call(
        _hconv_inv_kernel, grid=(b // nb, SEQ // HY_TM),
        in_specs=[pl.BlockSpec((HY_TM, FFT_N), lambda i, m: (m, 0)),
                  pl.BlockSpec((nb, FFT_N, w), lambda i, m: (i, 0, 0)),
                  blk(0),
                  pl.BlockSpec((1, w), lambda i, m: (0, 0)),
                  blk(xn_col)],
        out_specs=[blk(0), blk(0)],
        out_shape=[jax.ShapeDtypeStruct((b, SEQ, w), _F32), jax.ShapeDtypeStruct((b, SEQ, w), _BF)],
        compiler_params=_cp(("parallel", "parallel"), 48), name=f"hy_conv_inv{order}")(
            fmat_t, y.reshape(b, FFT_N, w), zsrc, skip[order].reshape(1, w), hbc)


def _outproj_ln_kernel(n_a, *refs):
    a_refs = refs[:n_a]
    w_refs = refs[n_a:2 * n_a]
    x_ref, g_ref, b_ref, wr_ref, o_ref, ob_ref, aff_ref = refs[2 * n_a:]
    m = _dot(a_refs[0][...], w_refs[0][...])
    for a_ref, w_ref in zip(a_refs[1:], w_refs[1:]):
        m = m + _dot(a_ref[...], w_ref[...])
    y = _layer_norm(DN_ALPHA * x_ref[...] + m, g_ref[...], b_ref[...])
    o_ref[...] = y
    ob_ref[...] = y.astype(_BF)
    wr = wr_ref[...]
    wr_hi = wr.astype(_BF)
    wr_lo = (wr - wr_hi.astype(_F32)).astype(_BF)
    y_hi = y.astype(_BF)
    y_lo = (y - y_hi.astype(_F32)).astype(_BF)
    by_hi = _dot_nt(jnp.concatenate([wr_hi, wr_lo], axis=0), y_hi)
    logits = by_hi[:N_EXPERTS] + by_hi[N_EXPERTS:] + _dot_nt(wr_hi, y_lo)
    ex = jnp.exp(logits - jnp.max(logits, axis=0, keepdims=True))
    aff_ref[0] = ex / jnp.sum(ex, axis=0, keepdims=True)


def _outproj_ln(a_list, w, x2d, g, b, w_router):
    m = x2d.shape[0]
    tm = 512
    per_seq = SEQ // tm
    in_specs, w_args, row0 = [], [], 0
    for a in a_list:
        in_specs.append(pl.BlockSpec((tm, a.shape[1]), lambda i: (i, 0)))
    for a in a_list:
        ka = a.shape[1]
        in_specs.append(pl.BlockSpec((ka, D_MODEL), lambda i, r=row0 // ka: (r, 0)))
        w_args.append(w)
        row0 += ka
    in_specs += [pl.BlockSpec((tm, D_MODEL), lambda i: (i, 0)),
                 pl.BlockSpec((1, D_MODEL), lambda i: (0, 0)),
                 pl.BlockSpec((1, D_MODEL), lambda i: (0, 0)),
                 pl.BlockSpec((N_EXPERTS, D_MODEL), lambda i: (0, 0))]
    return pl.pallas_call(
        functools.partial(_outproj_ln_kernel, len(a_list)), grid=(m // tm,),
        in_specs=in_specs,
        out_specs=[pl.BlockSpec((tm, D_MODEL), lambda i: (i, 0)),
                   pl.BlockSpec((tm, D_MODEL), lambda i: (i, 0)),
                   pl.BlockSpec((1, N_EXPERTS, tm), lambda i: (i // per_seq, 0, i % per_seq))],
        out_shape=[jax.ShapeDtypeStruct((m, D_MODEL), _F32), jax.ShapeDtypeStruct((m, D_MODEL), _BF),
                   jax.ShapeDtypeStruct((m // SEQ, N_EXPERTS, SEQ), _F32)],
        compiler_params=_cp(("parallel",), 40), name="outproj_ln_router")(
            *a_list, *w_args, x2d, g.reshape(1, -1), b.reshape(1, -1), w_router.T)


BISECT_STEPS = 160


def _select_kernel(a_ref, tri_ref, edge_ref, slot_ref, slot_t_ref, cnt_ref):
    a = a_ref[...]
    rows = a.shape[0]
    cap = float(CAP)

    def body(_, c):
        lo, hi = c
        mid = 0.5 * (lo + hi)
        cnt = jnp.sum(jnp.where(a > mid, 1.0, 0.0), axis=1, keepdims=True)
        ge = cnt >= cap
        return jnp.where(ge, mid, lo), jnp.where(ge, hi, mid)

    lo, hi = lax.fori_loop(0, BISECT_STEPS, body,
                           (jnp.full((rows, 1), -1.0, _F32), jnp.full((rows, 1), 1.0, _F32)))
    vstar = jnp.max(jnp.where(a <= hi, a, -1.0), axis=1, keepdims=True)
    gt = a > vstar
    eq = a == vstar
    need = cap - jnp.sum(jnp.where(gt, 1.0, 0.0), axis=1, keepdims=True)
    tri = tri_ref[...]
    eq_before = jnp.dot(jnp.where(eq, 1.0, 0.0).astype(_BF), tri, preferred_element_type=_F32)
    sel = jnp.where(gt, 1.0, jnp.where(eq, jnp.where(eq_before < need, 1.0, 0.0), 0.0))
    pos = jnp.dot(sel.astype(_BF), tri, preferred_element_type=_F32)
    slot = jnp.where(sel > 0.5, pos, -1.0)
    slot_ref[...] = slot.astype(jnp.int32)
    cnt_ref[...] = jnp.dot(sel.astype(_BF), edge_ref[...], preferred_element_type=_F32).astype(jnp.int32)
    if rows < LANE:
        slot = jnp.concatenate([slot, jnp.full((LANE - rows, SEQ), -1.0, _F32)], axis=0)
    slot_tm = slot.T
    for b in range(rows // N_EXPERTS):
        shifted = slot_tm if b == 0 else pltpu.roll(slot_tm, LANE - N_EXPERTS * b, axis=1)
        slot_t_ref[b] = shifted.astype(jnp.int32)


@functools.lru_cache(maxsize=None)
def _prefix_matrix():
    idx = np.arange(SEQ)
    return (idx[:, None] < idx[None, :]).astype(_BF)


WIN_TT = 256
WIN_W = 128
WIN_EDGES = 16


@functools.lru_cache(maxsize=None)
def _edge_matrix():
    return (np.arange(SEQ)[:, None] < np.arange(LANE)[None, :] * WIN_TT).astype(_BF)


def _select(aff2d):
    rows = aff2d.shape[0]
    assert rows <= LANE and rows % N_EXPERTS == 0
    bsz = rows // N_EXPERTS
    slot, slot_t, cnt = pl.pallas_call(
        _select_kernel, grid=(1,),
        in_specs=[pl.BlockSpec((rows, SEQ), lambda i: (0, 0)),
                  pl.BlockSpec((SEQ, SEQ), lambda i: (0, 0)),
                  pl.BlockSpec((SEQ, LANE), lambda i: (0, 0))],
        out_specs=[pl.BlockSpec((rows, SEQ), lambda i: (0, 0)),
                   pl.BlockSpec((bsz, SEQ, LANE), lambda i: (0, 0, 0)),
                   pl.BlockSpec((rows, LANE), lambda i: (0, 0))],
        out_shape=[jax.ShapeDtypeStruct((rows, SEQ), jnp.int32),
                   jax.ShapeDtypeStruct((bsz, SEQ, LANE), jnp.int32),
                   jax.ShapeDtypeStruct((rows, LANE), jnp.int32)],
        compiler_params=_cp(("arbitrary",), 48), name="moe_select")(aff2d, _prefix_matrix(), _edge_matrix())
    return slot, slot_t, cnt[:, :WIN_EDGES].reshape(rows * WIN_EDGES)


def _onehot(slot_row):
    return slot_row == lax.broadcasted_iota(jnp.int32, (CAP, SEQ), 0)


GATHER_NE = 4
GATHER_TC = 512
GATHER_W = 64


def _window_start(cnt_ref, seq, e, tile, width):
    base = (seq * N_EXPERTS + e) * WIN_EDGES + tile
    start = jnp.minimum((cnt_ref[base] >> 4) << 4, CAP - width)
    return start, cnt_ref[base + 1] - start <= width


def _gather_kernel(cnt_ref, slot_ref, a_ref, xb_ref, xe_ref, g_ref):
    i = pl.program_id(0)
    n_tiles = SEQ // WIN_TT
    wd = GATHER_W
    starts, fits = {}, None
    for t in range(n_tiles):
        for e in range(N_EXPERTS):
            starts[e, t], ok = _window_start(cnt_ref, i, e, t, wd)
            fits = ok if fits is None else jnp.logical_and(fits, ok)

    @pl.when(fits)
    def _():
        xe_ref[...] = jnp.zeros(xe_ref.shape, _BF)
        g_ref[...] = jnp.zeros(g_ref.shape, _F32)
        sub_w = lax.broadcasted_iota(jnp.int32, (wd, WIN_TT), 0)
        for t in range(n_tiles):
            toks = slice(t * WIN_TT, (t + 1) * WIN_TT)
            ps = []
            for e in range(N_EXPERTS):
                hit = slot_ref[0, e, :, toks] - starts[e, t] == sub_w
                ps.append(jnp.where(hit, 1.0, 0.0).astype(_BF))
                rows = pl.ds(pl.multiple_of(starts[e, t], 16), wd)
                g_ref[e, rows, :] += jnp.sum(jnp.where(hit, a_ref[0, e, :, toks], 0.0), axis=1, keepdims=True)
            pcat = jnp.concatenate(ps, axis=0)
            for c in range(D_MODEL // GATHER_TC):
                cols = slice(c * GATHER_TC, (c + 1) * GATHER_TC)
                res = jnp.dot(pcat, xb_ref[0, toks, cols], preferred_element_type=_F32).astype(_BF)
                for e in range(N_EXPERTS):
                    rows = pl.ds(pl.multiple_of(starts[e, t], 16), wd)
                    xe_ref[e, rows, cols] += res[e * wd:(e + 1) * wd]

    @pl.when(jnp.logical_not(fits))
    def _():
        for grp in range(N_EXPERTS // GATHER_NE):
            es = range(grp * GATHER_NE, (grp + 1) * GATHER_NE)
            hits = [_onehot(slot_ref[0, e]) for e in es]
            p = jnp.concatenate([jnp.where(h, 1.0, 0.0).astype(_BF) for h in hits], axis=0)
            xe = jnp.dot(p, xb_ref[0], preferred_element_type=_F32).astype(_BF)
            xe_ref[grp * GATHER_NE:(grp + 1) * GATHER_NE] = xe.reshape(GATHER_NE, CAP, D_MODEL)
            for k, e in enumerate(es):
                g_ref[e] = jnp.sum(jnp.where(hits[k], a_ref[0, e], 0.0), axis=1, keepdims=True)


def _gather(cnt, slot4, aff4, xb3d):
    b = xb3d.shape[0]
    row = pl.BlockSpec((1, N_EXPERTS, 1, SEQ), lambda i, c: (i, 0, 0, 0))
    grid_spec = pltpu.PrefetchScalarGridSpec(
        num_scalar_prefetch=1, grid=(b,),
        in_specs=[row, row, pl.BlockSpec((1, SEQ, D_MODEL), lambda i, c: (i, 0, 0))],
        out_specs=[pl.BlockSpec((N_EXPERTS, CAP, D_MODEL), lambda i, c: (0, i, 0)),
                   pl.BlockSpec((N_EXPERTS, CAP, 1), lambda i, c: (0, i, 0))])
    return pl.pallas_call(
        _gather_kernel, grid_spec=grid_spec,
        out_shape=[jax.ShapeDtypeStruct((N_EXPERTS, b * CAP, D_MODEL), _BF),
                   jax.ShapeDtypeStruct((N_EXPERTS, b * CAP, 1), _F32)],
        compiler_params=_cp(("parallel",), 56), name="moe_gather")(cnt, slot4, aff4, xb3d)


FFN_TF = 512
FFN_TM = 512


def _ffn_up_kernel(xe_ref, wg_ref, wu_ref, h_ref):
    wg = wg_ref[0].astype(_BF)
    wu = wu_ref[0].astype(_BF)
    for c in range(xe_ref.shape[1] // FFN_TM):
        rows = slice(c * FFN_TM, (c + 1) * FFN_TM)
        xe = xe_ref[0, rows, :]
        hg = _dot(xe, wg)
        hu = _dot(xe, wu)
        h_ref[0, rows, :] = ((hg * (1.0 / (1.0 + jnp.exp(-hg)))) * hu).astype(_BF)


def _ffn_down_kernel(h_ref, wd_ref, g_ref, ye_ref):
    wd = wd_ref[0].astype(_BF)
    for c in range(h_ref.shape[1] // FFN_TM):
        rows = slice(c * FFN_TM, (c + 1) * FFN_TM)
        ye_ref[0, rows, :] = (_dot(h_ref[0, rows, :], wd) * g_ref[0, rows, :]).astype(_BF)


def _expert_ffn(xe, gates, wg, wu, wd, layer):
    rows = xe.shape[1]
    h = pl.pallas_call(
        _ffn_up_kernel, grid=(N_EXPERTS, D_FF_EXPERT // FFN_TF),
        in_specs=[pl.BlockSpec((1, rows, D_MODEL), lambda e, f: (e, 0, 0)),
                  pl.BlockSpec((None, 1, D_MODEL, FFN_TF), lambda e, f: (layer, e, 0, f)),
                  pl.BlockSpec((None, 1, D_MODEL, FFN_TF), lambda e, f: (layer, e, 0, f))],
        out_specs=pl.BlockSpec((1, rows, FFN_TF), lambda e, f: (e, 0, f)),
        out_shape=jax.ShapeDtypeStruct((N_EXPERTS, rows, D_FF_EXPERT), _BF),
        compiler_params=_cp(("parallel", "parallel"), 40), name="moe_ffn_up")(xe, wg, wu)
    return pl.pallas_call(
        _ffn_down_kernel, grid=(N_EXPERTS,),
        in_specs=[pl.BlockSpec((1, rows, D_FF_EXPERT), lambda e: (e, 0, 0)),
                  pl.BlockSpec((None, 1, D_FF_EXPERT, D_MODEL), lambda e: (layer, e, 0, 0)),
                  pl.BlockSpec((1, rows, 1), lambda e: (e, 0, 0))],
        out_specs=pl.BlockSpec((1, rows, D_MODEL), lambda e: (e, 0, 0)),
        out_shape=jax.ShapeDtypeStruct((N_EXPERTS, rows, D_MODEL), _BF),
        compiler_params=_cp(("parallel",), 56), name="moe_ffn_down")(h, wd, gates)


def _scatter_ln_ple_kernel(cnt_ref, slot_t_ref, ye_ref, x_ref, g_ref, b_ref, p_ref, wg_ref, wp_ref, o_ref,
                           pt_ref, yw_ref, f_ref):
    i = pl.program_id(0)
    r = pl.program_id(1)
    slot_t = slot_t_ref[0]
    starts = []
    fits = None
    for e in range(N_EXPERTS):
        start, ok = _window_start(cnt_ref, i, e, r, WIN_W)
        starts.append(start)
        fits = ok if fits is None else jnp.logical_and(fits, ok)

    @pl.when(fits)
    def _():
        lane_w = lax.broadcasted_iota(jnp.int32, (WIN_TT, WIN_W), 1)
        for e in range(N_EXPERTS):
            cols = slice(e * WIN_W, (e + 1) * WIN_W)
            pt_ref[:, cols] = jnp.where(slot_t[:, e:e + 1] - starts[e] == lane_w, 1.0, 0.0).astype(_BF)
            yw_ref[cols, :] = ye_ref[e, pl.ds(pl.multiple_of(starts[e], 16), WIN_W), :]
        f_ref[...] = jnp.dot(pt_ref[:, :N_EXPERTS * WIN_W], yw_ref[...], preferred_element_type=_F32)

    @pl.when(jnp.logical_not(fits))
    def _():
        lane_c = lax.broadcasted_iota(jnp.int32, (WIN_TT, CAP), 1)
        for e in range(N_EXPERTS):
            pt_ref[:, e * CAP:(e + 1) * CAP] = jnp.where(slot_t[:, e:e + 1] == lane_c, 1.0, 0.0).astype(_BF)
        f_ref[...] = jnp.dot(pt_ref[...], ye_ref[...].reshape(N_EXPERTS * CAP, D_MODEL),
                             preferred_element_type=_F32)

    y = _layer_norm(DN_ALPHA * x_ref[0] + f_ref[...], g_ref[...], b_ref[...])
    gate = _dot(y, wg_ref[...])
    emb = _dot(p_ref[0], wp_ref[...])
    o_ref[0] = y + (1.0 / (1.0 + jnp.exp(-gate))) * emb


def _scatter_ln_ple(cnt, slot_t, ye, x3d, g, b, p, ple_gate, ple_proj, layer):
    bsz = x3d.shape[0]
    vec = pl.BlockSpec((1, D_MODEL), lambda i, r, c: (0, 0))
    seq = pl.BlockSpec((1, WIN_TT, D_MODEL), lambda i, r, c: (i, r, 0))
    grid_spec = pltpu.PrefetchScalarGridSpec(
        num_scalar_prefetch=1, grid=(bsz, SEQ // WIN_TT),
        in_specs=[pl.BlockSpec((1, WIN_TT, LANE), lambda i, r, c: (i, r, 0)),
                  pl.BlockSpec((N_EXPERTS, CAP, D_MODEL), lambda i, r, c: (0, i, 0)),
                  seq, vec, vec,
                  pl.BlockSpec((None, 1, WIN_TT, PLE_DIM), lambda i, r, c: (layer, i, r, 0)),
                  pl.BlockSpec((None, D_MODEL, D_MODEL), lambda i, r, c: (layer, 0, 0)),
                  pl.BlockSpec((None, PLE_DIM, D_MODEL), lambda i, r, c: (layer, 0, 0))],
        out_specs=seq,
        scratch_shapes=[pltpu.VMEM((WIN_TT, N_EXPERTS * CAP), _BF),
                        pltpu.VMEM((N_EXPERTS * WIN_W, D_MODEL), _BF),
                        pltpu.VMEM((WIN_TT, D_MODEL), _F32)])
    return pl.pallas_call(
        _scatter_ln_ple_kernel, grid_spec=grid_spec,
        out_shape=jax.ShapeDtypeStruct((bsz, SEQ, D_MODEL), _F32),
        compiler_params=_cp(("parallel", "arbitrary"), 56), name="moe_scatter_ln_ple")(
            cnt, slot_t, ye, x3d, g.reshape(1, -1), b.reshape(1, -1), p, ple_gate, ple_proj)


def _moe_ple_block(x1, x1b, aff, wg, wu, wd, g, b, p, ple_gate, ple_proj, layer):
    bsz = x1.shape[0]
    slot, slot_t, cnt = _select(aff.reshape(bsz * N_EXPERTS, SEQ))
    slot4 = slot.reshape(bsz, N_EXPERTS, 1, SEQ)
    aff4 = aff.reshape(bsz, N_EXPERTS, 1, SEQ)
    xe, gates = _gather(cnt, slot4, aff4, x1b)
    ye = _expert_ffn(xe, gates, wg, wu, wd, layer)
    return _scatter_ln_ple(cnt, slot_t, ye, x1, g, b, p, ple_gate, ple_proj, layer)


MLA_HB = LANE
MLA_IN_PAD = 768
MLA_KR_COL = MLA_Q_LORA + MLA_KV_LORA


def _rms(x, g):
    return x * lax.rsqrt(jnp.mean(x * x, axis=-1, keepdims=True) + NORM_EPS) * g


def _mla_pre_kernel(x_ref, win_ref, qg_ref, wq_ref, kvg_ref, wkv_ref, qa_ref, qb_ref, ka_ref, kb_ref,
                    q_ref, kv_ref, kr_ref):
    h = _dot(x_ref[...], win_ref[...])
    q = _dot(_rms(h[:, :MLA_Q_LORA], qg_ref[...]), wq_ref[...])
    swap = pltpu.roll(q, q.shape[1] - MLA_ROPE, axis=1)
    qa = jnp.concatenate([qa_ref[...]] * MLA_HEADS, axis=1)
    qb = jnp.concatenate([qb_ref[...]] * MLA_HEADS, axis=1)
    q_ref[...] = (q * qa + swap * qb).astype(_BF)
    kv_ref[...] = _dot(_rms(h[:, MLA_Q_LORA:MLA_KR_COL], kvg_ref[...]), wkv_ref[...]).astype(_BF)
    kr = h[:, MLA_KR_COL:MLA_KR_COL + MLA_HB]
    kr_ref[...] = (kr * ka_ref[...] + pltpu.roll(kr, MLA_HB - MLA_ROPE, axis=1) * kb_ref[...]).astype(_BF)


@functools.lru_cache(maxsize=None)
def _rope_lane_tables():
    inv = 1.0 / (ROPE_THETA ** (np.arange(0, MLA_ROPE, 2) / MLA_ROPE))
    ang = np.arange(SEQ)[:, None] * inv[None, :]
    cos, sin = np.cos(ang), np.sin(ang)
    ones = np.ones((SEQ, MLA_NOPE))
    zeros_n = np.zeros((SEQ, MLA_NOPE))
    zeros_r = np.zeros((SEQ, MLA_ROPE))
    cc = np.concatenate([cos, cos], axis=1)
    ss = np.concatenate([-sin, sin], axis=1)
    keep_q = np.concatenate([ones, cc, zeros_r], axis=1)
    keep_k = np.concatenate([zeros_n, cc, zeros_r], axis=1)
    swp = np.concatenate([zeros_n, ss, zeros_r], axis=1)
    qscale = (MLA_NOPE + MLA_ROPE) ** -0.5 * LOG2E
    f32 = lambda a: a.astype(np.float32)
    return f32(keep_q * qscale), f32(swp * qscale), f32(keep_k), f32(swp)


def _mla_pre(x2d, w_in, q_norm, w_q_up, kv_norm, w_kv_up):
    m = x2d.shape[0]
    tm = 512
    half = MLA_ROPE // 2
    kr_w = w_in[:, MLA_KR_COL:]
    kr_swapped = jnp.concatenate([kr_w[:, half:], kr_w[:, :half]], axis=1)
    win = jnp.concatenate([w_in[:, :MLA_KR_COL], jnp.zeros((D_MODEL, MLA_NOPE), _F32), kr_w, kr_swapped],
                          axis=1).astype(_BF)
    wq = w_q_up.reshape(MLA_Q_LORA, MLA_HEADS, MLA_NOPE + MLA_ROPE)
    rope_w = wq[:, :, MLA_NOPE:]
    wq = jnp.concatenate([wq, rope_w[:, :, half:], rope_w[:, :, :half]], axis=2).reshape(
        MLA_Q_LORA, MLA_HEADS * MLA_HB).astype(_BF)
    w_kv_up = w_kv_up.astype(_BF)
    q_keep, q_swap, k_keep, k_swap = _rope_lane_tables()
    nq = MLA_HEADS * MLA_HB
    pos = lambda: pl.BlockSpec((tm, MLA_HB), lambda i: (i % (SEQ // tm), 0))
    full = lambda a: pl.BlockSpec(a.shape, lambda i: (0,) * a.ndim)
    qg, kvg = q_norm.reshape(1, -1), kv_norm.reshape(1, -1)
    return pl.pallas_call(
        _mla_pre_kernel, grid=(m // tm,),
        in_specs=[pl.BlockSpec((tm, D_MODEL), lambda i: (i, 0)), full(win), full(qg), full(wq), full(kvg),
                  full(w_kv_up), pos(), pos(), pos(), pos()],
        out_specs=[pl.BlockSpec((tm, nq), lambda i: (i, 0)),
                   pl.BlockSpec((tm, nq), lambda i: (i, 0)),
                   pl.BlockSpec((tm, MLA_HB), lambda i: (i, 0))],
        out_shape=[jax.ShapeDtypeStruct((m, nq), _BF), jax.ShapeDtypeStruct((m, nq), _BF),
                   jax.ShapeDtypeStruct((m, MLA_HB), _BF)],
        compiler_params=_cp(("parallel",), 48), name="mla_pre")(
            x2d, win, qg, wq, kvg, w_kv_up, q_keep, q_swap, k_keep, k_swap)


MLA_TQ = 256
MLA_PAIR = 2


def _mla_attn_kernel(q_ref, kv_ref, kr_ref, o_ref, kt_scr, v_scr, s_scr):
    lane = lax.broadcasted_iota(jnp.int32, (SEQ, MLA_HB), 1)
    kr = kr_ref[0].astype(_F32)
    for j in range(MLA_PAIR):
        kvh = kv_ref[0, :, j * MLA_HB:(j + 1) * MLA_HB].astype(_F32)
        kt_scr[j] = jnp.where(lane < MLA_NOPE, kvh, kr).T.astype(_BF)
        v_scr[j] = jnp.where(lane < MLA_NOPE, 1.0, kvh).astype(_BF)
    lane_q = lax.broadcasted_iota(jnp.int32, (MLA_TQ, MLA_HB), 1)

    def block_rows(i):
        return pl.ds(pl.multiple_of(i * MLA_TQ, MLA_TQ), MLA_TQ)

    def scores(i, slot):
        for j in range(MLA_PAIR):
            s_scr[slot, j] = _dot(q_ref[0, block_rows(i), j * MLA_HB:(j + 1) * MLA_HB], kt_scr[j])

    def finish(i, slot):
        outs = []
        for j in range(MLA_PAIR):
            s = s_scr[slot, j]
            p = jnp.exp2(s - jnp.max(s, axis=1, keepdims=True))
            o = _dot(p, v_scr[j])
            outs.append(o / o[:, 0:1])
        even = pltpu.roll(outs[0], MLA_V, axis=1)
        o_ref[0, block_rows(i), :] = jnp.where(lane_q < MLA_V, even, outs[1]).astype(_BF)

    nblk = SEQ // MLA_TQ
    scores(0, 0)

    def body(k, carry):
        scores(2 * k + 1, 1)
        finish(2 * k, 0)
        scores(2 * k + 2, 0)
        finish(2 * k + 1, 1)
        return carry

    lax.fori_loop(0, nblk // 2 - 1, body, 0)
    scores(nblk - 1, 1)
    finish(nblk - 2, 0)
    finish(nblk - 1, 1)


def _mla_attention(q, kv, kr):
    b = q.shape[0]
    w = MLA_PAIR * MLA_HB
    return pl.pallas_call(
        _mla_attn_kernel, grid=(b, MLA_HEADS // MLA_PAIR),
        in_specs=[pl.BlockSpec((1, SEQ, w), lambda i, h: (i, 0, h)),
                  pl.BlockSpec((1, SEQ, w), lambda i, h: (i, 0, h)),
                  pl.BlockSpec((1, SEQ, MLA_HB), lambda i, h: (i, 0, 0))],
        out_specs=pl.BlockSpec((1, SEQ, MLA_PAIR * MLA_V), lambda i, h: (i, 0, h)),
        out_shape=jax.ShapeDtypeStruct((b, SEQ, MLA_HEADS * MLA_V), _BF),
        scratch_shapes=[pltpu.VMEM((MLA_PAIR, MLA_HB, SEQ), _BF), pltpu.VMEM((MLA_PAIR, SEQ, MLA_HB), _BF),
                        pltpu.VMEM((2, MLA_PAIR, MLA_TQ, SEQ), _F32)],
        compiler_params=_cp(("parallel", "parallel"), 40), name="mla_attention")(q, kv, kr)


def _na_hyena_mixer(x, w_in, rpb, conv_w, conv_b, f_w1, f_b1, f_freq, f_w2, f_b2, f_w3, skip):
    b = x.shape[0]
    qkv, hbc, zb = _inproj_conv(x.reshape(b * SEQ, D_MODEL), w_in, conv_w, conv_b)
    y_a = _na_attention(qkv.reshape(b, SEQ, 3 * NA_WIDTH), rpb)
    hbc = hbc.reshape(b, SEQ, (HY_ORDER + 1) * HY_WIDTH)
    zb = zb.reshape(b, SEQ, HY_WIDTH)
    fmat, fmat_t = (jnp.asarray(f).astype(_BF) for f in _dft_matrices())
    kfilt = _hyena_filters(f_w1, f_b1, f_freq, f_w2, f_b2, f_w3)
    kf = _filter_spectrum(fmat, kfilt)
    z = hbc
    for o in range(HY_ORDER):
        z, zb = _long_conv_gate(hbc, z, zb, o + 1, o, fmat, fmat_t, kf, skip)
    return y_a.reshape(b * SEQ, NA_WIDTH), zb.reshape(b * SEQ, HY_WIDTH)


def kernel(x, p, ab_w_in, na_rpb, hy_conv_w, hy_conv_b, hy_f_w1, hy_f_b1, hy_f_freq, hy_f_w2, hy_f_b2, hy_f_w3, hy_skip, ab_w_out, mla_w_in, mla_q_norm, mla_w_q_up, mla_kv_norm, mla_w_kv_up, mla_w_out, ln1_g, ln1_b, ln2_g, ln2_b, moe_router, moe_w_gate, moe_w_up, moe_w_down, ple_gate, ple_proj):
    b = x.shape[0]
    m = b * SEQ
    ple_gate_b, ple_proj_b = ple_gate.astype(_BF), ple_proj.astype(_BF)
    for i in range(DEPTH):
        j = i // 2
        x2d = x.reshape(m, D_MODEL)
        if i % 2 == 0:
            y_a, z = _na_hyena_mixer(x, ab_w_in[j].astype(_BF), na_rpb[j], hy_conv_w[j], hy_conv_b[j], hy_f_w1[j],
                                     hy_f_b1[j], hy_f_freq[j], hy_f_w2[j], hy_f_b2[j], hy_f_w3[j], hy_skip[j])
            x1, x1b, aff = _outproj_ln([y_a, z], ab_w_out[j].astype(_BF), x2d, ln1_g[i], ln1_b[i], moe_router[i])
        else:
            q, kv, kr = _mla_pre(x2d, mla_w_in[j], mla_q_norm[j], mla_w_q_up[j], mla_kv_norm[j], mla_w_kv_up[j])
            att = _mla_attention(q.reshape(b, SEQ, -1), kv.reshape(b, SEQ, -1), kr.reshape(b, SEQ, -1))
            x1, x1b, aff = _outproj_ln([att.reshape(m, MLA_HEADS * MLA_V)], mla_w_out[j].astype(_BF), x2d,
                                       ln1_g[i], ln1_b[i], moe_router[i])
        x = _moe_ple_block(x1.reshape(b, SEQ, D_MODEL), x1b.reshape(b, SEQ, D_MODEL), aff,
                           moe_w_gate, moe_w_up, moe_w_down, ln2_g[i], ln2_b[i], p, ple_gate_b, ple_proj_b, i)
    return x
```

```python
import functools
import math

import numpy as np
import jax
import jax.numpy as jnp
from jax import lax
from jax.experimental import pallas as pl
from jax.experimental.pallas import tpu as pltpu

D_MODEL = 1024
BATCH = 8
SEQ = 2048
DEPTH = 2
GRID_W = 64
PLE_DIM = 256
NA_HEADS = 8
NA_HEAD_DIM = 64
NA_WIDTH = NA_HEADS * NA_HEAD_DIM
NA_WIN_ROWS_MAX = 8
NA_WIN_COLS = 16
HY_WIDTH = D_MODEL - NA_WIDTH
HY_ORDER = 2
HY_SHORT_K = 3
HY_EMB_DIM = 33
HY_FILTER_HIDDEN = 64
HY_FAST_DECAY_PCT = 0.3
HY_SLOW_DECAY_PCT = 1.5
HY_DECAY_TARGET = 1e-2
AB_IN_WIDTH = 3 * NA_WIDTH + (HY_ORDER + 1) * HY_WIDTH
MLA_HEADS = 16
MLA_Q_LORA = 384
MLA_KV_LORA = 256
MLA_NOPE = 64
MLA_ROPE = 32
MLA_V = 64
ROPE_THETA = 10000.0
N_EXPERTS = 16
EC_CAPACITY_FACTOR = 2
D_FF_EXPERT = 2048
DN_ALPHA = (2 * DEPTH) ** 0.25
NORM_EPS = 1e-5
NEG_INF = -1e30

CAP = EC_CAPACITY_FACTOR * SEQ // N_EXPERTS
NA_ROWS = SEQ // GRID_W
NA_WIN_ROWS = min(NA_WIN_ROWS_MAX, NA_ROWS)
NA_SLAB = NA_WIN_ROWS * GRID_W
FFT_N = 2 * SEQ
LOG2E = math.log2(math.e)

LANE = 128
MIB = 1 << 20

_BF = jnp.bfloat16
_F32 = jnp.float32
_HP = lax.Precision.HIGHEST


def _cp(sem, vmem_mib):
    return pltpu.CompilerParams(dimension_semantics=sem, vmem_limit_bytes=vmem_mib * MIB)


def _dot(a, b):
    return jnp.dot(a.astype(_BF), b.astype(_BF), preferred_element_type=_F32)


def _dot_nt(a, b):
    return lax.dot_general(a.astype(_BF), b.astype(_BF), (((1,), (1,)), ((), ())),
                           preferred_element_type=_F32)


def _layer_norm(y, g, b):
    mu = jnp.mean(y, axis=-1, keepdims=True)
    d = y - mu
    var = jnp.mean(d * d, axis=-1, keepdims=True)
    return d * lax.rsqrt(var + NORM_EPS) * g + b


INPROJ_TM = 512
HALO = 8


def _inproj_kernel(x_ref, xp_ref, xn_ref, w_ref, cw_ref, cb_ref, qkv_ref, hbc_ref, vb_ref):
    tile_in_seq = pl.program_id(0) % (SEQ // INPROJ_TM)
    x_ext = jnp.concatenate([xp_ref[...], x_ref[...], xn_ref[...]], axis=0)
    h = _dot(x_ext, w_ref[...])
    qkv_ref[...] = h[HALO:HALO + INPROJ_TM, :3 * NA_WIDTH].astype(_BF)
    hb = h[:, 3 * NA_WIDTH:]
    n = hb.shape[0]
    row = lax.broadcasted_iota(jnp.int32, hb.shape, 0)
    pad_row = jnp.logical_or(jnp.logical_and(row == HALO - 1, tile_in_seq == 0),
                             jnp.logical_and(row == HALO + INPROJ_TM, tile_in_seq == SEQ // INPROJ_TM - 1))
    hb = jnp.where(pad_row, 0.0, hb)
    y = (cb_ref[...] + pltpu.roll(hb, 1, axis=0) * cw_ref[0:1, :] + hb * cw_ref[1:2, :]
         + pltpu.roll(hb, n - 1, axis=0) * cw_ref[2:3, :])[HALO:HALO + INPROJ_TM]
    hbc_ref[...] = y
    vb_ref[...] = y[:, :HY_WIDTH].astype(_BF)


def _inproj_conv(x2d, w, conv_w, conv_b):
    m = x2d.shape[0]
    tm = INPROJ_TM
    nq, nh = 3 * NA_WIDTH, (HY_ORDER + 1) * HY_WIDTH
    per_tile = tm // HALO
    last = m // HALO - 1
    return pl.pallas_call(
        _inproj_kernel, grid=(m // tm,),
        in_specs=[pl.BlockSpec((tm, D_MODEL), lambda i: (i, 0)),
                  pl.BlockSpec((HALO, D_MODEL), lambda i: (jnp.maximum(i * per_tile - 1, 0), 0)),
                  pl.BlockSpec((HALO, D_MODEL), lambda i: (jnp.minimum((i + 1) * per_tile, last), 0)),
                  pl.BlockSpec((D_MODEL, AB_IN_WIDTH), lambda i: (0, 0)),
                  pl.BlockSpec((HY_SHORT_K, nh), lambda i: (0, 0)),
                  pl.BlockSpec((1, nh), lambda i: (0, 0))],
        out_specs=[pl.BlockSpec((tm, nq), lambda i: (i, 0)),
                   pl.BlockSpec((tm, nh), lambda i: (i, 0)),
                   pl.BlockSpec((tm, HY_WIDTH), lambda i: (i, 0))],
        out_shape=[jax.ShapeDtypeStruct((m, nq), _BF), jax.ShapeDtypeStruct((m, nh), _F32),
                   jax.ShapeDtypeStruct((m, HY_WIDTH), _BF)],
        compiler_params=_cp(("parallel",), 56), name="ab_inproj_conv")(
            x2d, x2d, x2d, w, conv_w, conv_b.reshape(1, nh))


NA_GROUP = 4
NA_GW = NA_GROUP * NA_HEAD_DIM


def _na_kernel(q_ref, k_ref, v_ref, tbl_ref, o_ref):
    lane_head = lax.broadcasted_iota(jnp.int32, (GRID_W, NA_GW), 1) >> int(math.log2(NA_HEAD_DIM))
    scale = NA_HEAD_DIM ** -0.5 * LOG2E

    def body(r, carry):
        r0 = jnp.clip(r - NA_WIN_ROWS // 2, 0, NA_ROWS - NA_WIN_ROWS)
        off = r0 - r + (NA_WIN_ROWS_MAX - 1)
        qr = q_ref[0, pl.ds(pl.multiple_of(r * GRID_W, GRID_W), GRID_W), :].astype(_F32)
        q4 = jnp.concatenate([jnp.where(lane_head == h, qr, 0.0) for h in range(NA_GROUP)], axis=0)
        ks = k_ref[0, pl.ds(pl.multiple_of(r0 * GRID_W, GRID_W), NA_SLAB), :]
        vs = v_ref[0, pl.ds(pl.multiple_of(r0 * GRID_W, GRID_W), NA_SLAB), :]
        bias = jnp.concatenate([tbl_ref[h, off] for h in range(NA_GROUP)], axis=0)
        s = _dot_nt(q4, ks) * scale + bias
        mx = jnp.max(s, axis=1, keepdims=True)
        p = jnp.exp2(s - mx)
        den = jnp.sum(p, axis=1, keepdims=True)
        o4 = _dot(p, vs) / den
        out = jnp.zeros((GRID_W, NA_GW), _F32)
        for h in range(NA_GROUP):
            out = out + jnp.where(lane_head == h, o4[h * GRID_W:(h + 1) * GRID_W], 0.0)
        o_ref[0, pl.ds(pl.multiple_of(r * GRID_W, GRID_W), GRID_W), :] = out.astype(_BF)
        return carry

    lax.fori_loop(0, NA_ROWS, body, 0, unroll=16)


def _na_bias_table(rpb):
    c = np.arange(GRID_W)
    c0 = np.clip(c - NA_WIN_COLS // 2, 0, GRID_W - NA_WIN_COLS)
    kc = np.arange(GRID_W)
    col_ok = (kc[None, :] >= c0[:, None]) & (kc[None, :] < c0[:, None] + NA_WIN_COLS)
    dc_idx = np.clip(kc[None, :] - c[:, None], -(NA_WIN_COLS - 1), NA_WIN_COLS - 1) + (NA_WIN_COLS - 1)
    pick = (dc_idx[None, :, :] == np.arange(2 * NA_WIN_COLS - 1)[:, None, None]).astype(np.float32)
    per_dr = jnp.einsum("hdk,kqc->hdqc", rpb.astype(_F32), pick, precision=_HP)
    per_dr = jnp.where(col_ok[None, None], per_dr * LOG2E, NEG_INF)
    slabs = jnp.stack([per_dr[:, off:off + NA_WIN_ROWS] for off in range(NA_WIN_ROWS_MAX)], axis=1)
    return slabs.transpose(0, 1, 3, 2, 4).reshape(NA_HEADS, NA_WIN_ROWS_MAX, GRID_W, NA_SLAB)


def _na_attention(qkv, rpb):
    b = qkv.shape[0]
    tbl = _na_bias_table(rpb)
    ng = NA_HEADS // NA_GROUP
    blk = lambda col0: pl.BlockSpec((1, SEQ, NA_GW), lambda i, g, c=col0: (i, 0, c + g))
    return pl.pallas_call(
        _na_kernel, grid=(b, ng),
        in_specs=[blk(0), blk(ng), blk(2 * ng),
                  pl.BlockSpec((NA_GROUP, NA_WIN_ROWS_MAX, GRID_W, NA_SLAB), lambda i, g: (g, 0, 0, 0))],
        out_specs=pl.BlockSpec((1, SEQ, NA_GW), lambda i, g: (i, 0, g)),
        out_shape=jax.ShapeDtypeStruct((b, SEQ, NA_WIDTH), _BF),
        compiler_params=_cp(("parallel", "parallel"), 32), name="na_attention")(qkv, qkv, qkv, tbl)


HY_HID_PAD = LANE
HY_FILT_TC = 256


def _filter_kernel(z_ref, w1_ref, b1_ref, fr_ref, w2_ref, b2_ref, w3f_ref, w3b_ref, dec_ref, k_ref, h_ref):
    @pl.when(jnp.logical_and(pl.program_id(0) == 0, pl.program_id(1) == 0))
    def _():
        fr = fr_ref[...]
        h1 = jnp.sin(fr * (jnp.dot(z_ref[...], w1_ref[...], precision=_HP, preferred_element_type=_F32) + b1_ref[...]))
        h_ref[...] = jnp.sin(fr * (jnp.dot(h1, w2_ref[...], precision=_HP, preferred_element_type=_F32) + b2_ref[...]))

    fwd = jnp.dot(h_ref[:SEQ, :], w3f_ref[...], precision=_HP, preferred_element_type=_F32)
    bwd = jnp.dot(h_ref[SEQ:, :], w3b_ref[...], precision=_HP, preferred_element_type=_F32)
    k = jnp.concatenate([fwd, bwd], axis=0) * dec_ref[...]
    ss = jnp.sum(k * k, axis=0, keepdims=True)
    k_ref[...] = (k * lax.rsqrt(ss + 1e-12)).astype(_BF)


@functools.lru_cache(maxsize=None)
def _filter_tables():
    bands = (HY_EMB_DIM - 1) // 2
    t = np.linspace(0.0, 1.0, SEQ)[:, None]
    w = 2.0 * math.pi * np.arange(SEQ)[:, None] / SEQ
    f = np.linspace(1e-4, bands - 1, bands)[None, :]
    z = np.concatenate([t, np.cos(f * w), -np.sin(f * w)], axis=-1)
    min_decay = math.log(HY_DECAY_TARGET) / HY_SLOW_DECAY_PCT
    max_decay = math.log(HY_DECAY_TARGET) / HY_FAST_DECAY_PCT
    deltas = np.abs(np.linspace(min_decay, max_decay, HY_WIDTH))
    dec = np.exp(-t * deltas)
    src = np.concatenate([np.arange(SEQ), [0], np.arange(SEQ - 1, 0, -1)])
    live = np.ones((FFT_N, 1))
    live[SEQ] = 0.0
    z2 = np.pad(z[src], ((0, 0), (0, HY_HID_PAD - HY_EMB_DIM))).astype(np.float32)
    dec2 = (dec[src] * live).astype(np.float32)
    return z2, dec2


def _hyena_filters(w1, b1, freq, w2, b2, w3):
    z2, dec2 = _filter_tables()
    hp = HY_HID_PAD - HY_FILTER_HIDDEN
    w1p = jnp.pad(w1, ((0, HY_HID_PAD - HY_EMB_DIM), (0, hp)))
    w2p = jnp.pad(w2, ((0, hp), (0, hp)))
    w3p = jnp.pad(w3, ((0, hp), (0, 0)))
    row = lambda v: jnp.pad(v, (0, hp)).reshape(1, HY_HID_PAD)
    nc = HY_WIDTH // HY_FILT_TC
    per_order = 2 * nc
    full = lambda shape: pl.BlockSpec(shape, lambda o, j: (0, 0))
    return pl.pallas_call(
        _filter_kernel, grid=(HY_ORDER, nc),
        in_specs=[full((FFT_N, HY_HID_PAD)), full((HY_HID_PAD, HY_HID_PAD)), full((1, HY_HID_PAD)),
                  full((1, HY_HID_PAD)), full((HY_HID_PAD, HY_HID_PAD)), full((1, HY_HID_PAD)),
                  pl.BlockSpec((HY_HID_PAD, HY_FILT_TC), lambda o, j: (0, o * per_order + j)),
                  pl.BlockSpec((HY_HID_PAD, HY_FILT_TC), lambda o, j: (0, o * per_order + nc + j)),
                  pl.BlockSpec((FFT_N, HY_FILT_TC), lambda o, j: (0, j))],
        out_specs=pl.BlockSpec((FFT_N, HY_FILT_TC), lambda o, j: (0, o * nc + j)),
        out_shape=jax.ShapeDtypeStruct((FFT_N, HY_ORDER * HY_WIDTH), _BF),
        scratch_shapes=[pltpu.VMEM((FFT_N, HY_HID_PAD), _F32)],
        compiler_params=_cp(("arbitrary", "arbitrary"), 40), name="hy_filters")(
            z2, w1p, row(b1), row(freq), w2p, row(b2), w3p, w3p, dec2)


@functools.lru_cache(maxsize=None)
def _dft_matrices():
    t = np.arange(SEQ)
    ang = ((t[:, None] * t[None, :]) % FFT_N) * (2.0 * math.pi / FFT_N)
    re = np.cos(ang)
    im = -np.sin(ang)
    im[0] = 1.0 - 2.0 * (t % 2)
    nf = SEQ // HY_FB
    packed = np.stack([re.reshape(nf, HY_FB, SEQ), im.reshape(nf, HY_FB, SEQ)], axis=1).reshape(FFT_N, SEQ)
    return packed.astype(np.float32), np.ascontiguousarray(packed.T).astype(np.float32)


def _kf_kernel(f_ref, k_ref, o_ref):
    i = pl.program_id(0)
    f = f_ref[...]
    tm = f.shape[0]
    p1 = jnp.dot(f, k_ref[:SEQ, :], preferred_element_type=_F32)
    p2 = jnp.dot(f, k_ref[SEQ:, :], preferred_element_type=_F32)
    row = lax.broadcasted_iota(jnp.int32, (tm, 1), 0) + i * tm
    sign = (1 - 2 * (row & 1)).astype(_F32)
    o_ref[...] = p1 + sign * p2


def _filter_spectrum(fmat, kfilt):
    tm = 512
    nw = kfilt.shape[1]
    return pl.pallas_call(
        _kf_kernel, grid=(FFT_N // tm,),
        in_specs=[pl.BlockSpec((tm, SEQ), lambda i: (i, 0)),
                  pl.BlockSpec((FFT_N, nw), lambda i: (0, 0))],
        out_specs=pl.BlockSpec((tm, nw), lambda i: (i, 0)),
        out_shape=jax.ShapeDtypeStruct((FFT_N, nw), _F32),
        compiler_params=_cp(("parallel",), 40), name="hy_filter_spectrum")(fmat, kfilt)


HY_FB = 512
HY_NB = 2


HY_TM = 512


def _hconv_fwd_kernel(zb_ref, f_ref, k_ref, y_ref):
    fk = pl.program_id(1)
    kr = k_ref[:HY_FB, :]
    ki = k_ref[HY_FB:, :]
    row0 = jnp.logical_and(lax.broadcasted_iota(jnp.int32, kr.shape, 0) == 0, fk == 0)
    sc = jnp.where(row0, 1.0 / FFT_N, 2.0 / FFT_N)
    for j in range(HY_NB):
        u = jnp.dot(f_ref[...], zb_ref[j], preferred_element_type=_F32)
        ur = u[:HY_FB]
        ui = u[HY_FB:]
        yr = jnp.where(row0, ur * kr, ur * kr - ui * ki)
        yi = jnp.where(row0, ui * ki, ur * ki + ui * kr)
        y_ref[j, 0, 0] = (yr * sc).astype(_BF)
        y_ref[j, 0, 1] = (yi * sc).astype(_BF)


def _hconv_inv_kernel(ft_ref, y_ref, z_ref, skip_ref, xn_ref, o_ref, ob_ref):
    for j in range(HY_NB):
        conv = jnp.dot(ft_ref[...], y_ref[j], preferred_element_type=_F32)
        out = xn_ref[j] * (conv + z_ref[j] * skip_ref[...])
        o_ref[j] = out
        ob_ref[j] = out.astype(_BF)


def _long_conv_gate(hbc, zsrc, zb, xn_col, order, fmat, fmat_t, kf, skip):
    b = hbc.shape[0]
    nf = SEQ // HY_FB
    w = HY_WIDTH
    nb = HY_NB
    y = pl.pallas_call(
        _hconv_fwd_kernel, grid=(b // nb, nf),
        in_specs=[pl.BlockSpec((nb, SEQ, w), lambda i, f: (i, 0, 0)),
                  pl.BlockSpec((2 * HY_FB, SEQ), lambda i, f: (f, 0)),
                  pl.BlockSpec((2 * HY_FB, w), lambda i, f: (f, order))],
        out_specs=pl.BlockSpec((nb, 1, 2, HY_FB, w), lambda i, f: (i, f, 0, 0, 0)),
        out_shape=jax.ShapeDtypeStruct((b, nf, 2, HY_FB, w), _BF),
        compiler_params=_cp(("parallel", "parallel"), 40), name=f"hy_conv_fwd{order}")(zb, fmat, kf)
    blk = lambda col: pl.BlockSpec((nb, HY_TM, w), lambda i, m, c=col: (i, m, c))
    return pl.pallas_call(
        _hconv_inv_kernel, grid=(b // nb, SEQ // HY_TM),
        in_specs=[pl.BlockSpec((HY_TM, FFT_N), lambda i, m: (m, 0)),
                  pl.BlockSpec((nb, FFT_N, w), lambda i, m: (i, 0, 0)),
                  blk(0),
                  pl.BlockSpec((1, w), lambda i, m: (0, 0)),
                  blk(xn_col)],
        out_specs=[blk(0), blk(0)],
        out_shape=[jax.ShapeDtypeStruct((b, SEQ, w), _F32), jax.ShapeDtypeStruct((b, SEQ, w), _BF)],
        compiler_params=_cp(("parallel", "parallel"), 48), name=f"hy_conv_inv{order}")(
            fmat_t, y.reshape(b, FFT_N, w), zsrc, skip[order].reshape(1, w), hbc)


def _outproj_ln_kernel(n_a, *refs):
    a_refs = refs[:n_a]
    w_refs = refs[n_a:2 * n_a]
    x_ref, g_ref, b_ref, wr_ref, o_ref, ob_ref, aff_ref = refs[2 * n_a:]
    m = _dot(a_refs[0][...], w_refs[0][...])
    for a_ref, w_ref in zip(a_refs[1:], w_refs[1:]):
        m = m + _dot(a_ref[...], w_ref[...])
    y = _layer_norm(DN_ALPHA * x_ref[...] + m, g_ref[...], b_ref[...])
    o_ref[...] = y
    ob_ref[...] = y.astype(_BF)
    wr = wr_ref[...]
    wr_hi = wr.astype(_BF)
    wr_lo = (wr - wr_hi.astype(_F32)).astype(_BF)
    y_hi = y.astype(_BF)
    y_lo = (y - y_hi.astype(_F32)).astype(_BF)
    by_hi = _dot_nt(jnp.concatenate([wr_hi, wr_lo], axis=0), y_hi)
    logits = by_hi[:N_EXPERTS] + by_hi[N_EXPERTS:] + _dot_nt(wr_hi, y_lo)
    ex = jnp.exp(logits - jnp.max(logits, axis=0, keepdims=True))
    aff_ref[0] = ex / jnp.sum(ex, axis=0, keepdims=True)


def _outproj_ln(a_list, w, x2d, g, b, w_router):
    m = x2d.shape[0]
    tm = 1024
    per_seq = SEQ // tm
    in_specs, w_args, row0 = [], [], 0
    for a in a_list:
        in_specs.append(pl.BlockSpec((tm, a.shape[1]), lambda i: (i, 0)))
    for a in a_list:
        ka = a.shape[1]
        in_specs.append(pl.BlockSpec((ka, D_MODEL), lambda i, r=row0 // ka: (r, 0)))
        w_args.append(w)
        row0 += ka
    in_specs += [pl.BlockSpec((tm, D_MODEL), lambda i: (i, 0)),
                 pl.BlockSpec((1, D_MODEL), lambda i: (0, 0)),
                 pl.BlockSpec((1, D_MODEL), lambda i: (0, 0)),
                 pl.BlockSpec((N_EXPERTS, D_MODEL), lambda i: (0, 0))]
    return pl.pallas_call(
        functools.partial(_outproj_ln_kernel, len(a_list)), grid=(m // tm,),
        in_specs=in_specs,
        out_specs=[pl.BlockSpec((tm, D_MODEL), lambda i: (i, 0)),
                   pl.BlockSpec((tm, D_MODEL), lambda i: (i, 0)),
                   pl.BlockSpec((1, N_EXPERTS, tm), lambda i: (i // per_seq, 0, i % per_seq))],
        out_shape=[jax.ShapeDtypeStruct((m, D_MODEL), _F32), jax.ShapeDtypeStruct((m, D_MODEL), _BF),
                   jax.ShapeDtypeStruct((m // SEQ, N_EXPERTS, SEQ), _F32)],
        compiler_params=_cp(("parallel",), 40), name="outproj_ln_router")(
            *a_list, *w_args, x2d, g.reshape(1, -1), b.reshape(1, -1), w_router.T)


BISECT_STEPS = 160


def _select_kernel(a_ref, tri_ref, edge_ref, slot_ref, slot_t_ref, cnt_ref):
    a = a_ref[...]
    rows = a.shape[0]
    cap = float(CAP)

    def body(_, c):
        lo, hi = c
        mid = 0.5 * (lo + hi)
        cnt = jnp.sum(jnp.where(a > mid, 1.0, 0.0), axis=1, keepdims=True)
        ge = cnt >= cap
        return jnp.where(ge, mid, lo), jnp.where(ge, hi, mid)

    lo, hi = lax.fori_loop(0, BISECT_STEPS, body,
                           (jnp.full((rows, 1), -1.0, _F32), jnp.full((rows, 1), 1.0, _F32)))
    vstar = jnp.max(jnp.where(a <= hi, a, -1.0), axis=1, keepdims=True)
    gt = a > vstar
    eq = a == vstar
    need = cap - jnp.sum(jnp.where(gt, 1.0, 0.0), axis=1, keepdims=True)
    tri = tri_ref[...]
    eq_before = jnp.dot(jnp.where(eq, 1.0, 0.0).astype(_BF), tri, preferred_element_type=_F32)
    sel = jnp.where(gt, 1.0, jnp.where(eq, jnp.where(eq_before < need, 1.0, 0.0), 0.0))
    pos = jnp.dot(sel.astype(_BF), tri, preferred_element_type=_F32)
    slot = jnp.where(sel > 0.5, pos, -1.0)
    slot_ref[...] = slot.astype(jnp.int32)
    cnt_ref[...] = jnp.dot(sel.astype(_BF), edge_ref[...], preferred_element_type=_F32).astype(jnp.int32)
    if rows < LANE:
        slot = jnp.concatenate([slot, jnp.full((LANE - rows, SEQ), -1.0, _F32)], axis=0)
    slot_tm = slot.T
    for b in range(rows // N_EXPERTS):
        shifted = slot_tm if b == 0 else pltpu.roll(slot_tm, LANE - N_EXPERTS * b, axis=1)
        slot_t_ref[b] = shifted.astype(jnp.int32)


@functools.lru_cache(maxsize=None)
def _prefix_matrix():
    idx = np.arange(SEQ)
    return (idx[:, None] < idx[None, :]).astype(_BF)


WIN_TT = 256
SCAT_TT = 512
WIN_W = 128
WIN_EDGES = 16


@functools.lru_cache(maxsize=None)
def _edge_matrix():
    return (np.arange(SEQ)[:, None] < np.arange(LANE)[None, :] * WIN_TT).astype(_BF)


def _select(aff2d):
    rows = aff2d.shape[0]
    assert rows <= LANE and rows % N_EXPERTS == 0
    bsz = rows // N_EXPERTS
    slot, slot_t, cnt = pl.pallas_call(
        _select_kernel, grid=(1,),
        in_specs=[pl.BlockSpec((rows, SEQ), lambda i: (0, 0)),
                  pl.BlockSpec((SEQ, SEQ), lambda i: (0, 0)),
                  pl.BlockSpec((SEQ, LANE), lambda i: (0, 0))],
        out_specs=[pl.BlockSpec((rows, SEQ), lambda i: (0, 0)),
                   pl.BlockSpec((bsz, SEQ, LANE), lambda i: (0, 0, 0)),
                   pl.BlockSpec((rows, LANE), lambda i: (0, 0))],
        out_shape=[jax.ShapeDtypeStruct((rows, SEQ), jnp.int32),
                   jax.ShapeDtypeStruct((bsz, SEQ, LANE), jnp.int32),
                   jax.ShapeDtypeStruct((rows, LANE), jnp.int32)],
        compiler_params=_cp(("arbitrary",), 48), name="moe_select")(aff2d, _prefix_matrix(), _edge_matrix())
    return slot, slot_t, cnt[:, :WIN_EDGES].reshape(rows * WIN_EDGES)


def _onehot(slot_row):
    return slot_row == lax.broadcasted_iota(jnp.int32, (CAP, SEQ), 0)


GATHER_NE = 4
GATHER_TC = 512
GATHER_W = 64


def _window_start(cnt_ref, seq, e, tile, width, span=1):
    base = (seq * N_EXPERTS + e) * WIN_EDGES + tile * span
    start = jnp.minimum((cnt_ref[base] >> 4) << 4, CAP - width)
    return start, cnt_ref[base + span] - start <= width


def _gather_kernel(cnt_ref, slot_ref, a_ref, xb_ref, xe_ref, g_ref):
    i = pl.program_id(0)
    n_tiles = SEQ // WIN_TT
    wd = GATHER_W
    starts, fits = {}, None
    for t in range(n_tiles):
        for e in range(N_EXPERTS):
            starts[e, t], ok = _window_start(cnt_ref, i, e, t, wd)
            fits = ok if fits is None else jnp.logical_and(fits, ok)

    @pl.when(fits)
    def _():
        xe_ref[...] = jnp.zeros(xe_ref.shape, _BF)
        g_ref[...] = jnp.zeros(g_ref.shape, _F32)
        sub_w = lax.broadcasted_iota(jnp.int32, (wd, WIN_TT), 0)
        for t in range(n_tiles):
            toks = slice(t * WIN_TT, (t + 1) * WIN_TT)
            ps = []
            for e in range(N_EXPERTS):
                hit = slot_ref[0, e, :, toks] - starts[e, t] == sub_w
                ps.append(jnp.where(hit, 1.0, 0.0).astype(_BF))
                rows = pl.ds(pl.multiple_of(starts[e, t], 16), wd)
                g_ref[e, rows, :] += jnp.sum(jnp.where(hit, a_ref[0, e, :, toks], 0.0), axis=1, keepdims=True)
            pcat = jnp.concatenate(ps, axis=0)
            for c in range(D_MODEL // GATHER_TC):
                cols = slice(c * GATHER_TC, (c + 1) * GATHER_TC)
                res = jnp.dot(pcat, xb_ref[0, toks, cols], preferred_element_type=_F32).astype(_BF)
                for e in range(N_EXPERTS):
                    rows = pl.ds(pl.multiple_of(starts[e, t], 16), wd)
                    xe_ref[e, rows, cols] += res[e * wd:(e + 1) * wd]

    @pl.when(jnp.logical_not(fits))
    def _():
        for grp in range(N_EXPERTS // GATHER_NE):
            es = range(grp * GATHER_NE, (grp + 1) * GATHER_NE)
            hits = [_onehot(slot_ref[0, e]) for e in es]
            p = jnp.concatenate([jnp.where(h, 1.0, 0.0).astype(_BF) for h in hits], axis=0)
            xe = jnp.dot(p, xb_ref[0], preferred_element_type=_F32).astype(_BF)
            xe_ref[grp * GATHER_NE:(grp + 1) * GATHER_NE] = xe.reshape(GATHER_NE, CAP, D_MODEL)
            for k, e in enumerate(es):
                g_ref[e] = jnp.sum(jnp.where(hits[k], a_ref[0, e], 0.0), axis=1, keepdims=True)


def _gather(cnt, slot4, aff4, xb3d):
    b = xb3d.shape[0]
    row = pl.BlockSpec((1, N_EXPERTS, 1, SEQ), lambda i, c: (i, 0, 0, 0))
    grid_spec = pltpu.PrefetchScalarGridSpec(
        num_scalar_prefetch=1, grid=(b,),
        in_specs=[row, row, pl.BlockSpec((1, SEQ, D_MODEL), lambda i, c: (i, 0, 0))],
        out_specs=[pl.BlockSpec((N_EXPERTS, CAP, D_MODEL), lambda i, c: (0, i, 0)),
                   pl.BlockSpec((N_EXPERTS, CAP, 1), lambda i, c: (0, i, 0))])
    return pl.pallas_call(
        _gather_kernel, grid_spec=grid_spec,
        out_shape=[jax.ShapeDtypeStruct((N_EXPERTS, b * CAP, D_MODEL), _BF),
                   jax.ShapeDtypeStruct((N_EXPERTS, b * CAP, 1), _F32)],
        compiler_params=_cp(("parallel",), 56), name="moe_gather")(cnt, slot4, aff4, xb3d)


FFN_TF = 512
FFN_TM = 512


def _ffn_up_kernel(xe_ref, wg_ref, wu_ref, h_ref):
    wg = wg_ref[0].astype(_BF)
    wu = wu_ref[0].astype(_BF)
    for c in range(xe_ref.shape[1] // FFN_TM):
        rows = slice(c * FFN_TM, (c + 1) * FFN_TM)
        xe = xe_ref[0, rows, :]
        hg = _dot(xe, wg)
        hu = _dot(xe, wu)
        h_ref[0, rows, :] = ((hg * (1.0 / (1.0 + jnp.exp(-hg)))) * hu).astype(_BF)


def _ffn_down_kernel(h_ref, wd_ref, g_ref, ye_ref):
    wd = wd_ref[0].astype(_BF)
    for c in range(h_ref.shape[1] // FFN_TM):
        rows = slice(c * FFN_TM, (c + 1) * FFN_TM)
        ye_ref[0, rows, :] = (_dot(h_ref[0, rows, :], wd) * g_ref[0, rows, :]).astype(_BF)


def _expert_ffn(xe, gates, wg, wu, wd, layer):
    rows = xe.shape[1]
    h = pl.pallas_call(
        _ffn_up_kernel, grid=(N_EXPERTS, D_FF_EXPERT // FFN_TF),
        in_specs=[pl.BlockSpec((1, rows, D_MODEL), lambda e, f: (e, 0, 0)),
                  pl.BlockSpec((None, 1, D_MODEL, FFN_TF), lambda e, f: (layer, e, 0, f)),
                  pl.BlockSpec((None, 1, D_MODEL, FFN_TF), lambda e, f: (layer, e, 0, f))],
        out_specs=pl.BlockSpec((1, rows, FFN_TF), lambda e, f: (e, 0, f)),
        out_shape=jax.ShapeDtypeStruct((N_EXPERTS, rows, D_FF_EXPERT), _BF),
        compiler_params=_cp(("parallel", "parallel"), 40), name="moe_ffn_up")(xe, wg, wu)
    return pl.pallas_call(
        _ffn_down_kernel, grid=(N_EXPERTS,),
        in_specs=[pl.BlockSpec((1, rows, D_FF_EXPERT), lambda e: (e, 0, 0)),
                  pl.BlockSpec((None, 1, D_FF_EXPERT, D_MODEL), lambda e: (layer, e, 0, 0)),
                  pl.BlockSpec((1, rows, 1), lambda e: (e, 0, 0))],
        out_specs=pl.BlockSpec((1, rows, D_MODEL), lambda e: (e, 0, 0)),
        out_shape=jax.ShapeDtypeStruct((N_EXPERTS, rows, D_MODEL), _BF),
        compiler_params=_cp(("parallel",), 56), name="moe_ffn_down")(h, wd, gates)


def _scatter_ln_ple_kernel(cnt_ref, slot_t_ref, ye_ref, x_ref, g_ref, b_ref, p_ref, wg_ref, wp_ref, o_ref,
                           pt_ref, yw_ref, f_ref):
    i = pl.program_id(0)
    r = pl.program_id(1)
    slot_t = slot_t_ref[0]
    starts = []
    fits = None
    for e in range(N_EXPERTS):
        start, ok = _window_start(cnt_ref, i, e, r, WIN_W, SCAT_TT // WIN_TT)
        starts.append(start)
        fits = ok if fits is None else jnp.logical_and(fits, ok)

    @pl.when(fits)
    def _():
        lane_w = lax.broadcasted_iota(jnp.int32, (SCAT_TT, WIN_W), 1)
        for e in range(N_EXPERTS):
            cols = slice(e * WIN_W, (e + 1) * WIN_W)
            pt_ref[:, cols] = jnp.where(slot_t[:, e:e + 1] - starts[e] == lane_w, 1.0, 0.0).astype(_BF)
            yw_ref[cols, :] = ye_ref[e, pl.ds(pl.multiple_of(starts[e], 16), WIN_W), :]
        f_ref[...] = jnp.dot(pt_ref[:, :N_EXPERTS * WIN_W], yw_ref[...], preferred_element_type=_F32)

    @pl.when(jnp.logical_not(fits))
    def _():
        lane_c = lax.broadcasted_iota(jnp.int32, (SCAT_TT, CAP), 1)
        for e in range(N_EXPERTS):
            pt_ref[:, e * CAP:(e + 1) * CAP] = jnp.where(slot_t[:, e:e + 1] == lane_c, 1.0, 0.0).astype(_BF)
        f_ref[...] = jnp.dot(pt_ref[...], ye_ref[...].reshape(N_EXPERTS * CAP, D_MODEL),
                             preferred_element_type=_F32)

    y = _layer_norm(DN_ALPHA * x_ref[0] + f_ref[...], g_ref[...], b_ref[...])
    gate = _dot(y, wg_ref[...])
    emb = _dot(p_ref[0], wp_ref[...])
    o_ref[0] = y + (1.0 / (1.0 + jnp.exp(-gate))) * emb


def _scatter_ln_ple(cnt, slot_t, ye, x3d, g, b, p, ple_gate, ple_proj, layer):
    bsz = x3d.shape[0]
    vec = pl.BlockSpec((1, D_MODEL), lambda i, r, c: (0, 0))
    seq = pl.BlockSpec((1, SCAT_TT, D_MODEL), lambda i, r, c: (i, r, 0))
    grid_spec = pltpu.PrefetchScalarGridSpec(
        num_scalar_prefetch=1, grid=(bsz, SEQ // SCAT_TT),
        in_specs=[pl.BlockSpec((1, SCAT_TT, LANE), lambda i, r, c: (i, r, 0)),
                  pl.BlockSpec((N_EXPERTS, CAP, D_MODEL), lambda i, r, c: (0, i, 0)),
                  seq, vec, vec,
                  pl.BlockSpec((None, 1, SCAT_TT, PLE_DIM), lambda i, r, c: (layer, i, r, 0)),
                  pl.BlockSpec((None, D_MODEL, D_MODEL), lambda i, r, c: (layer, 0, 0)),
                  pl.BlockSpec((None, PLE_DIM, D_MODEL), lambda i, r, c: (layer, 0, 0))],
        out_specs=seq,
        scratch_shapes=[pltpu.VMEM((SCAT_TT, N_EXPERTS * CAP), _BF),
                        pltpu.VMEM((N_EXPERTS * WIN_W, D_MODEL), _BF),
                        pltpu.VMEM((SCAT_TT, D_MODEL), _F32)])
    return pl.pallas_call(
        _scatter_ln_ple_kernel, grid_spec=grid_spec,
        out_shape=jax.ShapeDtypeStruct((bsz, SEQ, D_MODEL), _F32),
        compiler_params=_cp(("parallel", "arbitrary"), 56), name="moe_scatter_ln_ple")(
            cnt, slot_t, ye, x3d, g.reshape(1, -1), b.reshape(1, -1), p, ple_gate, ple_proj)


def _moe_ple_block(x1, x1b, aff, wg, wu, wd, g, b, p, ple_gate, ple_proj, layer):
    bsz = x1.shape[0]
    slot, slot_t, cnt = _select(aff.reshape(bsz * N_EXPERTS, SEQ))
    slot4 = slot.reshape(bsz, N_EXPERTS, 1, SEQ)
    aff4 = aff.reshape(bsz, N_EXPERTS, 1, SEQ)
    xe, gates = _gather(cnt, slot4, aff4, x1b)
    ye = _expert_ffn(xe, gates, wg, wu, wd, layer)
    return _scatter_ln_ple(cnt, slot_t, ye, x1, g, b, p, ple_gate, ple_proj, layer)


MLA_HB = LANE
MLA_IN_PAD = 768
MLA_KR_COL = MLA_Q_LORA + MLA_KV_LORA


def _rms(x, g):
    return x * lax.rsqrt(jnp.mean(x * x, axis=-1, keepdims=True) + NORM_EPS) * g


def _mla_pre_kernel(x_ref, win_ref, qg_ref, wq_ref, kvg_ref, wkv_ref, qa_ref, qb_ref, ka_ref, kb_ref,
                    q_ref, kv_ref, kr_ref):
    h = _dot(x_ref[...], win_ref[...])
    q = _dot(_rms(h[:, :MLA_Q_LORA], qg_ref[...]), wq_ref[...])
    swap = pltpu.roll(q, q.shape[1] - MLA_ROPE, axis=1)
    qa = jnp.concatenate([qa_ref[...]] * MLA_HEADS, axis=1)
    qb = jnp.concatenate([qb_ref[...]] * MLA_HEADS, axis=1)
    q_ref[...] = (q * qa + swap * qb).astype(_BF)
    kv_ref[...] = _dot(_rms(h[:, MLA_Q_LORA:MLA_KR_COL], kvg_ref[...]), wkv_ref[...]).astype(_BF)
    kr = h[:, MLA_KR_COL:MLA_KR_COL + MLA_HB]
    kr_ref[...] = (kr * ka_ref[...] + pltpu.roll(kr, MLA_HB - MLA_ROPE, axis=1) * kb_ref[...]).astype(_BF)


@functools.lru_cache(maxsize=None)
def _rope_lane_tables():
    inv = 1.0 / (ROPE_THETA ** (np.arange(0, MLA_ROPE, 2) / MLA_ROPE))
    ang = np.arange(SEQ)[:, None] * inv[None, :]
    cos, sin = np.cos(ang), np.sin(ang)
    ones = np.ones((SEQ, MLA_NOPE))
    zeros_n = np.zeros((SEQ, MLA_NOPE))
    zeros_r = np.zeros((SEQ, MLA_ROPE))
    cc = np.concatenate([cos, cos], axis=1)
    ss = np.concatenate([-sin, sin], axis=1)
    keep_q = np.concatenate([ones, cc, zeros_r], axis=1)
    keep_k = np.concatenate([zeros_n, cc, zeros_r], axis=1)
    swp = np.concatenate([zeros_n, ss, zeros_r], axis=1)
    qscale = (MLA_NOPE + MLA_ROPE) ** -0.5 * LOG2E
    f32 = lambda a: a.astype(np.float32)
    return f32(keep_q * qscale), f32(swp * qscale), f32(keep_k), f32(swp)


def _mla_pre(x2d, w_in, q_norm, w_q_up, kv_norm, w_kv_up):
    m = x2d.shape[0]
    tm = 512
    half = MLA_ROPE // 2
    kr_w = w_in[:, MLA_KR_COL:]
    kr_swapped = jnp.concatenate([kr_w[:, half:], kr_w[:, :half]], axis=1)
    win = jnp.concatenate([w_in[:, :MLA_KR_COL], jnp.zeros((D_MODEL, MLA_NOPE), _F32), kr_w, kr_swapped],
                          axis=1).astype(_BF)
    wq = w_q_up.reshape(MLA_Q_LORA, MLA_HEADS, MLA_NOPE + MLA_ROPE)
    rope_w = wq[:, :, MLA_NOPE:]
    wq = jnp.concatenate([wq, rope_w[:, :, half:], rope_w[:, :, :half]], axis=2).reshape(
        MLA_Q_LORA, MLA_HEADS * MLA_HB).astype(_BF)
    w_kv_up = w_kv_up.astype(_BF)
    q_keep, q_swap, k_keep, k_swap = _rope_lane_tables()
    nq = MLA_HEADS * MLA_HB
    pos = lambda: pl.BlockSpec((tm, MLA_HB), lambda i: (i % (SEQ // tm), 0))
    full = lambda a: pl.BlockSpec(a.shape, lambda i: (0,) * a.ndim)
    qg, kvg = q_norm.reshape(1, -1), kv_norm.reshape(1, -1)
    return pl.pallas_call(
        _mla_pre_kernel, grid=(m // tm,),
        in_specs=[pl.BlockSpec((tm, D_MODEL), lambda i: (i, 0)), full(win), full(qg), full(wq), full(kvg),
                  full(w_kv_up), pos(), pos(), pos(), pos()],
        out_specs=[pl.BlockSpec((tm, nq), lambda i: (i, 0)),
                   pl.BlockSpec((tm, nq), lambda i: (i, 0)),
                   pl.BlockSpec((tm, MLA_HB), lambda i: (i, 0))],
        out_shape=[jax.ShapeDtypeStruct((m, nq), _BF), jax.ShapeDtypeStruct((m, nq), _BF),
                   jax.ShapeDtypeStruct((m, MLA_HB), _BF)],
        compiler_params=_cp(("parallel",), 48), name="mla_pre")(
            x2d, win, qg, wq, kvg, w_kv_up, q_keep, q_swap, k_keep, k_swap)


MLA_TQ = 256
MLA_PAIR = 2


def _mla_attn_kernel(q_ref, kv_ref, kr_ref, o_ref, kt_scr, v_scr, s_scr):
    lane = lax.broadcasted_iota(jnp.int32, (SEQ, MLA_HB), 1)
    kr = kr_ref[0].astype(_F32)
    for j in range(MLA_PAIR):
        kvh = kv_ref[0, :, j * MLA_HB:(j + 1) * MLA_HB].astype(_F32)
        kt_scr[j] = jnp.where(lane < MLA_NOPE, kvh, kr).T.astype(_BF)
        v_scr[j] = jnp.where(lane < MLA_NOPE, 1.0, kvh).astype(_BF)
    lane_q = lax.broadcasted_iota(jnp.int32, (MLA_TQ, MLA_HB), 1)

    def block_rows(i):
        return pl.ds(pl.multiple_of(i * MLA_TQ, MLA_TQ), MLA_TQ)

    def scores(i, slot):
        for j in range(MLA_PAIR):
            s_scr[slot, j] = _dot(q_ref[0, block_rows(i), j * MLA_HB:(j + 1) * MLA_HB], kt_scr[j])

    def finish(i, slot):
        outs = []
        for j in range(MLA_PAIR):
            s = s_scr[slot, j]
            p = jnp.exp2(s - jnp.max(s, axis=1, keepdims=True))
            o = _dot(p, v_scr[j])
            outs.append(o / o[:, 0:1])
        even = pltpu.roll(outs[0], MLA_V, axis=1)
        o_ref[0, block_rows(i), :] = jnp.where(lane_q < MLA_V, even, outs[1]).astype(_BF)

    nblk = SEQ // MLA_TQ
    scores(0, 0)

    def body(k, carry):
        scores(2 * k + 1, 1)
        finish(2 * k, 0)
        scores(2 * k + 2, 0)
        finish(2 * k + 1, 1)
        return carry

    lax.fori_loop(0, nblk // 2 - 1, body, 0)
    scores(nblk - 1, 1)
    finish(nblk - 2, 0)
    finish(nblk - 1, 1)


def _mla_attention(q, kv, kr):
    b = q.shape[0]
    w = MLA_PAIR * MLA_HB
    return pl.pallas_call(
        _mla_attn_kernel, grid=(b, MLA_HEADS // MLA_PAIR),
        in_specs=[pl.BlockSpec((1, SEQ, w), lambda i, h: (i, 0, h)),
                  pl.BlockSpec((1, SEQ, w), lambda i, h: (i, 0, h)),
                  pl.BlockSpec((1, SEQ, MLA_HB), lambda i, h: (i, 0, 0))],
        out_specs=pl.BlockSpec((1, SEQ, MLA_PAIR * MLA_V), lambda i, h: (i, 0, h)),
        out_shape=jax.ShapeDtypeStruct((b, SEQ, MLA_HEADS * MLA_V), _BF),
        scratch_shapes=[pltpu.VMEM((MLA_PAIR, MLA_HB, SEQ), _BF), pltpu.VMEM((MLA_PAIR, SEQ, MLA_HB), _BF),
                        pltpu.VMEM((2, MLA_PAIR, MLA_TQ, SEQ), _F32)],
        compiler_params=_cp(("parallel", "parallel"), 40), name="mla_attention")(q, kv, kr)


def _na_hyena_mixer(x, w_in, rpb, conv_w, conv_b, f_w1, f_b1, f_freq, f_w2, f_b2, f_w3, skip):
    b = x.shape[0]
    qkv, hbc, zb = _inproj_conv(x.reshape(b * SEQ, D_MODEL), w_in, conv_w, conv_b)
    y_a = _na_attention(qkv.reshape(b, SEQ, 3 * NA_WIDTH), rpb)
    hbc = hbc.reshape(b, SEQ, (HY_ORDER + 1) * HY_WIDTH)
    zb = zb.reshape(b, SEQ, HY_WIDTH)
    fmat, fmat_t = (jnp.asarray(f).astype(_BF) for f in _dft_matrices())
    kfilt = _hyena_filters(f_w1, f_b1, f_freq, f_w2, f_b2, f_w3)
    kf = _filter_spectrum(fmat, kfilt)
    z = hbc
    for o in range(HY_ORDER):
        z, zb = _long_conv_gate(hbc, z, zb, o + 1, o, fmat, fmat_t, kf, skip)
    return y_a.reshape(b * SEQ, NA_WIDTH), zb.reshape(b * SEQ, HY_WIDTH)


def kernel(x, p, ab_w_in, na_rpb, hy_conv_w, hy_conv_b, hy_f_w1, hy_f_b1, hy_f_freq, hy_f_w2, hy_f_b2, hy_f_w3, hy_skip, ab_w_out, mla_w_in, mla_q_norm, mla_w_q_up, mla_kv_norm, mla_w_kv_up, mla_w_out, ln1_g, ln1_b, ln2_g, ln2_b, moe_router, moe_w_gate, moe_w_up, moe_w_down, ple_gate, ple_proj):
    b = x.shape[0]
    m = b * SEQ
    ple_gate_b, ple_proj_b = ple_gate.astype(_BF), ple_proj.astype(_BF)
    for i in range(DEPTH):
        j = i // 2
        x2d = x.reshape(m, D_MODEL)
        if i % 2 == 0:
            y_a, z = _na_hyena_mixer(x, ab_w_in[j].astype(_BF), na_rpb[j], hy_conv_w[j], hy_conv_b[j], hy_f_w1[j],
                                     hy_f_b1[j], hy_f_freq[j], hy_f_w2[j], hy_f_b2[j], hy_f_w3[j], hy_skip[j])
            x1, x1b, aff = _outproj_ln([y_a, z], ab_w_out[j].astype(_BF), x2d, ln1_g[i], ln1_b[i], moe_router[i])
        else:
            q, kv, kr = _mla_pre(x2d, mla_w_in[j], mla_q_norm[j], mla_w_q_up[j], mla_kv_norm[j], mla_w_kv_up[j])
            att = _mla_attention(q.reshape(b, SEQ, -1), kv.reshape(b, SEQ, -1), kr.reshape(b, SEQ, -1))
            x1, x1b, aff = _outproj_ln([att.reshape(m, MLA_HEADS * MLA_V)], mla_w_out[j].astype(_BF), x2d,
                                       ln1_g[i], ln1_b[i], moe_router[i])
        x = _moe_ple_block(x1.reshape(b, SEQ, D_MODEL), x1b.reshape(b, SEQ, D_MODEL), aff,
                           moe_w_gate, moe_w_up, moe_w_down, ln2_g[i], ln2_b[i], p, ple_gate_b, ple_proj_b, i)
    return x
```

```python
import functools
import math

import numpy as np
import jax
import jax.numpy as jnp
from jax import lax
from jax.experimental import pallas as pl
from jax.experimental.pallas import tpu as pltpu

D_MODEL = 1024
BATCH = 8
SEQ = 2048
DEPTH = 2
GRID_W = 64
PLE_DIM = 256
NA_HEADS = 8
NA_HEAD_DIM = 64
NA_WIDTH = NA_HEADS * NA_HEAD_DIM
NA_WIN_ROWS_MAX = 8
NA_WIN_COLS = 16
HY_WIDTH = D_MODEL - NA_WIDTH
HY_ORDER = 2
HY_SHORT_K = 3
HY_EMB_DIM = 33
HY_FILTER_HIDDEN = 64
HY_FAST_DECAY_PCT = 0.3
HY_SLOW_DECAY_PCT = 1.5
HY_DECAY_TARGET = 1e-2
AB_IN_WIDTH = 3 * NA_WIDTH + (HY_ORDER + 1) * HY_WIDTH
MLA_HEADS = 16
MLA_Q_LORA = 384
MLA_KV_LORA = 256
MLA_NOPE = 64
MLA_ROPE = 32
MLA_V = 64
ROPE_THETA = 10000.0
N_EXPERTS = 16
EC_CAPACITY_FACTOR = 2
D_FF_EXPERT = 2048
DN_ALPHA = (2 * DEPTH) ** 0.25
NORM_EPS = 1e-5
NEG_INF = -1e30

CAP = EC_CAPACITY_FACTOR * SEQ // N_EXPERTS
NA_ROWS = SEQ // GRID_W
NA_WIN_ROWS = min(NA_WIN_ROWS_MAX, NA_ROWS)
NA_SLAB = NA_WIN_ROWS * GRID_W
FFT_N = 2 * SEQ
LOG2E = math.log2(math.e)

LANE = 128
MIB = 1 << 20

_BF = jnp.bfloat16
_F32 = jnp.float32
_HP = lax.Precision.HIGHEST


def _cp(sem, vmem_mib):
    return pltpu.CompilerParams(dimension_semantics=sem, vmem_limit_bytes=vmem_mib * MIB)


def _dot(a, b):
    return jnp.dot(a.astype(_BF), b.astype(_BF), preferred_element_type=_F32)


def _dot_nt(a, b):
    return lax.dot_general(a.astype(_BF), b.astype(_BF), (((1,), (1,)), ((), ())),
                           preferred_element_type=_F32)


def _layer_norm(y, g, b):
    mu = jnp.mean(y, axis=-1, keepdims=True)
    d = y - mu
    var = jnp.mean(d * d, axis=-1, keepdims=True)
    return d * lax.rsqrt(var + NORM_EPS) * g + b


INPROJ_TM = 512
HALO = 8


def _inproj_kernel(x_ref, xp_ref, xn_ref, w_ref, cw_ref, cb_ref, qkv_ref, hbc_ref, vb_ref):
    tile_in_seq = pl.program_id(0) % (SEQ // INPROJ_TM)
    x_ext = jnp.concatenate([xp_ref[...], x_ref[...], xn_ref[...]], axis=0)
    h = _dot(x_ext, w_ref[...])
    qkv_ref[...] = h[HALO:HALO + INPROJ_TM, :3 * NA_WIDTH].astype(_BF)
    hb = h[:, 3 * NA_WIDTH:]
    n = hb.shape[0]
    row = lax.broadcasted_iota(jnp.int32, hb.shape, 0)
    pad_row = jnp.logical_or(jnp.logical_and(row == HALO - 1, tile_in_seq == 0),
                             jnp.logical_and(row == HALO + INPROJ_TM, tile_in_seq == SEQ // INPROJ_TM - 1))
    hb = jnp.where(pad_row, 0.0, hb)
    y = (cb_ref[...] + pltpu.roll(hb, 1, axis=0) * cw_ref[0:1, :] + hb * cw_ref[1:2, :]
         + pltpu.roll(hb, n - 1, axis=0) * cw_ref[2:3, :])[HALO:HALO + INPROJ_TM]
    hbc_ref[...] = y
    vb_ref[...] = y[:, :HY_WIDTH].astype(_BF)


def _inproj_conv(x2d, w, conv_w, conv_b):
    m = x2d.shape[0]
    tm = INPROJ_TM
    nq, nh = 3 * NA_WIDTH, (HY_ORDER + 1) * HY_WIDTH
    per_tile = tm // HALO
    last = m // HALO - 1
    return pl.pallas_call(
        _inproj_kernel, grid=(m // tm,),
        in_specs=[pl.BlockSpec((tm, D_MODEL), lambda i: (i, 0)),
                  pl.BlockSpec((HALO, D_MODEL), lambda i: (jnp.maximum(i * per_tile - 1, 0), 0)),
                  pl.BlockSpec((HALO, D_MODEL), lambda i: (jnp.minimum((i + 1) * per_tile, last), 0)),
                  pl.BlockSpec((D_MODEL, AB_IN_WIDTH), lambda i: (0, 0)),
                  pl.BlockSpec((HY_SHORT_K, nh), lambda i: (0, 0)),
                  pl.BlockSpec((1, nh), lambda i: (0, 0))],
        out_specs=[pl.BlockSpec((tm, nq), lambda i: (i, 0)),
                   pl.BlockSpec((tm, nh), lambda i: (i, 0)),
                   pl.BlockSpec((tm, HY_WIDTH), lambda i: (i, 0))],
        out_shape=[jax.ShapeDtypeStruct((m, nq), _BF), jax.ShapeDtypeStruct((m, nh), _F32),
                   jax.ShapeDtypeStruct((m, HY_WIDTH), _BF)],
        compiler_params=_cp(("parallel",), 56), name="ab_inproj_conv")(
            x2d, x2d, x2d, w, conv_w, conv_b.reshape(1, nh))


NA_GROUP = 4
NA_GW = NA_GROUP * NA_HEAD_DIM


def _na_kernel(q_ref, k_ref, v_ref, tbl_ref, o_ref):
    lane_head = lax.broadcasted_iota(jnp.int32, (GRID_W, NA_GW), 1) >> int(math.log2(NA_HEAD_DIM))
    scale = NA_HEAD_DIM ** -0.5 * LOG2E

    def body(r, carry):
        r0 = jnp.clip(r - NA_WIN_ROWS // 2, 0, NA_ROWS - NA_WIN_ROWS)
        off = r0 - r + (NA_WIN_ROWS_MAX - 1)
        qr = q_ref[0, pl.ds(pl.multiple_of(r * GRID_W, GRID_W), GRID_W), :].astype(_F32)
        q4 = jnp.concatenate([jnp.where(lane_head == h, qr, 0.0) for h in range(NA_GROUP)], axis=0)
        ks = k_ref[0, pl.ds(pl.multiple_of(r0 * GRID_W, GRID_W), NA_SLAB), :]
        vs = v_ref[0, pl.ds(pl.multiple_of(r0 * GRID_W, GRID_W), NA_SLAB), :]
        bias = jnp.concatenate([tbl_ref[h, off] for h in range(NA_GROUP)], axis=0)
        s = _dot_nt(q4, ks) * scale + bias
        mx = jnp.max(s, axis=1, keepdims=True)
        p = jnp.exp2(s - mx)
        den = jnp.sum(p, axis=1, keepdims=True)
        o4 = _dot(p, vs) / den
        out = jnp.zeros((GRID_W, NA_GW), _F32)
        for h in range(NA_GROUP):
            out = out + jnp.where(lane_head == h, o4[h * GRID_W:(h + 1) * GRID_W], 0.0)
        o_ref[0, pl.ds(pl.multiple_of(r * GRID_W, GRID_W), GRID_W), :] = out.astype(_BF)
        return carry

    lax.fori_loop(0, NA_ROWS, body, 0, unroll=True)


def _na_bias_table(rpb):
    c = np.arange(GRID_W)
    c0 = np.clip(c - NA_WIN_COLS // 2, 0, GRID_W - NA_WIN_COLS)
    kc = np.arange(GRID_W)
    col_ok = (kc[None, :] >= c0[:, None]) & (kc[None, :] < c0[:, None] + NA_WIN_COLS)
    dc_idx = np.clip(kc[None, :] - c[:, None], -(NA_WIN_COLS - 1), NA_WIN_COLS - 1) + (NA_WIN_COLS - 1)
    pick = (dc_idx[None, :, :] == np.arange(2 * NA_WIN_COLS - 1)[:, None, None]).astype(np.float32)
    per_dr = jnp.einsum("hdk,kqc->hdqc", rpb.astype(_F32), pick, precision=_HP)
    per_dr = jnp.where(col_ok[None, None], per_dr * LOG2E, NEG_INF)
    slabs = jnp.stack([per_dr[:, off:off + NA_WIN_ROWS] for off in range(NA_WIN_ROWS_MAX)], axis=1)
    return slabs.transpose(0, 1, 3, 2, 4).reshape(NA_HEADS, NA_WIN_ROWS_MAX, GRID_W, NA_SLAB)


def _na_attention(qkv, rpb):
    b = qkv.shape[0]
    tbl = _na_bias_table(rpb)
    ng = NA_HEADS // NA_GROUP
    blk = lambda col0: pl.BlockSpec((1, SEQ, NA_GW), lambda i, g, c=col0: (i, 0, c + g))
    return pl.pallas_call(
        _na_kernel, grid=(b, ng),
        in_specs=[blk(0), blk(ng), blk(2 * ng),
                  pl.BlockSpec((NA_GROUP, NA_WIN_ROWS_MAX, GRID_W, NA_SLAB), lambda i, g: (g, 0, 0, 0))],
        out_specs=pl.BlockSpec((1, SEQ, NA_GW), lambda i, g: (i, 0, g)),
        out_shape=jax.ShapeDtypeStruct((b, SEQ, NA_WIDTH), _BF),
        compiler_params=_cp(("parallel", "parallel"), 32), name="na_attention")(qkv, qkv, qkv, tbl)


HY_HID_PAD = LANE
HY_FILT_TC = 256


def _filter_kernel(z_ref, w1_ref, b1_ref, fr_ref, w2_ref, b2_ref, w3f_ref, w3b_ref, dec_ref, k_ref, h_ref):
    @pl.when(jnp.logical_and(pl.program_id(0) == 0, pl.program_id(1) == 0))
    def _():
        fr = fr_ref[...]
        h1 = jnp.sin(fr * (jnp.dot(z_ref[...], w1_ref[...], precision=_HP, preferred_element_type=_F32) + b1_ref[...]))
        h_ref[...] = jnp.sin(fr * (jnp.dot(h1, w2_ref[...], precision=_HP, preferred_element_type=_F32) + b2_ref[...]))

    fwd = jnp.dot(h_ref[:SEQ, :], w3f_ref[...], precision=_HP, preferred_element_type=_F32)
    bwd = jnp.dot(h_ref[SEQ:, :], w3b_ref[...], precision=_HP, preferred_element_type=_F32)
    k = jnp.concatenate([fwd, bwd], axis=0) * dec_ref[...]
    ss = jnp.sum(k * k, axis=0, keepdims=True)
    k_ref[...] = (k * lax.rsqrt(ss + 1e-12)).astype(_BF)


@functools.lru_cache(maxsize=None)
def _filter_tables():
    bands = (HY_EMB_DIM - 1) // 2
    t = np.linspace(0.0, 1.0, SEQ)[:, None]
    w = 2.0 * math.pi * np.arange(SEQ)[:, None] / SEQ
    f = np.linspace(1e-4, bands - 1, bands)[None, :]
    z = np.concatenate([t, np.cos(f * w), -np.sin(f * w)], axis=-1)
    min_decay = math.log(HY_DECAY_TARGET) / HY_SLOW_DECAY_PCT
    max_decay = math.log(HY_DECAY_TARGET) / HY_FAST_DECAY_PCT
    deltas = np.abs(np.linspace(min_decay, max_decay, HY_WIDTH))
    dec = np.exp(-t * deltas)
    src = np.concatenate([np.arange(SEQ), [0], np.arange(SEQ - 1, 0, -1)])
    live = np.ones((FFT_N, 1))
    live[SEQ] = 0.0
    z2 = np.pad(z[src], ((0, 0), (0, HY_HID_PAD - HY_EMB_DIM))).astype(np.float32)
    dec2 = (dec[src] * live).astype(np.float32)
    return z2, dec2


def _hyena_filters(w1, b1, freq, w2, b2, w3):
    z2, dec2 = _filter_tables()
    hp = HY_HID_PAD - HY_FILTER_HIDDEN
    w1p = jnp.pad(w1, ((0, HY_HID_PAD - HY_EMB_DIM), (0, hp)))
    w2p = jnp.pad(w2, ((0, hp), (0, hp)))
    w3p = jnp.pad(w3, ((0, hp), (0, 0)))
    row = lambda v: jnp.pad(v, (0, hp)).reshape(1, HY_HID_PAD)
    nc = HY_WIDTH // HY_FILT_TC
    per_order = 2 * nc
    full = lambda shape: pl.BlockSpec(shape, lambda o, j: (0, 0))
    return pl.pallas_call(
        _filter_kernel, grid=(HY_ORDER, nc),
        in_specs=[full((FFT_N, HY_HID_PAD)), full((HY_HID_PAD, HY_HID_PAD)), full((1, HY_HID_PAD)),
                  full((1, HY_HID_PAD)), full((HY_HID_PAD, HY_HID_PAD)), full((1, HY_HID_PAD)),
                  pl.BlockSpec((HY_HID_PAD, HY_FILT_TC), lambda o, j: (0, o * per_order + j)),
                  pl.BlockSpec((HY_HID_PAD, HY_FILT_TC), lambda o, j: (0, o * per_order + nc + j)),
                  pl.BlockSpec((FFT_N, HY_FILT_TC), lambda o, j: (0, j))],
        out_specs=pl.BlockSpec((FFT_N, HY_FILT_TC), lambda o, j: (0, o * nc + j)),
        out_shape=jax.ShapeDtypeStruct((FFT_N, HY_ORDER * HY_WIDTH), _BF),
        scratch_shapes=[pltpu.VMEM((FFT_N, HY_HID_PAD), _F32)],
        compiler_params=_cp(("arbitrary", "arbitrary"), 40), name="hy_filters")(
            z2, w1p, row(b1), row(freq), w2p, row(b2), w3p, w3p, dec2)


@functools.lru_cache(maxsize=None)
def _dft_matrices():
    t = np.arange(SEQ)
    ang = ((t[:, None] * t[None, :]) % FFT_N) * (2.0 * math.pi / FFT_N)
    re = np.cos(ang)
    im = -np.sin(ang)
    im[0] = 1.0 - 2.0 * (t % 2)
    nf = SEQ // HY_FB
    packed = np.stack([re.reshape(nf, HY_FB, SEQ), im.reshape(nf, HY_FB, SEQ)], axis=1).reshape(FFT_N, SEQ)
    return packed.astype(np.float32), np.ascontiguousarray(packed.T).astype(np.float32)


def _kf_kernel(f_ref, k_ref, o_ref):
    i = pl.program_id(0)
    f = f_ref[...]
    tm = f.shape[0]
    p1 = jnp.dot(f, k_ref[:SEQ, :], preferred_element_type=_F32)
    p2 = jnp.dot(f, k_ref[SEQ:, :], preferred_element_type=_F32)
    row = lax.broadcasted_iota(jnp.int32, (tm, 1), 0) + i * tm
    sign = (1 - 2 * (row & 1)).astype(_F32)
    o_ref[...] = p1 + sign * p2


def _filter_spectrum(fmat, kfilt):
    tm = 512
    nw = kfilt.shape[1]
    return pl.pallas_call(
        _kf_kernel, grid=(FFT_N // tm,),
        in_specs=[pl.BlockSpec((tm, SEQ), lambda i: (i, 0)),
                  pl.BlockSpec((FFT_N, nw), lambda i: (0, 0))],
        out_specs=pl.BlockSpec((tm, nw), lambda i: (i, 0)),
        out_shape=jax.ShapeDtypeStruct((FFT_N, nw), _F32),
        compiler_params=_cp(("parallel",), 40), name="hy_filter_spectrum")(fmat, kfilt)


HY_FB = 512
HY_NB = 2


HY_TM = 512


def _hconv_fwd_kernel(zb_ref, f_ref, k_ref, y_ref):
    fk = pl.program_id(1)
    kr = k_ref[:HY_FB, :]
    ki = k_ref[HY_FB:, :]
    row0 = jnp.logical_and(lax.broadcasted_iota(jnp.int32, kr.shape, 0) == 0, fk == 0)
    sc = jnp.where(row0, 1.0 / FFT_N, 2.0 / FFT_N)
    for j in range(HY_NB):
        u = jnp.dot(f_ref[...], zb_ref[j], preferred_element_type=_F32)
        ur = u[:HY_FB]
        ui = u[HY_FB:]
        yr = jnp.where(row0, ur * kr, ur * kr - ui * ki)
        yi = jnp.where(row0, ui * ki, ur * ki + ui * kr)
        y_ref[j, 0, 0] = (yr * sc).astype(_BF)
        y_ref[j, 0, 1] = (yi * sc).astype(_BF)


def _hconv_inv_kernel(ft_ref, y_ref, z_ref, skip_ref, xn_ref, o_ref, ob_ref):
    for j in range(HY_NB):
        conv = jnp.dot(ft_ref[...], y_ref[j], preferred_element_type=_F32)
        out = xn_ref[j] * (conv + z_ref[j] * skip_ref[...])
        o_ref[j] = out
        ob_ref[j] = out.astype(_BF)


def _long_conv_gate(hbc, zsrc, zb, xn_col, order, fmat, fmat_t, kf, skip):
    b = hbc.shape[0]
    nf = SEQ // HY_FB
    w = HY_WIDTH
    nb = HY_NB
    y = pl.pallas_call(
        _hconv_fwd_kernel, grid=(b // nb, nf),
        in_specs=[pl.BlockSpec((nb, SEQ, w), lambda i, f: (i, 0, 0)),
                  pl.BlockSpec((2 * HY_FB, SEQ), lambda i, f: (f, 0)),
                  pl.BlockSpec((2 * HY_FB, w), lambda i, f: (f, order))],
        out_specs=pl.BlockSpec((nb, 1, 2, HY_FB, w), lambda i, f: (i, f, 0, 0, 0)),
        out_shape=jax.ShapeDtypeStruct((b, nf, 2, HY_FB, w), _BF),
        compiler_params=_cp(("parallel", "parallel"), 40), name=f"hy_conv_fwd{order}")(zb, fmat, kf)
    blk = lambda col: pl.BlockSpec((nb, HY_TM, w), lambda i, m, c=col: (i, m, c))
    return pl.pallas_call(
        _hconv_inv_kernel, grid=(b // nb, SEQ // HY_TM),
        in_specs=[pl.BlockSpec((HY_TM, FFT_N), lambda i, m: (m, 0)),
                  pl.BlockSpec((nb, FFT_N, w), lambda i, m: (i, 0, 0)),
                  blk(0),
                  pl.BlockSpec((1, w), lambda i, m: (0, 0)),
                  blk(xn_col)],
        out_specs=[blk(0), blk(0)],
        out_shape=[jax.ShapeDtypeStruct((b, SEQ, w), _F32), jax.ShapeDtypeStruct((b, SEQ, w), _BF)],
        compiler_params=_cp(("parallel", "parallel"), 48), name=f"hy_conv_inv{order}")(
            fmat_t, y.reshape(b, FFT_N, w), zsrc, skip[order].reshape(1, w), hbc)


def _outproj_ln_kernel(n_a, *refs):
    a_refs = refs[:n_a]
    w_refs = refs[n_a:2 * n_a]
    x_ref, g_ref, b_ref, wr_ref, o_ref, ob_ref, aff_ref = refs[2 * n_a:]
    m = _dot(a_refs[0][...], w_refs[0][...])
    for a_ref, w_ref in zip(a_refs[1:], w_refs[1:]):
        m = m + _dot(a_ref[...], w_ref[...])
    y = _layer_norm(DN_ALPHA * x_ref[...] + m, g_ref[...], b_ref[...])
    o_ref[...] = y
    ob_ref[...] = y.astype(_BF)
    wr = wr_ref[...]
    wr_hi = wr.astype(_BF)
    wr_lo = (wr - wr_hi.astype(_F32)).astype(_BF)
    y_hi = y.astype(_BF)
    y_lo = (y - y_hi.astype(_F32)).astype(_BF)
    by_hi = _dot_nt(jnp.concatenate([wr_hi, wr_lo], axis=0), y_hi)
    logits = by_hi[:N_EXPERTS] + by_hi[N_EXPERTS:] + _dot_nt(wr_hi, y_lo)
    ex = jnp.exp(logits - jnp.max(logits, axis=0, keepdims=True))
    aff_ref[0] = ex / jnp.sum(ex, axis=0, keepdims=True)


def _outproj_ln(a_list, w, x2d, g, b, w_router):
    m = x2d.shape[0]
    tm = 1024
    per_seq = SEQ // tm
    in_specs, w_args, row0 = [], [], 0
    for a in a_list:
        in_specs.append(pl.BlockSpec((tm, a.shape[1]), lambda i: (i, 0)))
    for a in a_list:
        ka = a.shape[1]
        in_specs.append(pl.BlockSpec((ka, D_MODEL), lambda i, r=row0 // ka: (r, 0)))
        w_args.append(w)
        row0 += ka
    in_specs += [pl.BlockSpec((tm, D_MODEL), lambda i: (i, 0)),
                 pl.BlockSpec((1, D_MODEL), lambda i: (0, 0)),
                 pl.BlockSpec((1, D_MODEL), lambda i: (0, 0)),
                 pl.BlockSpec((N_EXPERTS, D_MODEL), lambda i: (0, 0))]
    return pl.pallas_call(
        functools.partial(_outproj_ln_kernel, len(a_list)), grid=(m // tm,),
        in_specs=in_specs,
        out_specs=[pl.BlockSpec((tm, D_MODEL), lambda i: (i, 0)),
                   pl.BlockSpec((tm, D_MODEL), lambda i: (i, 0)),
                   pl.BlockSpec((1, N_EXPERTS, tm), lambda i: (i // per_seq, 0, i % per_seq))],
        out_shape=[jax.ShapeDtypeStruct((m, D_MODEL), _F32), jax.ShapeDtypeStruct((m, D_MODEL), _BF),
                   jax.ShapeDtypeStruct((m // SEQ, N_EXPERTS, SEQ), _F32)],
        compiler_params=_cp(("parallel",), 40), name="outproj_ln_router")(
            *a_list, *w_args, x2d, g.reshape(1, -1), b.reshape(1, -1), w_router.T)


BISECT_FLOOR = 2.0 ** -120
BISECT_GEO = 8
BISECT_LIN = 60


def _select_kernel(a_ref, tri_ref, edge_ref, slot_ref, slot_t_ref, cnt_ref):
    a = a_ref[...]
    rows = a.shape[0]
    cap = float(CAP)

    def probe(mid, lo, hi):
        cnt = jnp.sum(jnp.where(a > mid, 1.0, 0.0), axis=1, keepdims=True)
        ge = cnt >= cap
        return jnp.where(ge, mid, lo), jnp.where(ge, hi, mid)

    def geo(_, c):
        lo, hi = c
        return probe(jnp.where(lo > 0.0, jnp.sqrt(lo * hi), 0.5 * (lo + hi)), lo, hi)

    def lin(_, c):
        lo, hi = c
        return probe(0.5 * (lo + hi), lo, hi)

    c = probe(jnp.full((rows, 1), BISECT_FLOOR, _F32),
              jnp.full((rows, 1), -1.0, _F32), jnp.full((rows, 1), 1.0, _F32))
    c = lax.fori_loop(0, BISECT_GEO, geo, c)
    lo, hi = lax.fori_loop(0, BISECT_LIN, lin, c)
    vstar = jnp.max(jnp.where(a <= hi, a, -1.0), axis=1, keepdims=True)
    gt = a > vstar
    eq = a == vstar
    need = cap - jnp.sum(jnp.where(gt, 1.0, 0.0), axis=1, keepdims=True)
    tri = tri_ref[...]
    eq_before = jnp.dot(jnp.where(eq, 1.0, 0.0).astype(_BF), tri, preferred_element_type=_F32)
    sel = jnp.where(gt, 1.0, jnp.where(eq, jnp.where(eq_before < need, 1.0, 0.0), 0.0))
    pos = jnp.dot(sel.astype(_BF), tri, preferred_element_type=_F32)
    slot = jnp.where(sel > 0.5, pos, -1.0)
    slot_ref[...] = slot.astype(jnp.int32)
    cnt_ref[...] = jnp.dot(sel.astype(_BF), edge_ref[...], preferred_element_type=_F32).astype(jnp.int32)
    if rows < LANE:
        slot = jnp.concatenate([slot, jnp.full((LANE - rows, SEQ), -1.0, _F32)], axis=0)
    slot_tm = slot.T
    for b in range(rows // N_EXPERTS):
        shifted = slot_tm if b == 0 else pltpu.roll(slot_tm, LANE - N_EXPERTS * b, axis=1)
        slot_t_ref[b] = shifted.astype(jnp.int32)


@functools.lru_cache(maxsize=None)
def _prefix_matrix():
    idx = np.arange(SEQ)
    return (idx[:, None] < idx[None, :]).astype(_BF)


WIN_TT = 256
SCAT_TT = 512
WIN_W = 128
WIN_EDGES = 16


@functools.lru_cache(maxsize=None)
def _edge_matrix():
    return (np.arange(SEQ)[:, None] < np.arange(LANE)[None, :] * WIN_TT).astype(_BF)


def _select(aff2d):
    rows = aff2d.shape[0]
    assert rows <= LANE and rows % N_EXPERTS == 0
    bsz = rows // N_EXPERTS
    slot, slot_t, cnt = pl.pallas_call(
        _select_kernel, grid=(1,),
        in_specs=[pl.BlockSpec((rows, SEQ), lambda i: (0, 0)),
                  pl.BlockSpec((SEQ, SEQ), lambda i: (0, 0)),
                  pl.BlockSpec((SEQ, LANE), lambda i: (0, 0))],
        out_specs=[pl.BlockSpec((rows, SEQ), lambda i: (0, 0)),
                   pl.BlockSpec((bsz, SEQ, LANE), lambda i: (0, 0, 0)),
                   pl.BlockSpec((rows, LANE), lambda i: (0, 0))],
        out_shape=[jax.ShapeDtypeStruct((rows, SEQ), jnp.int32),
                   jax.ShapeDtypeStruct((bsz, SEQ, LANE), jnp.int32),
                   jax.ShapeDtypeStruct((rows, LANE), jnp.int32)],
        compiler_params=_cp(("arbitrary",), 48), name="moe_select")(aff2d, _prefix_matrix(), _edge_matrix())
    return slot, slot_t, cnt[:, :WIN_EDGES].reshape(rows * WIN_EDGES)


def _onehot(slot_row):
    return slot_row == lax.broadcasted_iota(jnp.int32, (CAP, SEQ), 0)


GATHER_NE = 4
GATHER_TC = 512
GATHER_W = 64


def _window_start(cnt_ref, seq, e, tile, width, span=1):
    base = (seq * N_EXPERTS + e) * WIN_EDGES + tile * span
    start = jnp.minimum((cnt_ref[base] >> 4) << 4, CAP - width)
    return start, cnt_ref[base + span] - start <= width


def _gather_kernel(cnt_ref, slot_ref, a_ref, xb_ref, xe_ref, g_ref):
    i = pl.program_id(0)
    n_tiles = SEQ // WIN_TT
    wd = GATHER_W
    starts, fits = {}, None
    for t in range(n_tiles):
        for e in range(N_EXPERTS):
            starts[e, t], ok = _window_start(cnt_ref, i, e, t, wd)
            fits = ok if fits is None else jnp.logical_and(fits, ok)

    @pl.when(fits)
    def _():
        xe_ref[...] = jnp.zeros(xe_ref.shape, _BF)
        g_ref[...] = jnp.zeros(g_ref.shape, _F32)
        sub_w = lax.broadcasted_iota(jnp.int32, (wd, WIN_TT), 0)
        for t in range(n_tiles):
            toks = slice(t * WIN_TT, (t + 1) * WIN_TT)
            ps = []
            for e in range(N_EXPERTS):
                hit = slot_ref[0, e, :, toks] - starts[e, t] == sub_w
                ps.append(jnp.where(hit, 1.0, 0.0).astype(_BF))
                rows = pl.ds(pl.multiple_of(starts[e, t], 16), wd)
                g_ref[e, rows, :] += jnp.sum(jnp.where(hit, a_ref[0, e, :, toks], 0.0), axis=1, keepdims=True)
            pcat = jnp.concatenate(ps, axis=0)
            for c in range(D_MODEL // GATHER_TC):
                cols = slice(c * GATHER_TC, (c + 1) * GATHER_TC)
                res = jnp.dot(pcat, xb_ref[0, toks, cols], preferred_element_type=_F32).astype(_BF)
                for e in range(N_EXPERTS):
                    rows = pl.ds(pl.multiple_of(starts[e, t], 16), wd)
                    xe_ref[e, rows, cols] += res[e * wd:(e + 1) * wd]

    @pl.when(jnp.logical_not(fits))
    def _():
        for grp in range(N_EXPERTS // GATHER_NE):
            es = range(grp * GATHER_NE, (grp + 1) * GATHER_NE)
            hits = [_onehot(slot_ref[0, e]) for e in es]
            p = jnp.concatenate([jnp.where(h, 1.0, 0.0).astype(_BF) for h in hits], axis=0)
            xe = jnp.dot(p, xb_ref[0], preferred_element_type=_F32).astype(_BF)
            xe_ref[grp * GATHER_NE:(grp + 1) * GATHER_NE] = xe.reshape(GATHER_NE, CAP, D_MODEL)
            for k, e in enumerate(es):
                g_ref[e] = jnp.sum(jnp.where(hits[k], a_ref[0, e], 0.0), axis=1, keepdims=True)


def _gather(cnt, slot4, aff4, xb3d):
    b = xb3d.shape[0]
    row = pl.BlockSpec((1, N_EXPERTS, 1, SEQ), lambda i, c: (i, 0, 0, 0))
    grid_spec = pltpu.PrefetchScalarGridSpec(
        num_scalar_prefetch=1, grid=(b,),
        in_specs=[row, row, pl.BlockSpec((1, SEQ, D_MODEL), lambda i, c: (i, 0, 0))],
        out_specs=[pl.BlockSpec((N_EXPERTS, CAP, D_MODEL), lambda i, c: (0, i, 0)),
                   pl.BlockSpec((N_EXPERTS, CAP, 1), lambda i, c: (0, i, 0))])
    return pl.pallas_call(
        _gather_kernel, grid_spec=grid_spec,
        out_shape=[jax.ShapeDtypeStruct((N_EXPERTS, b * CAP, D_MODEL), _BF),
                   jax.ShapeDtypeStruct((N_EXPERTS, b * CAP, 1), _F32)],
        compiler_params=_cp(("parallel",), 56), name="moe_gather")(cnt, slot4, aff4, xb3d)


FFN_TF = 1024
FFN_TM = 512


def _ffn_up_kernel(xe_ref, wg_ref, wu_ref, h_ref):
    wg = wg_ref[0].astype(_BF)
    wu = wu_ref[0].astype(_BF)
    for c in range(xe_ref.shape[1] // FFN_TM):
        rows = slice(c * FFN_TM, (c + 1) * FFN_TM)
        xe = xe_ref[0, rows, :]
        hg = _dot(xe, wg)
        hu = _dot(xe, wu)
        h_ref[0, rows, :] = ((hg * (1.0 / (1.0 + jnp.exp(-hg)))) * hu).astype(_BF)


def _ffn_down_kernel(h_ref, wd_ref, g_ref, ye_ref):
    wd = wd_ref[0].astype(_BF)
    for c in range(h_ref.shape[1] // FFN_TM):
        rows = slice(c * FFN_TM, (c + 1) * FFN_TM)
        ye_ref[0, rows, :] = (_dot(h_ref[0, rows, :], wd) * g_ref[0, rows, :]).astype(_BF)


def _expert_ffn(xe, gates, wg, wu, wd, layer):
    rows = xe.shape[1]
    h = pl.pallas_call(
        _ffn_up_kernel, grid=(N_EXPERTS, D_FF_EXPERT // FFN_TF),
        in_specs=[pl.BlockSpec((1, rows, D_MODEL), lambda e, f: (e, 0, 0)),
                  pl.BlockSpec((None, 1, D_MODEL, FFN_TF), lambda e, f: (layer, e, 0, f)),
                  pl.BlockSpec((None, 1, D_MODEL, FFN_TF), lambda e, f: (layer, e, 0, f))],
        out_specs=pl.BlockSpec((1, rows, FFN_TF), lambda e, f: (e, 0, f)),
        out_shape=jax.ShapeDtypeStruct((N_EXPERTS, rows, D_FF_EXPERT), _BF),
        compiler_params=_cp(("parallel", "parallel"), 52), name="moe_ffn_up")(xe, wg, wu)
    return pl.pallas_call(
        _ffn_down_kernel, grid=(N_EXPERTS,),
        in_specs=[pl.BlockSpec((1, rows, D_FF_EXPERT), lambda e: (e, 0, 0)),
                  pl.BlockSpec((None, 1, D_FF_EXPERT, D_MODEL), lambda e: (layer, e, 0, 0)),
                  pl.BlockSpec((1, rows, 1), lambda e: (e, 0, 0))],
        out_specs=pl.BlockSpec((1, rows, D_MODEL), lambda e: (e, 0, 0)),
        out_shape=jax.ShapeDtypeStruct((N_EXPERTS, rows, D_MODEL), _BF),
        compiler_params=_cp(("parallel",), 56), name="moe_ffn_down")(h, wd, gates)


def _scatter_ln_ple_kernel(cnt_ref, slot_t_ref, ye_ref, x_ref, g_ref, b_ref, p_ref, wg_ref, wp_ref, o_ref,
                           pt_ref, yw_ref, f_ref):
    i = pl.program_id(0)
    r = pl.program_id(1)
    slot_t = slot_t_ref[0]
    starts = []
    fits = None
    for e in range(N_EXPERTS):
        start, ok = _window_start(cnt_ref, i, e, r, WIN_W, SCAT_TT // WIN_TT)
        starts.append(start)
        fits = ok if fits is None else jnp.logical_and(fits, ok)

    @pl.when(fits)
    def _():
        lane_w = lax.broadcasted_iota(jnp.int32, (SCAT_TT, WIN_W), 1)
        for e in range(N_EXPERTS):
            cols = slice(e * WIN_W, (e + 1) * WIN_W)
            pt_ref[:, cols] = jnp.where(slot_t[:, e:e + 1] - starts[e] == lane_w, 1.0, 0.0).astype(_BF)
            yw_ref[cols, :] = ye_ref[e, pl.ds(pl.multiple_of(starts[e], 16), WIN_W), :]
        f_ref[...] = jnp.dot(pt_ref[:, :N_EXPERTS * WIN_W], yw_ref[...], preferred_element_type=_F32)

    @pl.when(jnp.logical_not(fits))
    def _():
        lane_c = lax.broadcasted_iota(jnp.int32, (SCAT_TT, CAP), 1)
        for e in range(N_EXPERTS):
            pt_ref[:, e * CAP:(e + 1) * CAP] = jnp.where(slot_t[:, e:e + 1] == lane_c, 1.0, 0.0).astype(_BF)
        f_ref[...] = jnp.dot(pt_ref[...], ye_ref[...].reshape(N_EXPERTS * CAP, D_MODEL),
                             preferred_element_type=_F32)

    y = _layer_norm(DN_ALPHA * x_ref[0] + f_ref[...], g_ref[...], b_ref[...])
    gate = _dot(y, wg_ref[...])
    emb = _dot(p_ref[0], wp_ref[...])
    o_ref[0] = y + (1.0 / (1.0 + jnp.exp(-gate))) * emb


def _scatter_ln_ple(cnt, slot_t, ye, x3d, g, b, p, ple_gate, ple_proj, layer):
    bsz = x3d.shape[0]
    vec = pl.BlockSpec((1, D_MODEL), lambda i, r, c: (0, 0))
    seq = pl.BlockSpec((1, SCAT_TT, D_MODEL), lambda i, r, c: (i, r, 0))
    grid_spec = pltpu.PrefetchScalarGridSpec(
        num_scalar_prefetch=1, grid=(bsz, SEQ // SCAT_TT),
        in_specs=[pl.BlockSpec((1, SCAT_TT, LANE), lambda i, r, c: (i, r, 0)),
                  pl.BlockSpec((N_EXPERTS, CAP, D_MODEL), lambda i, r, c: (0, i, 0)),
                  seq, vec, vec,
                  pl.BlockSpec((None, 1, SCAT_TT, PLE_DIM), lambda i, r, c: (layer, i, r, 0)),
                  pl.BlockSpec((None, D_MODEL, D_MODEL), lambda i, r, c: (layer, 0, 0)),
                  pl.BlockSpec((None, PLE_DIM, D_MODEL), lambda i, r, c: (layer, 0, 0))],
        out_specs=seq,
        scratch_shapes=[pltpu.VMEM((SCAT_TT, N_EXPERTS * CAP), _BF),
                        pltpu.VMEM((N_EXPERTS * WIN_W, D_MODEL), _BF),
                        pltpu.VMEM((SCAT_TT, D_MODEL), _F32)])
    return pl.pallas_call(
        _scatter_ln_ple_kernel, grid_spec=grid_spec,
        out_shape=jax.ShapeDtypeStruct((bsz, SEQ, D_MODEL), _F32),
        compiler_params=_cp(("parallel", "arbitrary"), 56), name="moe_scatter_ln_ple")(
            cnt, slot_t, ye, x3d, g.reshape(1, -1), b.reshape(1, -1), p, ple_gate, ple_proj)


def _moe_ple_block(x1, x1b, aff, wg, wu, wd, g, b, p, ple_gate, ple_proj, layer):
    bsz = x1.shape[0]
    slot, slot_t, cnt = _select(aff.reshape(bsz * N_EXPERTS, SEQ))
    slot4 = slot.reshape(bsz, N_EXPERTS, 1, SEQ)
    aff4 = aff.reshape(bsz, N_EXPERTS, 1, SEQ)
    xe, gates = _gather(cnt, slot4, aff4, x1b)
    ye = _expert_ffn(xe, gates, wg, wu, wd, layer)
    return _scatter_ln_ple(cnt, slot_t, ye, x1, g, b, p, ple_gate, ple_proj, layer)


MLA_HB = LANE
MLA_IN_PAD = 768
MLA_KR_COL = MLA_Q_LORA + MLA_KV_LORA


def _rms(x, g):
    return x * lax.rsqrt(jnp.mean(x * x, axis=-1, keepdims=True) + NORM_EPS) * g


def _mla_pre_kernel(x_ref, win_ref, qg_ref, wq_ref, kvg_ref, wkv_ref, qa_ref, qb_ref, ka_ref, kb_ref,
                    q_ref, kv_ref, kr_ref):
    h = _dot(x_ref[...], win_ref[...])
    q = _dot(_rms(h[:, :MLA_Q_LORA], qg_ref[...]), wq_ref[...])
    swap = pltpu.roll(q, q.shape[1] - MLA_ROPE, axis=1)
    qa = jnp.concatenate([qa_ref[...]] * MLA_HEADS, axis=1)
    qb = jnp.concatenate([qb_ref[...]] * MLA_HEADS, axis=1)
    q_ref[...] = (q * qa + swap * qb).astype(_BF)
    kv_ref[...] = _dot(_rms(h[:, MLA_Q_LORA:MLA_KR_COL], kvg_ref[...]), wkv_ref[...]).astype(_BF)
    kr = h[:, MLA_KR_COL:MLA_KR_COL + MLA_HB]
    kr_ref[...] = (kr * ka_ref[...] + pltpu.roll(kr, MLA_HB - MLA_ROPE, axis=1) * kb_ref[...]).astype(_BF)


@functools.lru_cache(maxsize=None)
def _rope_lane_tables():
    inv = 1.0 / (ROPE_THETA ** (np.arange(0, MLA_ROPE, 2) / MLA_ROPE))
    ang = np.arange(SEQ)[:, None] * inv[None, :]
    cos, sin = np.cos(ang), np.sin(ang)
    ones = np.ones((SEQ, MLA_NOPE))
    zeros_n = np.zeros((SEQ, MLA_NOPE))
    zeros_r = np.zeros((SEQ, MLA_ROPE))
    cc = np.concatenate([cos, cos], axis=1)
    ss = np.concatenate([-sin, sin], axis=1)
    keep_q = np.concatenate([ones, cc, zeros_r], axis=1)
    keep_k = np.concatenate([zeros_n, cc, zeros_r], axis=1)
    swp = np.concatenate([zeros_n, ss, zeros_r], axis=1)
    qscale = (MLA_NOPE + MLA_ROPE) ** -0.5 * LOG2E
    f32 = lambda a: a.astype(np.float32)
    return f32(keep_q * qscale), f32(swp * qscale), f32(keep_k), f32(swp)


def _mla_pre(x2d, w_in, q_norm, w_q_up, kv_norm, w_kv_up):
    m = x2d.shape[0]
    tm = 512
    half = MLA_ROPE // 2
    kr_w = w_in[:, MLA_KR_COL:]
    kr_swapped = jnp.concatenate([kr_w[:, half:], kr_w[:, :half]], axis=1)
    win = jnp.concatenate([w_in[:, :MLA_KR_COL], jnp.zeros((D_MODEL, MLA_NOPE), _F32), kr_w, kr_swapped],
                          axis=1).astype(_BF)
    wq = w_q_up.reshape(MLA_Q_LORA, MLA_HEADS, MLA_NOPE + MLA_ROPE)
    rope_w = wq[:, :, MLA_NOPE:]
    wq = jnp.concatenate([wq, rope_w[:, :, half:], rope_w[:, :, :half]], axis=2).reshape(
        MLA_Q_LORA, MLA_HEADS * MLA_HB).astype(_BF)
    w_kv_up = w_kv_up.astype(_BF)
    q_keep, q_swap, k_keep, k_swap = _rope_lane_tables()
    nq = MLA_HEADS * MLA_HB
    pos = lambda: pl.BlockSpec((tm, MLA_HB), lambda i: (i % (SEQ // tm), 0))
    full = lambda a: pl.BlockSpec(a.shape, lambda i: (0,) * a.ndim)
    qg, kvg = q_norm.reshape(1, -1), kv_norm.reshape(1, -1)
    return pl.pallas_call(
        _mla_pre_kernel, grid=(m // tm,),
        in_specs=[pl.BlockSpec((tm, D_MODEL), lambda i: (i, 0)), full(win), full(qg), full(wq), full(kvg),
                  full(w_kv_up), pos(), pos(), pos(), pos()],
        out_specs=[pl.BlockSpec((tm, nq), lambda i: (i, 0)),
                   pl.BlockSpec((tm, nq), lambda i: (i, 0)),
                   pl.BlockSpec((tm, MLA_HB), lambda i: (i, 0))],
        out_shape=[jax.ShapeDtypeStruct((m, nq), _BF), jax.ShapeDtypeStruct((m, nq), _BF),
                   jax.ShapeDtypeStruct((m, MLA_HB), _BF)],
        compiler_params=_cp(("parallel",), 48), name="mla_pre")(
            x2d, win, qg, wq, kvg, w_kv_up, q_keep, q_swap, k_keep, k_swap)


MLA_TQ = 256
MLA_PAIR = 2


def _mla_attn_kernel(q_ref, kv_ref, kr_ref, o_ref, kt_scr, v_scr, s_scr):
    lane = lax.broadcasted_iota(jnp.int32, (SEQ, MLA_HB), 1)
    kr = kr_ref[0].astype(_F32)
    for j in range(MLA_PAIR):
        kvh = kv_ref[0, :, j * MLA_HB:(j + 1) * MLA_HB].astype(_F32)
        kt_scr[j] = jnp.where(lane < MLA_NOPE, kvh, kr).T.astype(_BF)
        v_scr[j] = jnp.where(lane < MLA_NOPE, 1.0, kvh).astype(_BF)
    lane_q = lax.broadcasted_iota(jnp.int32, (MLA_TQ, MLA_HB), 1)

    def block_rows(i):
        return pl.ds(pl.multiple_of(i * MLA_TQ, MLA_TQ), MLA_TQ)

    def scores(i, slot):
        for j in range(MLA_PAIR):
            s_scr[slot, j] = _dot(q_ref[0, block_rows(i), j * MLA_HB:(j + 1) * MLA_HB], kt_scr[j])

    def finish(i, slot):
        outs = []
        for j in range(MLA_PAIR):
            s = s_scr[slot, j]
            p = jnp.exp2(s - jnp.max(s, axis=1, keepdims=True))
            o = _dot(p, v_scr[j])
            outs.append(o / o[:, 0:1])
        even = pltpu.roll(outs[0], MLA_V, axis=1)
        o_ref[0, block_rows(i), :] = jnp.where(lane_q < MLA_V, even, outs[1]).astype(_BF)

    nblk = SEQ // MLA_TQ
    scores(0, 0)

    def body(k, carry):
        scores(2 * k + 1, 1)
        finish(2 * k, 0)
        scores(2 * k + 2, 0)
        finish(2 * k + 1, 1)
        return carry

    lax.fori_loop(0, nblk // 2 - 1, body, 0)
    scores(nblk - 1, 1)
    finish(nblk - 2, 0)
    finish(nblk - 1, 1)


def _mla_attention(q, kv, kr):
    b = q.shape[0]
    w = MLA_PAIR * MLA_HB
    return pl.pallas_call(
        _mla_attn_kernel, grid=(b, MLA_HEADS // MLA_PAIR),
        in_specs=[pl.BlockSpec((1, SEQ, w), lambda i, h: (i, 0, h)),
                  pl.BlockSpec((1, SEQ, w), lambda i, h: (i, 0, h)),
                  pl.BlockSpec((1, SEQ, MLA_HB), lambda i, h: (i, 0, 0))],
        out_specs=pl.BlockSpec((1, SEQ, MLA_PAIR * MLA_V), lambda i, h: (i, 0, h)),
        out_shape=jax.ShapeDtypeStruct((b, SEQ, MLA_HEADS * MLA_V), _BF),
        scratch_shapes=[pltpu.VMEM((MLA_PAIR, MLA_HB, SEQ), _BF), pltpu.VMEM((MLA_PAIR, SEQ, MLA_HB), _BF),
                        pltpu.VMEM((2, MLA_PAIR, MLA_TQ, SEQ), _F32)],
        compiler_params=_cp(("parallel", "parallel"), 40), name="mla_attention")(q, kv, kr)


def _na_hyena_mixer(x, w_in, rpb, conv_w, conv_b, f_w1, f_b1, f_freq, f_w2, f_b2, f_w3, skip):
    b = x.shape[0]
    qkv, hbc, zb = _inproj_conv(x.reshape(b * SEQ, D_MODEL), w_in, conv_w, conv_b)
    y_a = _na_attention(qkv.reshape(b, SEQ, 3 * NA_WIDTH), rpb)
    hbc = hbc.reshape(b, SEQ, (HY_ORDER + 1) * HY_WIDTH)
    zb = zb.reshape(b, SEQ, HY_WIDTH)
    fmat, fmat_t = (jnp.asarray(f).astype(_BF) for f in _dft_matrices())
    kfilt = _hyena_filters(f_w1, f_b1, f_freq, f_w2, f_b2, f_w3)
    kf = _filter_spectrum(fmat, kfilt)
    z = hbc
    for o in range(HY_ORDER):
        z, zb = _long_conv_gate(hbc, z, zb, o + 1, o, fmat, fmat_t, kf, skip)
    return y_a.reshape(b * SEQ, NA_WIDTH), zb.reshape(b * SEQ, HY_WIDTH)


def kernel(x, p, ab_w_in, na_rpb, hy_conv_w, hy_conv_b, hy_f_w1, hy_f_b1, hy_f_freq, hy_f_w2, hy_f_b2, hy_f_w3, hy_skip, ab_w_out, mla_w_in, mla_q_norm, mla_w_q_up, mla_kv_norm, mla_w_kv_up, mla_w_out, ln1_g, ln1_b, ln2_g, ln2_b, moe_router, moe_w_gate, moe_w_up, moe_w_down, ple_gate, ple_proj):
    b = x.shape[0]
    m = b * SEQ
    ple_gate_b, ple_proj_b = ple_gate.astype(_BF), ple_proj.astype(_BF)
    for i in range(DEPTH):
        j = i // 2
        x2d = x.reshape(m, D_MODEL)
        if i % 2 == 0:
            y_a, z = _na_hyena_mixer(x, ab_w_in[j].astype(_BF), na_rpb[j], hy_conv_w[j], hy_conv_b[j], hy_f_w1[j],
                                     hy_f_b1[j], hy_f_freq[j], hy_f_w2[j], hy_f_b2[j], hy_f_w3[j], hy_skip[j])
            x1, x1b, aff = _outproj_ln([y_a, z], ab_w_out[j].astype(_BF), x2d, ln1_g[i], ln1_b[i], moe_router[i])
        else:
            q, kv, kr = _mla_pre(x2d, mla_w_in[j], mla_q_norm[j], mla_w_q_up[j], mla_kv_norm[j], mla_w_kv_up[j])
            att = _mla_attention(q.reshape(b, SEQ, -1), kv.reshape(b, SEQ, -1), kr.reshape(b, SEQ, -1))
            x1, x1b, aff = _outproj_ln([att.reshape(m, MLA_HEADS * MLA_V)], mla_w_out[j].astype(_BF), x2d,
                                       ln1_g[i], ln1_b[i], moe_router[i])
        x = _moe_ple_block(x1.reshape(b, SEQ, D_MODEL), x1b.reshape(b, SEQ, D_MODEL), aff,
                           moe_w_gate, moe_w_up, moe_w_down, ln2_g[i], ln2_b[i], p, ple_gate_b, ple_proj_b, i)
    return x
```

```python
import functools
import math

import numpy as np
import jax
import jax.numpy as jnp
from jax import lax
from jax.experimental import pallas as pl
from jax.experimental.pallas import tpu as pltpu

D_MODEL = 1024
BATCH = 8
SEQ = 2048
DEPTH = 2
GRID_W = 64
PLE_DIM = 256
NA_HEADS = 8
NA_HEAD_DIM = 64
NA_WIDTH = NA_HEADS * NA_HEAD_DIM
NA_WIN_ROWS_MAX = 8
NA_WIN_COLS = 16
HY_WIDTH = D_MODEL - NA_WIDTH
HY_ORDER = 2
HY_SHORT_K = 3
HY_EMB_DIM = 33
HY_FILTER_HIDDEN = 64
HY_FAST_DECAY_PCT = 0.3
HY_SLOW_DECAY_PCT = 1.5
HY_DECAY_TARGET = 1e-2
AB_IN_WIDTH = 3 * NA_WIDTH + (HY_ORDER + 1) * HY_WIDTH
MLA_HEADS = 16
MLA_Q_LORA = 384
MLA_KV_LORA = 256
MLA_NOPE = 64
MLA_ROPE = 32
MLA_V = 64
ROPE_THETA = 10000.0
N_EXPERTS = 16
EC_CAPACITY_FACTOR = 2
D_FF_EXPERT = 2048
DN_ALPHA = (2 * DEPTH) ** 0.25
NORM_EPS = 1e-5
NEG_INF = -1e30

CAP = EC_CAPACITY_FACTOR * SEQ // N_EXPERTS
NA_ROWS = SEQ // GRID_W
NA_WIN_ROWS = min(NA_WIN_ROWS_MAX, NA_ROWS)
NA_SLAB = NA_WIN_ROWS * GRID_W
FFT_N = 2 * SEQ
LOG2E = math.log2(math.e)

LANE = 128
MIB = 1 << 20

_BF = jnp.bfloat16
_F32 = jnp.float32
_HP = lax.Precision.HIGHEST


def _cp(sem, vmem_mib):
    return pltpu.CompilerParams(dimension_semantics=sem, vmem_limit_bytes=vmem_mib * MIB)


def _dot(a, b):
    return jnp.dot(a.astype(_BF), b.astype(_BF), preferred_element_type=_F32)


def _dot_nt(a, b):
    return lax.dot_general(a.astype(_BF), b.astype(_BF), (((1,), (1,)), ((), ())),
                           preferred_element_type=_F32)


def _dot3(a, b):
    a_hi = a.astype(_BF)
    a_lo = (a - a_hi.astype(_F32)).astype(_BF)
    b_hi = b.astype(_BF)
    b_lo = (b - b_hi.astype(_F32)).astype(_BF)
    mm = lambda u, v: jnp.dot(u, v, preferred_element_type=_F32)
    return mm(a_hi, b_hi) + mm(a_hi, b_lo) + mm(a_lo, b_hi)


def _layer_norm(y, g, b):
    mu = jnp.mean(y, axis=-1, keepdims=True)
    d = y - mu
    var = jnp.mean(d * d, axis=-1, keepdims=True)
    return d * lax.rsqrt(var + NORM_EPS) * g + b


INPROJ_TM = 512
HALO = 8


def _inproj_kernel(x_ref, xp_ref, xn_ref, w_ref, cw_ref, cb_ref, qkv_ref, hbc_ref, vb_ref):
    tile_in_seq = pl.program_id(0) % (SEQ // INPROJ_TM)
    x_ext = jnp.concatenate([xp_ref[...], x_ref[...], xn_ref[...]], axis=0)
    hb = _dot(x_ext, w_ref[:, 3 * NA_WIDTH:])
    qkv_ref[...] = _dot(x_ref[...], w_ref[:, :3 * NA_WIDTH]).astype(_BF)
    n = hb.shape[0]
    row = lax.broadcasted_iota(jnp.int32, hb.shape, 0)
    pad_row = jnp.logical_or(jnp.logical_and(row == HALO - 1, tile_in_seq == 0),
                             jnp.logical_and(row == HALO + INPROJ_TM, tile_in_seq == SEQ // INPROJ_TM - 1))
    hb = jnp.where(pad_row, 0.0, hb)
    y = (cb_ref[...] + pltpu.roll(hb, 1, axis=0) * cw_ref[0:1, :] + hb * cw_ref[1:2, :]
         + pltpu.roll(hb, n - 1, axis=0) * cw_ref[2:3, :])[HALO:HALO + INPROJ_TM]
    hbc_ref[...] = y
    vb_ref[...] = y[:, :HY_WIDTH].astype(_BF)


def _inproj_conv(x2d, w, conv_w, conv_b):
    m = x2d.shape[0]
    tm = INPROJ_TM
    nq, nh = 3 * NA_WIDTH, (HY_ORDER + 1) * HY_WIDTH
    per_tile = tm // HALO
    last = m // HALO - 1
    return pl.pallas_call(
        _inproj_kernel, grid=(m // tm,),
        in_specs=[pl.BlockSpec((tm, D_MODEL), lambda i: (i, 0)),
                  pl.BlockSpec((HALO, D_MODEL), lambda i: (jnp.maximum(i * per_tile - 1, 0), 0)),
                  pl.BlockSpec((HALO, D_MODEL), lambda i: (jnp.minimum((i + 1) * per_tile, last), 0)),
                  pl.BlockSpec((D_MODEL, AB_IN_WIDTH), lambda i: (0, 0)),
                  pl.BlockSpec((HY_SHORT_K, nh), lambda i: (0, 0)),
                  pl.BlockSpec((1, nh), lambda i: (0, 0))],
        out_specs=[pl.BlockSpec((tm, nq), lambda i: (i, 0)),
                   pl.BlockSpec((tm, nh), lambda i: (i, 0)),
                   pl.BlockSpec((tm, HY_WIDTH), lambda i: (i, 0))],
        out_shape=[jax.ShapeDtypeStruct((m, nq), _BF), jax.ShapeDtypeStruct((m, nh), _F32),
                   jax.ShapeDtypeStruct((m, HY_WIDTH), _BF)],
        compiler_params=_cp(("parallel",), 56), name="ab_inproj_conv")(
            x2d, x2d, x2d, w, conv_w, conv_b.reshape(1, nh))


NA_GROUP = 4
NA_GW = NA_GROUP * NA_HEAD_DIM


def _na_kernel(q_ref, k_ref, v_ref, tbl_ref, o_ref):
    lane_head = lax.broadcasted_iota(jnp.int32, (GRID_W, NA_GW), 1) >> int(math.log2(NA_HEAD_DIM))
    scale = NA_HEAD_DIM ** -0.5 * LOG2E

    def body(r, carry):
        r0 = jnp.clip(r - NA_WIN_ROWS // 2, 0, NA_ROWS - NA_WIN_ROWS)
        off = r0 - r + (NA_WIN_ROWS_MAX - 1)
        qr = q_ref[0, pl.ds(pl.multiple_of(r * GRID_W, GRID_W), GRID_W), :].astype(_F32)
        q4 = jnp.concatenate([jnp.where(lane_head == h, qr, 0.0) for h in range(NA_GROUP)], axis=0)
        ks = k_ref[0, pl.ds(pl.multiple_of(r0 * GRID_W, GRID_W), NA_SLAB), :]
        vs = v_ref[0, pl.ds(pl.multiple_of(r0 * GRID_W, GRID_W), NA_SLAB), :]
        bias = jnp.concatenate([tbl_ref[h, off] for h in range(NA_GROUP)], axis=0)
        s = _dot_nt(q4, ks) * scale + bias
        mx = jnp.max(s, axis=1, keepdims=True)
        p = jnp.exp2(s - mx)
        den = jnp.sum(p, axis=1, keepdims=True)
        o4 = _dot(p, vs) / den
        out = jnp.zeros((GRID_W, NA_GW), _F32)
        for h in range(NA_GROUP):
            out = out + jnp.where(lane_head == h, o4[h * GRID_W:(h + 1) * GRID_W], 0.0)
        o_ref[0, pl.ds(pl.multiple_of(r * GRID_W, GRID_W), GRID_W), :] = out.astype(_BF)
        return carry

    lax.fori_loop(0, NA_ROWS, body, 0, unroll=True)


def _na_bias_table(rpb):
    c = np.arange(GRID_W)
    c0 = np.clip(c - NA_WIN_COLS // 2, 0, GRID_W - NA_WIN_COLS)
    kc = np.arange(GRID_W)
    col_ok = (kc[None, :] >= c0[:, None]) & (kc[None, :] < c0[:, None] + NA_WIN_COLS)
    dc_idx = np.clip(kc[None, :] - c[:, None], -(NA_WIN_COLS - 1), NA_WIN_COLS - 1) + (NA_WIN_COLS - 1)
    pick = (dc_idx[None, :, :] == np.arange(2 * NA_WIN_COLS - 1)[:, None, None]).astype(np.float32)
    per_dr = jnp.einsum("hdk,kqc->hdqc", rpb.astype(_F32), pick, precision=_HP)
    per_dr = jnp.where(col_ok[None, None], per_dr * LOG2E, NEG_INF)
    slabs = jnp.stack([per_dr[:, off:off + NA_WIN_ROWS] for off in range(NA_WIN_ROWS_MAX)], axis=1)
    return slabs.transpose(0, 1, 3, 2, 4).reshape(NA_HEADS, NA_WIN_ROWS_MAX, GRID_W, NA_SLAB)


def _na_attention(qkv, rpb):
    b = qkv.shape[0]
    tbl = _na_bias_table(rpb)
    ng = NA_HEADS // NA_GROUP
    blk = lambda col0: pl.BlockSpec((1, SEQ, NA_GW), lambda i, g, c=col0: (i, 0, c + g))
    return pl.pallas_call(
        _na_kernel, grid=(b, ng),
        in_specs=[blk(0), blk(ng), blk(2 * ng),
                  pl.BlockSpec((NA_GROUP, NA_WIN_ROWS_MAX, GRID_W, NA_SLAB), lambda i, g: (g, 0, 0, 0))],
        out_specs=pl.BlockSpec((1, SEQ, NA_GW), lambda i, g: (i, 0, g)),
        out_shape=jax.ShapeDtypeStruct((b, SEQ, NA_WIDTH), _BF),
        compiler_params=_cp(("parallel", "parallel"), 32), name="na_attention")(qkv, qkv, qkv, tbl)


HY_HID_PAD = LANE
HY_FILT_TC = 256


def _filter_kernel(z_ref, w1_ref, b1_ref, fr_ref, w2_ref, b2_ref, w3f_ref, w3b_ref, dec_ref, k_ref, h_ref):
    @pl.when(jnp.logical_and(pl.program_id(0) == 0, pl.program_id(1) == 0))
    def _():
        fr = fr_ref[...]
        h1 = jnp.sin(fr * (_dot3(z_ref[...], w1_ref[...]) + b1_ref[...]))
        h_ref[...] = jnp.sin(fr * (_dot3(h1, w2_ref[...]) + b2_ref[...]))

    fwd = _dot3(h_ref[:SEQ, :], w3f_ref[...])
    bwd = _dot3(h_ref[SEQ:, :], w3b_ref[...])
    k = jnp.concatenate([fwd, bwd], axis=0) * dec_ref[...]
    ss = jnp.sum(k * k, axis=0, keepdims=True)
    k_ref[...] = (k * lax.rsqrt(ss + 1e-12)).astype(_BF)


@functools.lru_cache(maxsize=None)
def _filter_tables():
    bands = (HY_EMB_DIM - 1) // 2
    t = np.linspace(0.0, 1.0, SEQ)[:, None]
    w = 2.0 * math.pi * np.arange(SEQ)[:, None] / SEQ
    f = np.linspace(1e-4, bands - 1, bands)[None, :]
    z = np.concatenate([t, np.cos(f * w), -np.sin(f * w)], axis=-1)
    min_decay = math.log(HY_DECAY_TARGET) / HY_SLOW_DECAY_PCT
    max_decay = math.log(HY_DECAY_TARGET) / HY_FAST_DECAY_PCT
    deltas = np.abs(np.linspace(min_decay, max_decay, HY_WIDTH))
    dec = np.exp(-t * deltas)
    src = np.concatenate([np.arange(SEQ), [0], np.arange(SEQ - 1, 0, -1)])
    live = np.ones((FFT_N, 1))
    live[SEQ] = 0.0
    z2 = np.pad(z[src], ((0, 0), (0, HY_HID_PAD - HY_EMB_DIM))).astype(np.float32)
    dec2 = (dec[src] * live).astype(np.float32)
    return z2, dec2


def _hyena_filters(w1, b1, freq, w2, b2, w3):
    z2, dec2 = _filter_tables()
    hp = HY_HID_PAD - HY_FILTER_HIDDEN
    w1p = jnp.pad(w1, ((0, HY_HID_PAD - HY_EMB_DIM), (0, hp)))
    w2p = jnp.pad(w2, ((0, hp), (0, hp)))
    w3p = jnp.pad(w3, ((0, hp), (0, 0)))
    row = lambda v: jnp.pad(v, (0, hp)).reshape(1, HY_HID_PAD)
    nc = HY_WIDTH // HY_FILT_TC
    per_order = 2 * nc
    full = lambda shape: pl.BlockSpec(shape, lambda o, j: (0, 0))
    return pl.pallas_call(
        _filter_kernel, grid=(HY_ORDER, nc),
        in_specs=[full((FFT_N, HY_HID_PAD)), full((HY_HID_PAD, HY_HID_PAD)), full((1, HY_HID_PAD)),
                  full((1, HY_HID_PAD)), full((HY_HID_PAD, HY_HID_PAD)), full((1, HY_HID_PAD)),
                  pl.BlockSpec((HY_HID_PAD, HY_FILT_TC), lambda o, j: (0, o * per_order + j)),
                  pl.BlockSpec((HY_HID_PAD, HY_FILT_TC), lambda o, j: (0, o * per_order + nc + j)),
                  pl.BlockSpec((FFT_N, HY_FILT_TC), lambda o, j: (0, j))],
        out_specs=pl.BlockSpec((FFT_N, HY_FILT_TC), lambda o, j: (0, o * nc + j)),
        out_shape=jax.ShapeDtypeStruct((FFT_N, HY_ORDER * HY_WIDTH), _BF),
        scratch_shapes=[pltpu.VMEM((FFT_N, HY_HID_PAD), _F32)],
        compiler_params=_cp(("arbitrary", "arbitrary"), 40), name="hy_filters")(
            z2, w1p, row(b1), row(freq), w2p, row(b2), w3p, w3p, dec2)


@functools.lru_cache(maxsize=None)
def _dft_matrices():
    t = np.arange(SEQ)
    ang = ((t[:, None] * t[None, :]) % FFT_N) * (2.0 * math.pi / FFT_N)
    re = np.cos(ang)
    im = -np.sin(ang)
    im[0] = 1.0 - 2.0 * (t % 2)
    nf = SEQ // HY_FB
    packed = np.stack([re.reshape(nf, HY_FB, SEQ), im.reshape(nf, HY_FB, SEQ)], axis=1).reshape(FFT_N, SEQ)
    return packed.astype(np.float32), np.ascontiguousarray(packed.T).astype(np.float32)


def _kf_kernel(f_ref, k_ref, o_ref):
    i = pl.program_id(0)
    f = f_ref[...]
    tm = f.shape[0]
    p1 = jnp.dot(f, k_ref[:SEQ, :], preferred_element_type=_F32)
    p2 = jnp.dot(f, k_ref[SEQ:, :], preferred_element_type=_F32)
    row = lax.broadcasted_iota(jnp.int32, (tm, 1), 0) + i * tm
    sign = (1 - 2 * (row & 1)).astype(_F32)
    o_ref[...] = p1 + sign * p2


def _filter_spectrum(fmat, kfilt):
    tm = 512
    nw = kfilt.shape[1]
    return pl.pallas_call(
        _kf_kernel, grid=(FFT_N // tm,),
        in_specs=[pl.BlockSpec((tm, SEQ), lambda i: (i, 0)),
                  pl.BlockSpec((FFT_N, nw), lambda i: (0, 0))],
        out_specs=pl.BlockSpec((tm, nw), lambda i: (i, 0)),
        out_shape=jax.ShapeDtypeStruct((FFT_N, nw), _F32),
        compiler_params=_cp(("parallel",), 40), name="hy_filter_spectrum")(fmat, kfilt)


HY_FB = 512
HY_NB = 2


HY_TM = 512


def _hconv_fwd_kernel(zb_ref, f_ref, k_ref, y_ref):
    fk = pl.program_id(1)
    kr = k_ref[:HY_FB, :]
    ki = k_ref[HY_FB:, :]
    row0 = jnp.logical_and(lax.broadcasted_iota(jnp.int32, kr.shape, 0) == 0, fk == 0)
    sc = jnp.where(row0, 1.0 / FFT_N, 2.0 / FFT_N)
    for j in range(HY_NB):
        u = jnp.dot(f_ref[...], zb_ref[j], preferred_element_type=_F32)
        ur = u[:HY_FB]
        ui = u[HY_FB:]
        yr = jnp.where(row0, ur * kr, ur * kr - ui * ki)
        yi = jnp.where(row0, ui * ki, ur * ki + ui * kr)
        y_ref[j, 0, 0] = (yr * sc).astype(_BF)
        y_ref[j, 0, 1] = (yi * sc).astype(_BF)


def _hconv_inv_kernel(ft_ref, y_ref, z_ref, skip_ref, xn_ref, o_ref, ob_ref):
    for j in range(HY_NB):
        conv = jnp.dot(ft_ref[...], y_ref[j], preferred_element_type=_F32)
        out = xn_ref[j] * (conv + z_ref[j] * skip_ref[...])
        o_ref[j] = out
        ob_ref[j] = out.astype(_BF)


def _long_conv_gate(hbc, zsrc, zb, xn_col, order, fmat, fmat_t, kf, skip):
    b = hbc.shape[0]
    nf = SEQ // HY_FB
    w = HY_WIDTH
    nb = HY_NB
    y = pl.pallas_call(
        _hconv_fwd_kernel, grid=(b // nb, nf),
        in_specs=[pl.BlockSpec((nb, SEQ, w), lambda i, f: (i, 0, 0)),
                  pl.BlockSpec((2 * HY_FB, SEQ), lambda i, f: (f, 0)),
                  pl.BlockSpec((2 * HY_FB, w), lambda i, f: (f, order))],
        out_specs=pl.BlockSpec((nb, 1, 2, HY_FB, w), lambda i, f: (i, f, 0, 0, 0)),
        out_shape=jax.ShapeDtypeStruct((b, nf, 2, HY_FB, w), _BF),
        compiler_params=_cp(("parallel", "parallel"), 40), name=f"hy_conv_fwd{order}")(zb, fmat, kf)
    blk = lambda col: pl.BlockSpec((nb, HY_TM, w), lambda i, m, c=col: (i, m, c))
    return pl.pallas_call(
        _hconv_inv_kernel, grid=(b // nb, SEQ // HY_TM),
        in_specs=[pl.BlockSpec((HY_TM, FFT_N), lambda i, m: (m, 0)),
                  pl.BlockSpec((nb, FFT_N, w), lambda i, m: (i, 0, 0)),
                  blk(0),
                  pl.BlockSpec((1, w), lambda i, m: (0, 0)),
                  blk(xn_col)],
        out_specs=[blk(0), blk(0)],
        out_shape=[jax.ShapeDtypeStruct((b, SEQ, w), _F32), jax.ShapeDtypeStruct((b, SEQ, w), _BF)],
        compiler_params=_cp(("parallel", "parallel"), 48), name=f"hy_conv_inv{order}")(
            fmat_t, y.reshape(b, FFT_N, w), zsrc, skip[order].reshape(1, w), hbc)


def _outproj_ln_kernel(n_a, *refs):
    a_refs = refs[:n_a]
    w_refs = refs[n_a:2 * n_a]
    x_ref, g_ref, b_ref, wr_ref, o_ref, ob_ref, aff_ref = refs[2 * n_a:]
    wr = wr_ref[...]
    wr_hi = wr.astype(_BF)
    wr_lo = (wr - wr_hi.astype(_F32)).astype(_BF)
    m = _dot(a_refs[0][...], w_refs[0][...])
    for a_ref, w_ref in zip(a_refs[1:], w_refs[1:]):
        m = m + _dot(a_ref[...], w_ref[...])
    y = _layer_norm(DN_ALPHA * x_ref[...] + m, g_ref[...], b_ref[...])
    o_ref[...] = y
    y_hi = y.astype(_BF)
    ob_ref[...] = y_hi
    y_lo = (y - y_hi.astype(_F32)).astype(_BF)
    by_hi = _dot_nt(jnp.concatenate([wr_hi, wr_lo], axis=0), y_hi)
    logits = by_hi[:N_EXPERTS] + by_hi[N_EXPERTS:] + _dot_nt(wr_hi, y_lo)
    ex = jnp.exp(logits - jnp.max(logits, axis=0, keepdims=True))
    aff_ref[0] = ex / jnp.sum(ex, axis=0, keepdims=True)


def _outproj_ln(a_list, w, x2d, g, b, w_router):
    m = x2d.shape[0]
    tm = 1024
    per_seq = SEQ // tm
    in_specs, w_args, row0 = [], [], 0
    for a in a_list:
        in_specs.append(pl.BlockSpec((tm, a.shape[1]), lambda i: (i, 0)))
    for a in a_list:
        ka = a.shape[1]
        in_specs.append(pl.BlockSpec((ka, D_MODEL), lambda i, r=row0 // ka: (r, 0)))
        w_args.append(w)
        row0 += ka
    in_specs += [pl.BlockSpec((tm, D_MODEL), lambda i: (i, 0)),
                 pl.BlockSpec((1, D_MODEL), lambda i: (0, 0)),
                 pl.BlockSpec((1, D_MODEL), lambda i: (0, 0)),
                 pl.BlockSpec((N_EXPERTS, D_MODEL), lambda i: (0, 0))]
    return pl.pallas_call(
        functools.partial(_outproj_ln_kernel, len(a_list)), grid=(m // tm,),
        in_specs=in_specs,
        out_specs=[pl.BlockSpec((tm, D_MODEL), lambda i: (i, 0)),
                   pl.BlockSpec((tm, D_MODEL), lambda i: (i, 0)),
                   pl.BlockSpec((1, N_EXPERTS, tm), lambda i: (i // per_seq, 0, i % per_seq))],
        out_shape=[jax.ShapeDtypeStruct((m, D_MODEL), _F32), jax.ShapeDtypeStruct((m, D_MODEL), _BF),
                   jax.ShapeDtypeStruct((m // SEQ, N_EXPERTS, SEQ), _F32)],
        compiler_params=_cp(("parallel",), 40), name="outproj_ln_router")(
            *a_list, *w_args, x2d, g.reshape(1, -1), b.reshape(1, -1), w_router.T)


BISECT_FLOOR = 2.0 ** -120
BISECT_GEO = 8
BISECT_LIN = 60


def _select_kernel(a_ref, tri_ref, edge_ref, slot_ref, slot_t_ref, cnt_ref):
    a = a_ref[...]
    rows = a.shape[0]
    cap = float(CAP)

    def probe(mid, lo, hi):
        cnt = jnp.sum(jnp.where(a > mid, 1.0, 0.0), axis=1, keepdims=True)
        ge = cnt >= cap
        return jnp.where(ge, mid, lo), jnp.where(ge, hi, mid)

    def geo(_, c):
        lo, hi = c
        return probe(jnp.where(lo > 0.0, jnp.sqrt(lo * hi), 0.5 * (lo + hi)), lo, hi)

    def lin(_, c):
        lo, hi = c
        return probe(0.5 * (lo + hi), lo, hi)

    c = probe(jnp.full((rows, 1), BISECT_FLOOR, _F32),
              jnp.full((rows, 1), -1.0, _F32), jnp.full((rows, 1), 1.0, _F32))
    c = lax.fori_loop(0, BISECT_GEO, geo, c)
    lo, hi = lax.fori_loop(0, BISECT_LIN, lin, c)
    vstar = jnp.max(jnp.where(a <= hi, a, -1.0), axis=1, keepdims=True)
    gt = a > vstar
    eq = a == vstar
    need = cap - jnp.sum(jnp.where(gt, 1.0, 0.0), axis=1, keepdims=True)
    tri = tri_ref[...]
    eq_before = jnp.dot(jnp.where(eq, 1.0, 0.0).astype(_BF), tri, preferred_element_type=_F32)
    sel = jnp.where(gt, 1.0, jnp.where(eq, jnp.where(eq_before < need, 1.0, 0.0), 0.0))
    pos = jnp.dot(sel.astype(_BF), tri, preferred_element_type=_F32)
    slot = jnp.where(sel > 0.5, pos, -1.0)
    slot_ref[...] = slot.astype(jnp.int32)
    cnt_ref[...] = jnp.dot(sel.astype(_BF), edge_ref[...], preferred_element_type=_F32).astype(jnp.int32)
    if rows < LANE:
        slot = jnp.concatenate([slot, jnp.full((LANE - rows, SEQ), -1.0, _F32)], axis=0)
    slot_tm = slot.T
    for b in range(rows // N_EXPERTS):
        shifted = slot_tm if b == 0 else pltpu.roll(slot_tm, LANE - N_EXPERTS * b, axis=1)
        slot_t_ref[b] = shifted.astype(jnp.int32)


@functools.lru_cache(maxsize=None)
def _prefix_matrix():
    idx = np.arange(SEQ)
    return (idx[:, None] < idx[None, :]).astype(_BF)


WIN_TT = 256
SCAT_TT = 512
WIN_W = 128
WIN_EDGES = 16


@functools.lru_cache(maxsize=None)
def _edge_matrix():
    return (np.arange(SEQ)[:, None] < np.arange(LANE)[None, :] * WIN_TT).astype(_BF)


def _select(aff2d):
    rows = aff2d.shape[0]
    assert rows <= LANE and rows % N_EXPERTS == 0
    bsz = rows // N_EXPERTS
    slot, slot_t, cnt = pl.pallas_call(
        _select_kernel, grid=(1,),
        in_specs=[pl.BlockSpec((rows, SEQ), lambda i: (0, 0)),
                  pl.BlockSpec((SEQ, SEQ), lambda i: (0, 0)),
                  pl.BlockSpec((SEQ, LANE), lambda i: (0, 0))],
        out_specs=[pl.BlockSpec((rows, SEQ), lambda i: (0, 0)),
                   pl.BlockSpec((bsz, SEQ, LANE), lambda i: (0, 0, 0)),
                   pl.BlockSpec((rows, LANE), lambda i: (0, 0))],
        out_shape=[jax.ShapeDtypeStruct((rows, SEQ), jnp.int32),
                   jax.ShapeDtypeStruct((bsz, SEQ, LANE), jnp.int32),
                   jax.ShapeDtypeStruct((rows, LANE), jnp.int32)],
        compiler_params=_cp(("arbitrary",), 48), name="moe_select")(aff2d, _prefix_matrix(), _edge_matrix())
    return slot, slot_t, cnt[:, :WIN_EDGES].reshape(rows * WIN_EDGES)


def _onehot(slot_row):
    return slot_row == lax.broadcasted_iota(jnp.int32, (CAP, SEQ), 0)


GATHER_NE = 4
GATHER_TC = 512
GATHER_W = 64


def _window_start(cnt_ref, seq, e, tile, width, span=1):
    base = (seq * N_EXPERTS + e) * WIN_EDGES + tile * span
    start = jnp.minimum((cnt_ref[base] >> 4) << 4, CAP - width)
    return start, cnt_ref[base + span] - start <= width


def _gather_kernel(cnt_ref, slot_ref, a_ref, xb_ref, xe_ref, g_ref):
    i = pl.program_id(0)
    n_tiles = SEQ // WIN_TT
    wd = GATHER_W
    starts, fits = {}, None
    for t in range(n_tiles):
        for e in range(N_EXPERTS):
            starts[e, t], ok = _window_start(cnt_ref, i, e, t, wd)
            fits = ok if fits is None else jnp.logical_and(fits, ok)

    @pl.when(fits)
    def _():
        xe_ref[...] = jnp.zeros(xe_ref.shape, _BF)
        g_ref[...] = jnp.zeros(g_ref.shape, _F32)
        sub_w = lax.broadcasted_iota(jnp.int32, (wd, WIN_TT), 0)
        for t in range(n_tiles):
            toks = slice(t * WIN_TT, (t + 1) * WIN_TT)
            ps = []
            for e in range(N_EXPERTS):
                hit = slot_ref[0, e, :, toks] - starts[e, t] == sub_w
                ps.append(jnp.where(hit, 1.0, 0.0).astype(_BF))
                rows = pl.ds(pl.multiple_of(starts[e, t], 16), wd)
                g_ref[e, rows, :] += jnp.sum(jnp.where(hit, a_ref[0, e, :, toks], 0.0), axis=1, keepdims=True)
            pcat = jnp.concatenate(ps, axis=0)
            for c in range(D_MODEL // GATHER_TC):
                cols = slice(c * GATHER_TC, (c + 1) * GATHER_TC)
                res = jnp.dot(pcat, xb_ref[0, toks, cols], preferred_element_type=_F32).astype(_BF)
                for e in range(N_EXPERTS):
                    rows = pl.ds(pl.multiple_of(starts[e, t], 16), wd)
                    xe_ref[e, rows, cols] += res[e * wd:(e + 1) * wd]

    @pl.when(jnp.logical_not(fits))
    def _():
        for grp in range(N_EXPERTS // GATHER_NE):
            es = range(grp * GATHER_NE, (grp + 1) * GATHER_NE)
            hits = [_onehot(slot_ref[0, e]) for e in es]
            p = jnp.concatenate([jnp.where(h, 1.0, 0.0).astype(_BF) for h in hits], axis=0)
            xe = jnp.dot(p, xb_ref[0], preferred_element_type=_F32).astype(_BF)
            xe_ref[grp * GATHER_NE:(grp + 1) * GATHER_NE] = xe.reshape(GATHER_NE, CAP, D_MODEL)
            for k, e in enumerate(es):
                g_ref[e] = jnp.sum(jnp.where(hits[k], a_ref[0, e], 0.0), axis=1, keepdims=True)


def _gather(cnt, slot4, aff4, xb3d):
    b = xb3d.shape[0]
    row = pl.BlockSpec((1, N_EXPERTS, 1, SEQ), lambda i, c: (i, 0, 0, 0))
    grid_spec = pltpu.PrefetchScalarGridSpec(
        num_scalar_prefetch=1, grid=(b,),
        in_specs=[row, row, pl.BlockSpec((1, SEQ, D_MODEL), lambda i, c: (i, 0, 0))],
        out_specs=[pl.BlockSpec((N_EXPERTS, CAP, D_MODEL), lambda i, c: (0, i, 0)),
                   pl.BlockSpec((N_EXPERTS, CAP, 1), lambda i, c: (0, i, 0))])
    return pl.pallas_call(
        _gather_kernel, grid_spec=grid_spec,
        out_shape=[jax.ShapeDtypeStruct((N_EXPERTS, b * CAP, D_MODEL), _BF),
                   jax.ShapeDtypeStruct((N_EXPERTS, b * CAP, 1), _F32)],
        compiler_params=_cp(("parallel",), 56), name="moe_gather")(cnt, slot4, aff4, xb3d)


FFN_TF = 1024
FFN_TM = 512


def _ffn_up_kernel(xe_ref, wg_ref, wu_ref, h_ref):
    wg = wg_ref[0].astype(_BF)
    wu = wu_ref[0].astype(_BF)
    for c in range(xe_ref.shape[1] // FFN_TM):
        rows = slice(c * FFN_TM, (c + 1) * FFN_TM)
        xe = xe_ref[0, rows, :]
        hg = _dot(xe, wg)
        hu = _dot(xe, wu)
        h_ref[0, rows, :] = ((hg * (1.0 / (1.0 + jnp.exp(-hg)))) * hu).astype(_BF)


def _ffn_down_kernel(h_ref, wd_ref, g_ref, ye_ref):
    wd = wd_ref[0].astype(_BF)
    for c in range(h_ref.shape[1] // FFN_TM):
        rows = slice(c * FFN_TM, (c + 1) * FFN_TM)
        ye_ref[0, rows, :] = (_dot(h_ref[0, rows, :], wd) * g_ref[0, rows, :]).astype(_BF)


def _expert_ffn(xe, gates, wg, wu, wd, layer):
    rows = xe.shape[1]
    h = pl.pallas_call(
        _ffn_up_kernel, grid=(N_EXPERTS, D_FF_EXPERT // FFN_TF),
        in_specs=[pl.BlockSpec((1, rows, D_MODEL), lambda e, f: (e, 0, 0)),
                  pl.BlockSpec((None, 1, D_MODEL, FFN_TF), lambda e, f: (layer, e, 0, f)),
                  pl.BlockSpec((None, 1, D_MODEL, FFN_TF), lambda e, f: (layer, e, 0, f))],
        out_specs=pl.BlockSpec((1, rows, FFN_TF), lambda e, f: (e, 0, f)),
        out_shape=jax.ShapeDtypeStruct((N_EXPERTS, rows, D_FF_EXPERT), _BF),
        compiler_params=_cp(("parallel", "parallel"), 52), name="moe_ffn_up")(xe, wg, wu)
    return pl.pallas_call(
        _ffn_down_kernel, grid=(N_EXPERTS,),
        in_specs=[pl.BlockSpec((1, rows, D_FF_EXPERT), lambda e: (e, 0, 0)),
                  pl.BlockSpec((None, 1, D_FF_EXPERT, D_MODEL), lambda e: (layer, e, 0, 0)),
                  pl.BlockSpec((1, rows, 1), lambda e: (e, 0, 0))],
        out_specs=pl.BlockSpec((1, rows, D_MODEL), lambda e: (e, 0, 0)),
        out_shape=jax.ShapeDtypeStruct((N_EXPERTS, rows, D_MODEL), _BF),
        compiler_params=_cp(("parallel",), 56), name="moe_ffn_down")(h, wd, gates)


def _scatter_ln_ple_kernel(cnt_ref, slot_t_ref, ye_ref, x_ref, g_ref, b_ref, p_ref, wg_ref, wp_ref, o_ref,
                           pt_ref, yw_ref, f_ref):
    i = pl.program_id(0)
    r = pl.program_id(1)
    slot_t = slot_t_ref[0]
    starts = []
    fits = None
    for e in range(N_EXPERTS):
        start, ok = _window_start(cnt_ref, i, e, r, WIN_W, SCAT_TT // WIN_TT)
        starts.append(start)
        fits = ok if fits is None else jnp.logical_and(fits, ok)

    @pl.when(fits)
    def _():
        lane_w = lax.broadcasted_iota(jnp.int32, (SCAT_TT, WIN_W), 1)
        for e in range(N_EXPERTS):
            cols = slice(e * WIN_W, (e + 1) * WIN_W)
            pt_ref[:, cols] = jnp.where(slot_t[:, e:e + 1] - starts[e] == lane_w, 1.0, 0.0).astype(_BF)
            yw_ref[cols, :] = ye_ref[e, pl.ds(pl.multiple_of(starts[e], 16), WIN_W), :]
        f_ref[...] = jnp.dot(pt_ref[:, :N_EXPERTS * WIN_W], yw_ref[...], preferred_element_type=_F32)

    @pl.when(jnp.logical_not(fits))
    def _():
        lane_c = lax.broadcasted_iota(jnp.int32, (SCAT_TT, CAP), 1)
        for e in range(N_EXPERTS):
            pt_ref[:, e * CAP:(e + 1) * CAP] = jnp.where(slot_t[:, e:e + 1] == lane_c, 1.0, 0.0).astype(_BF)
        f_ref[...] = jnp.dot(pt_ref[...], ye_ref[...].reshape(N_EXPERTS * CAP, D_MODEL),
                             preferred_element_type=_F32)

    y = _layer_norm(DN_ALPHA * x_ref[0] + f_ref[...], g_ref[...], b_ref[...])
    gate = _dot(y, wg_ref[...])
    emb = _dot(p_ref[0], wp_ref[...])
    o_ref[0] = y + (1.0 / (1.0 + jnp.exp(-gate))) * emb


def _scatter_ln_ple(cnt, slot_t, ye, x3d, g, b, p, ple_gate, ple_proj, layer):
    bsz = x3d.shape[0]
    vec = pl.BlockSpec((1, D_MODEL), lambda i, r, c: (0, 0))
    seq = pl.BlockSpec((1, SCAT_TT, D_MODEL), lambda i, r, c: (i, r, 0))
    grid_spec = pltpu.PrefetchScalarGridSpec(
        num_scalar_prefetch=1, grid=(bsz, SEQ // SCAT_TT),
        in_specs=[pl.BlockSpec((1, SCAT_TT, LANE), lambda i, r, c: (i, r, 0)),
                  pl.BlockSpec((N_EXPERTS, CAP, D_MODEL), lambda i, r, c: (0, i, 0)),
                  seq, vec, vec,
                  pl.BlockSpec((None, 1, SCAT_TT, PLE_DIM), lambda i, r, c: (layer, i, r, 0)),
                  pl.BlockSpec((None, D_MODEL, D_MODEL), lambda i, r, c: (layer, 0, 0)),
                  pl.BlockSpec((None, PLE_DIM, D_MODEL), lambda i, r, c: (layer, 0, 0))],
        out_specs=seq,
        scratch_shapes=[pltpu.VMEM((SCAT_TT, N_EXPERTS * CAP), _BF),
                        pltpu.VMEM((N_EXPERTS * WIN_W, D_MODEL), _BF),
                        pltpu.VMEM((SCAT_TT, D_MODEL), _F32)])
    return pl.pallas_call(
        _scatter_ln_ple_kernel, grid_spec=grid_spec,
        out_shape=jax.ShapeDtypeStruct((bsz, SEQ, D_MODEL), _F32),
        compiler_params=_cp(("parallel", "arbitrary"), 56), name="moe_scatter_ln_ple")(
            cnt, slot_t, ye, x3d, g.reshape(1, -1), b.reshape(1, -1), p, ple_gate, ple_proj)


def _moe_ple_block(x1, x1b, aff, wg, wu, wd, g, b, p, ple_gate, ple_proj, layer):
    bsz = x1.shape[0]
    slot, slot_t, cnt = _select(aff.reshape(bsz * N_EXPERTS, SEQ))
    slot4 = slot.reshape(bsz, N_EXPERTS, 1, SEQ)
    aff4 = aff.reshape(bsz, N_EXPERTS, 1, SEQ)
    xe, gates = _gather(cnt, slot4, aff4, x1b)
    ye = _expert_ffn(xe, gates, wg, wu, wd, layer)
    return _scatter_ln_ple(cnt, slot_t, ye, x1, g, b, p, ple_gate, ple_proj, layer)


MLA_HB = LANE
MLA_IN_PAD = 768
MLA_KR_COL = MLA_Q_LORA + MLA_KV_LORA


def _rms(x, g):
    return x * lax.rsqrt(jnp.mean(x * x, axis=-1, keepdims=True) + NORM_EPS) * g


def _mla_pre_kernel(x_ref, win_ref, qg_ref, wq_ref, kvg_ref, wkv_ref, qa_ref, qb_ref, ka_ref, kb_ref,
                    q_ref, kv_ref, kr_ref):
    h = _dot(x_ref[...], win_ref[...])
    q = _dot(_rms(h[:, :MLA_Q_LORA], qg_ref[...]), wq_ref[...])
    swap = pltpu.roll(q, q.shape[1] - MLA_ROPE, axis=1)
    qa = jnp.concatenate([qa_ref[...]] * MLA_HEADS, axis=1)
    qb = jnp.concatenate([qb_ref[...]] * MLA_HEADS, axis=1)
    q_ref[...] = (q * qa + swap * qb).astype(_BF)
    kv_ref[...] = _dot(_rms(h[:, MLA_Q_LORA:MLA_KR_COL], kvg_ref[...]), wkv_ref[...]).astype(_BF)
    kr = h[:, MLA_KR_COL:MLA_KR_COL + MLA_HB]
    kr_ref[...] = (kr * ka_ref[...] + pltpu.roll(kr, MLA_HB - MLA_ROPE, axis=1) * kb_ref[...]).astype(_BF)


@functools.lru_cache(maxsize=None)
def _rope_lane_tables():
    inv = 1.0 / (ROPE_THETA ** (np.arange(0, MLA_ROPE, 2) / MLA_ROPE))
    ang = np.arange(SEQ)[:, None] * inv[None, :]
    cos, sin = np.cos(ang), np.sin(ang)
    ones = np.ones((SEQ, MLA_NOPE))
    zeros_n = np.zeros((SEQ, MLA_NOPE))
    zeros_r = np.zeros((SEQ, MLA_ROPE))
    cc = np.concatenate([cos, cos], axis=1)
    ss = np.concatenate([-sin, sin], axis=1)
    keep_q = np.concatenate([ones, cc, zeros_r], axis=1)
    keep_k = np.concatenate([zeros_n, cc, zeros_r], axis=1)
    swp = np.concatenate([zeros_n, ss, zeros_r], axis=1)
    qscale = (MLA_NOPE + MLA_ROPE) ** -0.5 * LOG2E
    f32 = lambda a: a.astype(np.float32)
    return f32(keep_q * qscale), f32(swp * qscale), f32(keep_k), f32(swp)


def _mla_pre(x2d, w_in, q_norm, w_q_up, kv_norm, w_kv_up):
    m = x2d.shape[0]
    tm = 512
    half = MLA_ROPE // 2
    kr_w = w_in[:, MLA_KR_COL:]
    kr_swapped = jnp.concatenate([kr_w[:, half:], kr_w[:, :half]], axis=1)
    win = jnp.concatenate([w_in[:, :MLA_KR_COL], jnp.zeros((D_MODEL, MLA_NOPE), _F32), kr_w, kr_swapped],
                          axis=1).astype(_BF)
    wq = w_q_up.reshape(MLA_Q_LORA, MLA_HEADS, MLA_NOPE + MLA_ROPE)
    rope_w = wq[:, :, MLA_NOPE:]
    wq = jnp.concatenate([wq, rope_w[:, :, half:], rope_w[:, :, :half]], axis=2).reshape(
        MLA_Q_LORA, MLA_HEADS * MLA_HB).astype(_BF)
    w_kv_up = w_kv_up.astype(_BF)
    q_keep, q_swap, k_keep, k_swap = _rope_lane_tables()
    nq = MLA_HEADS * MLA_HB
    pos = lambda: pl.BlockSpec((tm, MLA_HB), lambda i: (i % (SEQ // tm), 0))
    full = lambda a: pl.BlockSpec(a.shape, lambda i: (0,) * a.ndim)
    qg, kvg = q_norm.reshape(1, -1), kv_norm.reshape(1, -1)
    return pl.pallas_call(
        _mla_pre_kernel, grid=(m // tm,),
        in_specs=[pl.BlockSpec((tm, D_MODEL), lambda i: (i, 0)), full(win), full(qg), full(wq), full(kvg),
                  full(w_kv_up), pos(), pos(), pos(), pos()],
        out_specs=[pl.BlockSpec((tm, nq), lambda i: (i, 0)),
                   pl.BlockSpec((tm, nq), lambda i: (i, 0)),
                   pl.BlockSpec((tm, MLA_HB), lambda i: (i, 0))],
        out_shape=[jax.ShapeDtypeStruct((m, nq), _BF), jax.ShapeDtypeStruct((m, nq), _BF),
                   jax.ShapeDtypeStruct((m, MLA_HB), _BF)],
        compiler_params=_cp(("parallel",), 48), name="mla_pre")(
            x2d, win, qg, wq, kvg, w_kv_up, q_keep, q_swap, k_keep, k_swap)


MLA_TQ = 256
MLA_PAIR = 2


def _mla_attn_kernel(q_ref, kv_ref, kr_ref, o_ref, kt_scr, v_scr, s_scr):
    lane = lax.broadcasted_iota(jnp.int32, (SEQ, MLA_HB), 1)
    kr = kr_ref[0].astype(_F32)
    for j in range(MLA_PAIR):
        kvh = kv_ref[0, :, j * MLA_HB:(j + 1) * MLA_HB].astype(_F32)
        kt_scr[j] = jnp.where(lane < MLA_NOPE, kvh, kr).T.astype(_BF)
        v_scr[j] = jnp.where(lane < MLA_NOPE, 1.0, kvh).astype(_BF)
    lane_q = lax.broadcasted_iota(jnp.int32, (MLA_TQ, MLA_HB), 1)

    def block_rows(i):
        return pl.ds(pl.multiple_of(i * MLA_TQ, MLA_TQ), MLA_TQ)

    def scores(i, slot):
        for j in range(MLA_PAIR):
            s_scr[slot, j] = _dot(q_ref[0, block_rows(i), j * MLA_HB:(j + 1) * MLA_HB], kt_scr[j])

    def finish(i, slot):
        outs = []
        for j in range(MLA_PAIR):
            s = s_scr[slot, j]
            p = jnp.exp2(s - jnp.max(s, axis=1, keepdims=True))
            o = _dot(p, v_scr[j])
            outs.append(o / o[:, 0:1])
        even = pltpu.roll(outs[0], MLA_V, axis=1)
        o_ref[0, block_rows(i), :] = jnp.where(lane_q < MLA_V, even, outs[1]).astype(_BF)

    nblk = SEQ // MLA_TQ
    scores(0, 0)

    def body(k, carry):
        scores(2 * k + 1, 1)
        finish(2 * k, 0)
        scores(2 * k + 2, 0)
        finish(2 * k + 1, 1)
        return carry

    lax.fori_loop(0, nblk // 2 - 1, body, 0)
    scores(nblk - 1, 1)
    finish(nblk - 2, 0)
    finish(nblk - 1, 1)


def _mla_attention(q, kv, kr):
    b = q.shape[0]
    w = MLA_PAIR * MLA_HB
    return pl.pallas_call(
        _mla_attn_kernel, grid=(b, MLA_HEADS // MLA_PAIR),
        in_specs=[pl.BlockSpec((1, SEQ, w), lambda i, h: (i, 0, h)),
                  pl.BlockSpec((1, SEQ, w), lambda i, h: (i, 0, h)),
                  pl.BlockSpec((1, SEQ, MLA_HB), lambda i, h: (i, 0, 0))],
        out_specs=pl.BlockSpec((1, SEQ, MLA_PAIR * MLA_V), lambda i, h: (i, 0, h)),
        out_shape=jax.ShapeDtypeStruct((b, SEQ, MLA_HEADS * MLA_V), _BF),
        scratch_shapes=[pltpu.VMEM((MLA_PAIR, MLA_HB, SEQ), _BF), pltpu.VMEM((MLA_PAIR, SEQ, MLA_HB), _BF),
                        pltpu.VMEM((2, MLA_PAIR, MLA_TQ, SEQ), _F32)],
        compiler_params=_cp(("parallel", "parallel"), 40), name="mla_attention")(q, kv, kr)


def _na_hyena_mixer(x, w_in, rpb, conv_w, conv_b, f_w1, f_b1, f_freq, f_w2, f_b2, f_w3, skip):
    b = x.shape[0]
    qkv, hbc, zb = _inproj_conv(x.reshape(b * SEQ, D_MODEL), w_in, conv_w, conv_b)
    y_a = _na_attention(qkv.reshape(b, SEQ, 3 * NA_WIDTH), rpb)
    hbc = hbc.reshape(b, SEQ, (HY_ORDER + 1) * HY_WIDTH)
    zb = zb.reshape(b, SEQ, HY_WIDTH)
    fmat, fmat_t = (jnp.asarray(f).astype(_BF) for f in _dft_matrices())
    kfilt = _hyena_filters(f_w1, f_b1, f_freq, f_w2, f_b2, f_w3)
    kf = _filter_spectrum(fmat, kfilt)
    z = hbc
    for o in range(HY_ORDER):
        z, zb = _long_conv_gate(hbc, z, zb, o + 1, o, fmat, fmat_t, kf, skip)
    return y_a.reshape(b * SEQ, NA_WIDTH), zb.reshape(b * SEQ, HY_WIDTH)


def kernel(x, p, ab_w_in, na_rpb, hy_conv_w, hy_conv_b, hy_f_w1, hy_f_b1, hy_f_freq, hy_f_w2, hy_f_b2, hy_f_w3, hy_skip, ab_w_out, mla_w_in, mla_q_norm, mla_w_q_up, mla_kv_norm, mla_w_kv_up, mla_w_out, ln1_g, ln1_b, ln2_g, ln2_b, moe_router, moe_w_gate, moe_w_up, moe_w_down, ple_gate, ple_proj):
    b = x.shape[0]
    m = b * SEQ
    ple_gate_b, ple_proj_b = ple_gate.astype(_BF), ple_proj.astype(_BF)
    for i in range(DEPTH):
        j = i // 2
        x2d = x.reshape(m, D_MODEL)
        if i % 2 == 0:
            y_a, z = _na_hyena_mixer(x, ab_w_in[j].astype(_BF), na_rpb[j], hy_conv_w[j], hy_conv_b[j], hy_f_w1[j],
                                     hy_f_b1[j], hy_f_freq[j], hy_f_w2[j], hy_f_b2[j], hy_f_w3[j], hy_skip[j])
            x1, x1b, aff = _outproj_ln([y_a, z], ab_w_out[j].astype(_BF), x2d, ln1_g[i], ln1_b[i], moe_router[i])
        else:
            q, kv, kr = _mla_pre(x2d, mla_w_in[j], mla_q_norm[j], mla_w_q_up[j], mla_kv_norm[j], mla_w_kv_up[j])
            att = _mla_attention(q.reshape(b, SEQ, -1), kv.reshape(b, SEQ, -1), kr.reshape(b, SEQ, -1))
            x1, x1b, aff = _outproj_ln([att.reshape(m, MLA_HEADS * MLA_V)], mla_w_out[j].astype(_BF), x2d,
                                       ln1_g[i], ln1_b[i], moe_router[i])
        x = _moe_ple_block(x1.reshape(b, SEQ, D_MODEL), x1b.reshape(b, SEQ, D_MODEL), aff,
                           moe_w_gate, moe_w_up, moe_w_down, ln2_g[i], ln2_b[i], p, ple_gate_b, ple_proj_b, i)
    return x
```

```python
import functools
import math

import numpy as np
import jax
import jax.numpy as jnp
from jax import lax
from jax.experimental import pallas as pl
from jax.experimental.pallas import tpu as pltpu

D_MODEL = 1024
BATCH = 8
SEQ = 2048
DEPTH = 2
GRID_W = 64
PLE_DIM = 256
NA_HEADS = 8
NA_HEAD_DIM = 64
NA_WIDTH = NA_HEADS * NA_HEAD_DIM
NA_WIN_ROWS_MAX = 8
NA_WIN_COLS = 16
HY_WIDTH = D_MODEL - NA_WIDTH
HY_ORDER = 2
HY_SHORT_K = 3
HY_EMB_DIM = 33
HY_FILTER_HIDDEN = 64
HY_FAST_DECAY_PCT = 0.3
HY_SLOW_DECAY_PCT = 1.5
HY_DECAY_TARGET = 1e-2
AB_IN_WIDTH = 3 * NA_WIDTH + (HY_ORDER + 1) * HY_WIDTH
MLA_HEADS = 16
MLA_Q_LORA = 384
MLA_KV_LORA = 256
MLA_NOPE = 64
MLA_ROPE = 32
MLA_V = 64
ROPE_THETA = 10000.0
N_EXPERTS = 16
EC_CAPACITY_FACTOR = 2
D_FF_EXPERT = 2048
DN_ALPHA = (2 * DEPTH) ** 0.25
NORM_EPS = 1e-5
NEG_INF = -1e30

CAP = EC_CAPACITY_FACTOR * SEQ // N_EXPERTS
NA_ROWS = SEQ // GRID_W
NA_WIN_ROWS = min(NA_WIN_ROWS_MAX, NA_ROWS)
NA_SLAB = NA_WIN_ROWS * GRID_W
FFT_N = 2 * SEQ
LOG2E = math.log2(math.e)

LANE = 128
MIB = 1 << 20

_BF = jnp.bfloat16
_F32 = jnp.float32
_HP = lax.Precision.HIGHEST


def _cp(sem, vmem_mib):
    return pltpu.CompilerParams(dimension_semantics=sem, vmem_limit_bytes=vmem_mib * MIB)


def _dot(a, b):
    return jnp.dot(a.astype(_BF), b.astype(_BF), preferred_element_type=_F32)


def _dot_nt(a, b):
    return lax.dot_general(a.astype(_BF), b.astype(_BF), (((1,), (1,)), ((), ())),
                           preferred_element_type=_F32)


def _dot3(a, b):
    a_hi = a.astype(_BF)
    a_lo = (a - a_hi.astype(_F32)).astype(_BF)
    b_hi = b.astype(_BF)
    b_lo = (b - b_hi.astype(_F32)).astype(_BF)
    mm = lambda u, v: jnp.dot(u, v, preferred_element_type=_F32)
    return mm(a_hi, b_hi) + mm(a_hi, b_lo) + mm(a_lo, b_hi)


def _layer_norm(y, g, b):
    mu = jnp.mean(y, axis=-1, keepdims=True)
    d = y - mu
    var = jnp.mean(d * d, axis=-1, keepdims=True)
    return d * lax.rsqrt(var + NORM_EPS) * g + b


INPROJ_TM = 512
HALO = 8


def _inproj_kernel(x_ref, xp_ref, xn_ref, w_ref, cw_ref, cb_ref, qkv_ref, hbc_ref, vb_ref):
    tile_in_seq = pl.program_id(0) % (SEQ // INPROJ_TM)
    x_ext = jnp.concatenate([xp_ref[...], x_ref[...], xn_ref[...]], axis=0)
    hb = _dot(x_ext, w_ref[:, 3 * NA_WIDTH:])
    qkv_ref[...] = _dot(x_ref[...], w_ref[:, :3 * NA_WIDTH]).astype(_BF)
    n = hb.shape[0]
    row = lax.broadcasted_iota(jnp.int32, hb.shape, 0)
    pad_row = jnp.logical_or(jnp.logical_and(row == HALO - 1, tile_in_seq == 0),
                             jnp.logical_and(row == HALO + INPROJ_TM, tile_in_seq == SEQ // INPROJ_TM - 1))
    hb = jnp.where(pad_row, 0.0, hb)
    y = (cb_ref[...] + pltpu.roll(hb, 1, axis=0) * cw_ref[0:1, :] + hb * cw_ref[1:2, :]
         + pltpu.roll(hb, n - 1, axis=0) * cw_ref[2:3, :])[HALO:HALO + INPROJ_TM]
    hbc_ref[...] = y
    vb_ref[...] = y[:, :HY_WIDTH].astype(_BF)


def _inproj_conv(x2d, w, conv_w, conv_b):
    m = x2d.shape[0]
    tm = INPROJ_TM
    nq, nh = 3 * NA_WIDTH, (HY_ORDER + 1) * HY_WIDTH
    per_tile = tm // HALO
    last = m // HALO - 1
    return pl.pallas_call(
        _inproj_kernel, grid=(m // tm,),
        in_specs=[pl.BlockSpec((tm, D_MODEL), lambda i: (i, 0)),
                  pl.BlockSpec((HALO, D_MODEL), lambda i: (jnp.maximum(i * per_tile - 1, 0), 0)),
                  pl.BlockSpec((HALO, D_MODEL), lambda i: (jnp.minimum((i + 1) * per_tile, last), 0)),
                  pl.BlockSpec((D_MODEL, AB_IN_WIDTH), lambda i: (0, 0)),
                  pl.BlockSpec((HY_SHORT_K, nh), lambda i: (0, 0)),
                  pl.BlockSpec((1, nh), lambda i: (0, 0))],
        out_specs=[pl.BlockSpec((tm, nq), lambda i: (i, 0)),
                   pl.BlockSpec((tm, nh), lambda i: (i, 0)),
                   pl.BlockSpec((tm, HY_WIDTH), lambda i: (i, 0))],
        out_shape=[jax.ShapeDtypeStruct((m, nq), _BF), jax.ShapeDtypeStruct((m, nh), _F32),
                   jax.ShapeDtypeStruct((m, HY_WIDTH), _BF)],
        compiler_params=_cp(("parallel",), 56), name="ab_inproj_conv")(
            x2d, x2d, x2d, w, conv_w, conv_b.reshape(1, nh))


NA_GROUP = 4
NA_GW = NA_GROUP * NA_HEAD_DIM


def _na_kernel(q_ref, k_ref, v_ref, tbl_ref, o_ref):
    lane_head = lax.broadcasted_iota(jnp.int32, (GRID_W, NA_GW), 1) >> int(math.log2(NA_HEAD_DIM))
    scale = NA_HEAD_DIM ** -0.5 * LOG2E

    def body(r, carry):
        r0 = jnp.clip(r - NA_WIN_ROWS // 2, 0, NA_ROWS - NA_WIN_ROWS)
        off = r0 - r + (NA_WIN_ROWS_MAX - 1)
        qr = q_ref[0, pl.ds(pl.multiple_of(r * GRID_W, GRID_W), GRID_W), :].astype(_F32)
        q4 = jnp.concatenate([jnp.where(lane_head == h, qr, 0.0) for h in range(NA_GROUP)], axis=0)
        ks = k_ref[0, pl.ds(pl.multiple_of(r0 * GRID_W, GRID_W), NA_SLAB), :]
        vs = v_ref[0, pl.ds(pl.multiple_of(r0 * GRID_W, GRID_W), NA_SLAB), :]
        bias = jnp.concatenate([tbl_ref[h, off] for h in range(NA_GROUP)], axis=0)
        s = _dot_nt(q4, ks) * scale + bias
        mx = jnp.max(s, axis=1, keepdims=True)
        p = jnp.exp2(s - mx)
        den = jnp.sum(p, axis=1, keepdims=True)
        o4 = _dot(p, vs) / den
        out = jnp.zeros((GRID_W, NA_GW), _F32)
        for h in range(NA_GROUP):
            out = out + jnp.where(lane_head == h, o4[h * GRID_W:(h + 1) * GRID_W], 0.0)
        o_ref[0, pl.ds(pl.multiple_of(r * GRID_W, GRID_W), GRID_W), :] = out.astype(_BF)
        return carry

    lax.fori_loop(0, NA_ROWS, body, 0, unroll=True)


def _na_bias_table(rpb):
    c = np.arange(GRID_W)
    c0 = np.clip(c - NA_WIN_COLS // 2, 0, GRID_W - NA_WIN_COLS)
    kc = np.arange(GRID_W)
    col_ok = (kc[None, :] >= c0[:, None]) & (kc[None, :] < c0[:, None] + NA_WIN_COLS)
    dc_idx = np.clip(kc[None, :] - c[:, None], -(NA_WIN_COLS - 1), NA_WIN_COLS - 1) + (NA_WIN_COLS - 1)
    pick = (dc_idx[None, :, :] == np.arange(2 * NA_WIN_COLS - 1)[:, None, None]).astype(np.float32)
    per_dr = jnp.einsum("hdk,kqc->hdqc", rpb.astype(_F32), pick, precision=_HP)
    per_dr = jnp.where(col_ok[None, None], per_dr * LOG2E, NEG_INF)
    slabs = jnp.stack([per_dr[:, off:off + NA_WIN_ROWS] for off in range(NA_WIN_ROWS_MAX)], axis=1)
    return slabs.transpose(0, 1, 3, 2, 4).reshape(NA_HEADS, NA_WIN_ROWS_MAX, GRID_W, NA_SLAB)


def _na_attention(qkv, rpb):
    b = qkv.shape[0]
    tbl = _na_bias_table(rpb)
    ng = NA_HEADS // NA_GROUP
    blk = lambda col0: pl.BlockSpec((1, SEQ, NA_GW), lambda i, g, c=col0: (i, 0, c + g))
    return pl.pallas_call(
        _na_kernel, grid=(b, ng),
        in_specs=[blk(0), blk(ng), blk(2 * ng),
                  pl.BlockSpec((NA_GROUP, NA_WIN_ROWS_MAX, GRID_W, NA_SLAB), lambda i, g: (g, 0, 0, 0))],
        out_specs=pl.BlockSpec((1, SEQ, NA_GW), lambda i, g: (i, 0, g)),
        out_shape=jax.ShapeDtypeStruct((b, SEQ, NA_WIDTH), _BF),
        compiler_params=_cp(("parallel", "parallel"), 32), name="na_attention")(qkv, qkv, qkv, tbl)


HY_HID_PAD = LANE
HY_FILT_TC = 256


def _filter_kernel(z_ref, w1_ref, b1_ref, fr_ref, w2_ref, b2_ref, w3f_ref, w3b_ref, dec_ref, k_ref, h_ref):
    @pl.when(jnp.logical_and(pl.program_id(0) == 0, pl.program_id(1) == 0))
    def _():
        fr = fr_ref[...]
        h1 = jnp.sin(fr * (_dot3(z_ref[...], w1_ref[...]) + b1_ref[...]))
        h_ref[...] = jnp.sin(fr * (_dot3(h1, w2_ref[...]) + b2_ref[...]))

    fwd = _dot3(h_ref[:SEQ, :], w3f_ref[...])
    bwd = _dot3(h_ref[SEQ:, :], w3b_ref[...])
    k = jnp.concatenate([fwd, bwd], axis=0) * dec_ref[...]
    ss = jnp.sum(k * k, axis=0, keepdims=True)
    k_ref[...] = (k * lax.rsqrt(ss + 1e-12)).astype(_BF)


@functools.lru_cache(maxsize=None)
def _filter_tables():
    bands = (HY_EMB_DIM - 1) // 2
    t = np.linspace(0.0, 1.0, SEQ)[:, None]
    w = 2.0 * math.pi * np.arange(SEQ)[:, None] / SEQ
    f = np.linspace(1e-4, bands - 1, bands)[None, :]
    z = np.concatenate([t, np.cos(f * w), -np.sin(f * w)], axis=-1)
    min_decay = math.log(HY_DECAY_TARGET) / HY_SLOW_DECAY_PCT
    max_decay = math.log(HY_DECAY_TARGET) / HY_FAST_DECAY_PCT
    deltas = np.abs(np.linspace(min_decay, max_decay, HY_WIDTH))
    dec = np.exp(-t * deltas)
    src = np.concatenate([np.arange(SEQ), [0], np.arange(SEQ - 1, 0, -1)])
    live = np.ones((FFT_N, 1))
    live[SEQ] = 0.0
    z2 = np.pad(z[src], ((0, 0), (0, HY_HID_PAD - HY_EMB_DIM))).astype(np.float32)
    dec2 = (dec[src] * live).astype(np.float32)
    return z2, dec2


def _hyena_filters(w1, b1, freq, w2, b2, w3):
    z2, dec2 = _filter_tables()
    hp = HY_HID_PAD - HY_FILTER_HIDDEN
    w1p = jnp.pad(w1, ((0, HY_HID_PAD - HY_EMB_DIM), (0, hp)))
    w2p = jnp.pad(w2, ((0, hp), (0, hp)))
    w3p = jnp.pad(w3, ((0, hp), (0, 0)))
    row = lambda v: jnp.pad(v, (0, hp)).reshape(1, HY_HID_PAD)
    nc = HY_WIDTH // HY_FILT_TC
    per_order = 2 * nc
    full = lambda shape: pl.BlockSpec(shape, lambda o, j: (0, 0))
    return pl.pallas_call(
        _filter_kernel, grid=(HY_ORDER, nc),
        in_specs=[full((FFT_N, HY_HID_PAD)), full((HY_HID_PAD, HY_HID_PAD)), full((1, HY_HID_PAD)),
                  full((1, HY_HID_PAD)), full((HY_HID_PAD, HY_HID_PAD)), full((1, HY_HID_PAD)),
                  pl.BlockSpec((HY_HID_PAD, HY_FILT_TC), lambda o, j: (0, o * per_order + j)),
                  pl.BlockSpec((HY_HID_PAD, HY_FILT_TC), lambda o, j: (0, o * per_order + nc + j)),
                  pl.BlockSpec((FFT_N, HY_FILT_TC), lambda o, j: (0, j))],
        out_specs=pl.BlockSpec((FFT_N, HY_FILT_TC), lambda o, j: (0, o * nc + j)),
        out_shape=jax.ShapeDtypeStruct((FFT_N, HY_ORDER * HY_WIDTH), _BF),
        scratch_shapes=[pltpu.VMEM((FFT_N, HY_HID_PAD), _F32)],
        compiler_params=_cp(("arbitrary", "arbitrary"), 40), name="hy_filters")(
            z2, w1p, row(b1), row(freq), w2p, row(b2), w3p, w3p, dec2)


@functools.lru_cache(maxsize=None)
def _dft_matrices():
    t = np.arange(SEQ)
    ang = ((t[:, None] * t[None, :]) % FFT_N) * (2.0 * math.pi / FFT_N)
    re = np.cos(ang)
    im = -np.sin(ang)
    im[0] = 1.0 - 2.0 * (t % 2)
    nf = SEQ // HY_FB
    packed = np.stack([re.reshape(nf, HY_FB, SEQ), im.reshape(nf, HY_FB, SEQ)], axis=1).reshape(FFT_N, SEQ)
    return packed.astype(np.float32), np.ascontiguousarray(packed.T).astype(np.float32)


def _kf_kernel(f_ref, k_ref, o_ref):
    i = pl.program_id(0)
    f = f_ref[...]
    tm = f.shape[0]
    p1 = jnp.dot(f, k_ref[:SEQ, :], preferred_element_type=_F32)
    p2 = jnp.dot(f, k_ref[SEQ:, :], preferred_element_type=_F32)
    row = lax.broadcasted_iota(jnp.int32, (tm, 1), 0) + i * tm
    sign = (1 - 2 * (row & 1)).astype(_F32)
    o_ref[...] = p1 + sign * p2


def _filter_spectrum(fmat, kfilt):
    tm = 512
    nw = kfilt.shape[1]
    return pl.pallas_call(
        _kf_kernel, grid=(FFT_N // tm,),
        in_specs=[pl.BlockSpec((tm, SEQ), lambda i: (i, 0)),
                  pl.BlockSpec((FFT_N, nw), lambda i: (0, 0))],
        out_specs=pl.BlockSpec((tm, nw), lambda i: (i, 0)),
        out_shape=jax.ShapeDtypeStruct((FFT_N, nw), _F32),
        compiler_params=_cp(("parallel",), 40), name="hy_filter_spectrum")(fmat, kfilt)


HY_FB = 512
HY_NB = 2
HY_NB_FWD = 4


HY_TM = 512


def _hconv_fwd_kernel(zb_ref, f_ref, k_ref, y_ref):
    fk = pl.program_id(1)
    kr = k_ref[:HY_FB, :]
    ki = k_ref[HY_FB:, :]
    row0 = jnp.logical_and(lax.broadcasted_iota(jnp.int32, kr.shape, 0) == 0, fk == 0)
    sc = jnp.where(row0, 1.0 / FFT_N, 2.0 / FFT_N)
    for j in range(zb_ref.shape[0]):
        u = jnp.dot(f_ref[...], zb_ref[j], preferred_element_type=_F32)
        ur = u[:HY_FB]
        ui = u[HY_FB:]
        yr = jnp.where(row0, ur * kr, ur * kr - ui * ki)
        yi = jnp.where(row0, ui * ki, ur * ki + ui * kr)
        y_ref[j, 0, 0] = (yr * sc).astype(_BF)
        y_ref[j, 0, 1] = (yi * sc).astype(_BF)


def _hconv_inv_kernel(ft_ref, y_ref, z_ref, skip_ref, xn_ref, o_ref, ob_ref):
    for j in range(HY_NB):
        conv = jnp.dot(ft_ref[...], y_ref[j], preferred_element_type=_F32)
        out = xn_ref[j] * (conv + z_ref[j] * skip_ref[...])
        o_ref[j] = out
        ob_ref[j] = out.astype(_BF)


def _long_conv_gate(hbc, zsrc, zb, xn_col, order, fmat, fmat_t, kf, skip):
    b = hbc.shape[0]
    nf = SEQ // HY_FB
    w = HY_WIDTH
    nb = HY_NB
    nbf = HY_NB_FWD
    y = pl.pallas_call(
        _hconv_fwd_kernel, grid=(b // nbf, nf),
        in_specs=[pl.BlockSpec((nbf, SEQ, w), lambda i, f: (i, 0, 0)),
                  pl.BlockSpec((2 * HY_FB, SEQ), lambda i, f: (f, 0)),
                  pl.BlockSpec((2 * HY_FB, w), lambda i, f: (f, order))],
        out_specs=pl.BlockSpec((nbf, 1, 2, HY_FB, w), lambda i, f: (i, f, 0, 0, 0)),
        out_shape=jax.ShapeDtypeStruct((b, nf, 2, HY_FB, w), _BF),
        compiler_params=_cp(("parallel", "parallel"), 48), name=f"hy_conv_fwd{order}")(zb, fmat, kf)
    blk = lambda col: pl.BlockSpec((nb, HY_TM, w), lambda i, m, c=col: (i, m, c))
    return pl.pallas_call(
        _hconv_inv_kernel, grid=(b // nb, SEQ // HY_TM),
        in_specs=[pl.BlockSpec((HY_TM, FFT_N), lambda i, m: (m, 0)),
                  pl.BlockSpec((nb, FFT_N, w), lambda i, m: (i, 0, 0)),
                  blk(0),
                  pl.BlockSpec((1, w), lambda i, m: (0, 0)),
                  blk(xn_col)],
        out_specs=[blk(0), blk(0)],
        out_shape=[jax.ShapeDtypeStruct((b, SEQ, w), _F32), jax.ShapeDtypeStruct((b, SEQ, w), _BF)],
        compiler_params=_cp(("parallel", "parallel"), 48), name=f"hy_conv_inv{order}")(
            fmat_t, y.reshape(b, FFT_N, w), zsrc, skip[order].reshape(1, w), hbc)


def _outproj_ln_kernel(n_a, *refs):
    a_refs = refs[:n_a]
    w_refs = refs[n_a:2 * n_a]
    x_ref, g_ref, b_ref, wr_ref, o_ref, ob_ref, aff_ref = refs[2 * n_a:]
    wr = wr_ref[...]
    wr_hi = wr.astype(_BF)
    wr_lo = (wr - wr_hi.astype(_F32)).astype(_BF)
    m = _dot(a_refs[0][...], w_refs[0][...])
    for a_ref, w_ref in zip(a_refs[1:], w_refs[1:]):
        m = m + _dot(a_ref[...], w_ref[...])
    y = _layer_norm(DN_ALPHA * x_ref[...] + m, g_ref[...], b_ref[...])
    o_ref[...] = y
    y_hi = y.astype(_BF)
    ob_ref[...] = y_hi
    y_lo = (y - y_hi.astype(_F32)).astype(_BF)
    by_hi = _dot_nt(jnp.concatenate([wr_hi, wr_lo], axis=0), y_hi)
    logits = by_hi[:N_EXPERTS] + by_hi[N_EXPERTS:] + _dot_nt(wr_hi, y_lo)
    ex = jnp.exp(logits - jnp.max(logits, axis=0, keepdims=True))
    aff_ref[0] = ex / jnp.sum(ex, axis=0, keepdims=True)


def _outproj_ln(a_list, w, x2d, g, b, w_router):
    m = x2d.shape[0]
    tm = 1024
    per_seq = SEQ // tm
    in_specs, w_args, row0 = [], [], 0
    for a in a_list:
        in_specs.append(pl.BlockSpec((tm, a.shape[1]), lambda i: (i, 0)))
    for a in a_list:
        ka = a.shape[1]
        in_specs.append(pl.BlockSpec((ka, D_MODEL), lambda i, r=row0 // ka: (r, 0)))
        w_args.append(w)
        row0 += ka
    in_specs += [pl.BlockSpec((tm, D_MODEL), lambda i: (i, 0)),
                 pl.BlockSpec((1, D_MODEL), lambda i: (0, 0)),
                 pl.BlockSpec((1, D_MODEL), lambda i: (0, 0)),
                 pl.BlockSpec((N_EXPERTS, D_MODEL), lambda i: (0, 0))]
    return pl.pallas_call(
        functools.partial(_outproj_ln_kernel, len(a_list)), grid=(m // tm,),
        in_specs=in_specs,
        out_specs=[pl.BlockSpec((tm, D_MODEL), lambda i: (i, 0)),
                   pl.BlockSpec((tm, D_MODEL), lambda i: (i, 0)),
                   pl.BlockSpec((1, N_EXPERTS, tm), lambda i: (i // per_seq, 0, i % per_seq))],
        out_shape=[jax.ShapeDtypeStruct((m, D_MODEL), _F32), jax.ShapeDtypeStruct((m, D_MODEL), _BF),
                   jax.ShapeDtypeStruct((m // SEQ, N_EXPERTS, SEQ), _F32)],
        compiler_params=_cp(("parallel",), 40), name="outproj_ln_router")(
            *a_list, *w_args, x2d, g.reshape(1, -1), b.reshape(1, -1), w_router.T)


BISECT_FLOOR = 2.0 ** -120
BISECT_GEO = 8
BISECT_LIN = 60


def _select_kernel(a_ref, tri_ref, edge_ref, slot_ref, slot_t_ref, cnt_ref):
    a = a_ref[...]
    rows = a.shape[0]
    cap = float(CAP)

    def probe(mid, lo, hi):
        cnt = jnp.sum(jnp.where(a > mid, 1.0, 0.0), axis=1, keepdims=True)
        ge = cnt >= cap
        return jnp.where(ge, mid, lo), jnp.where(ge, hi, mid)

    def geo(_, c):
        lo, hi = c
        return probe(jnp.where(lo > 0.0, jnp.sqrt(lo * hi), 0.5 * (lo + hi)), lo, hi)

    def lin(_, c):
        lo, hi = c
        return probe(0.5 * (lo + hi), lo, hi)

    c = probe(jnp.full((rows, 1), BISECT_FLOOR, _F32),
              jnp.full((rows, 1), -1.0, _F32), jnp.full((rows, 1), 1.0, _F32))
    c = lax.fori_loop(0, BISECT_GEO, geo, c)
    lo, hi = lax.fori_loop(0, BISECT_LIN, lin, c)
    vstar = jnp.max(jnp.where(a <= hi, a, -1.0), axis=1, keepdims=True)
    gt = a > vstar
    eq = a == vstar
    need = cap - jnp.sum(jnp.where(gt, 1.0, 0.0), axis=1, keepdims=True)
    tri = tri_ref[...]
    eq_before = jnp.dot(jnp.where(eq, 1.0, 0.0).astype(_BF), tri, preferred_element_type=_F32)
    sel = jnp.where(gt, 1.0, jnp.where(eq, jnp.where(eq_before < need, 1.0, 0.0), 0.0))
    pos = jnp.dot(sel.astype(_BF), tri, preferred_element_type=_F32)
    slot = jnp.where(sel > 0.5, pos, -1.0)
    slot_ref[...] = slot.astype(jnp.int32)
    cnt_ref[...] = jnp.dot(sel.astype(_BF), edge_ref[...], preferred_element_type=_F32).astype(jnp.int32)
    if rows < LANE:
        slot = jnp.concatenate([slot, jnp.full((LANE - rows, SEQ), -1.0, _F32)], axis=0)
    slot_tm = slot.T
    for b in range(rows // N_EXPERTS):
        shifted = slot_tm if b == 0 else pltpu.roll(slot_tm, LANE - N_EXPERTS * b, axis=1)
        slot_t_ref[b] = shifted.astype(jnp.int32)


@functools.lru_cache(maxsize=None)
def _prefix_matrix():
    idx = np.arange(SEQ)
    return (idx[:, None] < idx[None, :]).astype(_BF)


WIN_TT = 256
SCAT_TT = 512
WIN_W = 128
WIN_EDGES = 16


@functools.lru_cache(maxsize=None)
def _edge_matrix():
    return (np.arange(SEQ)[:, None] < np.arange(LANE)[None, :] * WIN_TT).astype(_BF)


def _select(aff2d):
    rows = aff2d.shape[0]
    assert rows <= LANE and rows % N_EXPERTS == 0
    bsz = rows // N_EXPERTS
    slot, slot_t, cnt = pl.pallas_call(
        _select_kernel, grid=(1,),
        in_specs=[pl.BlockSpec((rows, SEQ), lambda i: (0, 0)),
                  pl.BlockSpec((SEQ, SEQ), lambda i: (0, 0)),
                  pl.BlockSpec((SEQ, LANE), lambda i: (0, 0))],
        out_specs=[pl.BlockSpec((rows, SEQ), lambda i: (0, 0)),
                   pl.BlockSpec((bsz, SEQ, LANE), lambda i: (0, 0, 0)),
                   pl.BlockSpec((rows, LANE), lambda i: (0, 0))],
        out_shape=[jax.ShapeDtypeStruct((rows, SEQ), jnp.int32),
                   jax.ShapeDtypeStruct((bsz, SEQ, LANE), jnp.int32),
                   jax.ShapeDtypeStruct((rows, LANE), jnp.int32)],
        compiler_params=_cp(("arbitrary",), 48), name="moe_select")(aff2d, _prefix_matrix(), _edge_matrix())
    return slot, slot_t, cnt[:, :WIN_EDGES].reshape(rows * WIN_EDGES)


def _onehot(slot_row):
    return slot_row == lax.broadcasted_iota(jnp.int32, (CAP, SEQ), 0)


GATHER_NE = 4
GATHER_TC = 512
GATHER_W = 64


def _window_start(cnt_ref, seq, e, tile, width, span=1):
    base = (seq * N_EXPERTS + e) * WIN_EDGES + tile * span
    start = jnp.minimum((cnt_ref[base] >> 4) << 4, CAP - width)
    return start, cnt_ref[base + span] - start <= width


def _gather_kernel(cnt_ref, slot_ref, a_ref, xb_ref, xe_ref, g_ref):
    i = pl.program_id(0)
    n_tiles = SEQ // WIN_TT
    wd = GATHER_W
    starts, fits = {}, None
    for t in range(n_tiles):
        for e in range(N_EXPERTS):
            starts[e, t], ok = _window_start(cnt_ref, i, e, t, wd)
            fits = ok if fits is None else jnp.logical_and(fits, ok)

    @pl.when(fits)
    def _():
        xe_ref[...] = jnp.zeros(xe_ref.shape, _BF)
        g_ref[...] = jnp.zeros(g_ref.shape, _F32)
        sub_w = lax.broadcasted_iota(jnp.int32, (wd, WIN_TT), 0)
        for t in range(n_tiles):
            toks = slice(t * WIN_TT, (t + 1) * WIN_TT)
            ps = []
            for e in range(N_EXPERTS):
                hit = slot_ref[0, e, :, toks] - starts[e, t] == sub_w
                ps.append(jnp.where(hit, 1.0, 0.0).astype(_BF))
                rows = pl.ds(pl.multiple_of(starts[e, t], 16), wd)
                g_ref[e, rows, :] += jnp.sum(jnp.where(hit, a_ref[0, e, :, toks], 0.0), axis=1, keepdims=True)
            pcat = jnp.concatenate(ps, axis=0)
            for c in range(D_MODEL // GATHER_TC):
                cols = slice(c * GATHER_TC, (c + 1) * GATHER_TC)
                res = jnp.dot(pcat, xb_ref[0, toks, cols], preferred_element_type=_F32).astype(_BF)
                for e in range(N_EXPERTS):
                    rows = pl.ds(pl.multiple_of(starts[e, t], 16), wd)
                    xe_ref[e, rows, cols] += res[e * wd:(e + 1) * wd]

    @pl.when(jnp.logical_not(fits))
    def _():
        for grp in range(N_EXPERTS // GATHER_NE):
            es = range(grp * GATHER_NE, (grp + 1) * GATHER_NE)
            hits = [_onehot(slot_ref[0, e]) for e in es]
            p = jnp.concatenate([jnp.where(h, 1.0, 0.0).astype(_BF) for h in hits], axis=0)
            xe = jnp.dot(p, xb_ref[0], preferred_element_type=_F32).astype(_BF)
            xe_ref[grp * GATHER_NE:(grp + 1) * GATHER_NE] = xe.reshape(GATHER_NE, CAP, D_MODEL)
            for k, e in enumerate(es):
                g_ref[e] = jnp.sum(jnp.where(hits[k], a_ref[0, e], 0.0), axis=1, keepdims=True)


def _gather(cnt, slot4, aff4, xb3d):
    b = xb3d.shape[0]
    row = pl.BlockSpec((1, N_EXPERTS, 1, SEQ), lambda i, c: (i, 0, 0, 0))
    grid_spec = pltpu.PrefetchScalarGridSpec(
        num_scalar_prefetch=1, grid=(b,),
        in_specs=[row, row, pl.BlockSpec((1, SEQ, D_MODEL), lambda i, c: (i, 0, 0))],
        out_specs=[pl.BlockSpec((N_EXPERTS, CAP, D_MODEL), lambda i, c: (0, i, 0)),
                   pl.BlockSpec((N_EXPERTS, CAP, 1), lambda i, c: (0, i, 0))])
    return pl.pallas_call(
        _gather_kernel, grid_spec=grid_spec,
        out_shape=[jax.ShapeDtypeStruct((N_EXPERTS, b * CAP, D_MODEL), _BF),
                   jax.ShapeDtypeStruct((N_EXPERTS, b * CAP, 1), _F32)],
        compiler_params=_cp(("parallel",), 56), name="moe_gather")(cnt, slot4, aff4, xb3d)


FFN_TF = 1024
FFN_TM = 512


def _ffn_up_kernel(xe_ref, wg_ref, wu_ref, h_ref):
    wg = wg_ref[0].astype(_BF)
    wu = wu_ref[0].astype(_BF)
    for c in range(xe_ref.shape[1] // FFN_TM):
        rows = slice(c * FFN_TM, (c + 1) * FFN_TM)
        xe = xe_ref[0, rows, :]
        hg = _dot(xe, wg)
        hu = _dot(xe, wu)
        h_ref[0, rows, :] = ((hg * (1.0 / (1.0 + jnp.exp(-hg)))) * hu).astype(_BF)


def _ffn_down_kernel(h_ref, wd_ref, g_ref, ye_ref):
    wd = wd_ref[0].astype(_BF)
    for c in range(h_ref.shape[1] // FFN_TM):
        rows = slice(c * FFN_TM, (c + 1) * FFN_TM)
        ye_ref[0, rows, :] = (_dot(h_ref[0, rows, :], wd) * g_ref[0, rows, :]).astype(_BF)


def _expert_ffn(xe, gates, wg, wu, wd, layer):
    rows = xe.shape[1]
    h = pl.pallas_call(
        _ffn_up_kernel, grid=(N_EXPERTS, D_FF_EXPERT // FFN_TF),
        in_specs=[pl.BlockSpec((1, rows, D_MODEL), lambda e, f: (e, 0, 0)),
                  pl.BlockSpec((None, 1, D_MODEL, FFN_TF), lambda e, f: (layer, e, 0, f)),
                  pl.BlockSpec((None, 1, D_MODEL, FFN_TF), lambda e, f: (layer, e, 0, f))],
        out_specs=pl.BlockSpec((1, rows, FFN_TF), lambda e, f: (e, 0, f)),
        out_shape=jax.ShapeDtypeStruct((N_EXPERTS, rows, D_FF_EXPERT), _BF),
        compiler_params=_cp(("parallel", "parallel"), 52), name="moe_ffn_up")(xe, wg, wu)
    return pl.pallas_call(
        _ffn_down_kernel, grid=(N_EXPERTS,),
        in_specs=[pl.BlockSpec((1, rows, D_FF_EXPERT), lambda e: (e, 0, 0)),
                  pl.BlockSpec((None, 1, D_FF_EXPERT, D_MODEL), lambda e: (layer, e, 0, 0)),
                  pl.BlockSpec((1, rows, 1), lambda e: (e, 0, 0))],
        out_specs=pl.BlockSpec((1, rows, D_MODEL), lambda e: (e, 0, 0)),
        out_shape=jax.ShapeDtypeStruct((N_EXPERTS, rows, D_MODEL), _BF),
        compiler_params=_cp(("parallel",), 56), name="moe_ffn_down")(h, wd, gates)


def _scatter_ln_ple_kernel(cnt_ref, slot_t_ref, ye_ref, x_ref, g_ref, b_ref, p_ref, wg_ref, wp_ref, o_ref,
                           pt_ref, yw_ref, f_ref):
    i = pl.program_id(0)
    r = pl.program_id(1)
    slot_t = slot_t_ref[0]
    starts = []
    fits = None
    for e in range(N_EXPERTS):
        start, ok = _window_start(cnt_ref, i, e, r, WIN_W, SCAT_TT // WIN_TT)
        starts.append(start)
        fits = ok if fits is None else jnp.logical_and(fits, ok)

    @pl.when(fits)
    def _():
        lane_w = lax.broadcasted_iota(jnp.int32, (SCAT_TT, WIN_W), 1)
        for e in range(N_EXPERTS):
            cols = slice(e * WIN_W, (e + 1) * WIN_W)
            pt_ref[:, cols] = jnp.where(slot_t[:, e:e + 1] - starts[e] == lane_w, 1.0, 0.0).astype(_BF)
            yw_ref[cols, :] = ye_ref[e, pl.ds(pl.multiple_of(starts[e], 16), WIN_W), :]
        f_ref[...] = jnp.dot(pt_ref[:, :N_EXPERTS * WIN_W], yw_ref[...], preferred_element_type=_F32)

    @pl.when(jnp.logical_not(fits))
    def _():
        lane_c = lax.broadcasted_iota(jnp.int32, (SCAT_TT, CAP), 1)
        for e in range(N_EXPERTS):
            pt_ref[:, e * CAP:(e + 1) * CAP] = jnp.where(slot_t[:, e:e + 1] == lane_c, 1.0, 0.0).astype(_BF)
        f_ref[...] = jnp.dot(pt_ref[...], ye_ref[...].reshape(N_EXPERTS * CAP, D_MODEL),
                             preferred_element_type=_F32)

    y = _layer_norm(DN_ALPHA * x_ref[0] + f_ref[...], g_ref[...], b_ref[...])
    gate = _dot(y, wg_ref[...])
    emb = _dot(p_ref[0], wp_ref[...])
    o_ref[0] = y + (1.0 / (1.0 + jnp.exp(-gate))) * emb


def _scatter_ln_ple(cnt, slot_t, ye, x3d, g, b, p, ple_gate, ple_proj, layer):
    bsz = x3d.shape[0]
    vec = pl.BlockSpec((1, D_MODEL), lambda i, r, c: (0, 0))
    seq = pl.BlockSpec((1, SCAT_TT, D_MODEL), lambda i, r, c: (i, r, 0))
    grid_spec = pltpu.PrefetchScalarGridSpec(
        num_scalar_prefetch=1, grid=(bsz, SEQ // SCAT_TT),
        in_specs=[pl.BlockSpec((1, SCAT_TT, LANE), lambda i, r, c: (i, r, 0)),
                  pl.BlockSpec((N_EXPERTS, CAP, D_MODEL), lambda i, r, c: (0, i, 0)),
                  seq, vec, vec,
                  pl.BlockSpec((None, 1, SCAT_TT, PLE_DIM), lambda i, r, c: (layer, i, r, 0)),
                  pl.BlockSpec((None, D_MODEL, D_MODEL), lambda i, r, c: (layer, 0, 0)),
                  pl.BlockSpec((None, PLE_DIM, D_MODEL), lambda i, r, c: (layer, 0, 0))],
        out_specs=seq,
        scratch_shapes=[pltpu.VMEM((SCAT_TT, N_EXPERTS * CAP), _BF),
                        pltpu.VMEM((N_EXPERTS * WIN_W, D_MODEL), _BF),
                        pltpu.VMEM((SCAT_TT, D_MODEL), _F32)])
    return pl.pallas_call(
        _scatter_ln_ple_kernel, grid_spec=grid_spec,
        out_shape=jax.ShapeDtypeStruct((bsz, SEQ, D_MODEL), _F32),
        compiler_params=_cp(("parallel", "arbitrary"), 56), name="moe_scatter_ln_ple")(
            cnt, slot_t, ye, x3d, g.reshape(1, -1), b.reshape(1, -1), p, ple_gate, ple_proj)


def _moe_ple_block(x1, x1b, aff, wg, wu, wd, g, b, p, ple_gate, ple_proj, layer):
    bsz = x1.shape[0]
    slot, slot_t, cnt = _select(aff.reshape(bsz * N_EXPERTS, SEQ))
    slot4 = slot.reshape(bsz, N_EXPERTS, 1, SEQ)
    aff4 = aff.reshape(bsz, N_EXPERTS, 1, SEQ)
    xe, gates = _gather(cnt, slot4, aff4, x1b)
    ye = _expert_ffn(xe, gates, wg, wu, wd, layer)
    return _scatter_ln_ple(cnt, slot_t, ye, x1, g, b, p, ple_gate, ple_proj, layer)


MLA_HB = LANE
MLA_IN_PAD = 768
MLA_KR_COL = MLA_Q_LORA + MLA_KV_LORA


def _rms(x, g):
    return x * lax.rsqrt(jnp.mean(x * x, axis=-1, keepdims=True) + NORM_EPS) * g


def _mla_pre_kernel(x_ref, win_ref, qg_ref, wq_ref, kvg_ref, wkv_ref, qa_ref, qb_ref, ka_ref, kb_ref,
                    q_ref, kv_ref, kr_ref):
    h = _dot(x_ref[...], win_ref[...])
    q = _dot(_rms(h[:, :MLA_Q_LORA], qg_ref[...]), wq_ref[...])
    swap = pltpu.roll(q, q.shape[1] - MLA_ROPE, axis=1)
    qa = jnp.concatenate([qa_ref[...]] * MLA_HEADS, axis=1)
    qb = jnp.concatenate([qb_ref[...]] * MLA_HEADS, axis=1)
    q_ref[...] = (q * qa + swap * qb).astype(_BF)
    kv_ref[...] = _dot(_rms(h[:, MLA_Q_LORA:MLA_KR_COL], kvg_ref[...]), wkv_ref[...]).astype(_BF)
    kr = h[:, MLA_KR_COL:MLA_KR_COL + MLA_HB]
    kr_ref[...] = (kr * ka_ref[...] + pltpu.roll(kr, MLA_HB - MLA_ROPE, axis=1) * kb_ref[...]).astype(_BF)


@functools.lru_cache(maxsize=None)
def _rope_lane_tables():
    inv = 1.0 / (ROPE_THETA ** (np.arange(0, MLA_ROPE, 2) / MLA_ROPE))
    ang = np.arange(SEQ)[:, None] * inv[None, :]
    cos, sin = np.cos(ang), np.sin(ang)
    ones = np.ones((SEQ, MLA_NOPE))
    zeros_n = np.zeros((SEQ, MLA_NOPE))
    zeros_r = np.zeros((SEQ, MLA_ROPE))
    cc = np.concatenate([cos, cos], axis=1)
    ss = np.concatenate([-sin, sin], axis=1)
    keep_q = np.concatenate([ones, cc, zeros_r], axis=1)
    keep_k = np.concatenate([zeros_n, cc, zeros_r], axis=1)
    swp = np.concatenate([zeros_n, ss, zeros_r], axis=1)
    qscale = (MLA_NOPE + MLA_ROPE) ** -0.5 * LOG2E
    f32 = lambda a: a.astype(np.float32)
    return f32(keep_q * qscale), f32(swp * qscale), f32(keep_k), f32(swp)


def _mla_pre(x2d, w_in, q_norm, w_q_up, kv_norm, w_kv_up):
    m = x2d.shape[0]
    tm = 512
    half = MLA_ROPE // 2
    kr_w = w_in[:, MLA_KR_COL:]
    kr_swapped = jnp.concatenate([kr_w[:, half:], kr_w[:, :half]], axis=1)
    win = jnp.concatenate([w_in[:, :MLA_KR_COL], jnp.zeros((D_MODEL, MLA_NOPE), _F32), kr_w, kr_swapped],
                          axis=1).astype(_BF)
    wq = w_q_up.reshape(MLA_Q_LORA, MLA_HEADS, MLA_NOPE + MLA_ROPE)
    rope_w = wq[:, :, MLA_NOPE:]
    wq = jnp.concatenate([wq, rope_w[:, :, half:], rope_w[:, :, :half]], axis=2).reshape(
        MLA_Q_LORA, MLA_HEADS * MLA_HB).astype(_BF)
    w_kv_up = w_kv_up.astype(_BF)
    q_keep, q_swap, k_keep, k_swap = _rope_lane_tables()
    nq = MLA_HEADS * MLA_HB
    pos = lambda: pl.BlockSpec((tm, MLA_HB), lambda i: (i % (SEQ // tm), 0))
    full = lambda a: pl.BlockSpec(a.shape, lambda i: (0,) * a.ndim)
    qg, kvg = q_norm.reshape(1, -1), kv_norm.reshape(1, -1)
    return pl.pallas_call(
        _mla_pre_kernel, grid=(m // tm,),
        in_specs=[pl.BlockSpec((tm, D_MODEL), lambda i: (i, 0)), full(win), full(qg), full(wq), full(kvg),
                  full(w_kv_up), pos(), pos(), pos(), pos()],
        out_specs=[pl.BlockSpec((tm, nq), lambda i: (i, 0)),
                   pl.BlockSpec((tm, nq), lambda i: (i, 0)),
                   pl.BlockSpec((tm, MLA_HB), lambda i: (i, 0))],
        out_shape=[jax.ShapeDtypeStruct((m, nq), _BF), jax.ShapeDtypeStruct((m, nq), _BF),
                   jax.ShapeDtypeStruct((m, MLA_HB), _BF)],
        compiler_params=_cp(("parallel",), 48), name="mla_pre")(
            x2d, win, qg, wq, kvg, w_kv_up, q_keep, q_swap, k_keep, k_swap)


MLA_TQ = 256
MLA_PAIR = 2


def _mla_attn_kernel(q_ref, kv_ref, kr_ref, o_ref, kt_scr, v_scr, s_scr):
    lane = lax.broadcasted_iota(jnp.int32, (SEQ, MLA_HB), 1)
    kr = kr_ref[0].astype(_F32)
    for j in range(MLA_PAIR):
        kvh = kv_ref[0, :, j * MLA_HB:(j + 1) * MLA_HB].astype(_F32)
        kt_scr[j] = jnp.where(lane < MLA_NOPE, kvh, kr).T.astype(_BF)
        v_scr[j] = jnp.where(lane < MLA_NOPE, 1.0, kvh).astype(_BF)
    lane_q = lax.broadcasted_iota(jnp.int32, (MLA_TQ, MLA_HB), 1)

    def block_rows(i):
        return pl.ds(pl.multiple_of(i * MLA_TQ, MLA_TQ), MLA_TQ)

    def scores(i, slot):
        for j in range(MLA_PAIR):
            s_scr[slot, j] = _dot(q_ref[0, block_rows(i), j * MLA_HB:(j + 1) * MLA_HB], kt_scr[j])

    def finish(i, slot):
        outs = []
        for j in range(MLA_PAIR):
            s = s_scr[slot, j]
            p = jnp.exp2(s - jnp.max(s, axis=1, keepdims=True))
            o = _dot(p, v_scr[j])
            outs.append(o / o[:, 0:1])
        even = pltpu.roll(outs[0], MLA_V, axis=1)
        o_ref[0, block_rows(i), :] = jnp.where(lane_q < MLA_V, even, outs[1]).astype(_BF)

    nblk = SEQ // MLA_TQ
    scores(0, 0)

    def body(k, carry):
        scores(2 * k + 1, 1)
        finish(2 * k, 0)
        scores(2 * k + 2, 0)
        finish(2 * k + 1, 1)
        return carry

    lax.fori_loop(0, nblk // 2 - 1, body, 0)
    scores(nblk - 1, 1)
    finish(nblk - 2, 0)
    finish(nblk - 1, 1)


def _mla_attention(q, kv, kr):
    b = q.shape[0]
    w = MLA_PAIR * MLA_HB
    return pl.pallas_call(
        _mla_attn_kernel, grid=(b, MLA_HEADS // MLA_PAIR),
        in_specs=[pl.BlockSpec((1, SEQ, w), lambda i, h: (i, 0, h)),
                  pl.BlockSpec((1, SEQ, w), lambda i, h: (i, 0, h)),
                  pl.BlockSpec((1, SEQ, MLA_HB), lambda i, h: (i, 0, 0))],
        out_specs=pl.BlockSpec((1, SEQ, MLA_PAIR * MLA_V), lambda i, h: (i, 0, h)),
        out_shape=jax.ShapeDtypeStruct((b, SEQ, MLA_HEADS * MLA_V), _BF),
        scratch_shapes=[pltpu.VMEM((MLA_PAIR, MLA_HB, SEQ), _BF), pltpu.VMEM((MLA_PAIR, SEQ, MLA_HB), _BF),
                        pltpu.VMEM((2, MLA_PAIR, MLA_TQ, SEQ), _F32)],
        compiler_params=_cp(("parallel", "parallel"), 40), name="mla_attention")(q, kv, kr)


def _na_hyena_mixer(x, w_in, rpb, conv_w, conv_b, f_w1, f_b1, f_freq, f_w2, f_b2, f_w3, skip):
    b = x.shape[0]
    qkv, hbc, zb = _inproj_conv(x.reshape(b * SEQ, D_MODEL), w_in, conv_w, conv_b)
    y_a = _na_attention(qkv.reshape(b, SEQ, 3 * NA_WIDTH), rpb)
    hbc = hbc.reshape(b, SEQ, (HY_ORDER + 1) * HY_WIDTH)
    zb = zb.reshape(b, SEQ, HY_WIDTH)
    fmat, fmat_t = (jnp.asarray(f).astype(_BF) for f in _dft_matrices())
    kfilt = _hyena_filters(f_w1, f_b1, f_freq, f_w2, f_b2, f_w3)
    kf = _filter_spectrum(fmat, kfilt)
    z = hbc
    for o in range(HY_ORDER):
        z, zb = _long_conv_gate(hbc, z, zb, o + 1, o, fmat, fmat_t, kf, skip)
    return y_a.reshape(b * SEQ, NA_WIDTH), zb.reshape(b * SEQ, HY_WIDTH)


def kernel(x, p, ab_w_in, na_rpb, hy_conv_w, hy_conv_b, hy_f_w1, hy_f_b1, hy_f_freq, hy_f_w2, hy_f_b2, hy_f_w3, hy_skip, ab_w_out, mla_w_in, mla_q_norm, mla_w_q_up, mla_kv_norm, mla_w_kv_up, mla_w_out, ln1_g, ln1_b, ln2_g, ln2_b, moe_router, moe_w_gate, moe_w_up, moe_w_down, ple_gate, ple_proj):
    b = x.shape[0]
    m = b * SEQ
    ple_gate_b, ple_proj_b = ple_gate.astype(_BF), ple_proj.astype(_BF)
    for i in range(DEPTH):
        j = i // 2
        x2d = x.reshape(m, D_MODEL)
        if i % 2 == 0:
            y_a, z = _na_hyena_mixer(x, ab_w_in[j].astype(_BF), na_rpb[j], hy_conv_w[j], hy_conv_b[j], hy_f_w1[j],
                                     hy_f_b1[j], hy_f_freq[j], hy_f_w2[j], hy_f_b2[j], hy_f_w3[j], hy_skip[j])
            x1, x1b, aff = _outproj_ln([y_a, z], ab_w_out[j].astype(_BF), x2d, ln1_g[i], ln1_b[i], moe_router[i])
        else:
            q, kv, kr = _mla_pre(x2d, mla_w_in[j], mla_q_norm[j], mla_w_q_up[j], mla_kv_norm[j], mla_w_kv_up[j])
            att = _mla_attention(q.reshape(b, SEQ, -1), kv.reshape(b, SEQ, -1), kr.reshape(b, SEQ, -1))
            x1, x1b, aff = _outproj_ln([att.reshape(m, MLA_HEADS * MLA_V)], mla_w_out[j].astype(_BF), x2d,
                                       ln1_g[i], ln1_b[i], moe_router[i])
        x = _moe_ple_block(x1.reshape(b, SEQ, D_MODEL), x1b.reshape(b, SEQ, D_MODEL), aff,
                           moe_w_gate, moe_w_up, moe_w_down, ln2_g[i], ln2_b[i], p, ple_gate_b, ple_proj_b, i)
    return x
```

```python
import functools
import math

import numpy as np
import jax
import jax.numpy as jnp
from jax import lax
from jax.experimental import pallas as pl
from jax.experimental.pallas import tpu as pltpu

D_MODEL = 1024
BATCH = 8
SEQ = 2048
DEPTH = 2
GRID_W = 64
PLE_DIM = 256
NA_HEADS = 8
NA_HEAD_DIM = 64
NA_WIDTH = NA_HEADS * NA_HEAD_DIM
NA_WIN_ROWS_MAX = 8
NA_WIN_COLS = 16
HY_WIDTH = D_MODEL - NA_WIDTH
HY_ORDER = 2
HY_SHORT_K = 3
HY_EMB_DIM = 33
HY_FILTER_HIDDEN = 64
HY_FAST_DECAY_PCT = 0.3
HY_SLOW_DECAY_PCT = 1.5
HY_DECAY_TARGET = 1e-2
AB_IN_WIDTH = 3 * NA_WIDTH + (HY_ORDER + 1) * HY_WIDTH
MLA_HEADS = 16
MLA_Q_LORA = 384
MLA_KV_LORA = 256
MLA_NOPE = 64
MLA_ROPE = 32
MLA_V = 64
ROPE_THETA = 10000.0
N_EXPERTS = 16
EC_CAPACITY_FACTOR = 2
D_FF_EXPERT = 2048
DN_ALPHA = (2 * DEPTH) ** 0.25
NORM_EPS = 1e-5
NEG_INF = -1e30

CAP = EC_CAPACITY_FACTOR * SEQ // N_EXPERTS
NA_ROWS = SEQ // GRID_W
NA_WIN_ROWS = min(NA_WIN_ROWS_MAX, NA_ROWS)
NA_SLAB = NA_WIN_ROWS * GRID_W
FFT_N = 2 * SEQ
LOG2E = math.log2(math.e)

LANE = 128
MIB = 1 << 20

_BF = jnp.bfloat16
_F32 = jnp.float32
_HP = lax.Precision.HIGHEST


def _cp(sem, vmem_mib):
    return pltpu.CompilerParams(dimension_semantics=sem, vmem_limit_bytes=vmem_mib * MIB)


def _dot(a, b):
    return jnp.dot(a.astype(_BF), b.astype(_BF), preferred_element_type=_F32)


def _dot_nt(a, b):
    return lax.dot_general(a.astype(_BF), b.astype(_BF), (((1,), (1,)), ((), ())),
                           preferred_element_type=_F32)


def _dot3(a, b):
    a_hi = a.astype(_BF)
    a_lo = (a - a_hi.astype(_F32)).astype(_BF)
    b_hi = b.astype(_BF)
    b_lo = (b - b_hi.astype(_F32)).astype(_BF)
    mm = lambda u, v: jnp.dot(u, v, preferred_element_type=_F32)
    return mm(a_hi, b_hi) + mm(a_hi, b_lo) + mm(a_lo, b_hi)


def _layer_norm(y, g, b):
    mu = jnp.mean(y, axis=-1, keepdims=True)
    d = y - mu
    var = jnp.mean(d * d, axis=-1, keepdims=True)
    return d * lax.rsqrt(var + NORM_EPS) * g + b


INPROJ_TM = 512
HALO = 8


def _inproj_kernel(x_ref, xp_ref, xn_ref, w_ref, cw_ref, cb_ref, qkv_ref, hbc_ref, vb_ref):
    tile_in_seq = pl.program_id(0) % (SEQ // INPROJ_TM)
    x_ext = jnp.concatenate([xp_ref[...], x_ref[...], xn_ref[...]], axis=0)
    hb = _dot(x_ext, w_ref[:, 3 * NA_WIDTH:])
    qkv_ref[...] = _dot(x_ref[...], w_ref[:, :3 * NA_WIDTH]).astype(_BF)
    n = hb.shape[0]
    row = lax.broadcasted_iota(jnp.int32, hb.shape, 0)
    pad_row = jnp.logical_or(jnp.logical_and(row == HALO - 1, tile_in_seq == 0),
                             jnp.logical_and(row == HALO + INPROJ_TM, tile_in_seq == SEQ // INPROJ_TM - 1))
    hb = jnp.where(pad_row, 0.0, hb)
    y = (cb_ref[...] + pltpu.roll(hb, 1, axis=0) * cw_ref[0:1, :] + hb * cw_ref[1:2, :]
         + pltpu.roll(hb, n - 1, axis=0) * cw_ref[2:3, :])[HALO:HALO + INPROJ_TM]
    hbc_ref[...] = y
    vb_ref[...] = y[:, :HY_WIDTH].astype(_BF)


def _inproj_conv(x2d, w, conv_w, conv_b):
    m = x2d.shape[0]
    tm = INPROJ_TM
    nq, nh = 3 * NA_WIDTH, (HY_ORDER + 1) * HY_WIDTH
    per_tile = tm // HALO
    last = m // HALO - 1
    return pl.pallas_call(
        _inproj_kernel, grid=(m // tm,),
        in_specs=[pl.BlockSpec((tm, D_MODEL), lambda i: (i, 0)),
                  pl.BlockSpec((HALO, D_MODEL), lambda i: (jnp.maximum(i * per_tile - 1, 0), 0)),
                  pl.BlockSpec((HALO, D_MODEL), lambda i: (jnp.minimum((i + 1) * per_tile, last), 0)),
                  pl.BlockSpec((D_MODEL, AB_IN_WIDTH), lambda i: (0, 0)),
                  pl.BlockSpec((HY_SHORT_K, nh), lambda i: (0, 0)),
                  pl.BlockSpec((1, nh), lambda i: (0, 0))],
        out_specs=[pl.BlockSpec((tm, nq), lambda i: (i, 0)),
                   pl.BlockSpec((tm, nh), lambda i: (i, 0)),
                   pl.BlockSpec((tm, HY_WIDTH), lambda i: (i, 0))],
        out_shape=[jax.ShapeDtypeStruct((m, nq), _BF), jax.ShapeDtypeStruct((m, nh), _F32),
                   jax.ShapeDtypeStruct((m, HY_WIDTH), _BF)],
        compiler_params=_cp(("parallel",), 56), name="ab_inproj_conv")(
            x2d, x2d, x2d, w, conv_w, conv_b.reshape(1, nh))


NA_GROUP = 4
NA_GW = NA_GROUP * NA_HEAD_DIM


def _na_kernel(q_ref, k_ref, v_ref, tbl_ref, o_ref):
    lane_head = lax.broadcasted_iota(jnp.int32, (GRID_W, NA_GW), 1) >> int(math.log2(NA_HEAD_DIM))
    scale = NA_HEAD_DIM ** -0.5 * LOG2E

    def body(r, carry):
        r0 = jnp.clip(r - NA_WIN_ROWS // 2, 0, NA_ROWS - NA_WIN_ROWS)
        off = r0 - r + (NA_WIN_ROWS_MAX - 1)
        qr = q_ref[0, pl.ds(pl.multiple_of(r * GRID_W, GRID_W), GRID_W), :].astype(_F32)
        q4 = jnp.concatenate([jnp.where(lane_head == h, qr, 0.0) for h in range(NA_GROUP)], axis=0)
        ks = k_ref[0, pl.ds(pl.multiple_of(r0 * GRID_W, GRID_W), NA_SLAB), :]
        vs = v_ref[0, pl.ds(pl.multiple_of(r0 * GRID_W, GRID_W), NA_SLAB), :]
        bias = jnp.concatenate([tbl_ref[h, off] for h in range(NA_GROUP)], axis=0)
        s = _dot_nt(q4, ks) * scale + bias
        mx = jnp.max(s, axis=1, keepdims=True)
        p = jnp.exp2(s - mx)
        den = jnp.sum(p, axis=1, keepdims=True)
        o4 = _dot(p, vs) / den
        out = jnp.zeros((GRID_W, NA_GW), _F32)
        for h in range(NA_GROUP):
            out = out + jnp.where(lane_head == h, o4[h * GRID_W:(h + 1) * GRID_W], 0.0)
        o_ref[0, pl.ds(pl.multiple_of(r * GRID_W, GRID_W), GRID_W), :] = out.astype(_BF)
        return carry

    lax.fori_loop(0, NA_ROWS, body, 0, unroll=True)


def _na_bias_table(rpb):
    c = np.arange(GRID_W)
    c0 = np.clip(c - NA_WIN_COLS // 2, 0, GRID_W - NA_WIN_COLS)
    kc = np.arange(GRID_W)
    col_ok = (kc[None, :] >= c0[:, None]) & (kc[None, :] < c0[:, None] + NA_WIN_COLS)
    dc_idx = np.clip(kc[None, :] - c[:, None], -(NA_WIN_COLS - 1), NA_WIN_COLS - 1) + (NA_WIN_COLS - 1)
    pick = (dc_idx[None, :, :] == np.arange(2 * NA_WIN_COLS - 1)[:, None, None]).astype(np.float32)
    per_dr = jnp.einsum("hdk,kqc->hdqc", rpb.astype(_F32), pick, precision=_HP)
    per_dr = jnp.where(col_ok[None, None], per_dr * LOG2E, NEG_INF)
    slabs = jnp.stack([per_dr[:, off:off + NA_WIN_ROWS] for off in range(NA_WIN_ROWS_MAX)], axis=1)
    return slabs.transpose(0, 1, 3, 2, 4).reshape(NA_HEADS, NA_WIN_ROWS_MAX, GRID_W, NA_SLAB)


def _na_attention(qkv, rpb):
    b = qkv.shape[0]
    tbl = _na_bias_table(rpb)
    ng = NA_HEADS // NA_GROUP
    blk = lambda col0: pl.BlockSpec((1, SEQ, NA_GW), lambda i, g, c=col0: (i, 0, c + g))
    return pl.pallas_call(
        _na_kernel, grid=(b, ng),
        in_specs=[blk(0), blk(ng), blk(2 * ng),
                  pl.BlockSpec((NA_GROUP, NA_WIN_ROWS_MAX, GRID_W, NA_SLAB), lambda i, g: (g, 0, 0, 0))],
        out_specs=pl.BlockSpec((1, SEQ, NA_GW), lambda i, g: (i, 0, g)),
        out_shape=jax.ShapeDtypeStruct((b, SEQ, NA_WIDTH), _BF),
        compiler_params=_cp(("parallel", "parallel"), 32), name="na_attention")(qkv, qkv, qkv, tbl)


HY_HID_PAD = LANE
HY_FILT_TC = 256


def _filter_kernel(z_ref, w1_ref, b1_ref, fr_ref, w2_ref, b2_ref, w3f_ref, w3b_ref, dec_ref, k_ref, h_ref):
    @pl.when(jnp.logical_and(pl.program_id(0) == 0, pl.program_id(1) == 0))
    def _():
        fr = fr_ref[...]
        h1 = jnp.sin(fr * (_dot3(z_ref[...], w1_ref[...]) + b1_ref[...]))
        h_ref[...] = jnp.sin(fr * (_dot3(h1, w2_ref[...]) + b2_ref[...]))

    fwd = _dot3(h_ref[:SEQ, :], w3f_ref[...])
    bwd = _dot3(h_ref[SEQ:, :], w3b_ref[...])
    k = jnp.concatenate([fwd, bwd], axis=0) * dec_ref[...]
    ss = jnp.sum(k * k, axis=0, keepdims=True)
    k_ref[...] = (k * lax.rsqrt(ss + 1e-12)).astype(_BF)


@functools.lru_cache(maxsize=None)
def _filter_tables():
    bands = (HY_EMB_DIM - 1) // 2
    t = np.linspace(0.0, 1.0, SEQ)[:, None]
    w = 2.0 * math.pi * np.arange(SEQ)[:, None] / SEQ
    f = np.linspace(1e-4, bands - 1, bands)[None, :]
    z = np.concatenate([t, np.cos(f * w), -np.sin(f * w)], axis=-1)
    min_decay = math.log(HY_DECAY_TARGET) / HY_SLOW_DECAY_PCT
    max_decay = math.log(HY_DECAY_TARGET) / HY_FAST_DECAY_PCT
    deltas = np.abs(np.linspace(min_decay, max_decay, HY_WIDTH))
    dec = np.exp(-t * deltas)
    src = np.concatenate([np.arange(SEQ), [0], np.arange(SEQ - 1, 0, -1)])
    live = np.ones((FFT_N, 1))
    live[SEQ] = 0.0
    z2 = np.pad(z[src], ((0, 0), (0, HY_HID_PAD - HY_EMB_DIM))).astype(np.float32)
    dec2 = (dec[src] * live).astype(np.float32)
    return z2, dec2


def _hyena_filters(w1, b1, freq, w2, b2, w3):
    z2, dec2 = _filter_tables()
    hp = HY_HID_PAD - HY_FILTER_HIDDEN
    w1p = jnp.pad(w1, ((0, HY_HID_PAD - HY_EMB_DIM), (0, hp)))
    w2p = jnp.pad(w2, ((0, hp), (0, hp)))
    w3p = jnp.pad(w3, ((0, hp), (0, 0)))
    row = lambda v: jnp.pad(v, (0, hp)).reshape(1, HY_HID_PAD)
    nc = HY_WIDTH // HY_FILT_TC
    per_order = 2 * nc
    full = lambda shape: pl.BlockSpec(shape, lambda o, j: (0, 0))
    return pl.pallas_call(
        _filter_kernel, grid=(HY_ORDER, nc),
        in_specs=[full((FFT_N, HY_HID_PAD)), full((HY_HID_PAD, HY_HID_PAD)), full((1, HY_HID_PAD)),
                  full((1, HY_HID_PAD)), full((HY_HID_PAD, HY_HID_PAD)), full((1, HY_HID_PAD)),
                  pl.BlockSpec((HY_HID_PAD, HY_FILT_TC), lambda o, j: (0, o * per_order + j)),
                  pl.BlockSpec((HY_HID_PAD, HY_FILT_TC), lambda o, j: (0, o * per_order + nc + j)),
                  pl.BlockSpec((FFT_N, HY_FILT_TC), lambda o, j: (0, j))],
        out_specs=pl.BlockSpec((FFT_N, HY_FILT_TC), lambda o, j: (0, o * nc + j)),
        out_shape=jax.ShapeDtypeStruct((FFT_N, HY_ORDER * HY_WIDTH), _BF),
        scratch_shapes=[pltpu.VMEM((FFT_N, HY_HID_PAD), _F32)],
        compiler_params=_cp(("arbitrary", "arbitrary"), 40), name="hy_filters")(
            z2, w1p, row(b1), row(freq), w2p, row(b2), w3p, w3p, dec2)


@functools.lru_cache(maxsize=None)
def _dft_matrices():
    t = np.arange(SEQ)
    ang = ((t[:, None] * t[None, :]) % FFT_N) * (2.0 * math.pi / FFT_N)
    re = np.cos(ang)
    im = -np.sin(ang)
    im[0] = 1.0 - 2.0 * (t % 2)
    nf = SEQ // HY_FB
    packed = np.stack([re.reshape(nf, HY_FB, SEQ), im.reshape(nf, HY_FB, SEQ)], axis=1).reshape(FFT_N, SEQ)
    return packed.astype(np.float32), np.ascontiguousarray(packed.T).astype(np.float32)


def _kf_kernel(f_ref, k_ref, o_ref):
    i = pl.program_id(0)
    f = f_ref[...]
    tm = f.shape[0]
    p1 = jnp.dot(f, k_ref[:SEQ, :], preferred_element_type=_F32)
    p2 = jnp.dot(f, k_ref[SEQ:, :], preferred_element_type=_F32)
    row = lax.broadcasted_iota(jnp.int32, (tm, 1), 0) + i * tm
    sign = (1 - 2 * (row & 1)).astype(_F32)
    o_ref[...] = p1 + sign * p2


def _filter_spectrum(fmat, kfilt):
    tm = 512
    nw = kfilt.shape[1]
    return pl.pallas_call(
        _kf_kernel, grid=(FFT_N // tm,),
        in_specs=[pl.BlockSpec((tm, SEQ), lambda i: (i, 0)),
                  pl.BlockSpec((FFT_N, nw), lambda i: (0, 0))],
        out_specs=pl.BlockSpec((tm, nw), lambda i: (i, 0)),
        out_shape=jax.ShapeDtypeStruct((FFT_N, nw), _F32),
        compiler_params=_cp(("parallel",), 40), name="hy_filter_spectrum")(fmat, kfilt)


HY_FB = 512
HY_NB = 2


HY_TM = 512


def _hconv_fwd_kernel(zb_ref, f_ref, k_ref, y_ref):
    fk = pl.program_id(1)
    kr = k_ref[:HY_FB, :]
    ki = k_ref[HY_FB:, :]
    row0 = jnp.logical_and(lax.broadcasted_iota(jnp.int32, kr.shape, 0) == 0, fk == 0)
    sc = jnp.where(row0, 1.0 / FFT_N, 2.0 / FFT_N)
    for j in range(HY_NB):
        u = jnp.dot(f_ref[...], zb_ref[j], preferred_element_type=_F32)
        ur = u[:HY_FB]
        ui = u[HY_FB:]
        yr = jnp.where(row0, ur * kr, ur * kr - ui * ki)
        yi = jnp.where(row0, ui * ki, ur * ki + ui * kr)
        y_ref[j, 0, 0] = (yr * sc).astype(_BF)
        y_ref[j, 0, 1] = (yi * sc).astype(_BF)


def _hconv_inv_kernel(ft_ref, y_ref, z_ref, skip_ref, xn_ref, o_ref, ob_ref):
    for j in range(HY_NB):
        conv = jnp.dot(ft_ref[...], y_ref[j], preferred_element_type=_F32)
        out = xn_ref[j] * (conv + z_ref[j] * skip_ref[...])
        o_ref[j] = out
        ob_ref[j] = out.astype(_BF)


def _long_conv_gate(hbc, zsrc, zb, xn_col, order, fmat, fmat_t, kf, skip):
    b = hbc.shape[0]
    nf = SEQ // HY_FB
    w = HY_WIDTH
    nb = HY_NB
    y = pl.pallas_call(
        _hconv_fwd_kernel, grid=(b // nb, nf),
        in_specs=[pl.BlockSpec((nb, SEQ, w), lambda i, f: (i, 0, 0)),
                  pl.BlockSpec((2 * HY_FB, SEQ), lambda i, f: (f, 0)),
                  pl.BlockSpec((2 * HY_FB, w), lambda i, f: (f, order))],
        out_specs=pl.BlockSpec((nb, 1, 2, HY_FB, w), lambda i, f: (i, f, 0, 0, 0)),
        out_shape=jax.ShapeDtypeStruct((b, nf, 2, HY_FB, w), _BF),
        compiler_params=_cp(("parallel", "parallel"), 40), name=f"hy_conv_fwd{order}")(zb, fmat, kf)
    blk = lambda col: pl.BlockSpec((nb, HY_TM, w), lambda i, m, c=col: (i, m, c))
    return pl.pallas_call(
        _hconv_inv_kernel, grid=(b // nb, SEQ // HY_TM),
        in_specs=[pl.BlockSpec((HY_TM, FFT_N), lambda i, m: (m, 0)),
                  pl.BlockSpec((nb, FFT_N, w), lambda i, m: (i, 0, 0)),
                  blk(0),
                  pl.BlockSpec((1, w), lambda i, m: (0, 0)),
                  blk(xn_col)],
        out_specs=[blk(0), blk(0)],
        out_shape=[jax.ShapeDtypeStruct((b, SEQ, w), _F32), jax.ShapeDtypeStruct((b, SEQ, w), _BF)],
        compiler_params=_cp(("parallel", "parallel"), 48), name=f"hy_conv_inv{order}")(
            fmat_t, y.reshape(b, FFT_N, w), zsrc, skip[order].reshape(1, w), hbc)


def _outproj_ln_kernel(n_a, *refs):
    a_refs = refs[:n_a]
    w_refs = refs[n_a:2 * n_a]
    x_ref, g_ref, b_ref, wr_ref, o_ref, ob_ref, aff_ref = refs[2 * n_a:]
    wr = wr_ref[...]
    wr_hi = wr.astype(_BF)
    wr_lo = (wr - wr_hi.astype(_F32)).astype(_BF)
    m = _dot(a_refs[0][...], w_refs[0][...])
    for a_ref, w_ref in zip(a_refs[1:], w_refs[1:]):
        m = m + _dot(a_ref[...], w_ref[...])
    y = _layer_norm(DN_ALPHA * x_ref[...] + m, g_ref[...], b_ref[...])
    o_ref[...] = y
    y_hi = y.astype(_BF)
    ob_ref[...] = y_hi
    y_lo = (y - y_hi.astype(_F32)).astype(_BF)
    by_hi = _dot_nt(jnp.concatenate([wr_hi, wr_lo], axis=0), y_hi)
    logits = by_hi[:N_EXPERTS] + by_hi[N_EXPERTS:] + _dot_nt(wr_hi, y_lo)
    ex = jnp.exp(logits - jnp.max(logits, axis=0, keepdims=True))
    aff_ref[0] = ex / jnp.sum(ex, axis=0, keepdims=True)


def _outproj_ln(a_list, w, x2d, g, b, w_router):
    m = x2d.shape[0]
    tm = 1024
    per_seq = SEQ // tm
    in_specs, w_args, row0 = [], [], 0
    for a in a_list:
        in_specs.append(pl.BlockSpec((tm, a.shape[1]), lambda i: (i, 0)))
    for a in a_list:
        ka = a.shape[1]
        in_specs.append(pl.BlockSpec((ka, D_MODEL), lambda i, r=row0 // ka: (r, 0)))
        w_args.append(w)
        row0 += ka
    in_specs += [pl.BlockSpec((tm, D_MODEL), lambda i: (i, 0)),
                 pl.BlockSpec((1, D_MODEL), lambda i: (0, 0)),
                 pl.BlockSpec((1, D_MODEL), lambda i: (0, 0)),
                 pl.BlockSpec((N_EXPERTS, D_MODEL), lambda i: (0, 0))]
    return pl.pallas_call(
        functools.partial(_outproj_ln_kernel, len(a_list)), grid=(m // tm,),
        in_specs=in_specs,
        out_specs=[pl.BlockSpec((tm, D_MODEL), lambda i: (i, 0)),
                   pl.BlockSpec((tm, D_MODEL), lambda i: (i, 0)),
                   pl.BlockSpec((1, N_EXPERTS, tm), lambda i: (i // per_seq, 0, i % per_seq))],
        out_shape=[jax.ShapeDtypeStruct((m, D_MODEL), _F32), jax.ShapeDtypeStruct((m, D_MODEL), _BF),
                   jax.ShapeDtypeStruct((m // SEQ, N_EXPERTS, SEQ), _F32)],
        compiler_params=_cp(("parallel",), 40), name="outproj_ln_router")(
            *a_list, *w_args, x2d, g.reshape(1, -1), b.reshape(1, -1), w_router.T)


BISECT_FLOOR = 2.0 ** -120
BISECT_GEO = 8
BISECT_LIN = 60


def _select_kernel(a_ref, tri_ref, edge_ref, slot_ref, slot_t_ref, cnt_ref):
    a = a_ref[...]
    rows = a.shape[0]
    cap = float(CAP)

    def probe(mid, lo, hi):
        cnt = jnp.sum(jnp.where(a > mid, 1.0, 0.0), axis=1, keepdims=True)
        ge = cnt >= cap
        return jnp.where(ge, mid, lo), jnp.where(ge, hi, mid)

    def geo(_, c):
        lo, hi = c
        return probe(jnp.where(lo > 0.0, jnp.sqrt(lo * hi), 0.5 * (lo + hi)), lo, hi)

    def lin(_, c):
        lo, hi = c
        return probe(0.5 * (lo + hi), lo, hi)

    c = probe(jnp.full((rows, 1), BISECT_FLOOR, _F32),
              jnp.full((rows, 1), -1.0, _F32), jnp.full((rows, 1), 1.0, _F32))
    c = lax.fori_loop(0, BISECT_GEO, geo, c)
    lo, hi = lax.fori_loop(0, BISECT_LIN, lin, c)
    vstar = jnp.max(jnp.where(a <= hi, a, -1.0), axis=1, keepdims=True)
    gt = a > vstar
    eq = a == vstar
    need = cap - jnp.sum(jnp.where(gt, 1.0, 0.0), axis=1, keepdims=True)
    tri = tri_ref[...]
    eq_before = jnp.dot(jnp.where(eq, 1.0, 0.0).astype(_BF), tri, preferred_element_type=_F32)
    sel = jnp.where(gt, 1.0, jnp.where(eq, jnp.where(eq_before < need, 1.0, 0.0), 0.0))
    pos = jnp.dot(sel.astype(_BF), tri, preferred_element_type=_F32)
    slot = jnp.where(sel > 0.5, pos, -1.0)
    slot_ref[...] = slot.astype(jnp.int32)
    cnt_ref[...] = jnp.dot(sel.astype(_BF), edge_ref[...], preferred_element_type=_F32).astype(jnp.int32)
    if rows < LANE:
        slot = jnp.concatenate([slot, jnp.full((LANE - rows, SEQ), -1.0, _F32)], axis=0)
    slot_tm = slot.T
    for b in range(rows // N_EXPERTS):
        shifted = slot_tm if b == 0 else pltpu.roll(slot_tm, LANE - N_EXPERTS * b, axis=1)
        slot_t_ref[b] = shifted.astype(jnp.int32)


@functools.lru_cache(maxsize=None)
def _prefix_matrix():
    idx = np.arange(SEQ)
    return (idx[:, None] < idx[None, :]).astype(_BF)


WIN_TT = 256
SCAT_TT = 512
WIN_W = 128
WIN_EDGES = 16


@functools.lru_cache(maxsize=None)
def _edge_matrix():
    return (np.arange(SEQ)[:, None] < np.arange(LANE)[None, :] * WIN_TT).astype(_BF)


def _select(aff2d):
    rows = aff2d.shape[0]
    assert rows <= LANE and rows % N_EXPERTS == 0
    bsz = rows // N_EXPERTS
    slot, slot_t, cnt = pl.pallas_call(
        _select_kernel, grid=(1,),
        in_specs=[pl.BlockSpec((rows, SEQ), lambda i: (0, 0)),
                  pl.BlockSpec((SEQ, SEQ), lambda i: (0, 0)),
                  pl.BlockSpec((SEQ, LANE), lambda i: (0, 0))],
        out_specs=[pl.BlockSpec((rows, SEQ), lambda i: (0, 0)),
                   pl.BlockSpec((bsz, SEQ, LANE), lambda i: (0, 0, 0)),
                   pl.BlockSpec((rows, LANE), lambda i: (0, 0))],
        out_shape=[jax.ShapeDtypeStruct((rows, SEQ), jnp.int32),
                   jax.ShapeDtypeStruct((bsz, SEQ, LANE), jnp.int32),
                   jax.ShapeDtypeStruct((rows, LANE), jnp.int32)],
        compiler_params=_cp(("arbitrary",), 48), name="moe_select")(aff2d, _prefix_matrix(), _edge_matrix())
    return slot, slot_t, cnt[:, :WIN_EDGES].reshape(rows * WIN_EDGES)


def _onehot(slot_row):
    return slot_row == lax.broadcasted_iota(jnp.int32, (CAP, SEQ), 0)


GATHER_NE = 4
GATHER_TC = 512
GATHER_W = 64


def _window_start(cnt_ref, seq, e, tile, width, span=1):
    base = (seq * N_EXPERTS + e) * WIN_EDGES + tile * span
    start = jnp.minimum((cnt_ref[base] >> 4) << 4, CAP - width)
    return start, cnt_ref[base + span] - start <= width


def _gather_kernel(cnt_ref, slot_ref, a_ref, xb_ref, xe_ref, g_ref):
    i = pl.program_id(0)
    n_tiles = SEQ // WIN_TT
    wd = GATHER_W
    starts, fits = {}, None
    for t in range(n_tiles):
        for e in range(N_EXPERTS):
            starts[e, t], ok = _window_start(cnt_ref, i, e, t, wd)
            fits = ok if fits is None else jnp.logical_and(fits, ok)

    @pl.when(fits)
    def _():
        xe_ref[...] = jnp.zeros(xe_ref.shape, _BF)
        g_ref[...] = jnp.zeros(g_ref.shape, _F32)
        sub_w = lax.broadcasted_iota(jnp.int32, (wd, WIN_TT), 0)
        for t in range(n_tiles):
            toks = slice(t * WIN_TT, (t + 1) * WIN_TT)
            ps = []
            for e in range(N_EXPERTS):
                hit = slot_ref[0, e:e + 1, toks] - starts[e, t] == sub_w
                ps.append(jnp.where(hit, 1.0, 0.0).astype(_BF))
                rows = pl.ds(pl.multiple_of(starts[e, t], 16), wd)
                g_ref[e, rows, :] += jnp.sum(jnp.where(hit, a_ref[0, e:e + 1, toks], 0.0), axis=1, keepdims=True)
            pcat = jnp.concatenate(ps, axis=0)
            for c in range(D_MODEL // GATHER_TC):
                cols = slice(c * GATHER_TC, (c + 1) * GATHER_TC)
                res = jnp.dot(pcat, xb_ref[0, toks, cols], preferred_element_type=_F32).astype(_BF)
                for e in range(N_EXPERTS):
                    rows = pl.ds(pl.multiple_of(starts[e, t], 16), wd)
                    xe_ref[e, rows, cols] += res[e * wd:(e + 1) * wd]

    @pl.when(jnp.logical_not(fits))
    def _():
        for grp in range(N_EXPERTS // GATHER_NE):
            es = range(grp * GATHER_NE, (grp + 1) * GATHER_NE)
            hits = [_onehot(slot_ref[0, e:e + 1, :]) for e in es]
            p = jnp.concatenate([jnp.where(h, 1.0, 0.0).astype(_BF) for h in hits], axis=0)
            xe = jnp.dot(p, xb_ref[0], preferred_element_type=_F32).astype(_BF)
            xe_ref[grp * GATHER_NE:(grp + 1) * GATHER_NE] = xe.reshape(GATHER_NE, CAP, D_MODEL)
            for k, e in enumerate(es):
                g_ref[e] = jnp.sum(jnp.where(hits[k], a_ref[0, e:e + 1, :], 0.0), axis=1, keepdims=True)


def _gather(cnt, slot3, aff3, xb3d):
    b = xb3d.shape[0]
    row = pl.BlockSpec((1, N_EXPERTS, SEQ), lambda i, c: (i, 0, 0))
    grid_spec = pltpu.PrefetchScalarGridSpec(
        num_scalar_prefetch=1, grid=(b,),
        in_specs=[row, row, pl.BlockSpec((1, SEQ, D_MODEL), lambda i, c: (i, 0, 0))],
        out_specs=[pl.BlockSpec((N_EXPERTS, CAP, D_MODEL), lambda i, c: (0, i, 0)),
                   pl.BlockSpec((N_EXPERTS, CAP, 1), lambda i, c: (0, i, 0))])
    return pl.pallas_call(
        _gather_kernel, grid_spec=grid_spec,
        out_shape=[jax.ShapeDtypeStruct((N_EXPERTS, b * CAP, D_MODEL), _BF),
                   jax.ShapeDtypeStruct((N_EXPERTS, b * CAP, 1), _F32)],
        compiler_params=_cp(("parallel",), 56), name="moe_gather")(cnt, slot3, aff3, xb3d)


FFN_TF = 1024
FFN_TM = 512


def _ffn_up_kernel(xe_ref, wg_ref, wu_ref, h_ref):
    wg = wg_ref[0].astype(_BF)
    wu = wu_ref[0].astype(_BF)
    for c in range(xe_ref.shape[1] // FFN_TM):
        rows = slice(c * FFN_TM, (c + 1) * FFN_TM)
        xe = xe_ref[0, rows, :]
        hg = _dot(xe, wg)
        hu = _dot(xe, wu)
        h_ref[0, rows, :] = ((hg * (1.0 / (1.0 + jnp.exp(-hg)))) * hu).astype(_BF)


def _ffn_down_kernel(h_ref, wd_ref, g_ref, ye_ref):
    wd = wd_ref[0].astype(_BF)
    for c in range(h_ref.shape[1] // FFN_TM):
        rows = slice(c * FFN_TM, (c + 1) * FFN_TM)
        ye_ref[0, rows, :] = (_dot(h_ref[0, rows, :], wd) * g_ref[0, rows, :]).astype(_BF)


def _expert_ffn(xe, gates, wg, wu, wd, layer):
    rows = xe.shape[1]
    h = pl.pallas_call(
        _ffn_up_kernel, grid=(N_EXPERTS, D_FF_EXPERT // FFN_TF),
        in_specs=[pl.BlockSpec((1, rows, D_MODEL), lambda e, f: (e, 0, 0)),
                  pl.BlockSpec((None, 1, D_MODEL, FFN_TF), lambda e, f: (layer, e, 0, f)),
                  pl.BlockSpec((None, 1, D_MODEL, FFN_TF), lambda e, f: (layer, e, 0, f))],
        out_specs=pl.BlockSpec((1, rows, FFN_TF), lambda e, f: (e, 0, f)),
        out_shape=jax.ShapeDtypeStruct((N_EXPERTS, rows, D_FF_EXPERT), _BF),
        compiler_params=_cp(("parallel", "parallel"), 52), name="moe_ffn_up")(xe, wg, wu)
    return pl.pallas_call(
        _ffn_down_kernel, grid=(N_EXPERTS,),
        in_specs=[pl.BlockSpec((1, rows, D_FF_EXPERT), lambda e: (e, 0, 0)),
                  pl.BlockSpec((None, 1, D_FF_EXPERT, D_MODEL), lambda e: (layer, e, 0, 0)),
                  pl.BlockSpec((1, rows, 1), lambda e: (e, 0, 0))],
        out_specs=pl.BlockSpec((1, rows, D_MODEL), lambda e: (e, 0, 0)),
        out_shape=jax.ShapeDtypeStruct((N_EXPERTS, rows, D_MODEL), _BF),
        compiler_params=_cp(("parallel",), 56), name="moe_ffn_down")(h, wd, gates)


def _scatter_ln_ple_kernel(cnt_ref, slot_t_ref, ye_ref, x_ref, g_ref, b_ref, p_ref, wg_ref, wp_ref, o_ref,
                           pt_ref, yw_ref, f_ref):
    i = pl.program_id(0)
    r = pl.program_id(1)
    slot_t = slot_t_ref[0]
    starts = []
    fits = None
    for e in range(N_EXPERTS):
        start, ok = _window_start(cnt_ref, i, e, r, WIN_W, SCAT_TT // WIN_TT)
        starts.append(start)
        fits = ok if fits is None else jnp.logical_and(fits, ok)

    @pl.when(fits)
    def _():
        lane_w = lax.broadcasted_iota(jnp.int32, (SCAT_TT, WIN_W), 1)
        for e in range(N_EXPERTS):
            cols = slice(e * WIN_W, (e + 1) * WIN_W)
            pt_ref[:, cols] = jnp.where(slot_t[:, e:e + 1] - starts[e] == lane_w, 1.0, 0.0).astype(_BF)
            yw_ref[cols, :] = ye_ref[e, pl.ds(pl.multiple_of(starts[e], 16), WIN_W), :]
        f_ref[...] = jnp.dot(pt_ref[:, :N_EXPERTS * WIN_W], yw_ref[...], preferred_element_type=_F32)

    @pl.when(jnp.logical_not(fits))
    def _():
        lane_c = lax.broadcasted_iota(jnp.int32, (SCAT_TT, CAP), 1)
        for e in range(N_EXPERTS):
            pt_ref[:, e * CAP:(e + 1) * CAP] = jnp.where(slot_t[:, e:e + 1] == lane_c, 1.0, 0.0).astype(_BF)
        f_ref[...] = jnp.dot(pt_ref[...], ye_ref[...].reshape(N_EXPERTS * CAP, D_MODEL),
                             preferred_element_type=_F32)

    y = _layer_norm(DN_ALPHA * x_ref[0] + f_ref[...], g_ref[...], b_ref[...])
    gate = _dot(y, wg_ref[...])
    emb = _dot(p_ref[0], wp_ref[...])
    o_ref[0] = y + (1.0 / (1.0 + jnp.exp(-gate))) * emb


def _scatter_ln_ple(cnt, slot_t, ye, x3d, g, b, p, ple_gate, ple_proj, layer):
    bsz = x3d.shape[0]
    vec = pl.BlockSpec((1, D_MODEL), lambda i, r, c: (0, 0))
    seq = pl.BlockSpec((1, SCAT_TT, D_MODEL), lambda i, r, c: (i, r, 0))
    grid_spec = pltpu.PrefetchScalarGridSpec(
        num_scalar_prefetch=1, grid=(bsz, SEQ // SCAT_TT),
        in_specs=[pl.BlockSpec((1, SCAT_TT, LANE), lambda i, r, c: (i, r, 0)),
                  pl.BlockSpec((N_EXPERTS, CAP, D_MODEL), lambda i, r, c: (0, i, 0)),
                  seq, vec, vec,
                  pl.BlockSpec((None, 1, SCAT_TT, PLE_DIM), lambda i, r, c: (layer, i, r, 0)),
                  pl.BlockSpec((None, D_MODEL, D_MODEL), lambda i, r, c: (layer, 0, 0)),
                  pl.BlockSpec((None, PLE_DIM, D_MODEL), lambda i, r, c: (layer, 0, 0))],
        out_specs=seq,
        scratch_shapes=[pltpu.VMEM((SCAT_TT, N_EXPERTS * CAP), _BF),
                        pltpu.VMEM((N_EXPERTS * WIN_W, D_MODEL), _BF),
                        pltpu.VMEM((SCAT_TT, D_MODEL), _F32)])
    return pl.pallas_call(
        _scatter_ln_ple_kernel, grid_spec=grid_spec,
        out_shape=jax.ShapeDtypeStruct((bsz, SEQ, D_MODEL), _F32),
        compiler_params=_cp(("parallel", "arbitrary"), 56), name="moe_scatter_ln_ple")(
            cnt, slot_t, ye, x3d, g.reshape(1, -1), b.reshape(1, -1), p, ple_gate, ple_proj)


def _moe_ple_block(x1, x1b, aff, wg, wu, wd, g, b, p, ple_gate, ple_proj, layer):
    bsz = x1.shape[0]
    slot, slot_t, cnt = _select(aff.reshape(bsz * N_EXPERTS, SEQ))
    xe, gates = _gather(cnt, slot.reshape(bsz, N_EXPERTS, SEQ), aff, x1b)
    ye = _expert_ffn(xe, gates, wg, wu, wd, layer)
    return _scatter_ln_ple(cnt, slot_t, ye, x1, g, b, p, ple_gate, ple_proj, layer)


MLA_HB = LANE
MLA_IN_PAD = 768
MLA_KR_COL = MLA_Q_LORA + MLA_KV_LORA


def _rms(x, g):
    return x * lax.rsqrt(jnp.mean(x * x, axis=-1, keepdims=True) + NORM_EPS) * g


def _mla_pre_kernel(x_ref, win_ref, qg_ref, wq_ref, kvg_ref, wkv_ref, qa_ref, qb_ref, ka_ref, kb_ref,
                    q_ref, kv_ref, kr_ref):
    h = _dot(x_ref[...], win_ref[...])
    q = _dot(_rms(h[:, :MLA_Q_LORA], qg_ref[...]), wq_ref[...])
    swap = pltpu.roll(q, q.shape[1] - MLA_ROPE, axis=1)
    qa = jnp.concatenate([qa_ref[...]] * MLA_HEADS, axis=1)
    qb = jnp.concatenate([qb_ref[...]] * MLA_HEADS, axis=1)
    q_ref[...] = (q * qa + swap * qb).astype(_BF)
    kv_ref[...] = _dot(_rms(h[:, MLA_Q_LORA:MLA_KR_COL], kvg_ref[...]), wkv_ref[...]).astype(_BF)
    kr = h[:, MLA_KR_COL:MLA_KR_COL + MLA_HB]
    kr_ref[...] = (kr * ka_ref[...] + pltpu.roll(kr, MLA_HB - MLA_ROPE, axis=1) * kb_ref[...]).astype(_BF)


@functools.lru_cache(maxsize=None)
def _rope_lane_tables():
    inv = 1.0 / (ROPE_THETA ** (np.arange(0, MLA_ROPE, 2) / MLA_ROPE))
    ang = np.arange(SEQ)[:, None] * inv[None, :]
    cos, sin = np.cos(ang), np.sin(ang)
    ones = np.ones((SEQ, MLA_NOPE))
    zeros_n = np.zeros((SEQ, MLA_NOPE))
    zeros_r = np.zeros((SEQ, MLA_ROPE))
    cc = np.concatenate([cos, cos], axis=1)
    ss = np.concatenate([-sin, sin], axis=1)
    keep_q = np.concatenate([ones, cc, zeros_r], axis=1)
    keep_k = np.concatenate([zeros_n, cc, zeros_r], axis=1)
    swp = np.concatenate([zeros_n, ss, zeros_r], axis=1)
    qscale = (MLA_NOPE + MLA_ROPE) ** -0.5 * LOG2E
    f32 = lambda a: a.astype(np.float32)
    return f32(keep_q * qscale), f32(swp * qscale), f32(keep_k), f32(swp)


def _mla_pre(x2d, w_in, q_norm, w_q_up, kv_norm, w_kv_up):
    m = x2d.shape[0]
    tm = 512
    half = MLA_ROPE // 2
    kr_w = w_in[:, MLA_KR_COL:]
    kr_swapped = jnp.concatenate([kr_w[:, half:], kr_w[:, :half]], axis=1)
    win = jnp.concatenate([w_in[:, :MLA_KR_COL], jnp.zeros((D_MODEL, MLA_NOPE), _F32), kr_w, kr_swapped],
                          axis=1).astype(_BF)
    wq = w_q_up.reshape(MLA_Q_LORA, MLA_HEADS, MLA_NOPE + MLA_ROPE)
    rope_w = wq[:, :, MLA_NOPE:]
    wq = jnp.concatenate([wq, rope_w[:, :, half:], rope_w[:, :, :half]], axis=2).reshape(
        MLA_Q_LORA, MLA_HEADS * MLA_HB).astype(_BF)
    w_kv_up = w_kv_up.astype(_BF)
    q_keep, q_swap, k_keep, k_swap = _rope_lane_tables()
    nq = MLA_HEADS * MLA_HB
    pos = lambda: pl.BlockSpec((tm, MLA_HB), lambda i: (i % (SEQ // tm), 0))
    full = lambda a: pl.BlockSpec(a.shape, lambda i: (0,) * a.ndim)
    qg, kvg = q_norm.reshape(1, -1), kv_norm.reshape(1, -1)
    return pl.pallas_call(
        _mla_pre_kernel, grid=(m // tm,),
        in_specs=[pl.BlockSpec((tm, D_MODEL), lambda i: (i, 0)), full(win), full(qg), full(wq), full(kvg),
                  full(w_kv_up), pos(), pos(), pos(), pos()],
        out_specs=[pl.BlockSpec((tm, nq), lambda i: (i, 0)),
                   pl.BlockSpec((tm, nq), lambda i: (i, 0)),
                   pl.BlockSpec((tm, MLA_HB), lambda i: (i, 0))],
        out_shape=[jax.ShapeDtypeStruct((m, nq), _BF), jax.ShapeDtypeStruct((m, nq), _BF),
                   jax.ShapeDtypeStruct((m, MLA_HB), _BF)],
        compiler_params=_cp(("parallel",), 48), name="mla_pre")(
            x2d, win, qg, wq, kvg, w_kv_up, q_keep, q_swap, k_keep, k_swap)


MLA_TQ = 256
MLA_PAIR = 2


def _mla_attn_kernel(q_ref, kv_ref, kr_ref, o_ref, kt_scr, v_scr, s_scr):
    lane = lax.broadcasted_iota(jnp.int32, (SEQ, MLA_HB), 1)
    kr = kr_ref[0].astype(_F32)
    for j in range(MLA_PAIR):
        kvh = kv_ref[0, :, j * MLA_HB:(j + 1) * MLA_HB].astype(_F32)
        kt_scr[j] = jnp.where(lane < MLA_NOPE, kvh, kr).T.astype(_BF)
        v_scr[j] = jnp.where(lane < MLA_NOPE, 1.0, kvh).astype(_BF)
    lane_q = lax.broadcasted_iota(jnp.int32, (MLA_TQ, MLA_HB), 1)

    def block_rows(i):
        return pl.ds(pl.multiple_of(i * MLA_TQ, MLA_TQ), MLA_TQ)

    def scores(i, slot):
        for j in range(MLA_PAIR):
            s_scr[slot, j] = _dot(q_ref[0, block_rows(i), j * MLA_HB:(j + 1) * MLA_HB], kt_scr[j])

    def finish(i, slot):
        outs = []
        for j in range(MLA_PAIR):
            s = s_scr[slot, j]
            p = jnp.exp2(s - jnp.max(s, axis=1, keepdims=True))
            o = _dot(p, v_scr[j])
            outs.append(o / o[:, 0:1])
        even = pltpu.roll(outs[0], MLA_V, axis=1)
        o_ref[0, block_rows(i), :] = jnp.where(lane_q < MLA_V, even, outs[1]).astype(_BF)

    nblk = SEQ // MLA_TQ
    scores(0, 0)

    def body(k, carry):
        scores(2 * k + 1, 1)
        finish(2 * k, 0)
        scores(2 * k + 2, 0)
        finish(2 * k + 1, 1)
        return carry

    lax.fori_loop(0, nblk // 2 - 1, body, 0)
    scores(nblk - 1, 1)
    finish(nblk - 2, 0)
    finish(nblk - 1, 1)


def _mla_attention(q, kv, kr):
    b = q.shape[0]
    w = MLA_PAIR * MLA_HB
    return pl.pallas_call(
        _mla_attn_kernel, grid=(b, MLA_HEADS // MLA_PAIR),
        in_specs=[pl.BlockSpec((1, SEQ, w), lambda i, h: (i, 0, h)),
                  pl.BlockSpec((1, SEQ, w), lambda i, h: (i, 0, h)),
                  pl.BlockSpec((1, SEQ, MLA_HB), lambda i, h: (i, 0, 0))],
        out_specs=pl.BlockSpec((1, SEQ, MLA_PAIR * MLA_V), lambda i, h: (i, 0, h)),
        out_shape=jax.ShapeDtypeStruct((b, SEQ, MLA_HEADS * MLA_V), _BF),
        scratch_shapes=[pltpu.VMEM((MLA_PAIR, MLA_HB, SEQ), _BF), pltpu.VMEM((MLA_PAIR, SEQ, MLA_HB), _BF),
                        pltpu.VMEM((2, MLA_PAIR, MLA_TQ, SEQ), _F32)],
        compiler_params=_cp(("parallel", "parallel"), 40), name="mla_attention")(q, kv, kr)


def _na_hyena_mixer(x, w_in, rpb, conv_w, conv_b, f_w1, f_b1, f_freq, f_w2, f_b2, f_w3, skip):
    b = x.shape[0]
    qkv, hbc, zb = _inproj_conv(x.reshape(b * SEQ, D_MODEL), w_in, conv_w, conv_b)
    y_a = _na_attention(qkv.reshape(b, SEQ, 3 * NA_WIDTH), rpb)
    hbc = hbc.reshape(b, SEQ, (HY_ORDER + 1) * HY_WIDTH)
    zb = zb.reshape(b, SEQ, HY_WIDTH)
    fmat, fmat_t = (jnp.asarray(f).astype(_BF) for f in _dft_matrices())
    kfilt = _hyena_filters(f_w1, f_b1, f_freq, f_w2, f_b2, f_w3)
    kf = _filter_spectrum(fmat, kfilt)
    z = hbc
    for o in range(HY_ORDER):
        z, zb = _long_conv_gate(hbc, z, zb, o + 1, o, fmat, fmat_t, kf, skip)
    return y_a.reshape(b * SEQ, NA_WIDTH), zb.reshape(b * SEQ, HY_WIDTH)


def kernel(x, p, ab_w_in, na_rpb, hy_conv_w, hy_conv_b, hy_f_w1, hy_f_b1, hy_f_freq, hy_f_w2, hy_f_b2, hy_f_w3, hy_skip, ab_w_out, mla_w_in, mla_q_norm, mla_w_q_up, mla_kv_norm, mla_w_kv_up, mla_w_out, ln1_g, ln1_b, ln2_g, ln2_b, moe_router, moe_w_gate, moe_w_up, moe_w_down, ple_gate, ple_proj):
    b = x.shape[0]
    m = b * SEQ
    ple_gate_b, ple_proj_b = ple_gate.astype(_BF), ple_proj.astype(_BF)
    for i in range(DEPTH):
        j = i // 2
        x2d = x.reshape(m, D_MODEL)
        if i % 2 == 0:
            y_a, z = _na_hyena_mixer(x, ab_w_in[j].astype(_BF), na_rpb[j], hy_conv_w[j], hy_conv_b[j], hy_f_w1[j],
                                     hy_f_b1[j], hy_f_freq[j], hy_f_w2[j], hy_f_b2[j], hy_f_w3[j], hy_skip[j])
            x1, x1b, aff = _outproj_ln([y_a, z], ab_w_out[j].astype(_BF), x2d, ln1_g[i], ln1_b[i], moe_router[i])
        else:
            q, kv, kr = _mla_pre(x2d, mla_w_in[j], mla_q_norm[j], mla_w_q_up[j], mla_kv_norm[j], mla_w_kv_up[j])
            att = _mla_attention(q.reshape(b, SEQ, -1), kv.reshape(b, SEQ, -1), kr.reshape(b, SEQ, -1))
            x1, x1b, aff = _outproj_ln([att.reshape(m, MLA_HEADS * MLA_V)], mla_w_out[j].astype(_BF), x2d,
                                       ln1_g[i], ln1_b[i], moe_router[i])
        x = _moe_ple_block(x1.reshape(b, SEQ, D_MODEL), x1b.reshape(b, SEQ, D_MODEL), aff,
                           moe_w_gate, moe_w_up, moe_w_down, ln2_g[i], ln2_b[i], p, ple_gate_b, ple_proj_b, i)
    return x
```

```python
import functools
import math

import numpy as np
import jax
import jax.numpy as jnp
from jax import lax
from jax.experimental import pallas as pl
from jax.experimental.pallas import tpu as pltpu

D_MODEL = 1024
BATCH = 8
SEQ = 2048
DEPTH = 2
GRID_W = 64
PLE_DIM = 256
NA_HEADS = 8
NA_HEAD_DIM = 64
NA_WIDTH = NA_HEADS * NA_HEAD_DIM
NA_WIN_ROWS_MAX = 8
NA_WIN_COLS = 16
HY_WIDTH = D_MODEL - NA_WIDTH
HY_ORDER = 2
HY_SHORT_K = 3
HY_EMB_DIM = 33
HY_FILTER_HIDDEN = 64
HY_FAST_DECAY_PCT = 0.3
HY_SLOW_DECAY_PCT = 1.5
HY_DECAY_TARGET = 1e-2
AB_IN_WIDTH = 3 * NA_WIDTH + (HY_ORDER + 1) * HY_WIDTH
MLA_HEADS = 16
MLA_Q_LORA = 384
MLA_KV_LORA = 256
MLA_NOPE = 64
MLA_ROPE = 32
MLA_V = 64
ROPE_THETA = 10000.0
N_EXPERTS = 16
EC_CAPACITY_FACTOR = 2
D_FF_EXPERT = 2048
DN_ALPHA = (2 * DEPTH) ** 0.25
NORM_EPS = 1e-5
NEG_INF = -1e30

CAP = EC_CAPACITY_FACTOR * SEQ // N_EXPERTS
NA_ROWS = SEQ // GRID_W
NA_WIN_ROWS = min(NA_WIN_ROWS_MAX, NA_ROWS)
NA_SLAB = NA_WIN_ROWS * GRID_W
FFT_N = 2 * SEQ
LOG2E = math.log2(math.e)

LANE = 128
MIB = 1 << 20

_BF = jnp.bfloat16
_F32 = jnp.float32
_HP = lax.Precision.HIGHEST


def _cp(sem, vmem_mib):
    return pltpu.CompilerParams(dimension_semantics=sem, vmem_limit_bytes=vmem_mib * MIB)


def _dot(a, b):
    return jnp.dot(a.astype(_BF), b.astype(_BF), preferred_element_type=_F32)


def _dot_nt(a, b):
    return lax.dot_general(a.astype(_BF), b.astype(_BF), (((1,), (1,)), ((), ())),
                           preferred_element_type=_F32)


def _dot3(a, b):
    a_hi = a.astype(_BF)
    a_lo = (a - a_hi.astype(_F32)).astype(_BF)
    b_hi = b.astype(_BF)
    b_lo = (b - b_hi.astype(_F32)).astype(_BF)
    mm = lambda u, v: jnp.dot(u, v, preferred_element_type=_F32)
    return mm(a_hi, b_hi) + mm(a_hi, b_lo) + mm(a_lo, b_hi)


def _layer_norm(y, g, b):
    mu = jnp.mean(y, axis=-1, keepdims=True)
    d = y - mu
    var = jnp.mean(d * d, axis=-1, keepdims=True)
    return d * lax.rsqrt(var + NORM_EPS) * g + b


INPROJ_TM = 512
HALO = 8


def _inproj_kernel(x_ref, xp_ref, xn_ref, w_ref, cw_ref, cb_ref, qkv_ref, hbc_ref, vb_ref):
    tile_in_seq = pl.program_id(0) % (SEQ // INPROJ_TM)
    x_ext = jnp.concatenate([xp_ref[...], x_ref[...], xn_ref[...]], axis=0)
    hb = _dot(x_ext, w_ref[:, 3 * NA_WIDTH:])
    qkv_ref[...] = _dot(x_ref[...], w_ref[:, :3 * NA_WIDTH]).astype(_BF)
    n = hb.shape[0]
    row = lax.broadcasted_iota(jnp.int32, hb.shape, 0)
    pad_row = jnp.logical_or(jnp.logical_and(row == HALO - 1, tile_in_seq == 0),
                             jnp.logical_and(row == HALO + INPROJ_TM, tile_in_seq == SEQ // INPROJ_TM - 1))
    hb = jnp.where(pad_row, 0.0, hb)
    y = (cb_ref[...] + pltpu.roll(hb, 1, axis=0) * cw_ref[0:1, :] + hb * cw_ref[1:2, :]
         + pltpu.roll(hb, n - 1, axis=0) * cw_ref[2:3, :])[HALO:HALO + INPROJ_TM]
    hbc_ref[...] = y
    vb_ref[...] = y[:, :HY_WIDTH].astype(_BF)


def _inproj_conv(x2d, w, conv_w, conv_b):
    m = x2d.shape[0]
    tm = INPROJ_TM
    nq, nh = 3 * NA_WIDTH, (HY_ORDER + 1) * HY_WIDTH
    per_tile = tm // HALO
    last = m // HALO - 1
    return pl.pallas_call(
        _inproj_kernel, grid=(m // tm,),
        in_specs=[pl.BlockSpec((tm, D_MODEL), lambda i: (i, 0)),
                  pl.BlockSpec((HALO, D_MODEL), lambda i: (jnp.maximum(i * per_tile - 1, 0), 0)),
                  pl.BlockSpec((HALO, D_MODEL), lambda i: (jnp.minimum((i + 1) * per_tile, last), 0)),
                  pl.BlockSpec((D_MODEL, AB_IN_WIDTH), lambda i: (0, 0)),
                  pl.BlockSpec((HY_SHORT_K, nh), lambda i: (0, 0)),
                  pl.BlockSpec((1, nh), lambda i: (0, 0))],
        out_specs=[pl.BlockSpec((tm, nq), lambda i: (i, 0)),
                   pl.BlockSpec((tm, nh), lambda i: (i, 0)),
                   pl.BlockSpec((tm, HY_WIDTH), lambda i: (i, 0))],
        out_shape=[jax.ShapeDtypeStruct((m, nq), _BF), jax.ShapeDtypeStruct((m, nh), _F32),
                   jax.ShapeDtypeStruct((m, HY_WIDTH), _BF)],
        compiler_params=_cp(("parallel",), 56), name="ab_inproj_conv")(
            x2d, x2d, x2d, w, conv_w, conv_b.reshape(1, nh))


NA_GROUP = 4
NA_GW = NA_GROUP * NA_HEAD_DIM


def _na_kernel(q_ref, k_ref, v_ref, tbl_ref, o_ref):
    lane_head = lax.broadcasted_iota(jnp.int32, (GRID_W, NA_GW), 1) >> int(math.log2(NA_HEAD_DIM))
    scale = NA_HEAD_DIM ** -0.5 * LOG2E

    def body(r, carry):
        r0 = jnp.clip(r - NA_WIN_ROWS // 2, 0, NA_ROWS - NA_WIN_ROWS)
        off = r0 - r + (NA_WIN_ROWS_MAX - 1)
        qr = q_ref[0, pl.ds(pl.multiple_of(r * GRID_W, GRID_W), GRID_W), :].astype(_F32)
        q4 = jnp.concatenate([jnp.where(lane_head == h, qr, 0.0) for h in range(NA_GROUP)], axis=0)
        ks = k_ref[0, pl.ds(pl.multiple_of(r0 * GRID_W, GRID_W), NA_SLAB), :]
        vs = v_ref[0, pl.ds(pl.multiple_of(r0 * GRID_W, GRID_W), NA_SLAB), :]
        bias = jnp.concatenate([tbl_ref[h, off] for h in range(NA_GROUP)], axis=0)
        s = _dot_nt(q4, ks) * scale + bias
        mx = jnp.max(s, axis=1, keepdims=True)
        p = jnp.exp2(s - mx)
        den = jnp.sum(p, axis=1, keepdims=True)
        o4 = _dot(p, vs) / den
        out = jnp.zeros((GRID_W, NA_GW), _F32)
        for h in range(NA_GROUP):
            out = out + jnp.where(lane_head == h, o4[h * GRID_W:(h + 1) * GRID_W], 0.0)
        o_ref[0, pl.ds(pl.multiple_of(r * GRID_W, GRID_W), GRID_W), :] = out.astype(_BF)
        return carry

    lax.fori_loop(0, NA_ROWS, body, 0, unroll=True)


def _na_bias_table(rpb):
    c = np.arange(GRID_W)
    c0 = np.clip(c - NA_WIN_COLS // 2, 0, GRID_W - NA_WIN_COLS)
    kc = np.arange(GRID_W)
    col_ok = (kc[None, :] >= c0[:, None]) & (kc[None, :] < c0[:, None] + NA_WIN_COLS)
    dc_idx = np.clip(kc[None, :] - c[:, None], -(NA_WIN_COLS - 1), NA_WIN_COLS - 1) + (NA_WIN_COLS - 1)
    pick = (dc_idx[None, :, :] == np.arange(2 * NA_WIN_COLS - 1)[:, None, None]).astype(np.float32)
    per_dr = jnp.einsum("hdk,kqc->hqdc", rpb.astype(_F32), pick, precision=_HP)
    per_dr = jnp.where(col_ok[None, :, None, :], per_dr * LOG2E, NEG_INF)
    return jnp.stack([per_dr[:, :, off:off + NA_WIN_ROWS].reshape(NA_HEADS, GRID_W, NA_SLAB)
                      for off in range(NA_WIN_ROWS_MAX)], axis=1)


def _na_attention(qkv, rpb):
    b = qkv.shape[0]
    tbl = _na_bias_table(rpb)
    ng = NA_HEADS // NA_GROUP
    blk = lambda col0: pl.BlockSpec((1, SEQ, NA_GW), lambda i, g, c=col0: (i, 0, c + g))
    return pl.pallas_call(
        _na_kernel, grid=(b, ng),
        in_specs=[blk(0), blk(ng), blk(2 * ng),
                  pl.BlockSpec((NA_GROUP, NA_WIN_ROWS_MAX, GRID_W, NA_SLAB), lambda i, g: (g, 0, 0, 0))],
        out_specs=pl.BlockSpec((1, SEQ, NA_GW), lambda i, g: (i, 0, g)),
        out_shape=jax.ShapeDtypeStruct((b, SEQ, NA_WIDTH), _BF),
        compiler_params=_cp(("parallel", "parallel"), 32), name="na_attention")(qkv, qkv, qkv, tbl)


HY_HID_PAD = LANE
HY_FILT_TC = 256


def _filter_kernel(z_ref, w1_ref, b1_ref, fr_ref, w2_ref, b2_ref, w3f_ref, w3b_ref, dec_ref, k_ref, h_ref):
    @pl.when(jnp.logical_and(pl.program_id(0) == 0, pl.program_id(1) == 0))
    def _():
        fr = fr_ref[...]
        h1 = jnp.sin(fr * (_dot3(z_ref[...], w1_ref[...]) + b1_ref[...]))
        h_ref[...] = jnp.sin(fr * (_dot3(h1, w2_ref[...]) + b2_ref[...]))

    fwd = _dot3(h_ref[:SEQ, :], w3f_ref[...])
    bwd = _dot3(h_ref[SEQ:, :], w3b_ref[...])
    k = jnp.concatenate([fwd, bwd], axis=0) * dec_ref[...]
    ss = jnp.sum(k * k, axis=0, keepdims=True)
    k_ref[...] = (k * lax.rsqrt(ss + 1e-12)).astype(_BF)


@functools.lru_cache(maxsize=None)
def _filter_tables():
    bands = (HY_EMB_DIM - 1) // 2
    t = np.linspace(0.0, 1.0, SEQ)[:, None]
    w = 2.0 * math.pi * np.arange(SEQ)[:, None] / SEQ
    f = np.linspace(1e-4, bands - 1, bands)[None, :]
    z = np.concatenate([t, np.cos(f * w), -np.sin(f * w)], axis=-1)
    min_decay = math.log(HY_DECAY_TARGET) / HY_SLOW_DECAY_PCT
    max_decay = math.log(HY_DECAY_TARGET) / HY_FAST_DECAY_PCT
    deltas = np.abs(np.linspace(min_decay, max_decay, HY_WIDTH))
    dec = np.exp(-t * deltas)
    src = np.concatenate([np.arange(SEQ), [0], np.arange(SEQ - 1, 0, -1)])
    live = np.ones((FFT_N, 1))
    live[SEQ] = 0.0
    z2 = np.pad(z[src], ((0, 0), (0, HY_HID_PAD - HY_EMB_DIM))).astype(np.float32)
    dec2 = (dec[src] * live).astype(np.float32)
    return z2, dec2


def _hyena_filters(w1, b1, freq, w2, b2, w3):
    z2, dec2 = _filter_tables()
    hp = HY_HID_PAD - HY_FILTER_HIDDEN
    w1p = jnp.pad(w1, ((0, HY_HID_PAD - HY_EMB_DIM), (0, hp)))
    w2p = jnp.pad(w2, ((0, hp), (0, hp)))
    w3p = jnp.pad(w3, ((0, hp), (0, 0)))
    row = lambda v: jnp.pad(v, (0, hp)).reshape(1, HY_HID_PAD)
    nc = HY_WIDTH // HY_FILT_TC
    per_order = 2 * nc
    full = lambda shape: pl.BlockSpec(shape, lambda o, j: (0, 0))
    return pl.pallas_call(
        _filter_kernel, grid=(HY_ORDER, nc),
        in_specs=[full((FFT_N, HY_HID_PAD)), full((HY_HID_PAD, HY_HID_PAD)), full((1, HY_HID_PAD)),
                  full((1, HY_HID_PAD)), full((HY_HID_PAD, HY_HID_PAD)), full((1, HY_HID_PAD)),
                  pl.BlockSpec((HY_HID_PAD, HY_FILT_TC), lambda o, j: (0, o * per_order + j)),
                  pl.BlockSpec((HY_HID_PAD, HY_FILT_TC), lambda o, j: (0, o * per_order + nc + j)),
                  pl.BlockSpec((FFT_N, HY_FILT_TC), lambda o, j: (0, j))],
        out_specs=pl.BlockSpec((FFT_N, HY_FILT_TC), lambda o, j: (0, o * nc + j)),
        out_shape=jax.ShapeDtypeStruct((FFT_N, HY_ORDER * HY_WIDTH), _BF),
        scratch_shapes=[pltpu.VMEM((FFT_N, HY_HID_PAD), _F32)],
        compiler_params=_cp(("arbitrary", "arbitrary"), 40), name="hy_filters")(
            z2, w1p, row(b1), row(freq), w2p, row(b2), w3p, w3p, dec2)


@functools.lru_cache(maxsize=None)
def _dft_matrices():
    t = np.arange(SEQ)
    ang = ((t[:, None] * t[None, :]) % FFT_N) * (2.0 * math.pi / FFT_N)
    re = np.cos(ang)
    im = -np.sin(ang)
    im[0] = 1.0 - 2.0 * (t % 2)
    nf = SEQ // HY_FB
    packed = np.stack([re.reshape(nf, HY_FB, SEQ), im.reshape(nf, HY_FB, SEQ)], axis=1).reshape(FFT_N, SEQ)
    return packed.astype(np.float32), np.ascontiguousarray(packed.T).astype(np.float32)


def _kf_kernel(f_ref, k_ref, o_ref):
    i = pl.program_id(0)
    f = f_ref[...]
    tm = f.shape[0]
    p1 = jnp.dot(f, k_ref[:SEQ, :], preferred_element_type=_F32)
    p2 = jnp.dot(f, k_ref[SEQ:, :], preferred_element_type=_F32)
    row = lax.broadcasted_iota(jnp.int32, (tm, 1), 0) + i * tm
    sign = (1 - 2 * (row & 1)).astype(_F32)
    o_ref[...] = p1 + sign * p2


def _filter_spectrum(fmat, kfilt):
    tm = 512
    nw = kfilt.shape[1]
    return pl.pallas_call(
        _kf_kernel, grid=(FFT_N // tm,),
        in_specs=[pl.BlockSpec((tm, SEQ), lambda i: (i, 0)),
                  pl.BlockSpec((FFT_N, nw), lambda i: (0, 0))],
        out_specs=pl.BlockSpec((tm, nw), lambda i: (i, 0)),
        out_shape=jax.ShapeDtypeStruct((FFT_N, nw), _F32),
        compiler_params=_cp(("parallel",), 40), name="hy_filter_spectrum")(fmat, kfilt)


HY_FB = 512
HY_NB = 2


HY_TM = 512


def _hconv_fwd_kernel(zb_ref, f_ref, k_ref, y_ref):
    fk = pl.program_id(1)
    kr = k_ref[:HY_FB, :]
    ki = k_ref[HY_FB:, :]
    row0 = jnp.logical_and(lax.broadcasted_iota(jnp.int32, kr.shape, 0) == 0, fk == 0)
    sc = jnp.where(row0, 1.0 / FFT_N, 2.0 / FFT_N)
    for j in range(HY_NB):
        u = jnp.dot(f_ref[...], zb_ref[j], preferred_element_type=_F32)
        ur = u[:HY_FB]
        ui = u[HY_FB:]
        yr = jnp.where(row0, ur * kr, ur * kr - ui * ki)
        yi = jnp.where(row0, ui * ki, ur * ki + ui * kr)
        y_ref[j, 0, 0] = (yr * sc).astype(_BF)
        y_ref[j, 0, 1] = (yi * sc).astype(_BF)


def _hconv_inv_kernel(ft_ref, y_ref, z_ref, skip_ref, xn_ref, o_ref, ob_ref):
    for j in range(HY_NB):
        conv = jnp.dot(ft_ref[...], y_ref[j], preferred_element_type=_F32)
        out = xn_ref[j] * (conv + z_ref[j] * skip_ref[...])
        o_ref[j] = out
        ob_ref[j] = out.astype(_BF)


def _long_conv_gate(hbc, zsrc, zb, xn_col, order, fmat, fmat_t, kf, skip):
    b = hbc.shape[0]
    nf = SEQ // HY_FB
    w = HY_WIDTH
    nb = HY_NB
    y = pl.pallas_call(
        _hconv_fwd_kernel, grid=(b // nb, nf),
        in_specs=[pl.BlockSpec((nb, SEQ, w), lambda i, f: (i, 0, 0)),
                  pl.BlockSpec((2 * HY_FB, SEQ), lambda i, f: (f, 0)),
                  pl.BlockSpec((2 * HY_FB, w), lambda i, f: (f, order))],
        out_specs=pl.BlockSpec((nb, 1, 2, HY_FB, w), lambda i, f: (i, f, 0, 0, 0)),
        out_shape=jax.ShapeDtypeStruct((b, nf, 2, HY_FB, w), _BF),
        compiler_params=_cp(("parallel", "parallel"), 40), name=f"hy_conv_fwd{order}")(zb, fmat, kf)
    blk = lambda col: pl.BlockSpec((nb, HY_TM, w), lambda i, m, c=col: (i, m, c))
    return pl.pallas_call(
        _hconv_inv_kernel, grid=(b // nb, SEQ // HY_TM),
        in_specs=[pl.BlockSpec((HY_TM, FFT_N), lambda i, m: (m, 0)),
                  pl.BlockSpec((nb, FFT_N, w), lambda i, m: (i, 0, 0)),
                  blk(0),
                  pl.BlockSpec((1, w), lambda i, m: (0, 0)),
                  blk(xn_col)],
        out_specs=[blk(0), blk(0)],
        out_shape=[jax.ShapeDtypeStruct((b, SEQ, w), _F32), jax.ShapeDtypeStruct((b, SEQ, w), _BF)],
        compiler_params=_cp(("parallel", "parallel"), 48), name=f"hy_conv_inv{order}")(
            fmat_t, y.reshape(b, FFT_N, w), zsrc, skip[order].reshape(1, w), hbc)


def _outproj_ln_kernel(n_a, *refs):
    a_refs = refs[:n_a]
    w_refs = refs[n_a:2 * n_a]
    x_ref, g_ref, b_ref, wr_ref, o_ref, ob_ref, aff_ref = refs[2 * n_a:]
    wr = wr_ref[...]
    wr_hi = wr.astype(_BF)
    wr_lo = (wr - wr_hi.astype(_F32)).astype(_BF)
    m = _dot(a_refs[0][...], w_refs[0][...])
    for a_ref, w_ref in zip(a_refs[1:], w_refs[1:]):
        m = m + _dot(a_ref[...], w_ref[...])
    y = _layer_norm(DN_ALPHA * x_ref[...] + m, g_ref[...], b_ref[...])
    o_ref[...] = y
    y_hi = y.astype(_BF)
    ob_ref[...] = y_hi
    y_lo = (y - y_hi.astype(_F32)).astype(_BF)
    by_hi = _dot_nt(jnp.concatenate([wr_hi, wr_lo], axis=0), y_hi)
    logits = by_hi[:N_EXPERTS] + by_hi[N_EXPERTS:] + _dot_nt(wr_hi, y_lo)
    ex = jnp.exp(logits - jnp.max(logits, axis=0, keepdims=True))
    aff_ref[0] = ex / jnp.sum(ex, axis=0, keepdims=True)


def _outproj_ln(a_list, w, x2d, g, b, w_router):
    m = x2d.shape[0]
    tm = 1024
    per_seq = SEQ // tm
    in_specs, w_args, row0 = [], [], 0
    for a in a_list:
        in_specs.append(pl.BlockSpec((tm, a.shape[1]), lambda i: (i, 0)))
    for a in a_list:
        ka = a.shape[1]
        in_specs.append(pl.BlockSpec((ka, D_MODEL), lambda i, r=row0 // ka: (r, 0)))
        w_args.append(w)
        row0 += ka
    in_specs += [pl.BlockSpec((tm, D_MODEL), lambda i: (i, 0)),
                 pl.BlockSpec((1, D_MODEL), lambda i: (0, 0)),
                 pl.BlockSpec((1, D_MODEL), lambda i: (0, 0)),
                 pl.BlockSpec((N_EXPERTS, D_MODEL), lambda i: (0, 0))]
    return pl.pallas_call(
        functools.partial(_outproj_ln_kernel, len(a_list)), grid=(m // tm,),
        in_specs=in_specs,
        out_specs=[pl.BlockSpec((tm, D_MODEL), lambda i: (i, 0)),
                   pl.BlockSpec((tm, D_MODEL), lambda i: (i, 0)),
                   pl.BlockSpec((1, N_EXPERTS, tm), lambda i: (i // per_seq, 0, i % per_seq))],
        out_shape=[jax.ShapeDtypeStruct((m, D_MODEL), _F32), jax.ShapeDtypeStruct((m, D_MODEL), _BF),
                   jax.ShapeDtypeStruct((m // SEQ, N_EXPERTS, SEQ), _F32)],
        compiler_params=_cp(("parallel",), 40), name="outproj_ln_router")(
            *a_list, *w_args, x2d, g.reshape(1, -1), b.reshape(1, -1), w_router.T)


BISECT_FLOOR = 2.0 ** -120
BISECT_GEO = 8
BISECT_LIN = 60


def _select_kernel(a_ref, tri_ref, edge_ref, slot_ref, slot_t_ref, cnt_ref):
    a = a_ref[...]
    rows = a.shape[0]
    cap = float(CAP)

    def probe(mid, lo, hi):
        cnt = jnp.sum(jnp.where(a > mid, 1.0, 0.0), axis=1, keepdims=True)
        ge = cnt >= cap
        return jnp.where(ge, mid, lo), jnp.where(ge, hi, mid)

    def geo(_, c):
        lo, hi = c
        return probe(jnp.where(lo > 0.0, jnp.sqrt(lo * hi), 0.5 * (lo + hi)), lo, hi)

    def lin(_, c):
        lo, hi = c
        return probe(0.5 * (lo + hi), lo, hi)

    c = probe(jnp.full((rows, 1), BISECT_FLOOR, _F32),
              jnp.full((rows, 1), -1.0, _F32), jnp.full((rows, 1), 1.0, _F32))
    c = lax.fori_loop(0, BISECT_GEO, geo, c)
    lo, hi = lax.fori_loop(0, BISECT_LIN, lin, c)
    vstar = jnp.max(jnp.where(a <= hi, a, -1.0), axis=1, keepdims=True)
    gt = a > vstar
    eq = a == vstar
    need = cap - jnp.sum(jnp.where(gt, 1.0, 0.0), axis=1, keepdims=True)
    tri = tri_ref[...]
    eq_before = jnp.dot(jnp.where(eq, 1.0, 0.0).astype(_BF), tri, preferred_element_type=_F32)
    sel = jnp.where(gt, 1.0, jnp.where(eq, jnp.where(eq_before < need, 1.0, 0.0), 0.0))
    pos = jnp.dot(sel.astype(_BF), tri, preferred_element_type=_F32)
    slot = jnp.where(sel > 0.5, pos, -1.0)
    slot_ref[...] = slot.astype(jnp.int32)
    cnt_ref[...] = jnp.dot(sel.astype(_BF), edge_ref[...], preferred_element_type=_F32).astype(jnp.int32)
    if rows < LANE:
        slot = jnp.concatenate([slot, jnp.full((LANE - rows, SEQ), -1.0, _F32)], axis=0)
    slot_tm = slot.T
    for b in range(rows // N_EXPERTS):
        shifted = slot_tm if b == 0 else pltpu.roll(slot_tm, LANE - N_EXPERTS * b, axis=1)
        slot_t_ref[b] = shifted.astype(jnp.int32)


@functools.lru_cache(maxsize=None)
def _prefix_matrix():
    idx = np.arange(SEQ)
    return (idx[:, None] < idx[None, :]).astype(_BF)


WIN_TT = 256
SCAT_TT = 512
WIN_W = 128
WIN_EDGES = 16


@functools.lru_cache(maxsize=None)
def _edge_matrix():
    return (np.arange(SEQ)[:, None] < np.arange(LANE)[None, :] * WIN_TT).astype(_BF)


def _select(aff2d):
    rows = aff2d.shape[0]
    assert rows <= LANE and rows % N_EXPERTS == 0
    bsz = rows // N_EXPERTS
    slot, slot_t, cnt = pl.pallas_call(
        _select_kernel, grid=(1,),
        in_specs=[pl.BlockSpec((rows, SEQ), lambda i: (0, 0)),
                  pl.BlockSpec((SEQ, SEQ), lambda i: (0, 0)),
                  pl.BlockSpec((SEQ, LANE), lambda i: (0, 0))],
        out_specs=[pl.BlockSpec((rows, SEQ), lambda i: (0, 0)),
                   pl.BlockSpec((bsz, SEQ, LANE), lambda i: (0, 0, 0)),
                   pl.BlockSpec((rows, LANE), lambda i: (0, 0))],
        out_shape=[jax.ShapeDtypeStruct((rows, SEQ), jnp.int32),
                   jax.ShapeDtypeStruct((bsz, SEQ, LANE), jnp.int32),
                   jax.ShapeDtypeStruct((rows, LANE), jnp.int32)],
        compiler_params=_cp(("arbitrary",), 48), name="moe_select")(aff2d, _prefix_matrix(), _edge_matrix())
    return slot, slot_t, cnt[:, :WIN_EDGES].reshape(rows * WIN_EDGES)


def _onehot(slot_row):
    return slot_row == lax.broadcasted_iota(jnp.int32, (CAP, SEQ), 0)


GATHER_NE = 4
GATHER_TC = 512
GATHER_W = 64


def _window_start(cnt_ref, seq, e, tile, width, span=1):
    base = (seq * N_EXPERTS + e) * WIN_EDGES + tile * span
    start = jnp.minimum((cnt_ref[base] >> 4) << 4, CAP - width)
    return start, cnt_ref[base + span] - start <= width


def _gather_kernel(cnt_ref, slot_ref, a_ref, xb_ref, xe_ref, g_ref):
    i = pl.program_id(0)
    n_tiles = SEQ // WIN_TT
    wd = GATHER_W
    starts, fits = {}, None
    for t in range(n_tiles):
        for e in range(N_EXPERTS):
            starts[e, t], ok = _window_start(cnt_ref, i, e, t, wd)
            fits = ok if fits is None else jnp.logical_and(fits, ok)

    @pl.when(fits)
    def _():
        xe_ref[...] = jnp.zeros(xe_ref.shape, _BF)
        g_ref[...] = jnp.zeros(g_ref.shape, _F32)
        sub_w = lax.broadcasted_iota(jnp.int32, (wd, WIN_TT), 0)
        for t in range(n_tiles):
            toks = slice(t * WIN_TT, (t + 1) * WIN_TT)
            ps = []
            for e in range(N_EXPERTS):
                hit = slot_ref[0, e:e + 1, toks] - starts[e, t] == sub_w
                ps.append(jnp.where(hit, 1.0, 0.0).astype(_BF))
                rows = pl.ds(pl.multiple_of(starts[e, t], 16), wd)
                g_ref[e, rows, :] += jnp.sum(jnp.where(hit, a_ref[0, e:e + 1, toks], 0.0), axis=1, keepdims=True)
            pcat = jnp.concatenate(ps, axis=0)
            for c in range(D_MODEL // GATHER_TC):
                cols = slice(c * GATHER_TC, (c + 1) * GATHER_TC)
                res = jnp.dot(pcat, xb_ref[0, toks, cols], preferred_element_type=_F32).astype(_BF)
                for e in range(N_EXPERTS):
                    rows = pl.ds(pl.multiple_of(starts[e, t], 16), wd)
                    xe_ref[e, rows, cols] += res[e * wd:(e + 1) * wd]

    @pl.when(jnp.logical_not(fits))
    def _():
        for grp in range(N_EXPERTS // GATHER_NE):
            es = range(grp * GATHER_NE, (grp + 1) * GATHER_NE)
            hits = [_onehot(slot_ref[0, e:e + 1, :]) for e in es]
            p = jnp.concatenate([jnp.where(h, 1.0, 0.0).astype(_BF) for h in hits], axis=0)
            xe = jnp.dot(p, xb_ref[0], preferred_element_type=_F32).astype(_BF)
            xe_ref[grp * GATHER_NE:(grp + 1) * GATHER_NE] = xe.reshape(GATHER_NE, CAP, D_MODEL)
            for k, e in enumerate(es):
                g_ref[e] = jnp.sum(jnp.where(hits[k], a_ref[0, e:e + 1, :], 0.0), axis=1, keepdims=True)


def _gather(cnt, slot3, aff3, xb3d):
    b = xb3d.shape[0]
    row = pl.BlockSpec((1, N_EXPERTS, SEQ), lambda i, c: (i, 0, 0))
    grid_spec = pltpu.PrefetchScalarGridSpec(
        num_scalar_prefetch=1, grid=(b,),
        in_specs=[row, row, pl.BlockSpec((1, SEQ, D_MODEL), lambda i, c: (i, 0, 0))],
        out_specs=[pl.BlockSpec((N_EXPERTS, CAP, D_MODEL), lambda i, c: (0, i, 0)),
                   pl.BlockSpec((N_EXPERTS, CAP, 1), lambda i, c: (0, i, 0))])
    return pl.pallas_call(
        _gather_kernel, grid_spec=grid_spec,
        out_shape=[jax.ShapeDtypeStruct((N_EXPERTS, b * CAP, D_MODEL), _BF),
                   jax.ShapeDtypeStruct((N_EXPERTS, b * CAP, 1), _F32)],
        compiler_params=_cp(("parallel",), 56), name="moe_gather")(cnt, slot3, aff3, xb3d)


FFN_TF = 1024
FFN_TM = 512


def _ffn_up_kernel(xe_ref, wg_ref, wu_ref, h_ref):
    wg = wg_ref[0].astype(_BF)
    wu = wu_ref[0].astype(_BF)
    for c in range(xe_ref.shape[1] // FFN_TM):
        rows = slice(c * FFN_TM, (c + 1) * FFN_TM)
        xe = xe_ref[0, rows, :]
        hg = _dot(xe, wg)
        hu = _dot(xe, wu)
        h_ref[0, rows, :] = ((hg * (1.0 / (1.0 + jnp.exp(-hg)))) * hu).astype(_BF)


def _ffn_down_kernel(h_ref, wd_ref, g_ref, ye_ref):
    wd = wd_ref[0].astype(_BF)
    for c in range(h_ref.shape[1] // FFN_TM):
        rows = slice(c * FFN_TM, (c + 1) * FFN_TM)
        ye_ref[0, rows, :] = (_dot(h_ref[0, rows, :], wd) * g_ref[0, rows, :]).astype(_BF)


def _expert_ffn(xe, gates, wg, wu, wd, layer):
    rows = xe.shape[1]
    h = pl.pallas_call(
        _ffn_up_kernel, grid=(N_EXPERTS, D_FF_EXPERT // FFN_TF),
        in_specs=[pl.BlockSpec((1, rows, D_MODEL), lambda e, f: (e, 0, 0)),
                  pl.BlockSpec((None, 1, D_MODEL, FFN_TF), lambda e, f: (layer, e, 0, f)),
                  pl.BlockSpec((None, 1, D_MODEL, FFN_TF), lambda e, f: (layer, e, 0, f))],
        out_specs=pl.BlockSpec((1, rows, FFN_TF), lambda e, f: (e, 0, f)),
        out_shape=jax.ShapeDtypeStruct((N_EXPERTS, rows, D_FF_EXPERT), _BF),
        compiler_params=_cp(("parallel", "parallel"), 52), name="moe_ffn_up")(xe, wg, wu)
    return pl.pallas_call(
        _ffn_down_kernel, grid=(N_EXPERTS,),
        in_specs=[pl.BlockSpec((1, rows, D_FF_EXPERT), lambda e: (e, 0, 0)),
                  pl.BlockSpec((None, 1, D_FF_EXPERT, D_MODEL), lambda e: (layer, e, 0, 0)),
                  pl.BlockSpec((1, rows, 1), lambda e: (e, 0, 0))],
        out_specs=pl.BlockSpec((1, rows, D_MODEL), lambda e: (e, 0, 0)),
        out_shape=jax.ShapeDtypeStruct((N_EXPERTS, rows, D_MODEL), _BF),
        compiler_params=_cp(("parallel",), 56), name="moe_ffn_down")(h, wd, gates)


def _scatter_ln_ple_kernel(cnt_ref, slot_t_ref, ye_ref, x_ref, g_ref, b_ref, p_ref, wg_ref, wp_ref, o_ref,
                           pt_ref, yw_ref, f_ref):
    i = pl.program_id(0)
    r = pl.program_id(1)
    slot_t = slot_t_ref[0]
    starts = []
    fits = None
    for e in range(N_EXPERTS):
        start, ok = _window_start(cnt_ref, i, e, r, WIN_W, SCAT_TT // WIN_TT)
        starts.append(start)
        fits = ok if fits is None else jnp.logical_and(fits, ok)

    @pl.when(fits)
    def _():
        lane_w = lax.broadcasted_iota(jnp.int32, (SCAT_TT, WIN_W), 1)
        for e in range(N_EXPERTS):
            cols = slice(e * WIN_W, (e + 1) * WIN_W)
            pt_ref[:, cols] = jnp.where(slot_t[:, e:e + 1] - starts[e] == lane_w, 1.0, 0.0).astype(_BF)
            yw_ref[cols, :] = ye_ref[e, pl.ds(pl.multiple_of(starts[e], 16), WIN_W), :]
        f_ref[...] = jnp.dot(pt_ref[:, :N_EXPERTS * WIN_W], yw_ref[...], preferred_element_type=_F32)

    @pl.when(jnp.logical_not(fits))
    def _():
        lane_c = lax.broadcasted_iota(jnp.int32, (SCAT_TT, CAP), 1)
        for e in range(N_EXPERTS):
            pt_ref[:, e * CAP:(e + 1) * CAP] = jnp.where(slot_t[:, e:e + 1] == lane_c, 1.0, 0.0).astype(_BF)
        f_ref[...] = jnp.dot(pt_ref[...], ye_ref[...].reshape(N_EXPERTS * CAP, D_MODEL),
                             preferred_element_type=_F32)

    y = _layer_norm(DN_ALPHA * x_ref[0] + f_ref[...], g_ref[...], b_ref[...])
    gate = _dot(y, wg_ref[...])
    emb = _dot(p_ref[0], wp_ref[...])
    o_ref[0] = y + (1.0 / (1.0 + jnp.exp(-gate))) * emb


def _scatter_ln_ple(cnt, slot_t, ye, x3d, g, b, p, ple_gate, ple_proj, layer):
    bsz = x3d.shape[0]
    vec = pl.BlockSpec((1, D_MODEL), lambda i, r, c: (0, 0))
    seq = pl.BlockSpec((1, SCAT_TT, D_MODEL), lambda i, r, c: (i, r, 0))
    grid_spec = pltpu.PrefetchScalarGridSpec(
        num_scalar_prefetch=1, grid=(bsz, SEQ // SCAT_TT),
        in_specs=[pl.BlockSpec((1, SCAT_TT, LANE), lambda i, r, c: (i, r, 0)),
                  pl.BlockSpec((N_EXPERTS, CAP, D_MODEL), lambda i, r, c: (0, i, 0)),
                  seq, vec, vec,
                  pl.BlockSpec((None, 1, SCAT_TT, PLE_DIM), lambda i, r, c: (layer, i, r, 0)),
                  pl.BlockSpec((None, D_MODEL, D_MODEL), lambda i, r, c: (layer, 0, 0)),
                  pl.BlockSpec((None, PLE_DIM, D_MODEL), lambda i, r, c: (layer, 0, 0))],
        out_specs=seq,
        scratch_shapes=[pltpu.VMEM((SCAT_TT, N_EXPERTS * CAP), _BF),
                        pltpu.VMEM((N_EXPERTS * WIN_W, D_MODEL), _BF),
                        pltpu.VMEM((SCAT_TT, D_MODEL), _F32)])
    return pl.pallas_call(
        _scatter_ln_ple_kernel, grid_spec=grid_spec,
        out_shape=jax.ShapeDtypeStruct((bsz, SEQ, D_MODEL), _F32),
        compiler_params=_cp(("parallel", "arbitrary"), 56), name="moe_scatter_ln_ple")(
            cnt, slot_t, ye, x3d, g.reshape(1, -1), b.reshape(1, -1), p, ple_gate, ple_proj)


def _moe_ple_block(x1, x1b, aff, wg, wu, wd, g, b, p, ple_gate, ple_proj, layer):
    bsz = x1.shape[0]
    slot, slot_t, cnt = _select(aff.reshape(bsz * N_EXPERTS, SEQ))
    xe, gates = _gather(cnt, slot.reshape(bsz, N_EXPERTS, SEQ), aff, x1b)
    ye = _expert_ffn(xe, gates, wg, wu, wd, layer)
    return _scatter_ln_ple(cnt, slot_t, ye, x1, g, b, p, ple_gate, ple_proj, layer)


MLA_HB = LANE
MLA_IN_PAD = 768
MLA_KR_COL = MLA_Q_LORA + MLA_KV_LORA


def _rms(x, g):
    return x * lax.rsqrt(jnp.mean(x * x, axis=-1, keepdims=True) + NORM_EPS) * g


def _mla_pre_kernel(x_ref, win_ref, qg_ref, wq_ref, kvg_ref, wkv_ref, qa_ref, qb_ref, ka_ref, kb_ref,
                    q_ref, kv_ref, kr_ref):
    h = _dot(x_ref[...], win_ref[...])
    q = _dot(_rms(h[:, :MLA_Q_LORA], qg_ref[...]), wq_ref[...])
    swap = pltpu.roll(q, q.shape[1] - MLA_ROPE, axis=1)
    qa = jnp.concatenate([qa_ref[...]] * MLA_HEADS, axis=1)
    qb = jnp.concatenate([qb_ref[...]] * MLA_HEADS, axis=1)
    q_ref[...] = (q * qa + swap * qb).astype(_BF)
    kv_ref[...] = _dot(_rms(h[:, MLA_Q_LORA:MLA_KR_COL], kvg_ref[...]), wkv_ref[...]).astype(_BF)
    kr = h[:, MLA_KR_COL:MLA_KR_COL + MLA_HB]
    kr_ref[...] = (kr * ka_ref[...] + pltpu.roll(kr, MLA_HB - MLA_ROPE, axis=1) * kb_ref[...]).astype(_BF)


@functools.lru_cache(maxsize=None)
def _rope_lane_tables():
    inv = 1.0 / (ROPE_THETA ** (np.arange(0, MLA_ROPE, 2) / MLA_ROPE))
    ang = np.arange(SEQ)[:, None] * inv[None, :]
    cos, sin = np.cos(ang), np.sin(ang)
    ones = np.ones((SEQ, MLA_NOPE))
    zeros_n = np.zeros((SEQ, MLA_NOPE))
    zeros_r = np.zeros((SEQ, MLA_ROPE))
    cc = np.concatenate([cos, cos], axis=1)
    ss = np.concatenate([-sin, sin], axis=1)
    keep_q = np.concatenate([ones, cc, zeros_r], axis=1)
    keep_k = np.concatenate([zeros_n, cc, zeros_r], axis=1)
    swp = np.concatenate([zeros_n, ss, zeros_r], axis=1)
    qscale = (MLA_NOPE + MLA_ROPE) ** -0.5 * LOG2E
    f32 = lambda a: a.astype(np.float32)
    return f32(keep_q * qscale), f32(swp * qscale), f32(keep_k), f32(swp)


def _mla_pre(x2d, w_in, q_norm, w_q_up, kv_norm, w_kv_up):
    m = x2d.shape[0]
    tm = 512
    half = MLA_ROPE // 2
    kr_w = w_in[:, MLA_KR_COL:]
    kr_swapped = jnp.concatenate([kr_w[:, half:], kr_w[:, :half]], axis=1)
    win = jnp.concatenate([w_in[:, :MLA_KR_COL], jnp.zeros((D_MODEL, MLA_NOPE), _F32), kr_w, kr_swapped],
                          axis=1).astype(_BF)
    wq = w_q_up.reshape(MLA_Q_LORA, MLA_HEADS, MLA_NOPE + MLA_ROPE)
    rope_w = wq[:, :, MLA_NOPE:]
    wq = jnp.concatenate([wq, rope_w[:, :, half:], rope_w[:, :, :half]], axis=2).reshape(
        MLA_Q_LORA, MLA_HEADS * MLA_HB).astype(_BF)
    w_kv_up = w_kv_up.astype(_BF)
    q_keep, q_swap, k_keep, k_swap = _rope_lane_tables()
    nq = MLA_HEADS * MLA_HB
    pos = lambda: pl.BlockSpec((tm, MLA_HB), lambda i: (i % (SEQ // tm), 0))
    full = lambda a: pl.BlockSpec(a.shape, lambda i: (0,) * a.ndim)
    qg, kvg = q_norm.reshape(1, -1), kv_norm.reshape(1, -1)
    return pl.pallas_call(
        _mla_pre_kernel, grid=(m // tm,),
        in_specs=[pl.BlockSpec((tm, D_MODEL), lambda i: (i, 0)), full(win), full(qg), full(wq), full(kvg),
                  full(w_kv_up), pos(), pos(), pos(), pos()],
        out_specs=[pl.BlockSpec((tm, nq), lambda i: (i, 0)),
                   pl.BlockSpec((tm, nq), lambda i: (i, 0)),
                   pl.BlockSpec((tm, MLA_HB), lambda i: (i, 0))],
        out_shape=[jax.ShapeDtypeStruct((m, nq), _BF), jax.ShapeDtypeStruct((m, nq), _BF),
                   jax.ShapeDtypeStruct((m, MLA_HB), _BF)],
        compiler_params=_cp(("parallel",), 48), name="mla_pre")(
            x2d, win, qg, wq, kvg, w_kv_up, q_keep, q_swap, k_keep, k_swap)


MLA_TQ = 256
MLA_PAIR = 2


def _mla_attn_kernel(q_ref, kv_ref, kr_ref, o_ref, kt_scr, v_scr, s_scr):
    lane = lax.broadcasted_iota(jnp.int32, (SEQ, MLA_HB), 1)
    kr = kr_ref[0].astype(_F32)
    for j in range(MLA_PAIR):
        kvh = kv_ref[0, :, j * MLA_HB:(j + 1) * MLA_HB].astype(_F32)
        kt_scr[j] = jnp.where(lane < MLA_NOPE, kvh, kr).T.astype(_BF)
        v_scr[j] = jnp.where(lane < MLA_NOPE, 1.0, kvh).astype(_BF)
    lane_q = lax.broadcasted_iota(jnp.int32, (MLA_TQ, MLA_HB), 1)

    def block_rows(i):
        return pl.ds(pl.multiple_of(i * MLA_TQ, MLA_TQ), MLA_TQ)

    def scores(i, slot):
        for j in range(MLA_PAIR):
            s_scr[slot, j] = _dot(q_ref[0, block_rows(i), j * MLA_HB:(j + 1) * MLA_HB], kt_scr[j])

    def finish(i, slot):
        outs = []
        for j in range(MLA_PAIR):
            s = s_scr[slot, j]
            p = jnp.exp2(s - jnp.max(s, axis=1, keepdims=True))
            o = _dot(p, v_scr[j])
            outs.append(o / o[:, 0:1])
        even = pltpu.roll(outs[0], MLA_V, axis=1)
        o_ref[0, block_rows(i), :] = jnp.where(lane_q < MLA_V, even, outs[1]).astype(_BF)

    nblk = SEQ // MLA_TQ
    scores(0, 0)

    def body(k, carry):
        scores(2 * k + 1, 1)
        finish(2 * k, 0)
        scores(2 * k + 2, 0)
        finish(2 * k + 1, 1)
        return carry

    lax.fori_loop(0, nblk // 2 - 1, body, 0)
    scores(nblk - 1, 1)
    finish(nblk - 2, 0)
    finish(nblk - 1, 1)


def _mla_attention(q, kv, kr):
    b = q.shape[0]
    w = MLA_PAIR * MLA_HB
    return pl.pallas_call(
        _mla_attn_kernel, grid=(b, MLA_HEADS // MLA_PAIR),
        in_specs=[pl.BlockSpec((1, SEQ, w), lambda i, h: (i, 0, h)),
                  pl.BlockSpec((1, SEQ, w), lambda i, h: (i, 0, h)),
                  pl.BlockSpec((1, SEQ, MLA_HB), lambda i, h: (i, 0, 0))],
        out_specs=pl.BlockSpec((1, SEQ, MLA_PAIR * MLA_V), lambda i, h: (i, 0, h)),
        out_shape=jax.ShapeDtypeStruct((b, SEQ, MLA_HEADS * MLA_V), _BF),
        scratch_shapes=[pltpu.VMEM((MLA_PAIR, MLA_HB, SEQ), _BF), pltpu.VMEM((MLA_PAIR, SEQ, MLA_HB), _BF),
                        pltpu.VMEM((2, MLA_PAIR, MLA_TQ, SEQ), _F32)],
        compiler_params=_cp(("parallel", "parallel"), 40), name="mla_attention")(q, kv, kr)


def _na_hyena_mixer(x, w_in, rpb, conv_w, conv_b, f_w1, f_b1, f_freq, f_w2, f_b2, f_w3, skip):
    b = x.shape[0]
    qkv, hbc, zb = _inproj_conv(x.reshape(b * SEQ, D_MODEL), w_in, conv_w, conv_b)
    y_a = _na_attention(qkv.reshape(b, SEQ, 3 * NA_WIDTH), rpb)
    hbc = hbc.reshape(b, SEQ, (HY_ORDER + 1) * HY_WIDTH)
    zb = zb.reshape(b, SEQ, HY_WIDTH)
    fmat, fmat_t = (jnp.asarray(f).astype(_BF) for f in _dft_matrices())
    kfilt = _hyena_filters(f_w1, f_b1, f_freq, f_w2, f_b2, f_w3)
    kf = _filter_spectrum(fmat, kfilt)
    z = hbc
    for o in range(HY_ORDER):
        z, zb = _long_conv_gate(hbc, z, zb, o + 1, o, fmat, fmat_t, kf, skip)
    return y_a.reshape(b * SEQ, NA_WIDTH), zb.reshape(b * SEQ, HY_WIDTH)


def kernel(x, p, ab_w_in, na_rpb, hy_conv_w, hy_conv_b, hy_f_w1, hy_f_b1, hy_f_freq, hy_f_w2, hy_f_b2, hy_f_w3, hy_skip, ab_w_out, mla_w_in, mla_q_norm, mla_w_q_up, mla_kv_norm, mla_w_kv_up, mla_w_out, ln1_g, ln1_b, ln2_g, ln2_b, moe_router, moe_w_gate, moe_w_up, moe_w_down, ple_gate, ple_proj):
    b = x.shape[0]
    m = b * SEQ
    ple_gate_b, ple_proj_b = ple_gate.astype(_BF), ple_proj.astype(_BF)
    for i in range(DEPTH):
        j = i // 2
        x2d = x.reshape(m, D_MODEL)
        if i % 2 == 0:
            y_a, z = _na_hyena_mixer(x, ab_w_in[j].astype(_BF), na_rpb[j], hy_conv_w[j], hy_conv_b[j], hy_f_w1[j],
                                     hy_f_b1[j], hy_f_freq[j], hy_f_w2[j], hy_f_b2[j], hy_f_w3[j], hy_skip[j])
            x1, x1b, aff = _outproj_ln([y_a, z], ab_w_out[j].astype(_BF), x2d, ln1_g[i], ln1_b[i], moe_router[i])
        else:
            q, kv, kr = _mla_pre(x2d, mla_w_in[j], mla_q_norm[j], mla_w_q_up[j], mla_kv_norm[j], mla_w_kv_up[j])
            att = _mla_attention(q.reshape(b, SEQ, -1), kv.reshape(b, SEQ, -1), kr.reshape(b, SEQ, -1))
            x1, x1b, aff = _outproj_ln([att.reshape(m, MLA_HEADS * MLA_V)], mla_w_out[j].astype(_BF), x2d,
                                       ln1_g[i], ln1_b[i], moe_router[i])
        x = _moe_ple_block(x1.reshape(b, SEQ, D_MODEL), x1b.reshape(b, SEQ, D_MODEL), aff,
                           moe_w_gate, moe_w_up, moe_w_down, ln2_g[i], ln2_b[i], p, ple_gate_b, ple_proj_b, i)
    return x
```
